```python
import jax, jax.numpy as jnp
from jax import lax
import numpy as np

D_MODEL = 1024
BATCH = 16
SEQ = 2048
DEPTH = 4

N_MIXERS = 3
HD = 64
H_MIX = 12
H_MEM = 4
MIX_W = H_MIX * HD
MEM_W = H_MEM * HD
N_MEM = 256
D_FF = 2816
ROPE_THETA = 10000.0
EPS = 1e-6
NEG = -1e30
FORCE = 1e9

RET_H = 6
RET_DK = 64
RET_DV = 128
RET_CHUNK = 128

DSA_H = 12
IDX_H = 8
IDX_D = 64
DSA_TOPK_MAX = 256
Q_BLK = 128

NSA_H = 12
NSA_G = 4
CMP_L = 32
CMP_S = 16
SLC_L = 64
SLC_N_MAX = 16
WIN = 512
SLC_Q_BLK = 16

N_A = (DEPTH + 2) // 3
N_B = (DEPTH + 1) // 3
N_C = DEPTH // 3

RET_SIZES = [RET_H * RET_DK, RET_H * RET_DK, RET_H * RET_DV, RET_H * RET_DV, MEM_W]
DSA_SIZES = [DSA_H * HD, HD, HD, IDX_H * IDX_D, IDX_D, IDX_H, MEM_W]
NSA_SIZES = [NSA_H * HD] + [NSA_G * HD] * 6 + [NSA_H * 3, MEM_W]
RET_COLS = sum(RET_SIZES)
DSA_COLS = sum(DSA_SIZES)
NSA_COLS = sum(NSA_SIZES)

kernel_name = 'hybrid_interleaved_retention_dsa_nsa_macaron'


def split_cols(a, sizes):
    return jnp.split(a, np.cumsum(sizes)[:-1].tolist(), axis=-1)


def rms_norm(x, g):
    xf = x.astype(jnp.float32)
    y = xf * lax.rsqrt(jnp.mean(xf * xf, axis=-1, keepdims=True) + EPS)
    return (y * g.astype(jnp.float32)).astype(x.dtype)


def rope(x, pos):
    half = x.shape[-1] // 2
    inv = ROPE_THETA ** (-jnp.arange(half, dtype=jnp.float32) / half)
    ang = pos.astype(jnp.float32)[:, None] * inv[None, :]
    cos = jnp.cos(ang)[:, None, :].astype(x.dtype)
    sin = jnp.sin(ang)[:, None, :].astype(x.dtype)
    x1, x2 = x[..., :half], x[..., half:]
    return jnp.concatenate([x1 * cos - x2 * sin, x2 * cos + x1 * sin], axis=-1)


def swiglu(x, wg, wu, wd):
    return (jax.nn.silu(x @ wg) * (x @ wu)) @ wd


def masked_softmax(logits, valid):
    return jax.nn.softmax(jnp.where(valid, logits.astype(jnp.float32), NEG), axis=-1)


def chunkwise_retention(q, k, v):
    B, L, H, DK = q.shape
    DV = v.shape[-1]
    C = RET_CHUNK
    nc = L // C
    log_g = jnp.log(1.0 - 2.0 ** (-5.0 - jnp.arange(H, dtype=jnp.float32)))
    qc = q.reshape(B, nc, C, H, DK)
    kc = k.reshape(B, nc, C, H, DK)
    vc = v.reshape(B, nc, C, H, DV)
    idx = jnp.arange(C, dtype=jnp.float32)
    diff = idx[:, None] - idx[None, :]
    dmask = jnp.where(diff >= 0, jnp.exp(log_g[:, None, None] * jnp.maximum(diff, 0.0)), 0.0)
    inner = jnp.einsum('bnihd,bnjhd->bnhij', qc, kc) * dmask
    o_inner = jnp.einsum('bnhij,bnjhe->bnihe', inner, vc)
    zeta = jnp.exp(log_g[:, None] * (C - 1.0 - idx)[None, :])
    kv = jnp.einsum('bnjhd,hj,bnjhe->bnhde', kc, zeta, vc)
    chunk_decay = jnp.exp(log_g * C)[:, None, None]

    def step(state, kv_n):
        return state * chunk_decay + kv_n, state

    state0 = jnp.zeros((B, H, DK, DV), kv.dtype)
    _, prev = lax.scan(step, state0, jnp.moveaxis(kv, 1, 0))
    prev = jnp.moveaxis(prev, 0, 1)
    xi = jnp.exp(log_g[:, None] * (idx + 1.0)[None, :])
    o_cross = jnp.einsum('bnihd,bnhde->bnihe', qc, prev) * xi.T[None, None, :, :, None]
    return (o_inner + o_cross).reshape(B, L, H, DV).astype(jnp.float32)


def retention_mixer(h, w_in):
    B, L, _ = h.shape
    pos = jnp.arange(L)
    q, k, v, g, q_mem = split_cols(h @ w_in, RET_SIZES)
    q = rope(q.reshape(B, L, RET_H, RET_DK), pos)
    k = rope(k.reshape(B, L, RET_H, RET_DK), pos) * (RET_DK ** -0.5)
    v = v.reshape(B, L, RET_H, RET_DV)
    o = chunkwise_retention(q, k, v)
    mu = jnp.mean(o, axis=-1, keepdims=True)
    var = jnp.mean(jnp.square(o - mu), axis=-1, keepdims=True)
    o = ((o - mu) * lax.rsqrt(var + EPS)).astype(h.dtype)
    y = jax.nn.silu(g) * o.reshape(B, L, RET_H * RET_DV)
    return y, q_mem


def dsa_mixer(h, w_in, qn, kn):
    B, L, _ = h.shape
    pos = jnp.arange(L)
    q, k, v, iq, ik, iw, q_mem = split_cols(h @ w_in, DSA_SIZES)
    q = rope(rms_norm(q.reshape(B, L, DSA_H, HD), qn), pos)
    k = rope(rms_norm(k.reshape(B, L, 1, HD), kn), pos)[:, :, 0]
    iq = rope(iq.reshape(B, L, IDX_H, IDX_D), pos)
    ik = rope(ik.reshape(B, L, 1, IDX_D), pos)[:, :, 0]
    topk = min(DSA_TOPK_MAX, L // 4)
    scale = HD ** -0.5
    gather = jax.vmap(lambda t, s: t[s])

    def block(i):
        qs = i * Q_BLK
        qpos = qs + jnp.arange(Q_BLK)
        q_b = lax.dynamic_slice_in_dim(q, qs, Q_BLK, axis=1)
        iq_b = lax.dynamic_slice_in_dim(iq, qs, Q_BLK, axis=1)
        iw_b = lax.dynamic_slice_in_dim(iw, qs, Q_BLK, axis=1)
        rel = jax.nn.relu(jnp.einsum('bqhd,bsd->bqhs', iq_b, ik).astype(jnp.float32))
        score = jnp.einsum('bqh,bqhs->bqs', iw_b.astype(jnp.float32), rel)
        causal = pos[None, :] <= qpos[:, None]
        score = jnp.where(causal[None], score, -jnp.inf)
        _, sel = lax.top_k(score, topk)
        k_sel = gather(k, sel)
        v_sel = gather(v, sel)
        valid = (sel <= qpos[None, :, None])[:, :, None, :]
        lg = jnp.einsum('bqhd,bqkd->bqhk', q_b, k_sel) * scale
        p = masked_softmax(lg, valid)
        return jnp.einsum('bqhk,bqkd->bqhd', p.astype(v_sel.dtype), v_sel)

    o = lax.map(block, jnp.arange(L // Q_BLK))
    o = jnp.moveaxis(o, 0, 1).reshape(B, L, DSA_H * HD)
    return o, q_mem


def nsa_mixer(h, w_in, qn, kn, cmp_pos_k, cmp_pos_v, cmp_wk, cmp_wv):
    B, L, _ = h.shape
    R = NSA_H // NSA_G
    pos = jnp.arange(L)
    scale = HD ** -0.5
    q, kc, vc, ks, vs, kw, vw, gates, q_mem = split_cols(h @ w_in, NSA_SIZES)
    q = rope(rms_norm(q.reshape(B, L, NSA_H, HD), qn), pos).reshape(B, L, NSA_G, R, HD)
    grp = lambda t: t.reshape(B, L, NSA_G, HD)
    kc, vc, vs, vw = grp(kc), grp(vc), grp(vs), grp(vw)
    ks = rope(rms_norm(grp(ks), kn[1]), pos)
    kw = rope(rms_norm(grp(kw), kn[2]), pos)

    n_cmp = (L - CMP_L) // CMP_S + 1
    starts = jnp.arange(n_cmp) * CMP_S
    ends = starts + CMP_L - 1
    blk = starts[:, None] + jnp.arange(CMP_L)[None, :]
    k_blk = kc[:, blk] + cmp_pos_k[None, None, :, None, :]
    v_blk = vc[:, blk] + cmp_pos_v[None, None, :, None, :]
    k_cmp = jnp.einsum('bnlgd,lde->bnge', k_blk, cmp_wk)
    v_cmp = jnp.einsum('bnlgd,lde->bnge', v_blk, cmp_wv)
    k_cmp = rope(rms_norm(k_cmp, kn[0]), ends)
    valid_c = ends[None, :] <= pos[:, None]
    lg = jnp.einsum('bqgrd,bngd->bgrqn', q, k_cmp) * scale
    p_cmp = masked_softmax(lg, valid_c) * jnp.any(valid_c, axis=-1)[:, None]
    o_cmp = jnp.einsum('bgrqn,bngd->bqgrd', p_cmp.astype(v_cmp.dtype), v_cmp)

    n_slc = L // SLC_L
    n_sel = min(SLC_N_MAX, n_slc)
    sstart = jnp.arange(n_slc) * SLC_L
    cover = ((starts[:, None] < sstart[None, :] + SLC_L) &
             (starts[:, None] + CMP_L > sstart[None, :])).astype(jnp.float32)
    imp = jnp.einsum('bgrqn,nj->bgqj', p_cmp, cover)
    jj = jnp.arange(n_slc)[None, :]
    qblk = (pos // SLC_L)[:, None]
    forced = (jj == 0) | (jj == qblk) | (jj == qblk - 1)
    causal_blk = sstart[None, :] <= pos[:, None]
    imp = jnp.where(causal_blk, imp + forced.astype(jnp.float32) * FORCE, NEG)
    _, sel = lax.top_k(imp, n_sel)
    ks_blk = ks.reshape(B, n_slc, SLC_L, NSA_G, HD).transpose(0, 3, 1, 2, 4)
    vs_blk = vs.reshape(B, n_slc, SLC_L, NSA_G, HD).transpose(0, 3, 1, 2, 4)
    gather = jax.vmap(jax.vmap(lambda t, s: t[s]))

    def slc_block(i):
        qs = i * SLC_Q_BLK
        qpos = qs + jnp.arange(SLC_Q_BLK)
        q_c = lax.dynamic_slice_in_dim(q, qs, SLC_Q_BLK, axis=1)
        s_c = lax.dynamic_slice_in_dim(sel, qs, SLC_Q_BLK, axis=2)
        k_g = gather(ks_blk, s_c)
        v_g = gather(vs_blk, s_c)
        kpos = s_c[..., None] * SLC_L + jnp.arange(SLC_L)
        valid = (kpos <= qpos[None, None, :, None, None])[:, :, None]
        lg_s = jnp.einsum('bqgrd,bgqnld->bgrqnl', q_c, k_g) * scale
        lg_s = jnp.where(valid, lg_s.astype(jnp.float32), NEG)
        p = jax.nn.softmax(lg_s.reshape(B, NSA_G, R, SLC_Q_BLK, n_sel * SLC_L), axis=-1).reshape(lg_s.shape)
        return jnp.einsum('bgrqnl,bgqnld->bqgrd', p.astype(v_g.dtype), v_g)

    o_slc = lax.map(slc_block, jnp.arange(L // SLC_Q_BLK))
    o_slc = jnp.moveaxis(o_slc, 0, 1).reshape(B, L, NSA_G, R, HD)

    pad = ((0, 0), (WIN, 0), (0, 0), (0, 0))
    kw_p, vw_p = jnp.pad(kw, pad), jnp.pad(vw, pad)

    def win_block(i):
        qs = i * Q_BLK
        qpos = qs + jnp.arange(Q_BLK)
        kpos = qs - WIN + jnp.arange(WIN + Q_BLK)
        q_b = lax.dynamic_slice_in_dim(q, qs, Q_BLK, axis=1)
        k_b = lax.dynamic_slice_in_dim(kw_p, qs, WIN + Q_BLK, axis=1)
        v_b = lax.dynamic_slice_in_dim(vw_p, qs, WIN + Q_BLK, axis=1)
        d = qpos[:, None] - kpos[None, :]
        valid = (d >= 0) & (d < WIN) & (kpos[None, :] >= 0)
        lg_w = jnp.einsum('bqgrd,bkgd->bgrqk', q_b, k_b) * scale
        p = masked_softmax(lg_w, valid)
        return jnp.einsum('bgrqk,bkgd->bqgrd', p.astype(v_b.dtype), v_b)

    o_win = lax.map(win_block, jnp.arange(L // Q_BLK))
    o_win = jnp.moveaxis(o_win, 0, 1).reshape(B, L, NSA_G, R, HD)

    g = jax.nn.sigmoid(gates.reshape(B, L, NSA_G, R, 3))
    o = g[..., 0:1] * o_cmp + g[..., 1:2] * o_slc + g[..., 2:3] * o_win
    return o.reshape(B, L, NSA_H * HD), q_mem


def memory_attention(q_raw, mem, norm_g, w_kv, qn, kn):
    B, L, _ = q_raw.shape
    M = mem.shape[1]
    k, v = split_cols(rms_norm(mem, norm_g) @ w_kv, [MEM_W, MEM_W])
    k = rms_norm(k.reshape(B, M, H_MEM, HD), kn)
    v = v.reshape(B, M, H_MEM, HD)
    q = rms_norm(q_raw.reshape(B, L, H_MEM, HD), qn)
    lg = jnp.einsum('bqhd,bmhd->bhqm', q, k).astype(jnp.float32) * (HD ** -0.5)
    p = jax.nn.softmax(lg, axis=-1)
    return jnp.einsum('bhqm,bmhd->bqhd', p.astype(v.dtype), v).reshape(B, L, MEM_W)


def setup_inputs(seed: int = 0) -> dict:
    key = jax.random.key(seed)
    ks = jax.random.split(key, 24)
    f32 = jnp.float32

    def nrm(k, shape, fan_in):
        return jax.random.normal(k, shape, f32) * (fan_in ** -0.5)

    def gain(k, shape):
        return 1.0 + 0.05 * jax.random.normal(k, shape, f32)

    return {
        'x': jax.random.normal(ks[0], (BATCH, SEQ, D_MODEL), f32),
        'mem': jax.random.normal(ks[1], (BATCH, N_MEM, D_MODEL), f32),
        'ffn_norm': gain(ks[2], (DEPTH, 2, D_MODEL)),
        'ffn_w_gate': nrm(ks[3], (DEPTH, 2, D_MODEL, D_FF), D_MODEL),
        'ffn_w_up': nrm(ks[4], (DEPTH, 2, D_MODEL, D_FF), D_MODEL),
        'ffn_w_down': nrm(ks[5], (DEPTH, 2, D_FF, D_MODEL), D_FF),
        'mix_norm': gain(ks[6], (DEPTH, D_MODEL)),
        'w_out': nrm(ks[7], (DEPTH, MIX_W + MEM_W, D_MODEL), MIX_W + MEM_W),
        'mem_norm': gain(ks[8], (DEPTH, D_MODEL)),
        'mem_w_kv': nrm(ks[9], (DEPTH, D_MODEL, 2 * MEM_W), D_MODEL),
        'mem_qn': gain(ks[10], (DEPTH, HD)),
        'mem_kn': gain(ks[11], (DEPTH, HD)),
        'ret_w_in': nrm(ks[12], (N_A, D_MODEL, RET_COLS), D_MODEL),
        'dsa_w_in': nrm(ks[13], (N_B, D_MODEL, DSA_COLS), D_MODEL),
        'dsa_qn': gain(ks[14], (N_B, HD)),
        'dsa_kn': gain(ks[15], (N_B, HD)),
        'nsa_w_in': nrm(ks[16], (N_C, D_MODEL, NSA_COLS), D_MODEL),
        'nsa_qn': gain(ks[17], (N_C, HD)),
        'nsa_kn': gain(ks[18], (N_C, 3, HD)),
        'nsa_cmp_pos_k': 0.1 * jax.random.normal(ks[19], (N_C, CMP_L, HD), f32),
        'nsa_cmp_pos_v': 0.1 * jax.random.normal(ks[20], (N_C, CMP_L, HD), f32),
        'nsa_cmp_wk': nrm(ks[21], (N_C, CMP_L, HD, HD), CMP_L * HD),
        'nsa_cmp_wv': nrm(ks[22], (N_C, CMP_L, HD, HD), CMP_L * HD),
    }


def reference(x, mem, ffn_norm, ffn_w_gate, ffn_w_up, ffn_w_down, mix_norm, w_out,
              mem_norm, mem_w_kv, mem_qn, mem_kn, ret_w_in, dsa_w_in, dsa_qn, dsa_kn,
              nsa_w_in, nsa_qn, nsa_kn, nsa_cmp_pos_k, nsa_cmp_pos_v, nsa_cmp_wk, nsa_cmp_wv):
    for i in range(DEPTH):
        h = rms_norm(x, ffn_norm[i, 0])
        x = x + 0.5 * swiglu(h, ffn_w_gate[i, 0], ffn_w_up[i, 0], ffn_w_down[i, 0])
        h = rms_norm(x, mix_norm[i])
        kind, j = i % N_MIXERS, i // N_MIXERS
        if kind == 0:
            y_mix, q_mem = retention_mixer(h, ret_w_in[j])
        elif kind == 1:
            y_mix, q_mem = dsa_mixer(h, dsa_w_in[j], dsa_qn[j], dsa_kn[j])
        else:
            y_mix, q_mem = nsa_mixer(h, nsa_w_in[j], nsa_qn[j], nsa_kn[j], nsa_cmp_pos_k[j],
                                     nsa_cmp_pos_v[j], nsa_cmp_wk[j], nsa_cmp_wv[j])
        y_mem = memory_attention(q_mem, mem, mem_norm[i], mem_w_kv[i], mem_qn[i], mem_kn[i])
        x = x + jnp.concatenate([y_mix, y_mem], axis=-1) @ w_out[i]
        h = rms_norm(x, ffn_norm[i, 1])
        x = x + 0.5 * swiglu(h, ffn_w_gate[i, 1], ffn_w_up[i, 1], ffn_w_down[i, 1])
    return x
```

```python
import functools
import math

import jax
import jax.numpy as jnp
import numpy as np
from jax import lax
from jax.experimental import pallas as pl
from jax.experimental.pallas import tpu as pltpu

D_MODEL = 1024
HD = 64
H_MIX = 12
H_MEM = 4
MIX_W = H_MIX * HD
MEM_W = H_MEM * HD
D_FF = 2816
ROPE_THETA = 10000.0
EPS = 1e-6
NEG = -1e30
FORCE = 1e9

RET_H, RET_DK, RET_DV, RET_CHUNK = 6, 64, 128, 128
DSA_H, IDX_H, IDX_D, DSA_TOPK_MAX, Q_BLK = 12, 8, 64, 256, 128
NSA_H, NSA_G, CMP_L, CMP_S, SLC_L, SLC_N_MAX, WIN, SLC_Q_BLK = 12, 4, 32, 16, 64, 16, 512, 16
N_MIXERS = 3

RET_SIZES = [RET_H * RET_DK, RET_H * RET_DK, RET_H * RET_DV, RET_H * RET_DV, MEM_W]
DSA_SIZES = [DSA_H * HD, HD, HD, IDX_H * IDX_D, IDX_D, IDX_H, MEM_W]
NSA_SIZES = [NSA_H * HD] + [NSA_G * HD] * 6 + [NSA_H * 3, MEM_W]

LANES = 128
MXU_N = 256
VMEM_LIMIT_BYTES = 56 * 1024 * 1024

BF16 = jnp.bfloat16
F32 = jnp.float32


def _cparams(*sem):
    return pltpu.CompilerParams(dimension_semantics=sem, vmem_limit_bytes=VMEM_LIMIT_BYTES)


def _rms(x, g):
    return x * lax.rsqrt(jnp.mean(x * x, axis=-1, keepdims=True) + EPS) * g


FFN_TM = 512
FFN_TF = MXU_N


def _ffn_body(x_ref, g_ref, wg_ref, wu_ref, wd_ref, o_ref):
    x = x_ref[...]
    h = _rms(x, g_ref[...]).astype(BF16)
    acc = jnp.zeros(x.shape, F32)
    for c in range(D_FF // FFN_TF):
        sl = slice(c * FFN_TF, (c + 1) * FFN_TF)
        gate = jnp.dot(h, wg_ref[:, sl], preferred_element_type=F32)
        up = jnp.dot(h, wu_ref[:, sl], preferred_element_type=F32)
        act = (gate * jax.nn.sigmoid(gate) * up).astype(BF16)
        acc = acc + jnp.dot(act, wd_ref[sl, :], preferred_element_type=F32)
    o_ref[...] = x + 0.5 * acc


def _resident(shape):
    return pl.BlockSpec(shape, lambda i: (0,) * len(shape), pipeline_mode=pl.Buffered(1))


def _ffn(x, g, wg, wu, wd):
    t, d = x.shape
    return pl.pallas_call(
        _ffn_body,
        out_shape=jax.ShapeDtypeStruct((t, d), F32),
        grid=(t // FFN_TM,),
        in_specs=[
            pl.BlockSpec((FFN_TM, d), lambda i: (i, 0)),
            _resident((1, d)),
            _resident((d, D_FF)),
            _resident((d, D_FF)),
            _resident((D_FF, d)),
        ],
        out_specs=pl.BlockSpec((FFN_TM, d), lambda i: (i, 0)),
        compiler_params=_cparams("parallel"),
        name="ffn",
    )(x, g.reshape(1, d), wg.astype(BF16), wu.astype(BF16), wd.astype(BF16))


PROJ_TM = 512


def _norm_proj_body(x_ref, g_ref, w_ref, o_ref):
    h = _rms(x_ref[...], g_ref[...]).astype(BF16)
    o_ref[...] = jnp.dot(h, w_ref[...], preferred_element_type=F32).astype(o_ref.dtype)


def _norm_proj(x, g, w, out_dtype=F32):
    t, d = x.shape
    n = w.shape[1]
    npad = -n % LANES
    if npad:
        w = jnp.pad(w, ((0, 0), (0, npad)))
    tm = min(PROJ_TM, t)
    out = pl.pallas_call(
        _norm_proj_body,
        out_shape=jax.ShapeDtypeStruct((t, n + npad), out_dtype),
        grid=(t // tm,),
        in_specs=[
            pl.BlockSpec((tm, d), lambda i: (i, 0)),
            _resident((1, d)),
            _resident((d, n + npad)),
        ],
        out_specs=pl.BlockSpec((tm, n + npad), lambda i: (i, 0)),
        compiler_params=_cparams("parallel"),
        name="norm_proj",
    )(x, g.reshape(1, d), w.astype(BF16))
    return out[:, :n] if npad else out


def _out_proj_body(y_ref, x_ref, w_ref, o_ref):
    o_ref[...] = x_ref[...] + jnp.dot(y_ref[...].astype(BF16), w_ref[...], preferred_element_type=F32)


def _out_proj(y, x, w):
    t, d = x.shape
    k = y.shape[1]
    return pl.pallas_call(
        _out_proj_body,
        out_shape=jax.ShapeDtypeStruct((t, d), F32),
        grid=(t // PROJ_TM,),
        in_specs=[
            pl.BlockSpec((PROJ_TM, k), lambda i: (i, 0)),
            pl.BlockSpec((PROJ_TM, d), lambda i: (i, 0)),
            _resident((k, d)),
        ],
        out_specs=pl.BlockSpec((PROJ_TM, d), lambda i: (i, 0)),
        compiler_params=_cparams("parallel"),
        name="out_proj",
    )(y, x, w.astype(BF16))


def split_cols(a, sizes):
    return jnp.split(a, np.cumsum(sizes)[:-1].tolist(), axis=-1)


def rms_norm(x, g):
    xf = x.astype(jnp.float32)
    y = xf * lax.rsqrt(jnp.mean(xf * xf, axis=-1, keepdims=True) + EPS)
    return (y * g.astype(jnp.float32)).astype(x.dtype)


def rope(x, pos):
    half = x.shape[-1] // 2
    inv = ROPE_THETA ** (-jnp.arange(half, dtype=jnp.float32) / half)
    ang = pos.astype(jnp.float32)[:, None] * inv[None, :]
    cos = jnp.cos(ang)[:, None, :].astype(x.dtype)
    sin = jnp.sin(ang)[:, None, :].astype(x.dtype)
    x1, x2 = x[..., :half], x[..., half:]
    return jnp.concatenate([x1 * cos - x2 * sin, x2 * cos + x1 * sin], axis=-1)


def masked_softmax(logits, valid):
    return jax.nn.softmax(jnp.where(valid, logits.astype(jnp.float32), NEG), axis=-1)


def chunkwise_retention(q, k, v):
    B, L, H, DK = q.shape
    DV = v.shape[-1]
    C = RET_CHUNK
    nc = L // C
    log_g = jnp.log(1.0 - 2.0 ** (-5.0 - jnp.arange(H, dtype=jnp.float32)))
    qc = q.reshape(B, nc, C, H, DK)
    kc = k.reshape(B, nc, C, H, DK)
    vc = v.reshape(B, nc, C, H, DV)
    idx = jnp.arange(C, dtype=jnp.float32)
    diff = idx[:, None] - idx[None, :]
    dmask = jnp.where(diff >= 0, jnp.exp(log_g[:, None, None] * jnp.maximum(diff, 0.0)), 0.0)
    inner = jnp.einsum('bnihd,bnjhd->bnhij', qc, kc) * dmask
    o_inner = jnp.einsum('bnhij,bnjhe->bnihe', inner, vc)
    zeta = jnp.exp(log_g[:, None] * (C - 1.0 - idx)[None, :])
    kv = jnp.einsum('bnjhd,hj,bnjhe->bnhde', kc, zeta, vc)
    chunk_decay = jnp.exp(log_g * C)[:, None, None]

    def step(state, kv_n):
        return state * chunk_decay + kv_n, state

    state0 = jnp.zeros((B, H, DK, DV), kv.dtype)
    _, prev = lax.scan(step, state0, jnp.moveaxis(kv, 1, 0))
    prev = jnp.moveaxis(prev, 0, 1)
    xi = jnp.exp(log_g[:, None] * (idx + 1.0)[None, :])
    o_cross = jnp.einsum('bnihd,bnhde->bnihe', qc, prev) * xi.T[None, None, :, :, None]
    return (o_inner + o_cross).reshape(B, L, H, DV).astype(jnp.float32)


def retention_mixer(proj, B, L):
    pos = jnp.arange(L)
    q, k, v, g, q_mem = split_cols(proj, RET_SIZES)
    q = rope(q.reshape(B, L, RET_H, RET_DK), pos)
    k = rope(k.reshape(B, L, RET_H, RET_DK), pos) * (RET_DK ** -0.5)
    v = v.reshape(B, L, RET_H, RET_DV)
    o = chunkwise_retention(q, k, v)
    mu = jnp.mean(o, axis=-1, keepdims=True)
    var = jnp.mean(jnp.square(o - mu), axis=-1, keepdims=True)
    o = ((o - mu) * lax.rsqrt(var + EPS))
    y = jax.nn.silu(g) * o.reshape(B, L, RET_H * RET_DV)
    return y, q_mem


def dsa_mixer(proj, B, L, qn, kn):
    pos = jnp.arange(L)
    q, k, v, iq, ik, iw, q_mem = split_cols(proj, DSA_SIZES)
    q = rope(rms_norm(q.reshape(B, L, DSA_H, HD), qn), pos)
    k = rope(rms_norm(k.reshape(B, L, 1, HD), kn), pos)[:, :, 0]
    iq = rope(iq.reshape(B, L, IDX_H, IDX_D), pos)
    ik = rope(ik.reshape(B, L, 1, IDX_D), pos)[:, :, 0]
    topk = min(DSA_TOPK_MAX, L // 4)
    scale = HD ** -0.5
    gather = jax.vmap(lambda t, s: t[s])

    def block(i):
        qs = i * Q_BLK
        qpos = qs + jnp.arange(Q_BLK)
        q_b = lax.dynamic_slice_in_dim(q, qs, Q_BLK, axis=1)
        iq_b = lax.dynamic_slice_in_dim(iq, qs, Q_BLK, axis=1)
        iw_b = lax.dynamic_slice_in_dim(iw, qs, Q_BLK, axis=1)
        rel = jax.nn.relu(jnp.einsum('bqhd,bsd->bqhs', iq_b, ik).astype(jnp.float32))
        score = jnp.einsum('bqh,bqhs->bqs', iw_b.astype(jnp.float32), rel)
        causal = pos[None, :] <= qpos[:, None]
        score = jnp.where(causal[None], score, -jnp.inf)
        _, sel = lax.top_k(score, topk)
        k_sel = gather(k, sel)
        v_sel = gather(v, sel)
        valid = (sel <= qpos[None, :, None])[:, :, None, :]
        lg = jnp.einsum('bqhd,bqkd->bqhk', q_b, k_sel) * scale
        p = masked_softmax(lg, valid)
        return jnp.einsum('bqhk,bqkd->bqhd', p.astype(v_sel.dtype), v_sel)

    o = lax.map(block, jnp.arange(L // Q_BLK))
    o = jnp.moveaxis(o, 0, 1).reshape(B, L, DSA_H * HD)
    return o, q_mem


def nsa_mixer(proj, B, L, qn, kn, cmp_pos_k, cmp_pos_v, cmp_wk, cmp_wv):
    R = NSA_H // NSA_G
    pos = jnp.arange(L)
    scale = HD ** -0.5
    q, kc, vc, ks, vs, kw, vw, gates, q_mem = split_cols(proj, NSA_SIZES)
    q = rope(rms_norm(q.reshape(B, L, NSA_H, HD), qn), pos).reshape(B, L, NSA_G, R, HD)
    grp = lambda t: t.reshape(B, L, NSA_G, HD)
    kc, vc, vs, vw = grp(kc), grp(vc), grp(vs), grp(vw)
    ks = rope(rms_norm(grp(ks), kn[1]), pos)
    kw = rope(rms_norm(grp(kw), kn[2]), pos)
    n_cmp = (L - CMP_L) // CMP_S + 1
    starts = jnp.arange(n_cmp) * CMP_S
    ends = starts + CMP_L - 1
    blk = starts[:, None] + jnp.arange(CMP_L)[None, :]
    k_blk = kc[:, blk] + cmp_pos_k[None, None, :, None, :]
    v_blk = vc[:, blk] + cmp_pos_v[None, None, :, None, :]
    k_cmp = jnp.einsum('bnlgd,lde->bnge', k_blk, cmp_wk)
    v_cmp = jnp.einsum('bnlgd,lde->bnge', v_blk, cmp_wv)
    k_cmp = rope(rms_norm(k_cmp, kn[0]), ends)
    valid_c = ends[None, :] <= pos[:, None]
    lg = jnp.einsum('bqgrd,bngd->bgrqn', q, k_cmp) * scale
    p_cmp = masked_softmax(lg, valid_c) * jnp.any(valid_c, axis=-1)[:, None]
    o_cmp = jnp.einsum('bgrqn,bngd->bqgrd', p_cmp.astype(v_cmp.dtype), v_cmp)
    n_slc = L // SLC_L
    n_sel = min(SLC_N_MAX, n_slc)
    sstart = jnp.arange(n_slc) * SLC_L
    cover = ((starts[:, None] < sstart[None, :] + SLC_L) &
             (starts[:, None] + CMP_L > sstart[None, :])).astype(jnp.float32)
    imp = jnp.einsum('bgrqn,nj->bgqj', p_cmp, cover)
    jj = jnp.arange(n_slc)[None, :]
    qblk = (pos // SLC_L)[:, None]
    forced = (jj == 0) | (jj == qblk) | (jj == qblk - 1)
    causal_blk = sstart[None, :] <= pos[:, None]
    imp = jnp.where(causal_blk, imp + forced.astype(jnp.float32) * FORCE, NEG)
    _, sel = lax.top_k(imp, n_sel)
    ks_blk = ks.reshape(B, n_slc, SLC_L, NSA_G, HD).transpose(0, 3, 1, 2, 4)
    vs_blk = vs.reshape(B, n_slc, SLC_L, NSA_G, HD).transpose(0, 3, 1, 2, 4)
    gather = jax.vmap(jax.vmap(lambda t, s: t[s]))

    def slc_block(i):
        qs = i * SLC_Q_BLK
        qpos = qs + jnp.arange(SLC_Q_BLK)
        q_c = lax.dynamic_slice_in_dim(q, qs, SLC_Q_BLK, axis=1)
        s_c = lax.dynamic_slice_in_dim(sel, qs, SLC_Q_BLK, axis=2)
        k_g = gather(ks_blk, s_c)
        v_g = gather(vs_blk, s_c)
        kpos = s_c[..., None] * SLC_L + jnp.arange(SLC_L)
        valid = (kpos <= qpos[None, None, :, None, None])[:, :, None]
        lg_s = jnp.einsum('bqgrd,bgqnld->bgrqnl', q_c, k_g) * scale
        lg_s = jnp.where(valid, lg_s.astype(jnp.float32), NEG)
        p = jax.nn.softmax(lg_s.reshape(B, NSA_G, R, SLC_Q_BLK, n_sel * SLC_L), axis=-1).reshape(lg_s.shape)
        return jnp.einsum('bgrqnl,bgqnld->bqgrd', p.astype(v_g.dtype), v_g)

    o_slc = lax.map(slc_block, jnp.arange(L // SLC_Q_BLK))
    o_slc = jnp.moveaxis(o_slc, 0, 1).reshape(B, L, NSA_G, R, HD)
    pad = ((0, 0), (WIN, 0), (0, 0), (0, 0))
    kw_p, vw_p = jnp.pad(kw, pad), jnp.pad(vw, pad)

    def win_block(i):
        qs = i * Q_BLK
        qpos = qs + jnp.arange(Q_BLK)
        kpos = qs - WIN + jnp.arange(WIN + Q_BLK)
        q_b = lax.dynamic_slice_in_dim(q, qs, Q_BLK, axis=1)
        k_b = lax.dynamic_slice_in_dim(kw_p, qs, WIN + Q_BLK, axis=1)
        v_b = lax.dynamic_slice_in_dim(vw_p, qs, WIN + Q_BLK, axis=1)
        d = qpos[:, None] - kpos[None, :]
        valid = (d >= 0) & (d < WIN) & (kpos[None, :] >= 0)
        lg_w = jnp.einsum('bqgrd,bkgd->bgrqk', q_b, k_b) * scale
        p = masked_softmax(lg_w, valid)
        return jnp.einsum('bgrqk,bkgd->bqgrd', p.astype(v_b.dtype), v_b)

    o_win = lax.map(win_block, jnp.arange(L // Q_BLK))
    o_win = jnp.moveaxis(o_win, 0, 1).reshape(B, L, NSA_G, R, HD)
    g = jax.nn.sigmoid(gates.reshape(B, L, NSA_G, R, 3))
    o = g[..., 0:1] * o_cmp + g[..., 1:2] * o_slc + g[..., 2:3] * o_win
    return o.reshape(B, L, NSA_H * HD), q_mem


def memory_attention(q_raw, kv, B, L, qn, kn):
    M = kv.shape[0] // B
    k, v = split_cols(kv.reshape(B, M, 2 * MEM_W), [MEM_W, MEM_W])
    k = rms_norm(k.reshape(B, M, H_MEM, HD), kn)
    v = v.reshape(B, M, H_MEM, HD)
    q = rms_norm(q_raw.reshape(B, L, H_MEM, HD), qn)
    lg = jnp.einsum('bqhd,bmhd->bhqm', q, k).astype(jnp.float32) * (HD ** -0.5)
    p = jax.nn.softmax(lg, axis=-1)
    return jnp.einsum('bhqm,bmhd->bqhd', p.astype(v.dtype), v).reshape(B, L, MEM_W)


def kernel(x, mem, ffn_norm, ffn_w_gate, ffn_w_up, ffn_w_down, mix_norm, w_out, mem_norm, mem_w_kv, mem_qn, mem_kn, ret_w_in, dsa_w_in, dsa_qn, dsa_kn, nsa_w_in, nsa_qn, nsa_kn, nsa_cmp_pos_k, nsa_cmp_pos_v, nsa_cmp_wk, nsa_cmp_wv):
    B, L, D = x.shape
    T = B * L
    x = x.reshape(T, D)
    mem2 = mem.reshape(-1, D)
    for i in range(ffn_norm.shape[0]):
        x = _ffn(x, ffn_norm[i, 0], ffn_w_gate[i, 0], ffn_w_up[i, 0], ffn_w_down[i, 0])
        kind, j = i % N_MIXERS, i // N_MIXERS
        if kind == 0:
            proj = _norm_proj(x, mix_norm[i], ret_w_in[j])
            y_mix, q_mem = retention_mixer(proj.reshape(B, L, -1), B, L)
        elif kind == 1:
            proj = _norm_proj(x, mix_norm[i], dsa_w_in[j])
            y_mix, q_mem = dsa_mixer(proj.reshape(B, L, -1), B, L, dsa_qn[j], dsa_kn[j])
        else:
            proj = _norm_proj(x, mix_norm[i], nsa_w_in[j])
            y_mix, q_mem = nsa_mixer(proj.reshape(B, L, -1), B, L, nsa_qn[j], nsa_kn[j], nsa_cmp_pos_k[j],
                                     nsa_cmp_pos_v[j], nsa_cmp_wk[j], nsa_cmp_wv[j])
        kv = _norm_proj(mem2, mem_norm[i], mem_w_kv[i])
        y_mem = memory_attention(q_mem, kv, B, L, mem_qn[i], mem_kn[i])
        y = jnp.concatenate([y_mix, y_mem], axis=-1).reshape(T, MIX_W + MEM_W)
        x = _out_proj(y, x, w_out[i])
        x = _ffn(x, ffn_norm[i, 1], ffn_w_gate[i, 1], ffn_w_up[i, 1], ffn_w_down[i, 1])
    return x.reshape(B, L, D)
```

```python
import functools
import math

import jax
import jax.numpy as jnp
import numpy as np
from jax import lax
from jax.experimental import pallas as pl
from jax.experimental.pallas import tpu as pltpu

D_MODEL = 1024
HD = 64
HALF = HD // 2
H_MIX = 12
H_MEM = 4
MIX_W = H_MIX * HD
MEM_W = H_MEM * HD
D_FF = 2816
ROPE_THETA = 10000.0
EPS = 1e-6
NEG = -1e30
FORCE = 1e9
SCALE = HD ** -0.5

RET_H, RET_DK, RET_DV, RET_CHUNK = 6, 64, 128, 128
DSA_H, IDX_H, IDX_D, DSA_TOPK_MAX = 12, 8, 64, 256
NSA_H, NSA_G, CMP_L, CMP_S, SLC_L, SLC_N_MAX, WIN = 12, 4, 32, 16, 64, 16, 512
NSA_R = NSA_H // NSA_G
N_MIXERS = 3

RET_SIZES = [RET_H * RET_DK, RET_H * RET_DK, RET_H * RET_DV, RET_H * RET_DV, MEM_W]
DSA_SIZES = [DSA_H * HD, HD, HD, IDX_H * IDX_D, IDX_D, IDX_H, MEM_W]
NSA_SIZES = [NSA_H * HD] + [NSA_G * HD] * 6 + [NSA_H * 3, MEM_W]

LANES = 128
SUBLANES = 8
MXU_N = 256
VMEM_LIMIT_BYTES = 56 * 1024 * 1024

BF16 = jnp.bfloat16
F32 = jnp.float32
I32 = jnp.int32


def _cparams(*sem):
    return pltpu.CompilerParams(dimension_semantics=sem, vmem_limit_bytes=VMEM_LIMIT_BYTES)


def _const_spec(shape):
    n = len(shape)
    return pl.BlockSpec(shape, lambda *_: (0,) * n, pipeline_mode=pl.Buffered(1))


def _rms(x, g):
    return x * lax.rsqrt(jnp.mean(x * x, axis=-1, keepdims=True) + EPS) * g


def _dot(a, b):
    return jnp.dot(a, b, preferred_element_type=F32)


def _dot_nt(a, b):
    return lax.dot_general(a, b, (((1,), (1,)), ((), ())), preferred_element_type=F32)


def _tile_lanes(a, n):
    reps = n // a.shape[-1]
    return a if reps == 1 else jnp.concatenate([a] * reps, axis=-1)


def _rope_tables(pos):
    inv = ROPE_THETA ** (-jnp.arange(HALF, dtype=F32) / HALF)
    ang = pos.astype(F32)[:, None] * inv[None, :]
    return jnp.cos(ang).T, jnp.sin(ang).T


def _lane_gain(g):
    return jnp.broadcast_to(g.astype(F32)[:, None], (g.shape[0], LANES))


def _heads_norm_rope(x, gain, cos, sin):
    t = x.shape[-1]
    x3 = x.reshape(x.shape[0] // HD, HD, t)
    if gain is not None:
        ms = jnp.sum(x3 * x3, axis=1, keepdims=True) * (1.0 / HD)
        x3 = x3 * lax.rsqrt(ms + EPS) * gain[None]
    if cos is not None:
        x1, x2 = x3[:, :HALF], x3[:, HALF:]
        x3 = jnp.concatenate([x1 * cos[None] - x2 * sin[None], x2 * cos[None] + x1 * sin[None]], axis=1)
    return x3.reshape(x.shape)


FFN_TM = 512
FFN_TF = MXU_N


def _ffn_body(x_ref, g_ref, wg_ref, wu_ref, wd_ref, o_ref):
    x = x_ref[...]
    h = _rms(x, g_ref[...]).astype(BF16)
    acc = jnp.zeros(x.shape, F32)
    for c in range(D_FF // FFN_TF):
        sl = slice(c * FFN_TF, (c + 1) * FFN_TF)
        gate = _dot(h, wg_ref[:, sl])
        up = _dot(h, wu_ref[:, sl])
        act = (gate * jax.nn.sigmoid(gate) * up).astype(BF16)
        acc = acc + _dot(act, wd_ref[sl, :])
    o_ref[...] = x + 0.5 * acc


def _ffn(x, g, wg, wu, wd):
    t, d = x.shape
    tm = min(FFN_TM, t)
    return pl.pallas_call(
        _ffn_body,
        out_shape=jax.ShapeDtypeStruct((t, d), F32),
        grid=(t // tm,),
        in_specs=[
            pl.BlockSpec((tm, d), lambda i: (i, 0)),
            _const_spec((1, d)),
            _const_spec((d, D_FF)),
            _const_spec((d, D_FF)),
            _const_spec((D_FF, d)),
        ],
        out_specs=pl.BlockSpec((tm, d), lambda i: (i, 0)),
        compiler_params=_cparams("parallel"),
        name="ffn",
    )(x, g.reshape(1, d), wg.astype(BF16), wu.astype(BF16), wd.astype(BF16))


PROJ_TM = 512


def _out_proj_body(ymix_ref, ymem_ref, x_ref, wmix_ref, wmem_ref, o_ref):
    o_ref[...] = x_ref[...] + _dot(ymix_ref[...], wmix_ref[...]) + _dot(ymem_ref[...], wmem_ref[...])


def _out_proj(y_mix, y_mem, x, w):
    t, d = x.shape
    tm = min(PROJ_TM, t)
    return pl.pallas_call(
        _out_proj_body,
        out_shape=jax.ShapeDtypeStruct((t, d), F32),
        grid=(t // tm,),
        in_specs=[
            pl.BlockSpec((tm, MIX_W), lambda i: (i, 0)),
            pl.BlockSpec((tm, MEM_W), lambda i: (i, 0)),
            pl.BlockSpec((tm, d), lambda i: (i, 0)),
            _const_spec((MIX_W, d)),
            _const_spec((MEM_W, d)),
        ],
        out_specs=pl.BlockSpec((tm, d), lambda i: (i, 0)),
        compiler_params=_cparams("parallel"),
        name="out_proj",
    )(y_mix, y_mem, x, w[:MIX_W].astype(BF16), w[MIX_W:].astype(BF16))


def _mem_kv_body(m_ref, g_ref, wk_ref, wvT_ref, kn_ref, k_ref, vT_ref):
    h = _rms(m_ref[0], g_ref[...]).astype(BF16)
    kT = _dot_nt(wk_ref[...], h)
    kT = _heads_norm_rope(kT, _tile_lanes(kn_ref[...], kT.shape[-1]), None, None)
    k_ref[0] = kT.T.astype(BF16)
    vT_ref[0] = _dot_nt(wvT_ref[...], h).astype(BF16)


def _mem_kv(mem, g, w_kv, kn):
    b, m, d = mem.shape
    return pl.pallas_call(
        _mem_kv_body,
        out_shape=(jax.ShapeDtypeStruct((b, m, MEM_W), BF16), jax.ShapeDtypeStruct((b, MEM_W, m), BF16)),
        grid=(b,),
        in_specs=[
            pl.BlockSpec((1, m, d), lambda i: (i, 0, 0)),
            _const_spec((1, d)),
            _const_spec((MEM_W, d)),
            _const_spec((MEM_W, d)),
            _const_spec((HD, LANES)),
        ],
        out_specs=(pl.BlockSpec((1, m, MEM_W), lambda i: (i, 0, 0)), pl.BlockSpec((1, MEM_W, m), lambda i: (i, 0, 0))),
        compiler_params=_cparams("parallel"),
        name="mem_kv",
    )(mem, g.reshape(1, d), w_kv[:, :MEM_W].T.astype(BF16), w_kv[:, MEM_W:].T.astype(BF16), _lane_gain(kn))


MEM_TQ = 512


def _pad_rows(blk, half, total=LANES):
    z = jnp.zeros_like(blk)
    parts = [z] * (total // HD)
    parts[half] = blk
    return jnp.concatenate(parts, axis=0)


def _mem_attn_body(qT_ref, k_ref, vT_ref, o_ref):
    qT = qT_ref[...]
    k = k_ref[0]
    vT = vT_ref[0]
    outs = []
    for h in range(H_MEM):
        pair = k[:, (h // 2) * LANES:(h // 2 + 1) * LANES]
        s = _dot(pair, _pad_rows(qT[h * HD:(h + 1) * HD], h % 2))
        m = jnp.max(s, axis=0, keepdims=True)
        p = jnp.exp(s - m)
        l = jnp.sum(p, axis=0, keepdims=True)
        o = _dot(vT[h * HD:(h + 1) * HD], p.astype(BF16))
        outs.append(o / l)
    o_ref[...] = jnp.concatenate(outs, axis=0).T.astype(BF16)


def _mem_attn(qmT, k, vT, seq):
    t = qmT.shape[1]
    b, m, _ = k.shape
    tq = min(MEM_TQ, seq)
    nq = seq // tq
    return pl.pallas_call(
        _mem_attn_body,
        out_shape=jax.ShapeDtypeStruct((t, MEM_W), BF16),
        grid=(b, nq),
        in_specs=[
            pl.BlockSpec((MEM_W, tq), lambda bi, i: (0, bi * nq + i)),
            pl.BlockSpec((1, m, MEM_W), lambda bi, i: (bi, 0, 0)),
            pl.BlockSpec((1, MEM_W, m), lambda bi, i: (bi, 0, 0)),
        ],
        out_specs=pl.BlockSpec((tq, MEM_W), lambda bi, i: (bi * nq + i, 0)),
        compiler_params=_cparams("parallel", "parallel"),
        name="mem_attn",
    )(qmT, k, vT)


def _ret_proj_body(x_ref, g_ref, wtok_ref, wfeat_ref, cosT_ref, sinT_ref, cosF_ref, sinF_ref, mqn_ref,
                   q_ref, v_ref, gate_ref, kT_ref, qmT_ref):
    h = _rms(x_ref[...], g_ref[...]).astype(BF16)
    tok = _dot(h, wtok_ref[...])
    nq, nv = RET_H * RET_DK, RET_H * RET_DV
    q = tok[:, :nq]
    lane = lax.broadcasted_iota(I32, q.shape, 1)
    rot = jnp.where(lane % HD < HALF, pltpu.roll(q, nq - HALF, axis=1), pltpu.roll(q, HALF, axis=1))
    q_ref[...] = (q * _tile_lanes(cosT_ref[...], nq) + rot * _tile_lanes(sinT_ref[...], nq)).astype(BF16)
    v_ref[...] = tok[:, nq:nq + nv].astype(BF16)
    gate_ref[...] = tok[:, nq + nv:]
    feat = _dot_nt(wfeat_ref[...], h)
    tm = feat.shape[-1]
    cos, sin = cosF_ref[...], sinF_ref[...]
    kT = _heads_norm_rope(feat[:nq], None, cos, sin) * (RET_DK ** -0.5)
    kT_ref[...] = kT.astype(BF16)
    qm = _heads_norm_rope(feat[nq:], _tile_lanes(mqn_ref[...], tm), None, None) * SCALE
    qmT_ref[...] = qm.astype(BF16)


def _token_rope_tables(seq):
    cosF, sinF = _rope_tables(jnp.arange(seq))
    cos = jnp.tile(cosF.T, (1, LANES // HALF))
    sign = jnp.where((jnp.arange(LANES) % HD) < HALF, -1.0, 1.0).astype(F32)
    sin = jnp.tile(sinF.T, (1, LANES // HALF)) * sign[None, :]
    return cos, sin


def _ret_proj(x, g, w_in, mem_qn, seq):
    t, d = x.shape
    tm = min(PROJ_TM, seq)
    npos = seq // tm
    wq, wk, wv, wg, wqm = jnp.split(w_in, np.cumsum(RET_SIZES)[:-1].tolist(), axis=1)
    wtok = jnp.concatenate([wq, wv, wg], axis=1).astype(BF16)
    wfeat = jnp.concatenate([wk, wqm], axis=1).T.astype(BF16)
    cosT, sinT = _token_rope_tables(seq)
    cosF, sinF = _rope_tables(jnp.arange(seq))
    nq, nv = RET_H * RET_DK, RET_H * RET_DV
    row = lambda n: pl.BlockSpec((tm, n), lambda i: (i, 0))
    col = lambda n: pl.BlockSpec((n, tm), lambda i: (0, i))
    return pl.pallas_call(
        _ret_proj_body,
        out_shape=(
            jax.ShapeDtypeStruct((t, nq), BF16), jax.ShapeDtypeStruct((t, nv), BF16),
            jax.ShapeDtypeStruct((t, nv), F32), jax.ShapeDtypeStruct((nq, t), BF16),
            jax.ShapeDtypeStruct((MEM_W, t), BF16),
        ),
        grid=(t // tm,),
        in_specs=[
            row(d), _const_spec((1, d)), _const_spec(wtok.shape), _const_spec(wfeat.shape),
            pl.BlockSpec((tm, LANES), lambda i: (i % npos, 0)), pl.BlockSpec((tm, LANES), lambda i: (i % npos, 0)),
            pl.BlockSpec((HALF, tm), lambda i: (0, i % npos)), pl.BlockSpec((HALF, tm), lambda i: (0, i % npos)),
            _const_spec((HD, LANES)),
        ],
        out_specs=(row(nq), row(nv), row(nv), col(nq), col(MEM_W)),
        compiler_params=_cparams("parallel"),
        name="ret_proj",
    )(x, g.reshape(1, d), wtok, wfeat, cosT, sinT, cosF, sinF, _lane_gain(mem_qn))


RET_TL = 512


def _ret_body(q_ref, kT_ref, v_ref, gate_ref, y_ref, state_ref):
    @pl.when(pl.program_id(1) == 0)
    def _():
        state_ref[...] = jnp.zeros_like(state_ref)

    c = RET_CHUNK
    ii = lax.broadcasted_iota(I32, (c, c), 0).astype(F32)
    jj = lax.broadcasted_iota(I32, (c, c), 1).astype(F32)
    diff = ii - jj
    jk = lax.broadcasted_iota(I32, (RET_DK, c), 1).astype(F32)
    for hh in range(RET_H):
        log_g = math.log(1.0 - 2.0 ** (-5.0 - hh))
        dmask = jnp.where(diff >= 0, jnp.exp(log_g * jnp.maximum(diff, 0.0)), 0.0)
        xi = jnp.exp(log_g * (ii + 1.0))
        zeta = jnp.exp(log_g * (c - 1.0 - jk))
        decay = math.exp(log_g * c)
        pair = slice((hh // 2) * LANES, (hh // 2 + 1) * LANES)
        for n in range(q_ref.shape[0] // c):
            rows = slice(n * c, (n + 1) * c)
            q2 = q_ref[rows, pair]
            kT = kT_ref[hh * RET_DK:(hh + 1) * RET_DK, rows]
            v = v_ref[rows, hh * RET_DV:(hh + 1) * RET_DV]
            state = state_ref[hh]
            inner = _dot(q2, _pad_rows(kT, hh % 2)) * dmask
            o = _dot(inner.astype(BF16), v)
            o = o + _dot(q2, _pad_rows(state.astype(BF16), hh % 2)) * xi
            kv = _dot((kT.astype(F32) * zeta).astype(BF16), v)
            state_ref[hh] = state * decay + kv
            mu = jnp.mean(o, axis=-1, keepdims=True)
            var = jnp.mean(jnp.square(o - mu), axis=-1, keepdims=True)
            o = (o - mu) * lax.rsqrt(var + EPS)
            gte = gate_ref[rows, hh * RET_DV:(hh + 1) * RET_DV]
            y_ref[rows, hh * RET_DV:(hh + 1) * RET_DV] = (gte * jax.nn.sigmoid(gte) * o).astype(BF16)


def _retention(q, kT, v, gate, batch, seq):
    t = q.shape[0]
    tl = min(RET_TL, seq)
    nl = seq // tl
    nq, nv = RET_H * RET_DK, RET_H * RET_DV
    return pl.pallas_call(
        _ret_body,
        out_shape=jax.ShapeDtypeStruct((t, nv), BF16),
        grid=(batch, nl),
        in_specs=[
            pl.BlockSpec((tl, nq), lambda b, i: (b * nl + i, 0)),
            pl.BlockSpec((nq, tl), lambda b, i: (0, b * nl + i)),
            pl.BlockSpec((tl, nv), lambda b, i: (b * nl + i, 0)),
            pl.BlockSpec((tl, nv), lambda b, i: (b * nl + i, 0)),
        ],
        out_specs=pl.BlockSpec((tl, nv), lambda b, i: (b * nl + i, 0)),
        scratch_shapes=[pltpu.VMEM((RET_H, RET_DK, RET_DV), F32)],
        compiler_params=_cparams("parallel", "arbitrary"),
        name="retention",
    )(q, kT, v, gate)


def _dsa_proj_body(x_ref, g_ref, w_ref, cos_ref, sin_ref, qn_ref, kn_ref, mqn_ref,
                   kk_ref, qT_ref, iqT_ref, vT_ref, iwT_ref, qmT_ref):
    h = _rms(x_ref[...], g_ref[...]).astype(BF16)
    feat = _dot_nt(w_ref[...], h)
    tm = feat.shape[-1]
    cos, sin = cos_ref[...], sin_ref[...]
    o = 0
    q = feat[o:o + DSA_H * HD]; o += DSA_H * HD
    iq = feat[o:o + IDX_H * IDX_D]; o += IDX_H * IDX_D
    k = feat[o:o + HD]; o += HD
    ik = feat[o:o + IDX_D]; o += IDX_D
    v = feat[o:o + HD]; o += HD
    qm = feat[o:o + MEM_W]; o += MEM_W
    iw = feat[o:o + IDX_H]
    qT_ref[...] = (_heads_norm_rope(q, _tile_lanes(qn_ref[...], tm), cos, sin) * SCALE).astype(BF16)
    iqT_ref[...] = _heads_norm_rope(iq, None, cos, sin).astype(BF16)
    k = _heads_norm_rope(k, _tile_lanes(kn_ref[...], tm), cos, sin)
    ik = _heads_norm_rope(ik, None, cos, sin)
    kk_ref[...] = jnp.concatenate([k, ik], axis=0).T.astype(BF16)
    vT_ref[...] = v.astype(BF16)
    iwT_ref[...] = iw
    qmT_ref[...] = (_heads_norm_rope(qm, _tile_lanes(mqn_ref[...], tm), None, None) * SCALE).astype(BF16)


def _dsa_proj(x, g, w_in, qn, kn, mem_qn, seq):
    t, d = x.shape
    tm = min(PROJ_TM, seq)
    npos = seq // tm
    wq, wk, wv, wiq, wik, wiw, wqm = jnp.split(w_in, np.cumsum(DSA_SIZES)[:-1].tolist(), axis=1)
    wfeat = jnp.concatenate([wq, wiq, wk, wik, wv, wqm, wiw], axis=1).T.astype(BF16)
    cosF, sinF = _rope_tables(jnp.arange(seq))
    col = lambda n: pl.BlockSpec((n, tm), lambda i: (0, i))
    tab = pl.BlockSpec((HALF, tm), lambda i: (0, i % npos))
    gain = _const_spec((HD, LANES))
    return pl.pallas_call(
        _dsa_proj_body,
        out_shape=(
            jax.ShapeDtypeStruct((t, LANES), BF16), jax.ShapeDtypeStruct((DSA_H * HD, t), BF16),
            jax.ShapeDtypeStruct((IDX_H * IDX_D, t), BF16), jax.ShapeDtypeStruct((HD, t), BF16),
            jax.ShapeDtypeStruct((IDX_H, t), F32), jax.ShapeDtypeStruct((MEM_W, t), BF16),
        ),
        grid=(t // tm,),
        in_specs=[pl.BlockSpec((tm, d), lambda i: (i, 0)), _const_spec((1, d)), _const_spec(wfeat.shape),
                  tab, tab, gain, gain, gain],
        out_specs=(pl.BlockSpec((tm, LANES), lambda i: (i, 0)), col(DSA_H * HD), col(IDX_H * IDX_D), col(HD),
                   col(IDX_H), col(MEM_W)),
        compiler_params=_cparams("parallel"),
        name="dsa_proj",
    )(x, g.reshape(1, d), wfeat, cosF, sinF, _lane_gain(qn), _lane_gain(kn), _lane_gain(mem_qn))


DSA_TQ = 256
DSA_TK = 256
INT_MIN = -2 ** 31


def _dsa_attn_body(kk_ref, qT_ref, iqT_ref, vT_ref, iwT_ref, y_ref, keys_ref, bias_ref, *, topk, seq):
    tq, tk = DSA_TQ, DSA_TK
    qs = pl.program_id(1) * tq
    nkc = (qs + tq) // tk
    qpos = qs + lax.broadcasted_iota(I32, (1, tq), 1)
    row = lax.broadcasted_iota(I32, (tk, tq), 0)

    def score_chunk(c, carry):
        r0 = pl.multiple_of(c * tk, tk)
        kkc = kk_ref[pl.ds(r0, tk), :]
        acc = jnp.zeros((tk, tq), F32)
        for h in range(IDX_H):
            r = _dot(kkc, _pad_rows(iqT_ref[h * IDX_D:(h + 1) * IDX_D, :], 1))
            acc = acc + jnp.maximum(r, 0.0) * iwT_ref[h:h + 1, :]
        sc = jnp.where(r0 + row <= qpos, acc + 0.0, -jnp.inf)
        bits = pltpu.bitcast(sc, I32)
        keys_ref[pl.ds(r0, tk), :] = jnp.where(bits < 0, bits ^ jnp.int32(0x7FFFFFFF), bits)
        return carry

    lax.fori_loop(0, nkc, score_chunk, 0)

    def count(pred):
        def body(c, acc):
            r0 = pl.multiple_of(c * tk, tk)
            m = pred(keys_ref[pl.ds(r0, tk), :], r0 + row)
            return acc + jnp.sum(m.astype(I32).reshape(tk // SUBLANES, SUBLANES, tq), axis=0)
        acc = lax.fori_loop(0, nkc, body, jnp.zeros((SUBLANES, tq), I32))
        return jnp.sum(acc, axis=0, keepdims=True)

    cnt = count(lambda k, _: k >= 0)
    thr = jnp.where(cnt >= topk, jnp.int32(0), jnp.int32(INT_MIN))

    def bit_step(b, thr):
        cand = thr | lax.shift_left(jnp.int32(1), 30 - b)
        return jnp.where(count(lambda k, _: k >= cand) >= topk, cand, thr)

    thr = lax.fori_loop(0, 31, bit_step, thr)
    need = topk - count(lambda k, _: k > thr)

    def idx_step(b, q):
        cand = q | lax.shift_left(jnp.int32(1), (seq.bit_length() - 2) - b)
        return jnp.where(count(lambda k, idx: (k == thr) & (idx < cand)) < need, cand, q)

    last = lax.fori_loop(0, seq.bit_length() - 1, idx_step, jnp.zeros((1, tq), I32))

    def bias_chunk(c, carry):
        r0 = pl.multiple_of(c * tk, tk)
        k = keys_ref[pl.ds(r0, tk), :]
        idx = r0 + row
        sel = (k > thr) | ((k == thr) & (idx <= last))
        bias_ref[pl.ds(r0, tk), :] = jnp.where(sel & (idx <= qpos), 0.0, NEG)
        return carry

    lax.fori_loop(0, nkc, bias_chunk, 0)

    outs = []
    for h in range(DSA_H):
        qh = _pad_rows(qT_ref[h * HD:(h + 1) * HD, :], 0)

        def attn_chunk(c, carry, qh=qh):
            m, l, acc = carry
            r0 = pl.multiple_of(c * tk, tk)
            s = _dot(kk_ref[pl.ds(r0, tk), :], qh) + bias_ref[pl.ds(r0, tk), :]
            m_new = jnp.maximum(m, jnp.max(s, axis=0, keepdims=True))
            alpha = jnp.exp(m - m_new)
            p = jnp.exp(s - m_new)
            l = l * alpha + jnp.sum(p, axis=0, keepdims=True)
            acc = acc * alpha + _dot(vT_ref[:, pl.ds(r0, tk)], p.astype(BF16))
            return m_new, l, acc

        init = (jnp.full((1, tq), -jnp.inf, F32), jnp.zeros((1, tq), F32), jnp.zeros((HD, tq), F32))
        _, l, acc = lax.fori_loop(0, nkc, attn_chunk, init)
        outs.append(acc / l)
    y_ref[...] = jnp.concatenate(outs, axis=0).T.astype(BF16)


def _dsa_attn(kk, qT, iqT, vT, iwT, batch, seq):
    t = kk.shape[0]
    tq = DSA_TQ
    nq = seq // tq
    topk = min(DSA_TOPK_MAX, seq // 4)
    colq = lambda n: pl.BlockSpec((n, tq), lambda b, i: (0, b * nq + i))
    return pl.pallas_call(
        functools.partial(_dsa_attn_body, topk=topk, seq=seq),
        out_shape=jax.ShapeDtypeStruct((t, DSA_H * HD), BF16),
        grid=(batch, nq),
        in_specs=[
            pl.BlockSpec((seq, LANES), lambda b, i: (b, 0)),
            colq(DSA_H * HD), colq(IDX_H * IDX_D),
            pl.BlockSpec((HD, seq), lambda b, i: (0, b)),
            colq(IDX_H),
        ],
        out_specs=pl.BlockSpec((tq, DSA_H * HD), lambda b, i: (b * nq + i, 0)),
        scratch_shapes=[pltpu.VMEM((seq, tq), I32), pltpu.VMEM((seq, tq), F32)],
        compiler_params=_cparams("parallel", "arbitrary"),
        name="dsa_attn",
    )(kk, qT, iqT, vT, iwT)


NSA_KV = NSA_G * HD
NSA_GATES = NSA_H * 3
NSA_GATES_PAD = -NSA_GATES % SUBLANES


def _nsa_proj_body(x_ref, g_ref, wtok_ref, wfeat_ref, cos_ref, sin_ref, qn_ref, kns_ref, knw_ref, mqn_ref,
                   kc_ref, vc_ref, ks_ref, kw_ref, qT_ref, vsT_ref, vwT_ref, gT_ref, qmT_ref):
    h = _rms(x_ref[...], g_ref[...]).astype(BF16)
    tok = _dot(h, wtok_ref[...])
    kc_ref[...] = tok[:, :NSA_KV]
    vc_ref[...] = tok[:, NSA_KV:]
    feat = _dot_nt(wfeat_ref[...], h)
    tm = feat.shape[-1]
    cos, sin = cos_ref[...], sin_ref[...]
    o = 0
    q = feat[o:o + NSA_H * HD]; o += NSA_H * HD
    ks = feat[o:o + NSA_KV]; o += NSA_KV
    kw = feat[o:o + NSA_KV]; o += NSA_KV
    vs = feat[o:o + NSA_KV]; o += NSA_KV
    vw = feat[o:o + NSA_KV]; o += NSA_KV
    qm = feat[o:o + MEM_W]; o += MEM_W
    gates = feat[o:]
    qT_ref[...] = (_heads_norm_rope(q, _tile_lanes(qn_ref[...], tm), cos, sin) * SCALE).astype(BF16)
    ks_ref[...] = _heads_norm_rope(ks, _tile_lanes(kns_ref[...], tm), cos, sin).T.astype(BF16)
    kw_ref[...] = _heads_norm_rope(kw, _tile_lanes(knw_ref[...], tm), cos, sin).T.astype(BF16)
    vsT_ref[...] = vs.astype(BF16)
    vwT_ref[...] = vw.astype(BF16)
    gT_ref[...] = jax.nn.sigmoid(gates)
    qmT_ref[...] = (_heads_norm_rope(qm, _tile_lanes(mqn_ref[...], tm), None, None) * SCALE).astype(BF16)


def _nsa_proj(x, g, w_in, qn, kn, mem_qn, seq):
    t, d = x.shape
    tm = min(PROJ_TM, seq)
    npos = seq // tm
    wq, wkc, wvc, wks, wvs, wkw, wvw, wgt, wqm = jnp.split(w_in, np.cumsum(NSA_SIZES)[:-1].tolist(), axis=1)
    wtok = jnp.concatenate([wkc, wvc], axis=1).astype(BF16)
    wgt = jnp.pad(wgt, ((0, 0), (0, NSA_GATES_PAD)))
    wfeat = jnp.concatenate([wq, wks, wkw, wvs, wvw, wqm, wgt], axis=1).T.astype(BF16)
    cosF, sinF = _rope_tables(jnp.arange(seq))
    row = lambda n: pl.BlockSpec((tm, n), lambda i: (i, 0))
    col = lambda n: pl.BlockSpec((n, tm), lambda i: (0, i))
    tab = pl.BlockSpec((HALF, tm), lambda i: (0, i % npos))
    gain = _const_spec((HD, LANES))
    ngt = NSA_GATES + NSA_GATES_PAD
    return pl.pallas_call(
        _nsa_proj_body,
        out_shape=(
            jax.ShapeDtypeStruct((t, NSA_KV), F32), jax.ShapeDtypeStruct((t, NSA_KV), F32),
            jax.ShapeDtypeStruct((t, NSA_KV), BF16), jax.ShapeDtypeStruct((t, NSA_KV), BF16),
            jax.ShapeDtypeStruct((NSA_H * HD, t), BF16), jax.ShapeDtypeStruct((NSA_KV, t), BF16),
            jax.ShapeDtypeStruct((NSA_KV, t), BF16), jax.ShapeDtypeStruct((ngt, t), F32),
            jax.ShapeDtypeStruct((MEM_W, t), BF16),
        ),
        grid=(t // tm,),
        in_specs=[row(d), _const_spec((1, d)), _const_spec(wtok.shape), _const_spec(wfeat.shape),
                  tab, tab, gain, gain, gain, gain],
        out_specs=(row(NSA_KV), row(NSA_KV), row(NSA_KV), row(NSA_KV), col(NSA_H * HD), col(NSA_KV), col(NSA_KV),
                   col(ngt), col(MEM_W)),
        compiler_params=_cparams("parallel"),
        name="nsa_proj",
    )(x, g.reshape(1, d), wtok, wfeat, cosF, sinF, _lane_gain(qn), _lane_gain(kn[1]), _lane_gain(kn[2]),
      _lane_gain(mem_qn))


def _nsa_cmp_body(xk_ref, xv_ref, pk_ref, pv_ref, wk_ref, wv_ref, kn_ref, cos_ref, sin_ref, k_ref, vT_ref):
    def compress(x, pos, w_ref):
        n = x.shape[0]
        xa = (x + pos[0:1]).astype(BF16)
        xb = (pltpu.roll(x, n - 1, axis=0) + pos[1:2]).astype(BF16)
        return _dot_nt(w_ref[0], xa) + _dot_nt(w_ref[1], xb)

    kT = compress(xk_ref[0], pk_ref[...], wk_ref)
    kT = _heads_norm_rope(kT, _tile_lanes(kn_ref[...], kT.shape[-1]), cos_ref[...], sin_ref[...])
    k_ref[0] = kT.T.astype(BF16)
    vT_ref[0] = compress(xv_ref[0], pv_ref[...], wv_ref).astype(BF16)


def _nsa_cmp_weights(w, pos):
    eye = jnp.eye(NSA_G, dtype=F32)
    halves = w.reshape(CMP_L // CMP_S, CMP_S, HD, HD)
    wt = jnp.einsum('hg,alde->ahelgd', eye, halves).reshape(CMP_L // CMP_S, NSA_KV, CMP_S * NSA_KV)
    p = jnp.broadcast_to(pos.reshape(CMP_L // CMP_S, CMP_S, 1, HD), (CMP_L // CMP_S, CMP_S, NSA_G, HD))
    return wt.astype(BF16), p.reshape(CMP_L // CMP_S, CMP_S * NSA_KV).astype(F32)


def _nsa_cmp(kc, vc, pos_k, pos_v, wk, wv, kn0, batch, seq):
    n = seq // CMP_S
    width = CMP_S * NSA_KV
    xk = kc.reshape(batch, n, width)
    xv = vc.reshape(batch, n, width)
    wkt, pk = _nsa_cmp_weights(wk, pos_k)
    wvt, pv = _nsa_cmp_weights(wv, pos_v)
    cosE, sinE = _rope_tables(jnp.arange(n) * CMP_S + (CMP_L - 1))
    xspec = pl.BlockSpec((1, n, width), lambda b: (b, 0, 0))
    return pl.pallas_call(
        _nsa_cmp_body,
        out_shape=(jax.ShapeDtypeStruct((batch, n, NSA_KV), BF16), jax.ShapeDtypeStruct((batch, NSA_KV, n), BF16)),
        grid=(batch,),
        in_specs=[xspec, xspec, _const_spec(pk.shape), _const_spec(pv.shape), _const_spec(wkt.shape),
                  _const_spec(wvt.shape), _const_spec((HD, LANES)), _const_spec((HALF, n)), _const_spec((HALF, n))],
        out_specs=(pl.BlockSpec((1, n, NSA_KV), lambda b: (b, 0, 0)), pl.BlockSpec((1, NSA_KV, n), lambda b: (b, 0, 0))),
        compiler_params=_cparams("parallel"),
        name="nsa_cmp",
    )(xk, xv, pk, pv, wkt, wvt, _lane_gain(kn0), cosE, sinE)


NSA_TQ = 256
NSA_TK = 256


def _softmax_step(carry, s, vT):
    m, l, acc = carry
    m_new = jnp.maximum(m, jnp.max(s, axis=0, keepdims=True))
    alpha = jnp.exp(m - m_new)
    p = jnp.exp(s - m_new)
    l = l * alpha + jnp.sum(p, axis=0, keepdims=True)
    return m_new, l, acc * alpha + _dot(vT, p.astype(BF16))


def _nsa_attn_body(qT_ref, gT_ref, ks_ref, vsT_ref, kw_ref, vwT_ref, kc_ref, vcT_ref, cov_ref, y_ref, bias_ref, *, seq):
    tq, tk = NSA_TQ, NSA_TK
    qs = pl.program_id(1) * tq
    nkc = (qs + tq) // tk
    wlo = jnp.maximum((qs - WIN) // tk, 0)
    qpos = qs + lax.broadcasted_iota(I32, (1, tq), 1)
    ncr = seq // CMP_S
    nblk = seq // SLC_L
    n_sel = min(SLC_N_MAX, nblk)
    valid_c = CMP_S * lax.broadcasted_iota(I32, (ncr, tq), 0) + (CMP_L - 1) <= qpos
    any_c = jnp.where(qpos >= CMP_L - 1, 1.0, 0.0)
    jrow = lax.broadcasted_iota(I32, (nblk, tq), 0)
    qblk = qpos // SLC_L
    forced = (jrow == 0) | (jrow == qblk) | (jrow == qblk - 1)
    causal_blk = jrow * SLC_L <= qpos
    krow = lax.broadcasted_iota(I32, (tk, tq), 0)
    brow = lax.broadcasted_iota(I32, (SLC_L, tq), 0)
    init = tuple((jnp.full((1, tq), -jnp.inf, F32), jnp.zeros((1, tq), F32), jnp.zeros((HD, tq), F32))
                 for _ in range(NSA_R))
    outs = []
    for g in range(NSA_G):
        pair = slice((g // 2) * LANES, (g // 2 + 1) * LANES)
        grows = slice(g * HD, (g + 1) * HD)
        heads = [g * NSA_R + r for r in range(NSA_R)]
        qhs = [_pad_rows(qT_ref[h * HD:(h + 1) * HD, :], g % 2) for h in heads]

        kcm = kc_ref[0][:, pair]
        vcm = vcT_ref[0][grows, :]
        o_cmp = []
        psum = jnp.zeros((ncr, tq), F32)
        for r in range(NSA_R):
            s = jnp.where(valid_c, _dot(kcm, qhs[r]), NEG)
            p = jnp.exp(s - jnp.max(s, axis=0, keepdims=True))
            p = p / jnp.sum(p, axis=0, keepdims=True) * any_c
            o_cmp.append(_dot(vcm, p.astype(BF16)))
            psum = psum + p
        p_hi = psum.astype(BF16)
        p_lo = (psum - p_hi.astype(F32)).astype(BF16)
        imp = _dot(cov_ref[...], p_hi) + _dot(cov_ref[...], p_lo)
        imp = jnp.where(causal_blk, imp + jnp.where(forced, FORCE, 0.0), NEG)
        rank = jnp.zeros((nblk, tq), I32)
        for j2 in range(nblk):
            rj = imp[j2:j2 + 1]
            beats = (rj > imp) | ((rj == imp) & (j2 < jrow))
            rank = rank + beats.astype(I32)
        sel = rank < n_sel
        for j in range(nblk):
            ok = sel[j:j + 1] & (j * SLC_L + brow <= qpos)
            bias_ref[j * SLC_L:(j + 1) * SLC_L, :] = jnp.where(ok, 0.0, NEG)

        def slc_chunk(c, carry, qhs=qhs, pair=pair, grows=grows):
            r0 = pl.multiple_of(c * tk, tk)
            kch = ks_ref[pl.ds(r0, tk), pair]
            vch = vsT_ref[grows, pl.ds(r0, tk)]
            b = bias_ref[pl.ds(r0, tk), :]
            return tuple(_softmax_step(carry[r], _dot(kch, qhs[r]) + b, vch) for r in range(NSA_R))

        slc = lax.fori_loop(0, nkc, slc_chunk, init)

        def win_chunk(c, carry, qhs=qhs, pair=pair, grows=grows):
            r0 = pl.multiple_of(c * tk, tk)
            kch = kw_ref[pl.ds(r0, tk), pair]
            vch = vwT_ref[grows, pl.ds(r0, tk)]
            dist = qpos - (r0 + krow)
            b = jnp.where((dist >= 0) & (dist < WIN), 0.0, NEG)
            return tuple(_softmax_step(carry[r], _dot(kch, qhs[r]) + b, vch) for r in range(NSA_R))

        win = lax.fori_loop(wlo, nkc, win_chunk, init)

        for r, h in enumerate(heads):
            gates = gT_ref[h * 3:(h + 1) * 3, :]
            outs.append(gates[0:1] * o_cmp[r] + gates[1:2] * (slc[r][2] / slc[r][1])
                        + gates[2:3] * (win[r][2] / win[r][1]))
    y_ref[...] = jnp.concatenate(outs, axis=0).T.astype(BF16)


def _nsa_attn(qT, gT, ks, vsT, kw, vwT, kcmp, vcmpT, batch, seq):
    t = ks.shape[0]
    tq = NSA_TQ
    nq = seq // tq
    ncr, nblk = seq // CMP_S, seq // SLC_L
    starts = np.arange(ncr) * CMP_S
    sstart = np.arange(nblk) * SLC_L
    cover = (starts[None, :] < sstart[:, None] + SLC_L) & (starts[None, :] + CMP_L > sstart[:, None])
    cover[:, ncr - 1] = False
    colq = lambda n: pl.BlockSpec((n, tq), lambda b, i: (0, b * nq + i))
    tok = pl.BlockSpec((seq, NSA_KV), lambda b, i: (b, 0))
    feat = pl.BlockSpec((NSA_KV, seq), lambda b, i: (0, b))
    return pl.pallas_call(
        functools.partial(_nsa_attn_body, seq=seq),
        out_shape=jax.ShapeDtypeStruct((t, NSA_H * HD), BF16),
        grid=(batch, nq),
        in_specs=[
            colq(NSA_H * HD), colq(gT.shape[0]), tok, feat, tok, feat,
            pl.BlockSpec((1, ncr, NSA_KV), lambda b, i: (b, 0, 0)),
            pl.BlockSpec((1, NSA_KV, ncr), lambda b, i: (b, 0, 0)),
            _const_spec((nblk, ncr)),
        ],
        out_specs=pl.BlockSpec((tq, NSA_H * HD), lambda b, i: (b * nq + i, 0)),
        scratch_shapes=[pltpu.VMEM((seq, tq), F32)],
        compiler_params=_cparams("parallel", "arbitrary"),
        name="nsa_attn",
    )(qT, gT, ks, vsT, kw, vwT, kcmp, vcmpT, jnp.asarray(cover, BF16))


def _nsa_layer_mix(x, g, w_in, qn, kn, pos_k, pos_v, wk, wv, mem_qn, batch, seq):
    kc, vc, ks, kw, qT, vsT, vwT, gT, qmT = _nsa_proj(x, g, w_in, qn, kn, mem_qn, seq)
    kcmp, vcmpT = _nsa_cmp(kc, vc, pos_k, pos_v, wk, wv, kn[0], batch, seq)
    return _nsa_attn(qT, gT, ks, vsT, kw, vwT, kcmp, vcmpT, batch, seq), qmT


def kernel(x, mem, ffn_norm, ffn_w_gate, ffn_w_up, ffn_w_down, mix_norm, w_out, mem_norm, mem_w_kv, mem_qn, mem_kn, ret_w_in, dsa_w_in, dsa_qn, dsa_kn, nsa_w_in, nsa_qn, nsa_kn, nsa_cmp_pos_k, nsa_cmp_pos_v, nsa_cmp_wk, nsa_cmp_wv):
    batch, seq, d = x.shape
    x = x.reshape(batch * seq, d)
    for i in range(ffn_norm.shape[0]):
        x = _ffn(x, ffn_norm[i, 0], ffn_w_gate[i, 0], ffn_w_up[i, 0], ffn_w_down[i, 0])
        kind, j = i % N_MIXERS, i // N_MIXERS
        if kind == 0:
            q, v, gate, kT, qmT = _ret_proj(x, mix_norm[i], ret_w_in[j], mem_qn[i], seq)
            y_mix = _retention(q, kT, v, gate, batch, seq)
        elif kind == 1:
            kk, qT, iqT, vT, iwT, qmT = _dsa_proj(x, mix_norm[i], dsa_w_in[j], dsa_qn[j], dsa_kn[j], mem_qn[i], seq)
            y_mix = _dsa_attn(kk, qT, iqT, vT, iwT, batch, seq)
        else:
            y_mix, qmT = _nsa_layer_mix(x, mix_norm[i], nsa_w_in[j], nsa_qn[j], nsa_kn[j], nsa_cmp_pos_k[j],
                                        nsa_cmp_pos_v[j], nsa_cmp_wk[j], nsa_cmp_wv[j], mem_qn[i], batch, seq)
        mem_k, mem_vT = _mem_kv(mem, mem_norm[i], mem_w_kv[i], mem_kn[i])
        y_mem = _mem_attn(qmT, mem_k, mem_vT, seq)
        x = _out_proj(y_mix, y_mem, x, w_out[i])
        x = _ffn(x, ffn_norm[i, 1], ffn_w_gate[i, 1], ffn_w_up[i, 1], ffn_w_down[i, 1])
    return x.reshape(batch, seq, d)
```

```python
import functools
import math

import jax
import jax.numpy as jnp
import numpy as np
from jax import lax
from jax.experimental import pallas as pl
from jax.experimental.pallas import tpu as pltpu

D_MODEL = 1024
HD = 64
HALF = HD // 2
H_MIX = 12
H_MEM = 4
MIX_W = H_MIX * HD
MEM_W = H_MEM * HD
D_FF = 2816
ROPE_THETA = 10000.0
EPS = 1e-6
NEG = -1e30
FORCE = 1e9
SCALE = HD ** -0.5

RET_H, RET_DK, RET_DV, RET_CHUNK = 6, 64, 128, 128
DSA_H, IDX_H, IDX_D, DSA_TOPK_MAX = 12, 8, 64, 256
NSA_H, NSA_G, CMP_L, CMP_S, SLC_L, SLC_N_MAX, WIN = 12, 4, 32, 16, 64, 16, 512
NSA_R = NSA_H // NSA_G
N_MIXERS = 3

RET_SIZES = [RET_H * RET_DK, RET_H * RET_DK, RET_H * RET_DV, RET_H * RET_DV, MEM_W]
DSA_SIZES = [DSA_H * HD, HD, HD, IDX_H * IDX_D, IDX_D, IDX_H, MEM_W]
NSA_SIZES = [NSA_H * HD] + [NSA_G * HD] * 6 + [NSA_H * 3, MEM_W]

LANES = 128
SUBLANES = 8
MXU_N = 256
VMEM_LIMIT_BYTES = 56 * 1024 * 1024

BF16 = jnp.bfloat16
F32 = jnp.float32
I32 = jnp.int32


def _cparams(*sem):
    return pltpu.CompilerParams(dimension_semantics=sem, vmem_limit_bytes=VMEM_LIMIT_BYTES)


def _const_spec(shape):
    n = len(shape)
    return pl.BlockSpec(shape, lambda *_: (0,) * n, pipeline_mode=pl.Buffered(1))


def _rms(x, g):
    return x * lax.rsqrt(jnp.mean(x * x, axis=-1, keepdims=True) + EPS) * g


def _dot(a, b):
    return jnp.dot(a, b, preferred_element_type=F32)


def _dot_nt(a, b):
    return lax.dot_general(a, b, (((1,), (1,)), ((), ())), preferred_element_type=F32)


def _tile_lanes(a, n):
    reps = n // a.shape[-1]
    return a if reps == 1 else jnp.concatenate([a] * reps, axis=-1)


def _rope_tables(pos):
    inv = ROPE_THETA ** (-jnp.arange(HALF, dtype=F32) / HALF)
    ang = pos.astype(F32)[:, None] * inv[None, :]
    return jnp.cos(ang).T, jnp.sin(ang).T


def _lane_gain(g):
    return jnp.broadcast_to(g.astype(F32)[:, None], (g.shape[0], LANES))


def _heads_norm_rope(x, gain, cos, sin):
    t = x.shape[-1]
    x3 = x.reshape(x.shape[0] // HD, HD, t)
    if gain is not None:
        ms = jnp.sum(x3 * x3, axis=1, keepdims=True) * (1.0 / HD)
        x3 = x3 * lax.rsqrt(ms + EPS) * gain[None]
    if cos is not None:
        x1, x2 = x3[:, :HALF], x3[:, HALF:]
        x3 = jnp.concatenate([x1 * cos[None] - x2 * sin[None], x2 * cos[None] + x1 * sin[None]], axis=1)
    return x3.reshape(x.shape)


FFN_TM = 512
FFN_TF = MXU_N


def _ffn_body(x_ref, g_ref, wg_ref, wu_ref, wd_ref, o_ref):
    x = x_ref[...]
    h = _rms(x, g_ref[...]).astype(BF16)
    acc = jnp.zeros(x.shape, F32)
    for c in range(D_FF // FFN_TF):
        sl = slice(c * FFN_TF, (c + 1) * FFN_TF)
        gate = _dot(h, wg_ref[:, sl])
        up = _dot(h, wu_ref[:, sl])
        act = (gate * jax.nn.sigmoid(gate) * up).astype(BF16)
        acc = acc + _dot(act, wd_ref[sl, :])
    o_ref[...] = x + 0.5 * acc


def _ffn(x, g, wg, wu, wd):
    t, d = x.shape
    tm = min(FFN_TM, t)
    return pl.pallas_call(
        _ffn_body,
        out_shape=jax.ShapeDtypeStruct((t, d), F32),
        grid=(t // tm,),
        in_specs=[
            pl.BlockSpec((tm, d), lambda i: (i, 0)),
            _const_spec((1, d)),
            _const_spec((d, D_FF)),
            _const_spec((d, D_FF)),
            _const_spec((D_FF, d)),
        ],
        out_specs=pl.BlockSpec((tm, d), lambda i: (i, 0)),
        compiler_params=_cparams("parallel"),
        name="ffn",
    )(x, g.reshape(1, d), wg.astype(BF16), wu.astype(BF16), wd.astype(BF16))


PROJ_TM = 512


def _out_proj_body(ymix_ref, ymem_ref, x_ref, wmix_ref, wmem_ref, o_ref):
    o_ref[...] = x_ref[...] + _dot(ymix_ref[...], wmix_ref[...]) + _dot(ymem_ref[...], wmem_ref[...])


def _out_proj(y_mix, y_mem, x, w):
    t, d = x.shape
    tm = min(PROJ_TM, t)
    return pl.pallas_call(
        _out_proj_body,
        out_shape=jax.ShapeDtypeStruct((t, d), F32),
        grid=(t // tm,),
        in_specs=[
            pl.BlockSpec((tm, MIX_W), lambda i: (i, 0)),
            pl.BlockSpec((tm, MEM_W), lambda i: (i, 0)),
            pl.BlockSpec((tm, d), lambda i: (i, 0)),
            _const_spec((MIX_W, d)),
            _const_spec((MEM_W, d)),
        ],
        out_specs=pl.BlockSpec((tm, d), lambda i: (i, 0)),
        compiler_params=_cparams("parallel"),
        name="out_proj",
    )(y_mix, y_mem, x, w[:MIX_W].astype(BF16), w[MIX_W:].astype(BF16))


def _mem_kv_body(m_ref, g_ref, wk_ref, wvT_ref, kn_ref, k_ref, vT_ref):
    h = _rms(m_ref[0], g_ref[...]).astype(BF16)
    kT = _dot_nt(wk_ref[...], h)
    kT = _heads_norm_rope(kT, _tile_lanes(kn_ref[...], kT.shape[-1]), None, None)
    k_ref[0] = kT.T.astype(BF16)
    vT_ref[0] = _dot_nt(wvT_ref[...], h).astype(BF16)


def _mem_kv(mem, g, w_kv, kn):
    b, m, d = mem.shape
    return pl.pallas_call(
        _mem_kv_body,
        out_shape=(jax.ShapeDtypeStruct((b, m, MEM_W), BF16), jax.ShapeDtypeStruct((b, MEM_W, m), BF16)),
        grid=(b,),
        in_specs=[
            pl.BlockSpec((1, m, d), lambda i: (i, 0, 0)),
            _const_spec((1, d)),
            _const_spec((MEM_W, d)),
            _const_spec((MEM_W, d)),
            _const_spec((HD, LANES)),
        ],
        out_specs=(pl.BlockSpec((1, m, MEM_W), lambda i: (i, 0, 0)), pl.BlockSpec((1, MEM_W, m), lambda i: (i, 0, 0))),
        compiler_params=_cparams("parallel"),
        name="mem_kv",
    )(mem, g.reshape(1, d), w_kv[:, :MEM_W].T.astype(BF16), w_kv[:, MEM_W:].T.astype(BF16), _lane_gain(kn))


MEM_TQ = 512


def _pad_rows(blk, half, total=LANES):
    z = jnp.zeros_like(blk)
    parts = [z] * (total // HD)
    parts[half] = blk
    return jnp.concatenate(parts, axis=0)


def _mem_attn_body(qT_ref, k_ref, vT_ref, o_ref):
    qT = qT_ref[...]
    k = k_ref[0]
    vT = vT_ref[0]
    outs = []
    for h in range(H_MEM):
        pair = k[:, (h // 2) * LANES:(h // 2 + 1) * LANES]
        s = _dot(pair, _pad_rows(qT[h * HD:(h + 1) * HD], h % 2))
        m = jnp.max(s, axis=0, keepdims=True)
        p = jnp.exp(s - m)
        l = jnp.sum(p, axis=0, keepdims=True)
        o = _dot(vT[h * HD:(h + 1) * HD], p.astype(BF16))
        outs.append(o / l)
    o_ref[...] = jnp.concatenate(outs, axis=0).T.astype(BF16)


def _mem_attn(qmT, k, vT, seq):
    t = qmT.shape[1]
    b, m, _ = k.shape
    tq = min(MEM_TQ, seq)
    nq = seq // tq
    return pl.pallas_call(
        _mem_attn_body,
        out_shape=jax.ShapeDtypeStruct((t, MEM_W), BF16),
        grid=(b, nq),
        in_specs=[
            pl.BlockSpec((MEM_W, tq), lambda bi, i: (0, bi * nq + i)),
            pl.BlockSpec((1, m, MEM_W), lambda bi, i: (bi, 0, 0)),
            pl.BlockSpec((1, MEM_W, m), lambda bi, i: (bi, 0, 0)),
        ],
        out_specs=pl.BlockSpec((tq, MEM_W), lambda bi, i: (bi * nq + i, 0)),
        compiler_params=_cparams("parallel", "parallel"),
        name="mem_attn",
    )(qmT, k, vT)


def _ret_proj_body(x_ref, g_ref, wtok_ref, wfeat_ref, cosT_ref, sinT_ref, cosF_ref, sinF_ref, mqn_ref,
                   q_ref, v_ref, gate_ref, kT_ref, qmT_ref):
    h = _rms(x_ref[...], g_ref[...]).astype(BF16)
    tok = _dot(h, wtok_ref[...])
    nq, nv = RET_H * RET_DK, RET_H * RET_DV
    q = tok[:, :nq]
    lane = lax.broadcasted_iota(I32, q.shape, 1)
    rot = jnp.where(lane % HD < HALF, pltpu.roll(q, nq - HALF, axis=1), pltpu.roll(q, HALF, axis=1))
    q_ref[...] = (q * _tile_lanes(cosT_ref[...], nq) + rot * _tile_lanes(sinT_ref[...], nq)).astype(BF16)
    v_ref[...] = tok[:, nq:nq + nv].astype(BF16)
    gate_ref[...] = tok[:, nq + nv:]
    feat = _dot_nt(wfeat_ref[...], h)
    tm = feat.shape[-1]
    cos, sin = cosF_ref[...], sinF_ref[...]
    kT = _heads_norm_rope(feat[:nq], None, cos, sin) * (RET_DK ** -0.5)
    kT_ref[...] = kT.astype(BF16)
    qm = _heads_norm_rope(feat[nq:], _tile_lanes(mqn_ref[...], tm), None, None) * SCALE
    qmT_ref[...] = qm.astype(BF16)


def _token_rope_tables(seq):
    cosF, sinF = _rope_tables(jnp.arange(seq))
    cos = jnp.tile(cosF.T, (1, LANES // HALF))
    sign = jnp.where((jnp.arange(LANES) % HD) < HALF, -1.0, 1.0).astype(F32)
    sin = jnp.tile(sinF.T, (1, LANES // HALF)) * sign[None, :]
    return cos, sin


def _ret_proj(x, g, w_in, mem_qn, seq):
    t, d = x.shape
    tm = min(PROJ_TM, seq)
    npos = seq // tm
    wq, wk, wv, wg, wqm = jnp.split(w_in, np.cumsum(RET_SIZES)[:-1].tolist(), axis=1)
    wtok = jnp.concatenate([wq, wv, wg], axis=1).astype(BF16)
    wfeat = jnp.concatenate([wk, wqm], axis=1).T.astype(BF16)
    cosT, sinT = _token_rope_tables(seq)
    cosF, sinF = _rope_tables(jnp.arange(seq))
    nq, nv = RET_H * RET_DK, RET_H * RET_DV
    row = lambda n: pl.BlockSpec((tm, n), lambda i: (i, 0))
    col = lambda n: pl.BlockSpec((n, tm), lambda i: (0, i))
    return pl.pallas_call(
        _ret_proj_body,
        out_shape=(
            jax.ShapeDtypeStruct((t, nq), BF16), jax.ShapeDtypeStruct((t, nv), BF16),
            jax.ShapeDtypeStruct((t, nv), F32), jax.ShapeDtypeStruct((nq, t), BF16),
            jax.ShapeDtypeStruct((MEM_W, t), BF16),
        ),
        grid=(t // tm,),
        in_specs=[
            row(d), _const_spec((1, d)), _const_spec(wtok.shape), _const_spec(wfeat.shape),
            pl.BlockSpec((tm, LANES), lambda i: (i % npos, 0)), pl.BlockSpec((tm, LANES), lambda i: (i % npos, 0)),
            pl.BlockSpec((HALF, tm), lambda i: (0, i % npos)), pl.BlockSpec((HALF, tm), lambda i: (0, i % npos)),
            _const_spec((HD, LANES)),
        ],
        out_specs=(row(nq), row(nv), row(nv), col(nq), col(MEM_W)),
        compiler_params=_cparams("parallel"),
        name="ret_proj",
    )(x, g.reshape(1, d), wtok, wfeat, cosT, sinT, cosF, sinF, _lane_gain(mem_qn))


RET_TL = 512


def _ret_body(q_ref, kT_ref, v_ref, gate_ref, y_ref, state_ref):
    @pl.when(pl.program_id(1) == 0)
    def _():
        state_ref[...] = jnp.zeros_like(state_ref)

    c = RET_CHUNK
    ii = lax.broadcasted_iota(I32, (c, c), 0).astype(F32)
    jj = lax.broadcasted_iota(I32, (c, c), 1).astype(F32)
    diff = ii - jj
    jk = lax.broadcasted_iota(I32, (RET_DK, c), 1).astype(F32)
    for hh in range(RET_H):
        log_g = math.log(1.0 - 2.0 ** (-5.0 - hh))
        dmask = jnp.where(diff >= 0, jnp.exp(log_g * jnp.maximum(diff, 0.0)), 0.0)
        xi = jnp.exp(log_g * (ii + 1.0))
        zeta = jnp.exp(log_g * (c - 1.0 - jk))
        decay = math.exp(log_g * c)
        pair = slice((hh // 2) * LANES, (hh // 2 + 1) * LANES)
        for n in range(q_ref.shape[0] // c):
            rows = slice(n * c, (n + 1) * c)
            q2 = q_ref[rows, pair]
            kT = kT_ref[hh * RET_DK:(hh + 1) * RET_DK, rows]
            v = v_ref[rows, hh * RET_DV:(hh + 1) * RET_DV]
            state = state_ref[hh]
            inner = _dot(q2, _pad_rows(kT, hh % 2)) * dmask
            o = _dot(inner.astype(BF16), v)
            o = o + _dot(q2, _pad_rows(state.astype(BF16), hh % 2)) * xi
            kv = _dot((kT.astype(F32) * zeta).astype(BF16), v)
            state_ref[hh] = state * decay + kv
            mu = jnp.mean(o, axis=-1, keepdims=True)
            var = jnp.mean(jnp.square(o - mu), axis=-1, keepdims=True)
            o = (o - mu) * lax.rsqrt(var + EPS)
            gte = gate_ref[rows, hh * RET_DV:(hh + 1) * RET_DV]
            y_ref[rows, hh * RET_DV:(hh + 1) * RET_DV] = (gte * jax.nn.sigmoid(gte) * o).astype(BF16)


def _retention(q, kT, v, gate, batch, seq):
    t = q.shape[0]
    tl = min(RET_TL, seq)
    nl = seq // tl
    nq, nv = RET_H * RET_DK, RET_H * RET_DV
    return pl.pallas_call(
        _ret_body,
        out_shape=jax.ShapeDtypeStruct((t, nv), BF16),
        grid=(batch, nl),
        in_specs=[
            pl.BlockSpec((tl, nq), lambda b, i: (b * nl + i, 0)),
            pl.BlockSpec((nq, tl), lambda b, i: (0, b * nl + i)),
            pl.BlockSpec((tl, nv), lambda b, i: (b * nl + i, 0)),
            pl.BlockSpec((tl, nv), lambda b, i: (b * nl + i, 0)),
        ],
        out_specs=pl.BlockSpec((tl, nv), lambda b, i: (b * nl + i, 0)),
        scratch_shapes=[pltpu.VMEM((RET_H, RET_DK, RET_DV), F32)],
        compiler_params=_cparams("parallel", "arbitrary"),
        name="retention",
    )(q, kT, v, gate)


def _dsa_proj_body(x_ref, g_ref, w_ref, cos_ref, sin_ref, qn_ref, kn_ref, mqn_ref,
                   kk_ref, qT_ref, iqT_ref, vT_ref, iwT_ref, qmT_ref):
    h = _rms(x_ref[...], g_ref[...]).astype(BF16)
    feat = _dot_nt(w_ref[...], h)
    tm = feat.shape[-1]
    cos, sin = cos_ref[...], sin_ref[...]
    o = 0
    q = feat[o:o + DSA_H * HD]; o += DSA_H * HD
    iq = feat[o:o + IDX_H * IDX_D]; o += IDX_H * IDX_D
    k = feat[o:o + HD]; o += HD
    ik = feat[o:o + IDX_D]; o += IDX_D
    v = feat[o:o + HD]; o += HD
    qm = feat[o:o + MEM_W]; o += MEM_W
    iw = feat[o:o + IDX_H]
    qT_ref[...] = (_heads_norm_rope(q, _tile_lanes(qn_ref[...], tm), cos, sin) * SCALE).astype(BF16)
    iqT_ref[...] = _heads_norm_rope(iq, None, cos, sin).astype(BF16)
    k = _heads_norm_rope(k, _tile_lanes(kn_ref[...], tm), cos, sin)
    ik = _heads_norm_rope(ik, None, cos, sin)
    kk_ref[...] = jnp.concatenate([k, ik], axis=0).T.astype(BF16)
    vT_ref[...] = v.astype(BF16)
    iwT_ref[...] = iw
    qmT_ref[...] = (_heads_norm_rope(qm, _tile_lanes(mqn_ref[...], tm), None, None) * SCALE).astype(BF16)


def _dsa_proj(x, g, w_in, qn, kn, mem_qn, seq):
    t, d = x.shape
    tm = min(PROJ_TM, seq)
    npos = seq // tm
    wq, wk, wv, wiq, wik, wiw, wqm = jnp.split(w_in, np.cumsum(DSA_SIZES)[:-1].tolist(), axis=1)
    wfeat = jnp.concatenate([wq, wiq, wk, wik, wv, wqm, wiw], axis=1).T.astype(BF16)
    cosF, sinF = _rope_tables(jnp.arange(seq))
    col = lambda n: pl.BlockSpec((n, tm), lambda i: (0, i))
    tab = pl.BlockSpec((HALF, tm), lambda i: (0, i % npos))
    gain = _const_spec((HD, LANES))
    return pl.pallas_call(
        _dsa_proj_body,
        out_shape=(
            jax.ShapeDtypeStruct((t, LANES), BF16), jax.ShapeDtypeStruct((DSA_H * HD, t), BF16),
            jax.ShapeDtypeStruct((IDX_H * IDX_D, t), BF16), jax.ShapeDtypeStruct((HD, t), BF16),
            jax.ShapeDtypeStruct((IDX_H, t), F32), jax.ShapeDtypeStruct((MEM_W, t), BF16),
        ),
        grid=(t // tm,),
        in_specs=[pl.BlockSpec((tm, d), lambda i: (i, 0)), _const_spec((1, d)), _const_spec(wfeat.shape),
                  tab, tab, gain, gain, gain],
        out_specs=(pl.BlockSpec((tm, LANES), lambda i: (i, 0)), col(DSA_H * HD), col(IDX_H * IDX_D), col(HD),
                   col(IDX_H), col(MEM_W)),
        compiler_params=_cparams("parallel"),
        name="dsa_proj",
    )(x, g.reshape(1, d), wfeat, cosF, sinF, _lane_gain(qn), _lane_gain(kn), _lane_gain(mem_qn))


DSA_TQ = 256
DSA_TK = 256
INT_MIN = -2 ** 31


def _dsa_attn_body(kk_ref, qT_ref, iqT_ref, vT_ref, iwT_ref, y_ref, keys_ref, bias_ref, m_ref, l_ref, acc_ref,
                   *, topk, seq):
    tq, tk = DSA_TQ, DSA_TK
    qs = pl.program_id(1) * tq
    nkc = (qs + tq) // tk
    qpos = qs + lax.broadcasted_iota(I32, (1, tq), 1)
    row = lax.broadcasted_iota(I32, (tk, tq), 0)

    def score_chunk(c, carry):
        r0 = pl.multiple_of(c * tk, tk)
        kkc = kk_ref[pl.ds(r0, tk), :]
        acc = jnp.zeros((tk, tq), F32)
        for h in range(IDX_H):
            r = _dot(kkc, _pad_rows(iqT_ref[h * IDX_D:(h + 1) * IDX_D, :], 1))
            acc = acc + jnp.maximum(r, 0.0) * iwT_ref[h:h + 1, :]
        sc = jnp.where(r0 + row <= qpos, acc + 0.0, -jnp.inf)
        bits = pltpu.bitcast(sc, I32)
        keys_ref[pl.ds(r0, tk), :] = jnp.where(bits < 0, bits ^ jnp.int32(0x7FFFFFFF), bits)
        return carry

    lax.fori_loop(0, nkc, score_chunk, 0)

    def count(pred):
        def body(c, acc):
            r0 = pl.multiple_of(c * tk, tk)
            m = pred(keys_ref[pl.ds(r0, tk), :], r0 + row)
            return acc + jnp.sum(m.astype(I32).reshape(tk // SUBLANES, SUBLANES, tq), axis=0)
        acc = lax.fori_loop(0, nkc, body, jnp.zeros((SUBLANES, tq), I32))
        return jnp.sum(acc, axis=0, keepdims=True)

    cnt = count(lambda k, _: k >= 0)
    thr = jnp.where(cnt >= topk, jnp.int32(0), jnp.int32(INT_MIN))

    def bit_step(b, thr):
        cand = thr | lax.shift_left(jnp.int32(1), 30 - b)
        return jnp.where(count(lambda k, _: k >= cand) >= topk, cand, thr)

    thr = lax.fori_loop(0, 31, bit_step, thr)
    def tie_search():
        need = topk - count(lambda k, _: k > thr)

        def idx_step(b, q):
            cand = q | lax.shift_left(jnp.int32(1), (seq.bit_length() - 2) - b)
            return jnp.where(count(lambda k, idx: (k == thr) & (idx < cand)) < need, cand, q)

        return lax.fori_loop(0, seq.bit_length() - 1, idx_step, jnp.zeros((1, tq), I32))

    has_ties = jnp.max(count(lambda k, _: k >= thr)) > topk
    last = lax.cond(has_ties, tie_search, lambda: jnp.full((1, tq), seq, I32))

    def bias_chunk(c, carry):
        r0 = pl.multiple_of(c * tk, tk)
        k = keys_ref[pl.ds(r0, tk), :]
        idx = r0 + row
        sel = (k > thr) | ((k == thr) & (idx <= last))
        bias_ref[pl.ds(r0, tk), :] = jnp.where(sel & (idx <= qpos), 0.0, NEG)
        return carry

    lax.fori_loop(0, nkc, bias_chunk, 0)

    m_ref[...] = jnp.full(m_ref.shape, -jnp.inf, F32)
    l_ref[...] = jnp.zeros(l_ref.shape, F32)
    acc_ref[...] = jnp.zeros(acc_ref.shape, F32)

    def attn_chunk(c, carry):
        r0 = pl.multiple_of(c * tk, tk)
        kkc = kk_ref[pl.ds(r0, tk), :]
        vTc = vT_ref[:, pl.ds(r0, tk)]
        b = bias_ref[pl.ds(r0, tk), :]
        for h in range(DSA_H):
            hrow, hrows = slice(h, h + 1), slice(h * HD, (h + 1) * HD)
            s = _dot(kkc, _pad_rows(qT_ref[hrows, :], 0)) + b
            m = m_ref[hrow, :]
            m_new = jnp.maximum(m, jnp.max(s, axis=0, keepdims=True))
            alpha = jnp.exp(m - m_new)
            p = jnp.exp(s - m_new)
            l_ref[hrow, :] = l_ref[hrow, :] * alpha + jnp.sum(p, axis=0, keepdims=True)
            acc_ref[hrows, :] = acc_ref[hrows, :] * alpha + _dot(vTc, p.astype(BF16))
            m_ref[hrow, :] = m_new
        return carry

    lax.fori_loop(0, nkc, attn_chunk, 0)
    outs = [acc_ref[h * HD:(h + 1) * HD, :] / l_ref[h:h + 1, :] for h in range(DSA_H)]
    y_ref[...] = jnp.concatenate(outs, axis=0).T.astype(BF16)


def _dsa_attn(kk, qT, iqT, vT, iwT, batch, seq):
    t = kk.shape[0]
    tq = DSA_TQ
    nq = seq // tq
    topk = min(DSA_TOPK_MAX, seq // 4)
    colq = lambda n: pl.BlockSpec((n, tq), lambda b, i: (0, b * nq + i))
    return pl.pallas_call(
        functools.partial(_dsa_attn_body, topk=topk, seq=seq),
        out_shape=jax.ShapeDtypeStruct((t, DSA_H * HD), BF16),
        grid=(batch, nq),
        in_specs=[
            pl.BlockSpec((seq, LANES), lambda b, i: (b, 0)),
            colq(DSA_H * HD), colq(IDX_H * IDX_D),
            pl.BlockSpec((HD, seq), lambda b, i: (0, b)),
            colq(IDX_H),
        ],
        out_specs=pl.BlockSpec((tq, DSA_H * HD), lambda b, i: (b * nq + i, 0)),
        scratch_shapes=[pltpu.VMEM((seq, tq), I32), pltpu.VMEM((seq, tq), F32),
                        pltpu.VMEM((2 * SUBLANES, tq), F32), pltpu.VMEM((2 * SUBLANES, tq), F32),
                        pltpu.VMEM((DSA_H * HD, tq), F32)],
        compiler_params=_cparams("parallel", "arbitrary"),
        name="dsa_attn",
    )(kk, qT, iqT, vT, iwT)


NSA_KV = NSA_G * HD
NSA_GATES = NSA_H * 3
NSA_GATES_PAD = -NSA_GATES % SUBLANES


def _nsa_proj_body(x_ref, g_ref, wtok_ref, wfeat_ref, cos_ref, sin_ref, qn_ref, kns_ref, knw_ref, mqn_ref,
                   kc_ref, vc_ref, ks_ref, kw_ref, qT_ref, vsT_ref, vwT_ref, gT_ref, qmT_ref):
    h = _rms(x_ref[...], g_ref[...]).astype(BF16)
    tok = _dot(h, wtok_ref[...])
    kc_ref[...] = tok[:, :NSA_KV]
    vc_ref[...] = tok[:, NSA_KV:]
    feat = _dot_nt(wfeat_ref[...], h)
    tm = feat.shape[-1]
    cos, sin = cos_ref[...], sin_ref[...]
    o = 0
    q = feat[o:o + NSA_H * HD]; o += NSA_H * HD
    ks = feat[o:o + NSA_KV]; o += NSA_KV
    kw = feat[o:o + NSA_KV]; o += NSA_KV
    vs = feat[o:o + NSA_KV]; o += NSA_KV
    vw = feat[o:o + NSA_KV]; o += NSA_KV
    qm = feat[o:o + MEM_W]; o += MEM_W
    gates = feat[o:]
    qT_ref[...] = (_heads_norm_rope(q, _tile_lanes(qn_ref[...], tm), cos, sin) * SCALE).astype(BF16)
    ks_ref[...] = _heads_norm_rope(ks, _tile_lanes(kns_ref[...], tm), cos, sin).T.astype(BF16)
    kw_ref[...] = _heads_norm_rope(kw, _tile_lanes(knw_ref[...], tm), cos, sin).T.astype(BF16)
    vsT_ref[...] = vs.astype(BF16)
    vwT_ref[...] = vw.astype(BF16)
    gT_ref[...] = jax.nn.sigmoid(gates)
    qmT_ref[...] = (_heads_norm_rope(qm, _tile_lanes(mqn_ref[...], tm), None, None) * SCALE).astype(BF16)


def _nsa_proj(x, g, w_in, qn, kn, mem_qn, seq):
    t, d = x.shape
    tm = min(PROJ_TM, seq)
    npos = seq // tm
    wq, wkc, wvc, wks, wvs, wkw, wvw, wgt, wqm = jnp.split(w_in, np.cumsum(NSA_SIZES)[:-1].tolist(), axis=1)
    wtok = jnp.concatenate([wkc, wvc], axis=1).astype(BF16)
    wgt = jnp.pad(wgt, ((0, 0), (0, NSA_GATES_PAD)))
    wfeat = jnp.concatenate([wq, wks, wkw, wvs, wvw, wqm, wgt], axis=1).T.astype(BF16)
    cosF, sinF = _rope_tables(jnp.arange(seq))
    row = lambda n: pl.BlockSpec((tm, n), lambda i: (i, 0))
    col = lambda n: pl.BlockSpec((n, tm), lambda i: (0, i))
    tab = pl.BlockSpec((HALF, tm), lambda i: (0, i % npos))
    gain = _const_spec((HD, LANES))
    ngt = NSA_GATES + NSA_GATES_PAD
    return pl.pallas_call(
        _nsa_proj_body,
        out_shape=(
            jax.ShapeDtypeStruct((t, NSA_KV), F32), jax.ShapeDtypeStruct((t, NSA_KV), F32),
            jax.ShapeDtypeStruct((t, NSA_KV), BF16), jax.ShapeDtypeStruct((t, NSA_KV), BF16),
            jax.ShapeDtypeStruct((NSA_H * HD, t), BF16), jax.ShapeDtypeStruct((NSA_KV, t), BF16),
            jax.ShapeDtypeStruct((NSA_KV, t), BF16), jax.ShapeDtypeStruct((ngt, t), F32),
            jax.ShapeDtypeStruct((MEM_W, t), BF16),
        ),
        grid=(t // tm,),
        in_specs=[row(d), _const_spec((1, d)), _const_spec(wtok.shape), _const_spec(wfeat.shape),
                  tab, tab, gain, gain, gain, gain],
        out_specs=(row(NSA_KV), row(NSA_KV), row(NSA_KV), row(NSA_KV), col(NSA_H * HD), col(NSA_KV), col(NSA_KV),
                   col(ngt), col(MEM_W)),
        compiler_params=_cparams("parallel"),
        name="nsa_proj",
    )(x, g.reshape(1, d), wtok, wfeat, cosF, sinF, _lane_gain(qn), _lane_gain(kn[1]), _lane_gain(kn[2]),
      _lane_gain(mem_qn))


def _nsa_cmp_body(xk_ref, xv_ref, pk_ref, pv_ref, wk_ref, wv_ref, kn_ref, cos_ref, sin_ref, k_ref, vT_ref):
    def compress(x, pos, w_ref):
        n = x.shape[0]
        xa = (x + pos[0:1]).astype(BF16)
        xb = (pltpu.roll(x, n - 1, axis=0) + pos[1:2]).astype(BF16)
        return _dot_nt(w_ref[0], xa) + _dot_nt(w_ref[1], xb)

    kT = compress(xk_ref[0], pk_ref[...], wk_ref)
    kT = _heads_norm_rope(kT, _tile_lanes(kn_ref[...], kT.shape[-1]), cos_ref[...], sin_ref[...])
    k_ref[0] = kT.T.astype(BF16)
    vT_ref[0] = compress(xv_ref[0], pv_ref[...], wv_ref).astype(BF16)


def _nsa_cmp_weights(w, pos):
    eye = jnp.eye(NSA_G, dtype=F32)
    halves = w.reshape(CMP_L // CMP_S, CMP_S, HD, HD)
    wt = jnp.einsum('hg,alde->ahelgd', eye, halves).reshape(CMP_L // CMP_S, NSA_KV, CMP_S * NSA_KV)
    p = jnp.broadcast_to(pos.reshape(CMP_L // CMP_S, CMP_S, 1, HD), (CMP_L // CMP_S, CMP_S, NSA_G, HD))
    return wt.astype(BF16), p.reshape(CMP_L // CMP_S, CMP_S * NSA_KV).astype(F32)


def _nsa_cmp(kc, vc, pos_k, pos_v, wk, wv, kn0, batch, seq):
    n = seq // CMP_S
    width = CMP_S * NSA_KV
    xk = kc.reshape(batch, n, width)
    xv = vc.reshape(batch, n, width)
    wkt, pk = _nsa_cmp_weights(wk, pos_k)
    wvt, pv = _nsa_cmp_weights(wv, pos_v)
    cosE, sinE = _rope_tables(jnp.arange(n) * CMP_S + (CMP_L - 1))
    xspec = pl.BlockSpec((1, n, width), lambda b: (b, 0, 0))
    return pl.pallas_call(
        _nsa_cmp_body,
        out_shape=(jax.ShapeDtypeStruct((batch, n, NSA_KV), BF16), jax.ShapeDtypeStruct((batch, NSA_KV, n), BF16)),
        grid=(batch,),
        in_specs=[xspec, xspec, _const_spec(pk.shape), _const_spec(pv.shape), _const_spec(wkt.shape),
                  _const_spec(wvt.shape), _const_spec((HD, LANES)), _const_spec((HALF, n)), _const_spec((HALF, n))],
        out_specs=(pl.BlockSpec((1, n, NSA_KV), lambda b: (b, 0, 0)), pl.BlockSpec((1, NSA_KV, n), lambda b: (b, 0, 0))),
        compiler_params=_cparams("parallel"),
        name="nsa_cmp",
    )(xk, xv, pk, pv, wkt, wvt, _lane_gain(kn0), cosE, sinE)


NSA_TQ = 256
NSA_TK = 256


def _softmax_step(carry, s, vT):
    m, l, acc = carry
    m_new = jnp.maximum(m, jnp.max(s, axis=0, keepdims=True))
    alpha = jnp.exp(m - m_new)
    p = jnp.exp(s - m_new)
    l = l * alpha + jnp.sum(p, axis=0, keepdims=True)
    return m_new, l, acc * alpha + _dot(vT, p.astype(BF16))


def _nsa_attn_body(qT_ref, gT_ref, ks_ref, vsT_ref, kw_ref, vwT_ref, kc_ref, vcT_ref, cov_ref, y_ref, bias_ref, *, seq):
    tq, tk = NSA_TQ, NSA_TK
    qs = pl.program_id(1) * tq
    nkc = (qs + tq) // tk
    wlo = jnp.maximum((qs - WIN) // tk, 0)
    qpos = qs + lax.broadcasted_iota(I32, (1, tq), 1)
    ncr = seq // CMP_S
    nblk = seq // SLC_L
    n_sel = min(SLC_N_MAX, nblk)
    valid_c = CMP_S * lax.broadcasted_iota(I32, (ncr, tq), 0) + (CMP_L - 1) <= qpos
    any_c = jnp.where(qpos >= CMP_L - 1, 1.0, 0.0)
    jrow = lax.broadcasted_iota(I32, (nblk, tq), 0)
    qblk = qpos // SLC_L
    forced = (jrow == 0) | (jrow == qblk) | (jrow == qblk - 1)
    causal_blk = jrow * SLC_L <= qpos
    krow = lax.broadcasted_iota(I32, (tk, tq), 0)
    brow = lax.broadcasted_iota(I32, (SLC_L, tq), 0)
    init = tuple((jnp.full((1, tq), -jnp.inf, F32), jnp.zeros((1, tq), F32), jnp.zeros((HD, tq), F32))
                 for _ in range(NSA_R))
    outs = []
    for g in range(NSA_G):
        pair = slice((g // 2) * LANES, (g // 2 + 1) * LANES)
        grows = slice(g * HD, (g + 1) * HD)
        heads = [g * NSA_R + r for r in range(NSA_R)]
        qhs = [_pad_rows(qT_ref[h * HD:(h + 1) * HD, :], g % 2) for h in heads]

        kcm = kc_ref[0][:, pair]
        vcm = vcT_ref[0][grows, :]
        o_cmp = []
        psum = jnp.zeros((ncr, tq), F32)
        for r in range(NSA_R):
            s = jnp.where(valid_c, _dot(kcm, qhs[r]), NEG)
            p = jnp.exp(s - jnp.max(s, axis=0, keepdims=True))
            p = p / jnp.sum(p, axis=0, keepdims=True) * any_c
            o_cmp.append(_dot(vcm, p.astype(BF16)))
            psum = psum + p
        p_hi = psum.astype(BF16)
        p_lo = (psum - p_hi.astype(F32)).astype(BF16)
        imp = _dot(cov_ref[...], p_hi) + _dot(cov_ref[...], p_lo)
        imp = jnp.where(causal_blk, imp + jnp.where(forced, FORCE, 0.0), NEG)
        rank = jnp.zeros((nblk, tq), I32)
        for j2 in range(nblk):
            rj = imp[j2:j2 + 1]
            beats = (rj > imp) | ((rj == imp) & (j2 < jrow))
            rank = rank + beats.astype(I32)
        selb = jnp.where(rank < n_sel, 0.0, NEG)
        for j in range(nblk):
            bias_ref[j * SLC_L:(j + 1) * SLC_L, :] = jnp.where(j * SLC_L + brow <= qpos, selb[j:j + 1], NEG)

        def slc_chunk(c, carry, qhs=qhs, pair=pair, grows=grows):
            r0 = pl.multiple_of(c * tk, tk)
            kch = ks_ref[pl.ds(r0, tk), pair]
            vch = vsT_ref[grows, pl.ds(r0, tk)]
            b = bias_ref[pl.ds(r0, tk), :]
            return tuple(_softmax_step(carry[r], _dot(kch, qhs[r]) + b, vch) for r in range(NSA_R))

        slc = lax.fori_loop(0, nkc, slc_chunk, init)

        def win_chunk(c, carry, qhs=qhs, pair=pair, grows=grows):
            r0 = pl.multiple_of(c * tk, tk)
            kch = kw_ref[pl.ds(r0, tk), pair]
            vch = vwT_ref[grows, pl.ds(r0, tk)]
            dist = qpos - (r0 + krow)
            b = jnp.where((dist >= 0) & (dist < WIN), 0.0, NEG)
            return tuple(_softmax_step(carry[r], _dot(kch, qhs[r]) + b, vch) for r in range(NSA_R))

        win = lax.fori_loop(wlo, nkc, win_chunk, init)

        for r, h in enumerate(heads):
            gates = gT_ref[h * 3:(h + 1) * 3, :]
            outs.append(gates[0:1] * o_cmp[r] + gates[1:2] * (slc[r][2] / slc[r][1])
                        + gates[2:3] * (win[r][2] / win[r][1]))
    y_ref[...] = jnp.concatenate(outs, axis=0).T.astype(BF16)


def _nsa_attn(qT, gT, ks, vsT, kw, vwT, kcmp, vcmpT, batch, seq):
    t = ks.shape[0]
    tq = NSA_TQ
    nq = seq // tq
    ncr, nblk = seq // CMP_S, seq // SLC_L
    starts = np.arange(ncr) * CMP_S
    sstart = np.arange(nblk) * SLC_L
    cover = (starts[None, :] < sstart[:, None] + SLC_L) & (starts[None, :] + CMP_L > sstart[:, None])
    cover[:, ncr - 1] = False
    colq = lambda n: pl.BlockSpec((n, tq), lambda b, i: (0, b * nq + i))
    tok = pl.BlockSpec((seq, NSA_KV), lambda b, i: (b, 0))
    feat = pl.BlockSpec((NSA_KV, seq), lambda b, i: (0, b))
    return pl.pallas_call(
        functools.partial(_nsa_attn_body, seq=seq),
        out_shape=jax.ShapeDtypeStruct((t, NSA_H * HD), BF16),
        grid=(batch, nq),
        in_specs=[
            colq(NSA_H * HD), colq(gT.shape[0]), tok, feat, tok, feat,
            pl.BlockSpec((1, ncr, NSA_KV), lambda b, i: (b, 0, 0)),
            pl.BlockSpec((1, NSA_KV, ncr), lambda b, i: (b, 0, 0)),
            _const_spec((nblk, ncr)),
        ],
        out_specs=pl.BlockSpec((tq, NSA_H * HD), lambda b, i: (b * nq + i, 0)),
        scratch_shapes=[pltpu.VMEM((seq, tq), F32)],
        compiler_params=_cparams("parallel", "arbitrary"),
        name="nsa_attn",
    )(qT, gT, ks, vsT, kw, vwT, kcmp, vcmpT, jnp.asarray(cover, BF16))


def _nsa_layer_mix(x, g, w_in, qn, kn, pos_k, pos_v, wk, wv, mem_qn, batch, seq):
    kc, vc, ks, kw, qT, vsT, vwT, gT, qmT = _nsa_proj(x, g, w_in, qn, kn, mem_qn, seq)
    kcmp, vcmpT = _nsa_cmp(kc, vc, pos_k, pos_v, wk, wv, kn[0], batch, seq)
    return _nsa_attn(qT, gT, ks, vsT, kw, vwT, kcmp, vcmpT, batch, seq), qmT


def kernel(x, mem, ffn_norm, ffn_w_gate, ffn_w_up, ffn_w_down, mix_norm, w_out, mem_norm, mem_w_kv, mem_qn, mem_kn, ret_w_in, dsa_w_in, dsa_qn, dsa_kn, nsa_w_in, nsa_qn, nsa_kn, nsa_cmp_pos_k, nsa_cmp_pos_v, nsa_cmp_wk, nsa_cmp_wv):
    batch, seq, d = x.shape
    x = x.reshape(batch * seq, d)
    for i in range(ffn_norm.shape[0]):
        x = _ffn(x, ffn_norm[i, 0], ffn_w_gate[i, 0], ffn_w_up[i, 0], ffn_w_down[i, 0])
        kind, j = i % N_MIXERS, i // N_MIXERS
        if kind == 0:
            q, v, gate, kT, qmT = _ret_proj(x, mix_norm[i], ret_w_in[j], mem_qn[i], seq)
            y_mix = _retention(q, kT, v, gate, batch, seq)
        elif kind == 1:
            kk, qT, iqT, vT, iwT, qmT = _dsa_proj(x, mix_norm[i], dsa_w_in[j], dsa_qn[j], dsa_kn[j], mem_qn[i], seq)
            y_mix = _dsa_attn(kk, qT, iqT, vT, iwT, batch, seq)
        else:
            y_mix, qmT = _nsa_layer_mix(x, mix_norm[i], nsa_w_in[j], nsa_qn[j], nsa_kn[j], nsa_cmp_pos_k[j],
                                        nsa_cmp_pos_v[j], nsa_cmp_wk[j], nsa_cmp_wv[j], mem_qn[i], batch, seq)
        mem_k, mem_vT = _mem_kv(mem, mem_norm[i], mem_w_kv[i], mem_kn[i])
        y_mem = _mem_attn(qmT, mem_k, mem_vT, seq)
        x = _out_proj(y_mix, y_mem, x, w_out[i])
        x = _ffn(x, ffn_norm[i, 1], ffn_w_gate[i, 1], ffn_w_up[i, 1], ffn_w_down[i, 1])
    return x.reshape(batch, seq, d)
```

```python
import functools
import math

import jax
import jax.numpy as jnp
import numpy as np
from jax import lax
from jax.experimental import pallas as pl
from jax.experimental.pallas import tpu as pltpu

D_MODEL = 1024
HD = 64
HALF = HD // 2
H_MIX = 12
H_MEM = 4
MIX_W = H_MIX * HD
MEM_W = H_MEM * HD
D_FF = 2816
ROPE_THETA = 10000.0
EPS = 1e-6
NEG = -1e30
FORCE = 1e9
SCALE = HD ** -0.5

RET_H, RET_DK, RET_DV, RET_CHUNK = 6, 64, 128, 128
DSA_H, IDX_H, IDX_D, DSA_TOPK_MAX = 12, 8, 64, 256
NSA_H, NSA_G, CMP_L, CMP_S, SLC_L, SLC_N_MAX, WIN = 12, 4, 32, 16, 64, 16, 512
NSA_R = NSA_H // NSA_G
N_MIXERS = 3

RET_SIZES = [RET_H * RET_DK, RET_H * RET_DK, RET_H * RET_DV, RET_H * RET_DV, MEM_W]
DSA_SIZES = [DSA_H * HD, HD, HD, IDX_H * IDX_D, IDX_D, IDX_H, MEM_W]
NSA_SIZES = [NSA_H * HD] + [NSA_G * HD] * 6 + [NSA_H * 3, MEM_W]

LANES = 128
SUBLANES = 8
MXU_N = 256
VMEM_LIMIT_BYTES = 56 * 1024 * 1024

BF16 = jnp.bfloat16
F32 = jnp.float32
I32 = jnp.int32


def _cparams(*sem):
    return pltpu.CompilerParams(dimension_semantics=sem, vmem_limit_bytes=VMEM_LIMIT_BYTES)


def _const_spec(shape):
    n = len(shape)
    return pl.BlockSpec(shape, lambda *_: (0,) * n, pipeline_mode=pl.Buffered(1))


def _rms(x, g):
    return x * lax.rsqrt(jnp.mean(x * x, axis=-1, keepdims=True) + EPS) * g


def _dot(a, b):
    return jnp.dot(a, b, preferred_element_type=F32)


def _dot_nt(a, b):
    return lax.dot_general(a, b, (((1,), (1,)), ((), ())), preferred_element_type=F32)


def _tile_lanes(a, n):
    reps = n // a.shape[-1]
    return a if reps == 1 else jnp.concatenate([a] * reps, axis=-1)


def _rope_tables(pos):
    inv = ROPE_THETA ** (-jnp.arange(HALF, dtype=F32) / HALF)
    ang = pos.astype(F32)[:, None] * inv[None, :]
    return jnp.cos(ang).T, jnp.sin(ang).T


def _lane_gain(g):
    return jnp.broadcast_to(g.astype(F32)[:, None], (g.shape[0], LANES))


def _heads_norm_rope(x, gain, cos, sin):
    t = x.shape[-1]
    x3 = x.reshape(x.shape[0] // HD, HD, t)
    if gain is not None:
        ms = jnp.sum(x3 * x3, axis=1, keepdims=True) * (1.0 / HD)
        x3 = x3 * lax.rsqrt(ms + EPS) * gain[None]
    if cos is not None:
        x1, x2 = x3[:, :HALF], x3[:, HALF:]
        x3 = jnp.concatenate([x1 * cos[None] - x2 * sin[None], x2 * cos[None] + x1 * sin[None]], axis=1)
    return x3.reshape(x.shape)


FFN_TM = 512
FFN_TF = MXU_N


def _ffn_body(x_ref, g_ref, wg_ref, wu_ref, wd_ref, o_ref):
    x = x_ref[...]
    h = _rms(x, g_ref[...]).astype(BF16)
    acc = jnp.zeros(x.shape, F32)
    for c in range(D_FF // FFN_TF):
        sl = slice(c * FFN_TF, (c + 1) * FFN_TF)
        gate = _dot(h, wg_ref[:, sl])
        up = _dot(h, wu_ref[:, sl])
        act = (gate * jax.nn.sigmoid(gate) * up).astype(BF16)
        acc = acc + _dot(act, wd_ref[sl, :])
    o_ref[...] = x + 0.5 * acc


def _ffn(x, g, wg, wu, wd):
    t, d = x.shape
    tm = min(FFN_TM, t)
    return pl.pallas_call(
        _ffn_body,
        out_shape=jax.ShapeDtypeStruct((t, d), F32),
        grid=(t // tm,),
        in_specs=[
            pl.BlockSpec((tm, d), lambda i: (i, 0)),
            _const_spec((1, d)),
            _const_spec((d, D_FF)),
            _const_spec((d, D_FF)),
            _const_spec((D_FF, d)),
        ],
        out_specs=pl.BlockSpec((tm, d), lambda i: (i, 0)),
        compiler_params=_cparams("parallel"),
        name="ffn",
    )(x, g.reshape(1, d), wg.astype(BF16), wu.astype(BF16), wd.astype(BF16))


PROJ_TM = 512


def _out_proj_body(ymix_ref, ymem_ref, x_ref, wmix_ref, wmem_ref, o_ref):
    o_ref[...] = x_ref[...] + _dot(ymix_ref[...], wmix_ref[...]) + _dot(ymem_ref[...], wmem_ref[...])


def _out_proj(y_mix, y_mem, x, w):
    t, d = x.shape
    tm = min(PROJ_TM, t)
    return pl.pallas_call(
        _out_proj_body,
        out_shape=jax.ShapeDtypeStruct((t, d), F32),
        grid=(t // tm,),
        in_specs=[
            pl.BlockSpec((tm, MIX_W), lambda i: (i, 0)),
            pl.BlockSpec((tm, MEM_W), lambda i: (i, 0)),
            pl.BlockSpec((tm, d), lambda i: (i, 0)),
            _const_spec((MIX_W, d)),
            _const_spec((MEM_W, d)),
        ],
        out_specs=pl.BlockSpec((tm, d), lambda i: (i, 0)),
        compiler_params=_cparams("parallel"),
        name="out_proj",
    )(y_mix, y_mem, x, w[:MIX_W].astype(BF16), w[MIX_W:].astype(BF16))


def _mem_kv_body(m_ref, g_ref, wk_ref, wvT_ref, kn_ref, k_ref, vT_ref):
    h = _rms(m_ref[0], g_ref[...]).astype(BF16)
    kT = _dot_nt(wk_ref[...], h)
    kT = _heads_norm_rope(kT, _tile_lanes(kn_ref[...], kT.shape[-1]), None, None)
    k_ref[0] = kT.T.astype(BF16)
    vT_ref[0] = _dot_nt(wvT_ref[...], h).astype(BF16)


def _mem_kv(mem, g, w_kv, kn):
    b, m, d = mem.shape
    return pl.pallas_call(
        _mem_kv_body,
        out_shape=(jax.ShapeDtypeStruct((b, m, MEM_W), BF16), jax.ShapeDtypeStruct((b, MEM_W, m), BF16)),
        grid=(b,),
        in_specs=[
            pl.BlockSpec((1, m, d), lambda i: (i, 0, 0)),
            _const_spec((1, d)),
            _const_spec((MEM_W, d)),
            _const_spec((MEM_W, d)),
            _const_spec((HD, LANES)),
        ],
        out_specs=(pl.BlockSpec((1, m, MEM_W), lambda i: (i, 0, 0)), pl.BlockSpec((1, MEM_W, m), lambda i: (i, 0, 0))),
        compiler_params=_cparams("parallel"),
        name="mem_kv",
    )(mem, g.reshape(1, d), w_kv[:, :MEM_W].T.astype(BF16), w_kv[:, MEM_W:].T.astype(BF16), _lane_gain(kn))


MEM_TQ = 512


def _pad_rows(blk, half, total=LANES):
    z = jnp.zeros_like(blk)
    parts = [z] * (total // HD)
    parts[half] = blk
    return jnp.concatenate(parts, axis=0)


def _mem_attn_body(qT_ref, k_ref, vT_ref, o_ref):
    qT = qT_ref[...]
    k = k_ref[0]
    vT = vT_ref[0]
    outs = []
    for h in range(H_MEM):
        pair = k[:, (h // 2) * LANES:(h // 2 + 1) * LANES]
        s = _dot(pair, _pad_rows(qT[h * HD:(h + 1) * HD], h % 2))
        m = jnp.max(s, axis=0, keepdims=True)
        p = jnp.exp(s - m)
        l = jnp.sum(p, axis=0, keepdims=True)
        o = _dot(vT[h * HD:(h + 1) * HD], p.astype(BF16))
        outs.append(o / l)
    o_ref[...] = jnp.concatenate(outs, axis=0).T.astype(BF16)


def _mem_attn(qmT, k, vT, seq):
    t = qmT.shape[1]
    b, m, _ = k.shape
    tq = min(MEM_TQ, seq)
    nq = seq // tq
    return pl.pallas_call(
        _mem_attn_body,
        out_shape=jax.ShapeDtypeStruct((t, MEM_W), BF16),
        grid=(b, nq),
        in_specs=[
            pl.BlockSpec((MEM_W, tq), lambda bi, i: (0, bi * nq + i)),
            pl.BlockSpec((1, m, MEM_W), lambda bi, i: (bi, 0, 0)),
            pl.BlockSpec((1, MEM_W, m), lambda bi, i: (bi, 0, 0)),
        ],
        out_specs=pl.BlockSpec((tq, MEM_W), lambda bi, i: (bi * nq + i, 0)),
        compiler_params=_cparams("parallel", "parallel"),
        name="mem_attn",
    )(qmT, k, vT)


def _ret_proj_body(x_ref, g_ref, wtok_ref, wfeat_ref, cosT_ref, sinT_ref, cosF_ref, sinF_ref, mqn_ref,
                   q_ref, v_ref, gate_ref, kT_ref, qmT_ref):
    h = _rms(x_ref[...], g_ref[...]).astype(BF16)
    tok = _dot(h, wtok_ref[...])
    nq, nv = RET_H * RET_DK, RET_H * RET_DV
    q = tok[:, :nq]
    lane = lax.broadcasted_iota(I32, q.shape, 1)
    rot = jnp.where(lane % HD < HALF, pltpu.roll(q, nq - HALF, axis=1), pltpu.roll(q, HALF, axis=1))
    q_ref[...] = (q * _tile_lanes(cosT_ref[...], nq) + rot * _tile_lanes(sinT_ref[...], nq)).astype(BF16)
    v_ref[...] = tok[:, nq:nq + nv].astype(BF16)
    gate_ref[...] = tok[:, nq + nv:]
    feat = _dot_nt(wfeat_ref[...], h)
    tm = feat.shape[-1]
    cos, sin = cosF_ref[...], sinF_ref[...]
    kT = _heads_norm_rope(feat[:nq], None, cos, sin) * (RET_DK ** -0.5)
    kT_ref[...] = kT.astype(BF16)
    qm = _heads_norm_rope(feat[nq:], _tile_lanes(mqn_ref[...], tm), None, None) * SCALE
    qmT_ref[...] = qm.astype(BF16)


def _token_rope_tables(seq):
    cosF, sinF = _rope_tables(jnp.arange(seq))
    cos = jnp.tile(cosF.T, (1, LANES // HALF))
    sign = jnp.where((jnp.arange(LANES) % HD) < HALF, -1.0, 1.0).astype(F32)
    sin = jnp.tile(sinF.T, (1, LANES // HALF)) * sign[None, :]
    return cos, sin


def _ret_proj(x, g, w_in, mem_qn, seq):
    t, d = x.shape
    tm = min(PROJ_TM, seq)
    npos = seq // tm
    wq, wk, wv, wg, wqm = jnp.split(w_in, np.cumsum(RET_SIZES)[:-1].tolist(), axis=1)
    wtok = jnp.concatenate([wq, wv, wg], axis=1).astype(BF16)
    wfeat = jnp.concatenate([wk, wqm], axis=1).T.astype(BF16)
    cosT, sinT = _token_rope_tables(seq)
    cosF, sinF = _rope_tables(jnp.arange(seq))
    nq, nv = RET_H * RET_DK, RET_H * RET_DV
    row = lambda n: pl.BlockSpec((tm, n), lambda i: (i, 0))
    col = lambda n: pl.BlockSpec((n, tm), lambda i: (0, i))
    return pl.pallas_call(
        _ret_proj_body,
        out_shape=(
            jax.ShapeDtypeStruct((t, nq), BF16), jax.ShapeDtypeStruct((t, nv), BF16),
            jax.ShapeDtypeStruct((t, nv), F32), jax.ShapeDtypeStruct((nq, t), BF16),
            jax.ShapeDtypeStruct((MEM_W, t), BF16),
        ),
        grid=(t // tm,),
        in_specs=[
            row(d), _const_spec((1, d)), _const_spec(wtok.shape), _const_spec(wfeat.shape),
            pl.BlockSpec((tm, LANES), lambda i: (i % npos, 0)), pl.BlockSpec((tm, LANES), lambda i: (i % npos, 0)),
            pl.BlockSpec((HALF, tm), lambda i: (0, i % npos)), pl.BlockSpec((HALF, tm), lambda i: (0, i % npos)),
            _const_spec((HD, LANES)),
        ],
        out_specs=(row(nq), row(nv), row(nv), col(nq), col(MEM_W)),
        compiler_params=_cparams("parallel"),
        name="ret_proj",
    )(x, g.reshape(1, d), wtok, wfeat, cosT, sinT, cosF, sinF, _lane_gain(mem_qn))


RET_TL = 512


def _ret_body(q_ref, kT_ref, v_ref, gate_ref, y_ref, state_ref):
    @pl.when(pl.program_id(1) == 0)
    def _():
        state_ref[...] = jnp.zeros_like(state_ref)

    c = RET_CHUNK
    ii = lax.broadcasted_iota(I32, (c, c), 0).astype(F32)
    jj = lax.broadcasted_iota(I32, (c, c), 1).astype(F32)
    diff = ii - jj
    jk = lax.broadcasted_iota(I32, (RET_DK, c), 1).astype(F32)
    for hh in range(RET_H):
        log_g = math.log(1.0 - 2.0 ** (-5.0 - hh))
        dmask = jnp.where(diff >= 0, jnp.exp(log_g * jnp.maximum(diff, 0.0)), 0.0)
        xi = jnp.exp(log_g * (ii + 1.0))
        zeta = jnp.exp(log_g * (c - 1.0 - jk))
        decay = math.exp(log_g * c)
        pair = slice((hh // 2) * LANES, (hh // 2 + 1) * LANES)
        for n in range(q_ref.shape[0] // c):
            rows = slice(n * c, (n + 1) * c)
            q2 = q_ref[rows, pair]
            kT = kT_ref[hh * RET_DK:(hh + 1) * RET_DK, rows]
            v = v_ref[rows, hh * RET_DV:(hh + 1) * RET_DV]
            state = state_ref[hh]
            inner = _dot(q2, _pad_rows(kT, hh % 2)) * dmask
            o = _dot(inner.astype(BF16), v)
            o = o + _dot(q2, _pad_rows(state.astype(BF16), hh % 2)) * xi
            kv = _dot((kT.astype(F32) * zeta).astype(BF16), v)
            state_ref[hh] = state * decay + kv
            mu = jnp.mean(o, axis=-1, keepdims=True)
            var = jnp.mean(jnp.square(o - mu), axis=-1, keepdims=True)
            o = (o - mu) * lax.rsqrt(var + EPS)
            gte = gate_ref[rows, hh * RET_DV:(hh + 1) * RET_DV]
            y_ref[rows, hh * RET_DV:(hh + 1) * RET_DV] = (gte * jax.nn.sigmoid(gte) * o).astype(BF16)


def _retention(q, kT, v, gate, batch, seq):
    t = q.shape[0]
    tl = min(RET_TL, seq)
    nl = seq // tl
    nq, nv = RET_H * RET_DK, RET_H * RET_DV
    return pl.pallas_call(
        _ret_body,
        out_shape=jax.ShapeDtypeStruct((t, nv), BF16),
        grid=(batch, nl),
        in_specs=[
            pl.BlockSpec((tl, nq), lambda b, i: (b * nl + i, 0)),
            pl.BlockSpec((nq, tl), lambda b, i: (0, b * nl + i)),
            pl.BlockSpec((tl, nv), lambda b, i: (b * nl + i, 0)),
            pl.BlockSpec((tl, nv), lambda b, i: (b * nl + i, 0)),
        ],
        out_specs=pl.BlockSpec((tl, nv), lambda b, i: (b * nl + i, 0)),
        scratch_shapes=[pltpu.VMEM((RET_H, RET_DK, RET_DV), F32)],
        compiler_params=_cparams("parallel", "arbitrary"),
        name="retention",
    )(q, kT, v, gate)


def _dsa_proj_body(x_ref, g_ref, w_ref, cos_ref, sin_ref, qn_ref, kn_ref, mqn_ref,
                   kk_ref, qT_ref, iqT_ref, vT_ref, iwT_ref, qmT_ref):
    h = _rms(x_ref[...], g_ref[...]).astype(BF16)
    feat = _dot_nt(w_ref[...], h)
    tm = feat.shape[-1]
    cos, sin = cos_ref[...], sin_ref[...]
    o = 0
    q = feat[o:o + DSA_H * HD]; o += DSA_H * HD
    iq = feat[o:o + IDX_H * IDX_D]; o += IDX_H * IDX_D
    k = feat[o:o + HD]; o += HD
    ik = feat[o:o + IDX_D]; o += IDX_D
    v = feat[o:o + HD]; o += HD
    qm = feat[o:o + MEM_W]; o += MEM_W
    iw = feat[o:o + IDX_H]
    qT_ref[...] = (_heads_norm_rope(q, _tile_lanes(qn_ref[...], tm), cos, sin) * SCALE).astype(BF16)
    iqT_ref[...] = _heads_norm_rope(iq, None, cos, sin).astype(BF16)
    k = _heads_norm_rope(k, _tile_lanes(kn_ref[...], tm), cos, sin)
    ik = _heads_norm_rope(ik, None, cos, sin)
    kk_ref[...] = jnp.concatenate([k, ik], axis=0).T.astype(BF16)
    vT_ref[...] = v.astype(BF16)
    iwT_ref[...] = iw
    qmT_ref[...] = (_heads_norm_rope(qm, _tile_lanes(mqn_ref[...], tm), None, None) * SCALE).astype(BF16)


def _dsa_proj(x, g, w_in, qn, kn, mem_qn, seq):
    t, d = x.shape
    tm = min(PROJ_TM, seq)
    npos = seq // tm
    wq, wk, wv, wiq, wik, wiw, wqm = jnp.split(w_in, np.cumsum(DSA_SIZES)[:-1].tolist(), axis=1)
    wfeat = jnp.concatenate([wq, wiq, wk, wik, wv, wqm, wiw], axis=1).T.astype(BF16)
    cosF, sinF = _rope_tables(jnp.arange(seq))
    col = lambda n: pl.BlockSpec((n, tm), lambda i: (0, i))
    tab = pl.BlockSpec((HALF, tm), lambda i: (0, i % npos))
    gain = _const_spec((HD, LANES))
    return pl.pallas_call(
        _dsa_proj_body,
        out_shape=(
            jax.ShapeDtypeStruct((t, LANES), BF16), jax.ShapeDtypeStruct((DSA_H * HD, t), BF16),
            jax.ShapeDtypeStruct((IDX_H * IDX_D, t), BF16), jax.ShapeDtypeStruct((HD, t), BF16),
            jax.ShapeDtypeStruct((IDX_H, t), F32), jax.ShapeDtypeStruct((MEM_W, t), BF16),
        ),
        grid=(t // tm,),
        in_specs=[pl.BlockSpec((tm, d), lambda i: (i, 0)), _const_spec((1, d)), _const_spec(wfeat.shape),
                  tab, tab, gain, gain, gain],
        out_specs=(pl.BlockSpec((tm, LANES), lambda i: (i, 0)), col(DSA_H * HD), col(IDX_H * IDX_D), col(HD),
                   col(IDX_H), col(MEM_W)),
        compiler_params=_cparams("parallel"),
        name="dsa_proj",
    )(x, g.reshape(1, d), wfeat, cosF, sinF, _lane_gain(qn), _lane_gain(kn), _lane_gain(mem_qn))


DSA_TQ = 256
DSA_TK = 256
INT_MIN = -2 ** 31


def _dsa_attn_body(kk_ref, qT_ref, iqT_ref, vT_ref, iwT_ref, y_ref, keys_ref, bias_ref, s_ref, m8_ref, m_ref, l8_ref,
                   acc_ref, *, topk, seq):
    tq, tk = DSA_TQ, DSA_TK
    qs = pl.program_id(1) * tq
    nkc = (qs + tq) // tk
    qpos = qs + lax.broadcasted_iota(I32, (1, tq), 1)
    row = lax.broadcasted_iota(I32, (tk, tq), 0)

    def score_chunk(c, carry):
        r0 = pl.multiple_of(c * tk, tk)
        kkc = kk_ref[pl.ds(r0, tk), :]
        acc = jnp.zeros((tk, tq), F32)
        for h in range(IDX_H):
            r = _dot(kkc, _pad_rows(iqT_ref[h * IDX_D:(h + 1) * IDX_D, :], 1))
            acc = acc + jnp.maximum(r, 0.0) * iwT_ref[h:h + 1, :]
        sc = jnp.where(r0 + row <= qpos, acc + 0.0, -jnp.inf)
        bits = pltpu.bitcast(sc, I32)
        keys_ref[pl.ds(r0, tk), :] = jnp.where(bits < 0, bits ^ jnp.int32(0x7FFFFFFF), bits)
        return carry

    lax.fori_loop(0, nkc, score_chunk, 0)

    def count(pred):
        def body(c, acc):
            r0 = pl.multiple_of(c * tk, tk)
            m = pred(keys_ref[pl.ds(r0, tk), :], r0 + row)
            return acc + jnp.sum(m.astype(I32).reshape(tk // SUBLANES, SUBLANES, tq), axis=0)
        acc = lax.fori_loop(0, nkc, body, jnp.zeros((SUBLANES, tq), I32))
        return jnp.sum(acc, axis=0, keepdims=True)

    cnt = count(lambda k, _: k >= 0)
    thr = jnp.where(cnt >= topk, jnp.int32(0), jnp.int32(INT_MIN))

    def bit_step(b, thr):
        cand = thr | lax.shift_left(jnp.int32(1), 30 - b)
        return jnp.where(count(lambda k, _: k >= cand) >= topk, cand, thr)

    thr = lax.fori_loop(0, 31, bit_step, thr)
    def tie_search():
        need = topk - count(lambda k, _: k > thr)

        def idx_step(b, q):
            cand = q | lax.shift_left(jnp.int32(1), (seq.bit_length() - 2) - b)
            return jnp.where(count(lambda k, idx: (k == thr) & (idx < cand)) < need, cand, q)

        return lax.fori_loop(0, seq.bit_length() - 1, idx_step, jnp.zeros((1, tq), I32))

    has_ties = jnp.max(count(lambda k, _: k >= thr)) > topk
    last = lax.cond(has_ties, tie_search, lambda: jnp.full((1, tq), seq, I32))

    def bias_chunk(c, carry):
        r0 = pl.multiple_of(c * tk, tk)
        k = keys_ref[pl.ds(r0, tk), :]
        idx = r0 + row
        sel = (k > thr) | ((k == thr) & (idx <= last))
        bias_ref[pl.ds(r0, tk), :] = jnp.where(sel & (idx <= qpos), 0.0, NEG)
        return carry

    lax.fori_loop(0, nkc, bias_chunk, 0)

    outs = _two_pass_attention(
        DSA_H, 0, nkc, tk,
        lambda h, r0: _dot(kk_ref[pl.ds(r0, tk), :], _pad_rows(qT_ref[h * HD:(h + 1) * HD, :], 0))
        + bias_ref[pl.ds(r0, tk), :],
        lambda h, r0: vT_ref[:, pl.ds(r0, tk)], s_ref, m8_ref, m_ref, l8_ref, acc_ref)
    y_ref[...] = jnp.concatenate(outs, axis=0).T.astype(BF16)


def _dsa_attn(kk, qT, iqT, vT, iwT, batch, seq):
    t = kk.shape[0]
    tq = DSA_TQ
    nq = seq // tq
    topk = min(DSA_TOPK_MAX, seq // 4)
    colq = lambda n: pl.BlockSpec((n, tq), lambda b, i: (0, b * nq + i))
    return pl.pallas_call(
        functools.partial(_dsa_attn_body, topk=topk, seq=seq),
        out_shape=jax.ShapeDtypeStruct((t, DSA_H * HD), BF16),
        grid=(batch, nq),
        in_specs=[
            pl.BlockSpec((seq, LANES), lambda b, i: (b, 0)),
            colq(DSA_H * HD), colq(IDX_H * IDX_D),
            pl.BlockSpec((HD, seq), lambda b, i: (0, b)),
            colq(IDX_H),
        ],
        out_specs=pl.BlockSpec((tq, DSA_H * HD), lambda b, i: (b * nq + i, 0)),
        scratch_shapes=[pltpu.VMEM((seq, tq), I32), pltpu.VMEM((seq, tq), F32),
                        pltpu.VMEM((DSA_H, seq, tq), F32), pltpu.VMEM((DSA_H * SUBLANES, tq), F32),
                        pltpu.VMEM((2 * SUBLANES, tq), F32), pltpu.VMEM((DSA_H * SUBLANES, tq), F32),
                        pltpu.VMEM((DSA_H * HD, tq), F32)],
        compiler_params=_cparams("parallel", "arbitrary"),
        name="dsa_attn",
    )(kk, qT, iqT, vT, iwT)


NSA_KV = NSA_G * HD
NSA_GATES = NSA_H * 3
NSA_GATES_PAD = -NSA_GATES % SUBLANES


def _nsa_proj_body(x_ref, g_ref, wtok_ref, wfeat_ref, cos_ref, sin_ref, qn_ref, kns_ref, knw_ref, mqn_ref,
                   kc_ref, vc_ref, ks_ref, kw_ref, qT_ref, vsT_ref, vwT_ref, gT_ref, qmT_ref):
    h = _rms(x_ref[...], g_ref[...]).astype(BF16)
    tok = _dot(h, wtok_ref[...])
    kc_ref[...] = tok[:, :NSA_KV]
    vc_ref[...] = tok[:, NSA_KV:]
    feat = _dot_nt(wfeat_ref[...], h)
    tm = feat.shape[-1]
    cos, sin = cos_ref[...], sin_ref[...]
    o = 0
    q = feat[o:o + NSA_H * HD]; o += NSA_H * HD
    ks = feat[o:o + NSA_KV]; o += NSA_KV
    kw = feat[o:o + NSA_KV]; o += NSA_KV
    vs = feat[o:o + NSA_KV]; o += NSA_KV
    vw = feat[o:o + NSA_KV]; o += NSA_KV
    qm = feat[o:o + MEM_W]; o += MEM_W
    gates = feat[o:]
    qT_ref[...] = (_heads_norm_rope(q, _tile_lanes(qn_ref[...], tm), cos, sin) * SCALE).astype(BF16)
    ks_ref[...] = _heads_norm_rope(ks, _tile_lanes(kns_ref[...], tm), cos, sin).T.astype(BF16)
    kw_ref[...] = _heads_norm_rope(kw, _tile_lanes(knw_ref[...], tm), cos, sin).T.astype(BF16)
    vsT_ref[...] = vs.astype(BF16)
    vwT_ref[...] = vw.astype(BF16)
    gT_ref[...] = jax.nn.sigmoid(gates)
    qmT_ref[...] = (_heads_norm_rope(qm, _tile_lanes(mqn_ref[...], tm), None, None) * SCALE).astype(BF16)


def _nsa_proj(x, g, w_in, qn, kn, mem_qn, seq):
    t, d = x.shape
    tm = min(PROJ_TM, seq)
    npos = seq // tm
    wq, wkc, wvc, wks, wvs, wkw, wvw, wgt, wqm = jnp.split(w_in, np.cumsum(NSA_SIZES)[:-1].tolist(), axis=1)
    wtok = jnp.concatenate([wkc, wvc], axis=1).astype(BF16)
    wgt = jnp.pad(wgt, ((0, 0), (0, NSA_GATES_PAD)))
    wfeat = jnp.concatenate([wq, wks, wkw, wvs, wvw, wqm, wgt], axis=1).T.astype(BF16)
    cosF, sinF = _rope_tables(jnp.arange(seq))
    row = lambda n: pl.BlockSpec((tm, n), lambda i: (i, 0))
    col = lambda n: pl.BlockSpec((n, tm), lambda i: (0, i))
    tab = pl.BlockSpec((HALF, tm), lambda i: (0, i % npos))
    gain = _const_spec((HD, LANES))
    ngt = NSA_GATES + NSA_GATES_PAD
    return pl.pallas_call(
        _nsa_proj_body,
        out_shape=(
            jax.ShapeDtypeStruct((t, NSA_KV), F32), jax.ShapeDtypeStruct((t, NSA_KV), F32),
            jax.ShapeDtypeStruct((t, NSA_KV), BF16), jax.ShapeDtypeStruct((t, NSA_KV), BF16),
            jax.ShapeDtypeStruct((NSA_H * HD, t), BF16), jax.ShapeDtypeStruct((NSA_KV, t), BF16),
            jax.ShapeDtypeStruct((NSA_KV, t), BF16), jax.ShapeDtypeStruct((ngt, t), F32),
            jax.ShapeDtypeStruct((MEM_W, t), BF16),
        ),
        grid=(t // tm,),
        in_specs=[row(d), _const_spec((1, d)), _const_spec(wtok.shape), _const_spec(wfeat.shape),
                  tab, tab, gain, gain, gain, gain],
        out_specs=(row(NSA_KV), row(NSA_KV), row(NSA_KV), row(NSA_KV), col(NSA_H * HD), col(NSA_KV), col(NSA_KV),
                   col(ngt), col(MEM_W)),
        compiler_params=_cparams("parallel"),
        name="nsa_proj",
    )(x, g.reshape(1, d), wtok, wfeat, cosF, sinF, _lane_gain(qn), _lane_gain(kn[1]), _lane_gain(kn[2]),
      _lane_gain(mem_qn))


def _nsa_cmp_body(xk_ref, xv_ref, pk_ref, pv_ref, wk_ref, wv_ref, kn_ref, cos_ref, sin_ref, k_ref, vT_ref):
    def compress(x, pos, w_ref):
        n = x.shape[0]
        xa = (x + pos[0:1]).astype(BF16)
        xb = (pltpu.roll(x, n - 1, axis=0) + pos[1:2]).astype(BF16)
        return _dot_nt(w_ref[0], xa) + _dot_nt(w_ref[1], xb)

    kT = compress(xk_ref[0], pk_ref[...], wk_ref)
    kT = _heads_norm_rope(kT, _tile_lanes(kn_ref[...], kT.shape[-1]), cos_ref[...], sin_ref[...])
    k_ref[0] = kT.T.astype(BF16)
    vT_ref[0] = compress(xv_ref[0], pv_ref[...], wv_ref).astype(BF16)


def _nsa_cmp_weights(w, pos):
    eye = jnp.eye(NSA_G, dtype=F32)
    halves = w.reshape(CMP_L // CMP_S, CMP_S, HD, HD)
    wt = jnp.einsum('hg,alde->ahelgd', eye, halves).reshape(CMP_L // CMP_S, NSA_KV, CMP_S * NSA_KV)
    p = jnp.broadcast_to(pos.reshape(CMP_L // CMP_S, CMP_S, 1, HD), (CMP_L // CMP_S, CMP_S, NSA_G, HD))
    return wt.astype(BF16), p.reshape(CMP_L // CMP_S, CMP_S * NSA_KV).astype(F32)


def _nsa_cmp(kc, vc, pos_k, pos_v, wk, wv, kn0, batch, seq):
    n = seq // CMP_S
    width = CMP_S * NSA_KV
    xk = kc.reshape(batch, n, width)
    xv = vc.reshape(batch, n, width)
    wkt, pk = _nsa_cmp_weights(wk, pos_k)
    wvt, pv = _nsa_cmp_weights(wv, pos_v)
    cosE, sinE = _rope_tables(jnp.arange(n) * CMP_S + (CMP_L - 1))
    xspec = pl.BlockSpec((1, n, width), lambda b: (b, 0, 0))
    return pl.pallas_call(
        _nsa_cmp_body,
        out_shape=(jax.ShapeDtypeStruct((batch, n, NSA_KV), BF16), jax.ShapeDtypeStruct((batch, NSA_KV, n), BF16)),
        grid=(batch,),
        in_specs=[xspec, xspec, _const_spec(pk.shape), _const_spec(pv.shape), _const_spec(wkt.shape),
                  _const_spec(wvt.shape), _const_spec((HD, LANES)), _const_spec((HALF, n)), _const_spec((HALF, n))],
        out_specs=(pl.BlockSpec((1, n, NSA_KV), lambda b: (b, 0, 0)), pl.BlockSpec((1, NSA_KV, n), lambda b: (b, 0, 0))),
        compiler_params=_cparams("parallel"),
        name="nsa_cmp",
    )(xk, xv, pk, pv, wkt, wvt, _lane_gain(kn0), cosE, sinE)


NSA_TQ = 256
NSA_TK = 256


def _two_pass_attention(nheads, lo, hi, tk, logits, values, s_ref, m8_ref, m_ref, l8_ref, acc_ref):
    tq = s_ref.shape[-1]
    fold = lambda a: a.reshape(tk // SUBLANES, SUBLANES, tq)
    m8_ref[...] = jnp.full(m8_ref.shape, -jnp.inf, F32)
    l8_ref[...] = jnp.zeros(l8_ref.shape, F32)
    acc_ref[...] = jnp.zeros(acc_ref.shape, F32)

    def logits_chunk(c, carry):
        r0 = pl.multiple_of(c * tk, tk)
        for h in range(nheads):
            g8 = slice(h * SUBLANES, (h + 1) * SUBLANES)
            s = logits(h, r0)
            s_ref[h, pl.ds(r0, tk), :] = s
            m8_ref[g8, :] = jnp.maximum(m8_ref[g8, :], jnp.max(fold(s), axis=0))
        return carry

    lax.fori_loop(lo, hi, logits_chunk, 0)
    for h in range(nheads):
        m_ref[h:h + 1, :] = jnp.max(m8_ref[h * SUBLANES:(h + 1) * SUBLANES, :], axis=0, keepdims=True)

    def pv_chunk(c, carry):
        r0 = pl.multiple_of(c * tk, tk)
        for h in range(nheads):
            g8, hrows = slice(h * SUBLANES, (h + 1) * SUBLANES), slice(h * HD, (h + 1) * HD)
            p = jnp.exp(s_ref[h, pl.ds(r0, tk), :] - m_ref[h:h + 1, :])
            l8_ref[g8, :] = l8_ref[g8, :] + jnp.sum(fold(p), axis=0)
            acc_ref[hrows, :] = acc_ref[hrows, :] + _dot(values(h, r0), p.astype(BF16))
        return carry

    lax.fori_loop(lo, hi, pv_chunk, 0)
    return [acc_ref[h * HD:(h + 1) * HD, :] / jnp.sum(l8_ref[h * SUBLANES:(h + 1) * SUBLANES, :], axis=0, keepdims=True)
            for h in range(nheads)]


def _nsa_attn_body(qT_ref, gT_ref, ks_ref, vsT_ref, kw_ref, vwT_ref, kc_ref, vcT_ref, cov_ref, y_ref,
                   bias_ref, out_ref, s_ref, m8_ref, m_ref, l8_ref, acc_ref, *, seq):
    tq, tk = NSA_TQ, NSA_TK
    qs = pl.program_id(1) * tq
    nkc = (qs + tq) // tk
    wlo = jnp.maximum((qs - WIN) // tk, 0)
    qpos = qs + lax.broadcasted_iota(I32, (1, tq), 1)
    ncr = seq // CMP_S
    nblk = seq // SLC_L
    n_sel = min(SLC_N_MAX, nblk)
    valid_c = CMP_S * lax.broadcasted_iota(I32, (ncr, tq), 0) + (CMP_L - 1) <= qpos
    any_c = jnp.where(qpos >= CMP_L - 1, 1.0, 0.0)
    jrow = lax.broadcasted_iota(I32, (nblk, tq), 0)
    qblk = qpos // SLC_L
    forced = (jrow == 0) | (jrow == qblk) | (jrow == qblk - 1)
    causal_blk = jrow * SLC_L <= qpos
    krow = lax.broadcasted_iota(I32, (tk, tq), 0)
    brow = lax.broadcasted_iota(I32, (SLC_L, tq), 0)
    group = lambda h: h // NSA_R
    pair = lambda h: slice((group(h) // 2) * LANES, (group(h) // 2 + 1) * LANES)
    grows = lambda h: slice(group(h) * HD, (group(h) + 1) * HD)
    q_of = lambda h: _pad_rows(qT_ref[h * HD:(h + 1) * HD, :], group(h) % 2)
    gate = lambda h, branch: gT_ref[h * 3 + branch:h * 3 + branch + 1, :]

    for g in range(NSA_G):
        heads = [g * NSA_R + r for r in range(NSA_R)]
        kcm = kc_ref[0][:, pair(heads[0])]
        vcm = vcT_ref[0][grows(heads[0]), :]
        psum = jnp.zeros((ncr, tq), F32)
        for h in heads:
            s = jnp.where(valid_c, _dot(kcm, q_of(h)), NEG)
            p = jnp.exp(s - jnp.max(s, axis=0, keepdims=True))
            p = p / jnp.sum(p, axis=0, keepdims=True) * any_c
            out_ref[h * HD:(h + 1) * HD, :] = gate(h, 0) * _dot(vcm, p.astype(BF16))
            psum = psum + p
        p_hi = psum.astype(BF16)
        p_lo = (psum - p_hi.astype(F32)).astype(BF16)
        imp = _dot(cov_ref[...], p_hi) + _dot(cov_ref[...], p_lo)
        imp = jnp.where(causal_blk, imp + jnp.where(forced, FORCE, 0.0), NEG)
        rank = jnp.zeros((nblk, tq), I32)
        for j2 in range(nblk):
            rj = imp[j2:j2 + 1]
            beats = (rj > imp) | ((rj == imp) & (j2 < jrow))
            rank = rank + beats.astype(I32)
        selb = jnp.where(rank < n_sel, 0.0, NEG)
        for j in range(nblk):
            bias_ref[g, j * SLC_L:(j + 1) * SLC_L, :] = jnp.where(j * SLC_L + brow <= qpos, selb[j:j + 1], NEG)

    stats = (s_ref, m8_ref, m_ref, l8_ref, acc_ref)
    slc = _two_pass_attention(
        NSA_H, 0, nkc, tk,
        lambda h, r0: _dot(ks_ref[pl.ds(r0, tk), pair(h)], q_of(h)) + bias_ref[group(h), pl.ds(r0, tk), :],
        lambda h, r0: vsT_ref[grows(h), pl.ds(r0, tk)], *stats)
    for h in range(NSA_H):
        out_ref[h * HD:(h + 1) * HD, :] = out_ref[h * HD:(h + 1) * HD, :] + gate(h, 1) * slc[h]

    def win_logits(h, r0):
        dist = qpos - (r0 + krow)
        return jnp.where((dist >= 0) & (dist < WIN), _dot(kw_ref[pl.ds(r0, tk), pair(h)], q_of(h)), NEG)

    win = _two_pass_attention(NSA_H, wlo, nkc, tk, win_logits, lambda h, r0: vwT_ref[grows(h), pl.ds(r0, tk)], *stats)
    outs = [out_ref[h * HD:(h + 1) * HD, :] + gate(h, 2) * win[h] for h in range(NSA_H)]
    y_ref[...] = jnp.concatenate(outs, axis=0).T.astype(BF16)


def _nsa_attn(qT, gT, ks, vsT, kw, vwT, kcmp, vcmpT, batch, seq):
    t = ks.shape[0]
    tq = NSA_TQ
    nq = seq // tq
    ncr, nblk = seq // CMP_S, seq // SLC_L
    starts = np.arange(ncr) * CMP_S
    sstart = np.arange(nblk) * SLC_L
    cover = (starts[None, :] < sstart[:, None] + SLC_L) & (starts[None, :] + CMP_L > sstart[:, None])
    cover[:, ncr - 1] = False
    colq = lambda n: pl.BlockSpec((n, tq), lambda b, i: (0, b * nq + i))
    tok = pl.BlockSpec((seq, NSA_KV), lambda b, i: (b, 0))
    feat = pl.BlockSpec((NSA_KV, seq), lambda b, i: (0, b))
    return pl.pallas_call(
        functools.partial(_nsa_attn_body, seq=seq),
        out_shape=jax.ShapeDtypeStruct((t, NSA_H * HD), BF16),
        grid=(batch, nq),
        in_specs=[
            colq(NSA_H * HD), colq(gT.shape[0]), tok, feat, tok, feat,
            pl.BlockSpec((1, ncr, NSA_KV), lambda b, i: (b, 0, 0)),
            pl.BlockSpec((1, NSA_KV, ncr), lambda b, i: (b, 0, 0)),
            _const_spec((nblk, ncr)),
        ],
        out_specs=pl.BlockSpec((tq, NSA_H * HD), lambda b, i: (b * nq + i, 0)),
        scratch_shapes=[pltpu.VMEM((NSA_G, seq, tq), F32), pltpu.VMEM((NSA_H * HD, tq), F32),
                        pltpu.VMEM((NSA_H, seq, tq), F32), pltpu.VMEM((NSA_H * SUBLANES, tq), F32),
                        pltpu.VMEM((2 * SUBLANES, tq), F32), pltpu.VMEM((NSA_H * SUBLANES, tq), F32),
                        pltpu.VMEM((NSA_H * HD, tq), F32)],
        compiler_params=_cparams("parallel", "arbitrary"),
        name="nsa_attn",
    )(qT, gT, ks, vsT, kw, vwT, kcmp, vcmpT, jnp.asarray(cover, BF16))


def _nsa_layer_mix(x, g, w_in, qn, kn, pos_k, pos_v, wk, wv, mem_qn, batch, seq):
    kc, vc, ks, kw, qT, vsT, vwT, gT, qmT = _nsa_proj(x, g, w_in, qn, kn, mem_qn, seq)
    kcmp, vcmpT = _nsa_cmp(kc, vc, pos_k, pos_v, wk, wv, kn[0], batch, seq)
    return _nsa_attn(qT, gT, ks, vsT, kw, vwT, kcmp, vcmpT, batch, seq), qmT


def kernel(x, mem, ffn_norm, ffn_w_gate, ffn_w_up, ffn_w_down, mix_norm, w_out, mem_norm, mem_w_kv, mem_qn, mem_kn, ret_w_in, dsa_w_in, dsa_qn, dsa_kn, nsa_w_in, nsa_qn, nsa_kn, nsa_cmp_pos_k, nsa_cmp_pos_v, nsa_cmp_wk, nsa_cmp_wv):
    batch, seq, d = x.shape
    x = x.reshape(batch * seq, d)
    for i in range(ffn_norm.shape[0]):
        x = _ffn(x, ffn_norm[i, 0], ffn_w_gate[i, 0], ffn_w_up[i, 0], ffn_w_down[i, 0])
        kind, j = i % N_MIXERS, i // N_MIXERS
        if kind == 0:
            q, v, gate, kT, qmT = _ret_proj(x, mix_norm[i], ret_w_in[j], mem_qn[i], seq)
            y_mix = _retention(q, kT, v, gate, batch, seq)
        elif kind == 1:
            kk, qT, iqT, vT, iwT, qmT = _dsa_proj(x, mix_norm[i], dsa_w_in[j], dsa_qn[j], dsa_kn[j], mem_qn[i], seq)
            y_mix = _dsa_attn(kk, qT, iqT, vT, iwT, batch, seq)
        else:
            y_mix, qmT = _nsa_layer_mix(x, mix_norm[i], nsa_w_in[j], nsa_qn[j], nsa_kn[j], nsa_cmp_pos_k[j],
                                        nsa_cmp_pos_v[j], nsa_cmp_wk[j], nsa_cmp_wv[j], mem_qn[i], batch, seq)
        mem_k, mem_vT = _mem_kv(mem, mem_norm[i], mem_w_kv[i], mem_kn[i])
        y_mem = _mem_attn(qmT, mem_k, mem_vT, seq)
        x = _out_proj(y_mix, y_mem, x, w_out[i])
        x = _ffn(x, ffn_norm[i, 1], ffn_w_gate[i, 1], ffn_w_up[i, 1], ffn_w_down[i, 1])
    return x.reshape(batch, seq, d)
```

```python
import functools
import math

import jax
import jax.numpy as jnp
import numpy as np
from jax import lax
from jax.experimental import pallas as pl
from jax.experimental.pallas import tpu as pltpu

D_MODEL = 1024
HD = 64
HALF = HD // 2
H_MIX = 12
H_MEM = 4
MIX_W = H_MIX * HD
MEM_W = H_MEM * HD
D_FF = 2816
ROPE_THETA = 10000.0
EPS = 1e-6
NEG = -1e30
FORCE = 1e9
SCALE = HD ** -0.5
LOG2E = math.log2(math.e)

RET_H, RET_DK, RET_DV, RET_CHUNK = 6, 64, 128, 128
DSA_H, IDX_H, IDX_D, DSA_TOPK_MAX = 12, 8, 64, 256
NSA_H, NSA_G, CMP_L, CMP_S, SLC_L, SLC_N_MAX, WIN = 12, 4, 32, 16, 64, 16, 512
NSA_R = NSA_H // NSA_G
N_MIXERS = 3

RET_SIZES = [RET_H * RET_DK, RET_H * RET_DK, RET_H * RET_DV, RET_H * RET_DV, MEM_W]
DSA_SIZES = [DSA_H * HD, HD, HD, IDX_H * IDX_D, IDX_D, IDX_H, MEM_W]
NSA_SIZES = [NSA_H * HD] + [NSA_G * HD] * 6 + [NSA_H * 3, MEM_W]

LANES = 128
SUBLANES = 8
MXU_N = 256
VMEM_LIMIT_BYTES = 56 * 1024 * 1024

BF16 = jnp.bfloat16
F32 = jnp.float32
I32 = jnp.int32
I16 = jnp.int16


def _cparams(*sem):
    return pltpu.CompilerParams(dimension_semantics=sem, vmem_limit_bytes=VMEM_LIMIT_BYTES)


def _const_spec(shape):
    n = len(shape)
    return pl.BlockSpec(shape, lambda *_: (0,) * n, pipeline_mode=pl.Buffered(1))


def _rms(x, g):
    return x * lax.rsqrt(jnp.mean(x * x, axis=-1, keepdims=True) + EPS) * g


def _dot(a, b):
    return jnp.dot(a, b, preferred_element_type=F32)


def _dot_nt(a, b):
    return lax.dot_general(a, b, (((1,), (1,)), ((), ())), preferred_element_type=F32)


def _tile_lanes(a, n):
    reps = n // a.shape[-1]
    return a if reps == 1 else jnp.concatenate([a] * reps, axis=-1)


def _rope_tables(pos):
    inv = ROPE_THETA ** (-jnp.arange(HALF, dtype=F32) / HALF)
    ang = pos.astype(F32)[:, None] * inv[None, :]
    return jnp.cos(ang).T, jnp.sin(ang).T


def _lane_gain(g):
    return jnp.broadcast_to(g.astype(F32)[:, None], (g.shape[0], LANES))


def _heads_norm_rope(x, gain, cos, sin):
    t = x.shape[-1]
    x3 = x.reshape(x.shape[0] // HD, HD, t)
    if gain is not None:
        ms = jnp.sum(x3 * x3, axis=1, keepdims=True) * (1.0 / HD)
        x3 = x3 * lax.rsqrt(ms + EPS) * gain[None]
    if cos is not None:
        x1, x2 = x3[:, :HALF], x3[:, HALF:]
        x3 = jnp.concatenate([x1 * cos[None] - x2 * sin[None], x2 * cos[None] + x1 * sin[None]], axis=1)
    return x3.reshape(x.shape)


FFN_TM = 512
FFN_TF = MXU_N


def _swiglu_half_step(x, g_ref, wg_ref, wu_ref, wd_ref):
    h = _rms(x, g_ref[...]).astype(BF16)
    acc = jnp.zeros(x.shape, F32)
    for c in range(D_FF // FFN_TF):
        sl = slice(c * FFN_TF, (c + 1) * FFN_TF)
        gate = _dot(h, wg_ref[:, sl])
        up = _dot(h, wu_ref[:, sl])
        act = (gate * jax.nn.sigmoid(gate) * up).astype(BF16)
        acc = acc + _dot(act, wd_ref[sl, :])
    return x + 0.5 * acc


def _ffn_body(x_ref, g_ref, wg_ref, wu_ref, wd_ref, o_ref):
    o_ref[...] = _swiglu_half_step(x_ref[...], g_ref, wg_ref, wu_ref, wd_ref)


def _mix_out_ffn_body(ymix_ref, ymem_ref, x_ref, wmix_ref, wmem_ref, g_ref, wg_ref, wu_ref, wd_ref, o_ref):
    x = x_ref[...] + _dot(ymix_ref[...], wmix_ref[...]) + _dot(ymem_ref[...], wmem_ref[...])
    o_ref[...] = _swiglu_half_step(x, g_ref, wg_ref, wu_ref, wd_ref)


def _ffn_specs(d):
    return [_const_spec((1, d)), _const_spec((d, D_FF)), _const_spec((d, D_FF)), _const_spec((D_FF, d))]


def _ffn(x, g, wg, wu, wd):
    t, d = x.shape
    tm = min(FFN_TM, t)
    return pl.pallas_call(
        _ffn_body,
        out_shape=jax.ShapeDtypeStruct((t, d), F32),
        grid=(t // tm,),
        in_specs=[pl.BlockSpec((tm, d), lambda i: (i, 0))] + _ffn_specs(d),
        out_specs=pl.BlockSpec((tm, d), lambda i: (i, 0)),
        compiler_params=_cparams("parallel"),
        name="ffn",
    )(x, g.reshape(1, d), wg.astype(BF16), wu.astype(BF16), wd.astype(BF16))


def _mix_out_ffn(y_mix, y_mem, x, w_out, g, wg, wu, wd):
    t, d = x.shape
    tm = min(FFN_TM, t)
    return pl.pallas_call(
        _mix_out_ffn_body,
        out_shape=jax.ShapeDtypeStruct((t, d), F32),
        grid=(t // tm,),
        in_specs=[
            pl.BlockSpec((tm, MIX_W), lambda i: (i, 0)),
            pl.BlockSpec((tm, MEM_W), lambda i: (i, 0)),
            pl.BlockSpec((tm, d), lambda i: (i, 0)),
            _const_spec((MIX_W, d)),
            _const_spec((MEM_W, d)),
        ] + _ffn_specs(d),
        out_specs=pl.BlockSpec((tm, d), lambda i: (i, 0)),
        compiler_params=_cparams("parallel"),
        name="mix_out_ffn",
    )(y_mix, y_mem, x, w_out[:MIX_W].astype(BF16), w_out[MIX_W:].astype(BF16), g.reshape(1, d), wg.astype(BF16),
      wu.astype(BF16), wd.astype(BF16))


PROJ_TM = 512


def _mem_kv_body(m_ref, g_ref, wk_ref, wvT_ref, kn_ref, k_ref, vT_ref):
    h = _rms(m_ref[0], g_ref[...]).astype(BF16)
    kT = _dot_nt(wk_ref[...], h)
    kT = _heads_norm_rope(kT, _tile_lanes(kn_ref[...], kT.shape[-1]), None, None)
    k_ref[0] = kT.T.astype(BF16)
    vT_ref[0] = _dot_nt(wvT_ref[...], h).astype(BF16)


def _mem_kv(mem, g, w_kv, kn):
    b, m, d = mem.shape
    return pl.pallas_call(
        _mem_kv_body,
        out_shape=(jax.ShapeDtypeStruct((b, m, MEM_W), BF16), jax.ShapeDtypeStruct((b, MEM_W, m), BF16)),
        grid=(b,),
        in_specs=[
            pl.BlockSpec((1, m, d), lambda i: (i, 0, 0)),
            _const_spec((1, d)),
            _const_spec((MEM_W, d)),
            _const_spec((MEM_W, d)),
            _const_spec((HD, LANES)),
        ],
        out_specs=(pl.BlockSpec((1, m, MEM_W), lambda i: (i, 0, 0)), pl.BlockSpec((1, MEM_W, m), lambda i: (i, 0, 0))),
        compiler_params=_cparams("parallel"),
        name="mem_kv",
    )(mem, g.reshape(1, d), w_kv[:, :MEM_W].T.astype(BF16), w_kv[:, MEM_W:].T.astype(BF16), _lane_gain(kn))


MEM_TQ = 512


def _pad_rows(blk, half, total=LANES):
    z = jnp.zeros_like(blk)
    parts = [z] * (total // HD)
    parts[half] = blk
    return jnp.concatenate(parts, axis=0)


def _mem_attn_body(qT_ref, k_ref, vT_ref, o_ref):
    qT = qT_ref[...]
    k = k_ref[0]
    vT = vT_ref[0]
    outs = []
    for h in range(H_MEM):
        pair = k[:, (h // 2) * LANES:(h // 2 + 1) * LANES]
        s = _dot(pair, _pad_rows(qT[h * HD:(h + 1) * HD], h % 2))
        m = jnp.max(s, axis=0, keepdims=True)
        p = jnp.exp(s - m)
        l = jnp.sum(p, axis=0, keepdims=True)
        o = _dot(vT[h * HD:(h + 1) * HD], p.astype(BF16))
        outs.append(o / l)
    o_ref[...] = jnp.concatenate(outs, axis=0).T.astype(BF16)


def _mem_attn(qmT, k, vT, seq):
    t = qmT.shape[1]
    b, m, _ = k.shape
    tq = min(MEM_TQ, seq)
    nq = seq // tq
    return pl.pallas_call(
        _mem_attn_body,
        out_shape=jax.ShapeDtypeStruct((t, MEM_W), BF16),
        grid=(b, nq),
        in_specs=[
            pl.BlockSpec((MEM_W, tq), lambda bi, i: (0, bi * nq + i)),
            pl.BlockSpec((1, m, MEM_W), lambda bi, i: (bi, 0, 0)),
            pl.BlockSpec((1, MEM_W, m), lambda bi, i: (bi, 0, 0)),
        ],
        out_specs=pl.BlockSpec((tq, MEM_W), lambda bi, i: (bi * nq + i, 0)),
        compiler_params=_cparams("parallel", "parallel"),
        name="mem_attn",
    )(qmT, k, vT)


def _ret_proj_body(x_ref, g_ref, wtok_ref, wfeat_ref, cosT_ref, sinT_ref, cosF_ref, sinF_ref, mqn_ref,
                   q_ref, v_ref, gate_ref, kT_ref, qmT_ref):
    h = _rms(x_ref[...], g_ref[...]).astype(BF16)
    tok = _dot(h, wtok_ref[...])
    nq, nv = RET_H * RET_DK, RET_H * RET_DV
    q = tok[:, :nq]
    lane = lax.broadcasted_iota(I32, q.shape, 1)
    rot = jnp.where(lane % HD < HALF, pltpu.roll(q, nq - HALF, axis=1), pltpu.roll(q, HALF, axis=1))
    q_ref[...] = (q * _tile_lanes(cosT_ref[...], nq) + rot * _tile_lanes(sinT_ref[...], nq)).astype(BF16)
    v_ref[...] = tok[:, nq:nq + nv].astype(BF16)
    gate_ref[...] = tok[:, nq + nv:]
    feat = _dot_nt(wfeat_ref[...], h)
    tm = feat.shape[-1]
    cos, sin = cosF_ref[...], sinF_ref[...]
    kT = _heads_norm_rope(feat[:nq], None, cos, sin) * (RET_DK ** -0.5)
    kT_ref[...] = kT.astype(BF16)
    qm = _heads_norm_rope(feat[nq:], _tile_lanes(mqn_ref[...], tm), None, None) * SCALE
    qmT_ref[...] = qm.astype(BF16)


def _token_rope_tables(seq):
    cosF, sinF = _rope_tables(jnp.arange(seq))
    cos = jnp.tile(cosF.T, (1, LANES // HALF))
    sign = jnp.where((jnp.arange(LANES) % HD) < HALF, -1.0, 1.0).astype(F32)
    sin = jnp.tile(sinF.T, (1, LANES // HALF)) * sign[None, :]
    return cos, sin


def _ret_proj(x, g, w_in, mem_qn, seq):
    t, d = x.shape
    tm = min(PROJ_TM, seq)
    npos = seq // tm
    wq, wk, wv, wg, wqm = jnp.split(w_in, np.cumsum(RET_SIZES)[:-1].tolist(), axis=1)
    wtok = jnp.concatenate([wq, wv, wg], axis=1).astype(BF16)
    wfeat = jnp.concatenate([wk, wqm], axis=1).T.astype(BF16)
    cosT, sinT = _token_rope_tables(seq)
    cosF, sinF = _rope_tables(jnp.arange(seq))
    nq, nv = RET_H * RET_DK, RET_H * RET_DV
    row = lambda n: pl.BlockSpec((tm, n), lambda i: (i, 0))
    col = lambda n: pl.BlockSpec((n, tm), lambda i: (0, i))
    return pl.pallas_call(
        _ret_proj_body,
        out_shape=(
            jax.ShapeDtypeStruct((t, nq), BF16), jax.ShapeDtypeStruct((t, nv), BF16),
            jax.ShapeDtypeStruct((t, nv), F32), jax.ShapeDtypeStruct((nq, t), BF16),
            jax.ShapeDtypeStruct((MEM_W, t), BF16),
        ),
        grid=(t // tm,),
        in_specs=[
            row(d), _const_spec((1, d)), _const_spec(wtok.shape), _const_spec(wfeat.shape),
            pl.BlockSpec((tm, LANES), lambda i: (i % npos, 0)), pl.BlockSpec((tm, LANES), lambda i: (i % npos, 0)),
            pl.BlockSpec((HALF, tm), lambda i: (0, i % npos)), pl.BlockSpec((HALF, tm), lambda i: (0, i % npos)),
            _const_spec((HD, LANES)),
        ],
        out_specs=(row(nq), row(nv), row(nv), col(nq), col(MEM_W)),
        compiler_params=_cparams("parallel"),
        name="ret_proj",
    )(x, g.reshape(1, d), wtok, wfeat, cosT, sinT, cosF, sinF, _lane_gain(mem_qn))


RET_TL = 512


def _ret_body(q_ref, kT_ref, v_ref, gate_ref, y_ref, state_ref):
    @pl.when(pl.program_id(1) == 0)
    def _():
        state_ref[...] = jnp.zeros_like(state_ref)

    c = RET_CHUNK
    ii = lax.broadcasted_iota(I32, (c, c), 0).astype(F32)
    jj = lax.broadcasted_iota(I32, (c, c), 1).astype(F32)
    diff = ii - jj
    jk = lax.broadcasted_iota(I32, (RET_DK, c), 1).astype(F32)
    for hh in range(RET_H):
        log_g = math.log(1.0 - 2.0 ** (-5.0 - hh))
        dmask = jnp.where(diff >= 0, jnp.exp(log_g * jnp.maximum(diff, 0.0)), 0.0)
        xi = jnp.exp(log_g * (ii + 1.0))
        zeta = jnp.exp(log_g * (c - 1.0 - jk))
        decay = math.exp(log_g * c)
        pair = slice((hh // 2) * LANES, (hh // 2 + 1) * LANES)
        for n in range(q_ref.shape[0] // c):
            rows = slice(n * c, (n + 1) * c)
            q2 = q_ref[rows, pair]
            kT = kT_ref[hh * RET_DK:(hh + 1) * RET_DK, rows]
            v = v_ref[rows, hh * RET_DV:(hh + 1) * RET_DV]
            state = state_ref[hh]
            inner = _dot(q2, _pad_rows(kT, hh % 2)) * dmask
            o = _dot(inner.astype(BF16), v)
            o = o + _dot(q2, _pad_rows(state.astype(BF16), hh % 2)) * xi
            kv = _dot((kT.astype(F32) * zeta).astype(BF16), v)
            state_ref[hh] = state * decay + kv
            mu = jnp.mean(o, axis=-1, keepdims=True)
            var = jnp.mean(jnp.square(o - mu), axis=-1, keepdims=True)
            o = (o - mu) * lax.rsqrt(var + EPS)
            gte = gate_ref[rows, hh * RET_DV:(hh + 1) * RET_DV]
            y_ref[rows, hh * RET_DV:(hh + 1) * RET_DV] = (gte * jax.nn.sigmoid(gte) * o).astype(BF16)


def _retention(q, kT, v, gate, batch, seq):
    t = q.shape[0]
    tl = min(RET_TL, seq)
    nl = seq // tl
    nq, nv = RET_H * RET_DK, RET_H * RET_DV
    return pl.pallas_call(
        _ret_body,
        out_shape=jax.ShapeDtypeStruct((t, nv), BF16),
        grid=(batch, nl),
        in_specs=[
            pl.BlockSpec((tl, nq), lambda b, i: (b * nl + i, 0)),
            pl.BlockSpec((nq, tl), lambda b, i: (0, b * nl + i)),
            pl.BlockSpec((tl, nv), lambda b, i: (b * nl + i, 0)),
            pl.BlockSpec((tl, nv), lambda b, i: (b * nl + i, 0)),
        ],
        out_specs=pl.BlockSpec((tl, nv), lambda b, i: (b * nl + i, 0)),
        scratch_shapes=[pltpu.VMEM((RET_H, RET_DK, RET_DV), F32)],
        compiler_params=_cparams("parallel", "arbitrary"),
        name="retention",
    )(q, kT, v, gate)


def _dsa_proj_body(x_ref, g_ref, w_ref, cos_ref, sin_ref, qn_ref, kn_ref, mqn_ref,
                   kk_ref, qT_ref, iqT_ref, vT_ref, iwT_ref, qmT_ref):
    h = _rms(x_ref[...], g_ref[...]).astype(BF16)
    feat = _dot_nt(w_ref[...], h)
    tm = feat.shape[-1]
    cos, sin = cos_ref[...], sin_ref[...]
    o = 0
    q = feat[o:o + DSA_H * HD]; o += DSA_H * HD
    iq = feat[o:o + IDX_H * IDX_D]; o += IDX_H * IDX_D
    k = feat[o:o + HD]; o += HD
    ik = feat[o:o + IDX_D]; o += IDX_D
    v = feat[o:o + HD]; o += HD
    qm = feat[o:o + MEM_W]; o += MEM_W
    iw = feat[o:o + IDX_H]
    qT_ref[...] = (_heads_norm_rope(q, _tile_lanes(qn_ref[...], tm), cos, sin) * SCALE).astype(BF16)
    iqT_ref[...] = _heads_norm_rope(iq, None, cos, sin).astype(BF16)
    k = _heads_norm_rope(k, _tile_lanes(kn_ref[...], tm), cos, sin)
    ik = _heads_norm_rope(ik, None, cos, sin)
    kk_ref[...] = jnp.concatenate([k, ik], axis=0).T.astype(BF16)
    vT_ref[...] = v.astype(BF16)
    iwT_ref[...] = iw
    qmT_ref[...] = (_heads_norm_rope(qm, _tile_lanes(mqn_ref[...], tm), None, None) * SCALE).astype(BF16)


def _dsa_proj(x, g, w_in, qn, kn, mem_qn, seq):
    t, d = x.shape
    tm = min(PROJ_TM, seq)
    npos = seq // tm
    wq, wk, wv, wiq, wik, wiw, wqm = jnp.split(w_in, np.cumsum(DSA_SIZES)[:-1].tolist(), axis=1)
    wfeat = jnp.concatenate([wq, wiq, wk, wik, wv, wqm, wiw], axis=1).T.astype(BF16)
    cosF, sinF = _rope_tables(jnp.arange(seq))
    col = lambda n: pl.BlockSpec((n, tm), lambda i: (0, i))
    tab = pl.BlockSpec((HALF, tm), lambda i: (0, i % npos))
    gain = _const_spec((HD, LANES))
    return pl.pallas_call(
        _dsa_proj_body,
        out_shape=(
            jax.ShapeDtypeStruct((t, LANES), BF16), jax.ShapeDtypeStruct((DSA_H * HD, t), BF16),
            jax.ShapeDtypeStruct((IDX_H * IDX_D, t), BF16), jax.ShapeDtypeStruct((HD, t), BF16),
            jax.ShapeDtypeStruct((IDX_H, t), F32), jax.ShapeDtypeStruct((MEM_W, t), BF16),
        ),
        grid=(t // tm,),
        in_specs=[pl.BlockSpec((tm, d), lambda i: (i, 0)), _const_spec((1, d)), _const_spec(wfeat.shape),
                  tab, tab, gain, gain, gain],
        out_specs=(pl.BlockSpec((tm, LANES), lambda i: (i, 0)), col(DSA_H * HD), col(IDX_H * IDX_D), col(HD),
                   col(IDX_H), col(MEM_W)),
        compiler_params=_cparams("parallel"),
        name="dsa_proj",
    )(x, g.reshape(1, d), wfeat, cosF, sinF, _lane_gain(qn), _lane_gain(kn), _lane_gain(mem_qn))


DSA_TQ = 256
DSA_TK = 256
HALF_BITS = 16
INT16_MIN = -2 ** (HALF_BITS - 1)
PACKED_SUBLANES = 2 * SUBLANES


def _dsa_attn_body(kk_ref, qT_ref, iqT_ref, vT_ref, iwT_ref, y_ref, keys_ref, hi_ref, lo_ref, bias_ref, s_ref, m8_ref,
                   m_ref, l8_ref, acc_ref, *, topk, seq):
    tq, tk = DSA_TQ, DSA_TK
    qs = pl.program_id(1) * tq
    nkc = (qs + tq) // tk
    qpos = qs + lax.broadcasted_iota(I32, (1, tq), 1)
    row = lax.broadcasted_iota(I32, (tk, tq), 0)

    def score_chunk(c, carry):
        r0 = pl.multiple_of(c * tk, tk)
        kkc = kk_ref[pl.ds(r0, tk), :]
        acc = jnp.zeros((tk, tq), F32)
        for h in range(IDX_H):
            r = _dot(kkc, _pad_rows(iqT_ref[h * IDX_D:(h + 1) * IDX_D, :], 1))
            acc = acc + jnp.maximum(r, 0.0) * iwT_ref[h:h + 1, :]
        sc = jnp.where(r0 + row <= qpos, acc + 0.0, -jnp.inf)
        bits = pltpu.bitcast(sc, I32)
        key = jnp.where(bits < 0, bits ^ jnp.int32(0x7FFFFFFF), bits)
        keys_ref[pl.ds(r0, tk), :] = key
        hi_ref[pl.ds(r0, tk), :] = lax.shift_right_arithmetic(key, HALF_BITS).astype(I16)
        return carry

    lax.fori_loop(0, nkc, score_chunk, 0)

    def count(pred):
        def body(c, acc):
            r0 = pl.multiple_of(c * tk, tk)
            m = pred(keys_ref[pl.ds(r0, tk), :], r0 + row)
            return acc + jnp.sum(m.astype(I32).reshape(tk // SUBLANES, SUBLANES, tq), axis=0)
        acc = lax.fori_loop(0, nkc, body, jnp.zeros((SUBLANES, tq), I32))
        return jnp.sum(acc, axis=0, keepdims=True)

    def count16(ref, pred):
        def body(c, acc):
            r0 = pl.multiple_of(c * tk, tk)
            ones = jnp.where(pred(ref[pl.ds(r0, tk), :]), jnp.int16(1), jnp.int16(0))
            parts = [ones[i:i + PACKED_SUBLANES] for i in range(0, tk, PACKED_SUBLANES)]
            while len(parts) > 1:
                parts = [a + b for a, b in zip(parts[::2], parts[1::2])]
            return acc + parts[0]
        acc = lax.fori_loop(0, nkc, body, jnp.zeros((PACKED_SUBLANES, tq), I16))
        return jnp.sum(acc.astype(I32), axis=0, keepdims=True)

    def kth_largest16(ref, k):
        thr = jnp.where(count16(ref, lambda v: v >= 0) >= k, jnp.int32(0), jnp.int32(INT16_MIN))

        def bit_step(b, thr):
            cand = thr | lax.shift_left(jnp.int32(1), (HALF_BITS - 2) - b)
            c16 = cand.astype(I16)
            return jnp.where(count16(ref, lambda v: v >= c16) >= k, cand, thr)

        return lax.fori_loop(0, HALF_BITS - 1, bit_step, thr)

    thr_hi = kth_largest16(hi_ref, topk)
    thr_hi16 = thr_hi.astype(I16)
    need_lo = topk - count16(hi_ref, lambda v: v > thr_hi16)

    def low_half_chunk(c, carry):
        r0 = pl.multiple_of(c * tk, tk)
        lo = ((keys_ref[pl.ds(r0, tk), :] & jnp.int32(0xFFFF)) + jnp.int32(INT16_MIN)).astype(I16)
        lo_ref[pl.ds(r0, tk), :] = jnp.where(hi_ref[pl.ds(r0, tk), :] == thr_hi16, lo, jnp.int16(INT16_MIN))
        return carry

    lax.fori_loop(0, nkc, low_half_chunk, 0)
    thr_lo = kth_largest16(lo_ref, need_lo)
    thr = lax.shift_left(thr_hi, HALF_BITS) | (thr_lo - jnp.int32(INT16_MIN))
    def tie_search():
        need = topk - count(lambda k, _: k > thr)

        def idx_step(b, q):
            cand = q | lax.shift_left(jnp.int32(1), (seq.bit_length() - 2) - b)
            return jnp.where(count(lambda k, idx: (k == thr) & (idx < cand)) < need, cand, q)

        return lax.fori_loop(0, seq.bit_length() - 1, idx_step, jnp.zeros((1, tq), I32))

    has_ties = jnp.max(count(lambda k, _: k >= thr)) > topk
    last = lax.cond(has_ties, tie_search, lambda: jnp.full((1, tq), seq, I32))

    def bias_chunk(c, carry):
        r0 = pl.multiple_of(c * tk, tk)
        k = keys_ref[pl.ds(r0, tk), :]
        idx = r0 + row
        sel = (k > thr) | ((k == thr) & (idx <= last))
        bias_ref[pl.ds(r0, tk), :] = jnp.where(sel & (idx <= qpos), 0.0, NEG)
        return carry

    lax.fori_loop(0, nkc, bias_chunk, 0)

    outs = _two_pass_attention(
        DSA_H, 0, nkc, tk,
        lambda h, r0: _dot(kk_ref[pl.ds(r0, tk), :], _pad_rows(qT_ref[h * HD:(h + 1) * HD, :], 0))
        + bias_ref[pl.ds(r0, tk), :],
        lambda h, r0: vT_ref[:, pl.ds(r0, tk)], s_ref, m8_ref, m_ref, l8_ref, acc_ref)
    y_ref[...] = jnp.concatenate(outs, axis=0).T.astype(BF16)


def _dsa_attn(kk, qT, iqT, vT, iwT, batch, seq):
    t = kk.shape[0]
    tq = DSA_TQ
    nq = seq // tq
    topk = min(DSA_TOPK_MAX, seq // 4)
    colq = lambda n: pl.BlockSpec((n, tq), lambda b, i: (0, b * nq + i))
    return pl.pallas_call(
        functools.partial(_dsa_attn_body, topk=topk, seq=seq),
        out_shape=jax.ShapeDtypeStruct((t, DSA_H * HD), BF16),
        grid=(batch, nq),
        in_specs=[
            pl.BlockSpec((seq, LANES), lambda b, i: (b, 0)),
            colq(DSA_H * HD), colq(IDX_H * IDX_D),
            pl.BlockSpec((HD, seq), lambda b, i: (0, b)),
            colq(IDX_H),
        ],
        out_specs=pl.BlockSpec((tq, DSA_H * HD), lambda b, i: (b * nq + i, 0)),
        scratch_shapes=[pltpu.VMEM((seq, tq), I32), pltpu.VMEM((seq, tq), I16), pltpu.VMEM((seq, tq), I16),
                        pltpu.VMEM((seq, tq), F32),
                        pltpu.VMEM((DSA_H, seq, tq), F32), pltpu.VMEM((DSA_H * SUBLANES, tq), F32),
                        pltpu.VMEM((2 * SUBLANES, tq), F32), pltpu.VMEM((DSA_H * SUBLANES, tq), F32),
                        pltpu.VMEM((DSA_H * HD, tq), F32)],
        compiler_params=_cparams("parallel", "arbitrary"),
        name="dsa_attn",
    )(kk, qT, iqT, vT, iwT)


NSA_KV = NSA_G * HD
NSA_GATES = NSA_H * 3
NSA_GATES_PAD = -NSA_GATES % SUBLANES


def _nsa_proj_body(x_ref, g_ref, wtok_ref, wfeat_ref, cos_ref, sin_ref, qn_ref, kns_ref, knw_ref, mqn_ref,
                   kc_ref, vc_ref, ks_ref, kw_ref, qT_ref, vsT_ref, vwT_ref, gT_ref, qmT_ref):
    h = _rms(x_ref[...], g_ref[...]).astype(BF16)
    tok = _dot(h, wtok_ref[...])
    kc_ref[...] = tok[:, :NSA_KV]
    vc_ref[...] = tok[:, NSA_KV:]
    feat = _dot_nt(wfeat_ref[...], h)
    tm = feat.shape[-1]
    cos, sin = cos_ref[...], sin_ref[...]
    o = 0
    q = feat[o:o + NSA_H * HD]; o += NSA_H * HD
    ks = feat[o:o + NSA_KV]; o += NSA_KV
    kw = feat[o:o + NSA_KV]; o += NSA_KV
    vs = feat[o:o + NSA_KV]; o += NSA_KV
    vw = feat[o:o + NSA_KV]; o += NSA_KV
    qm = feat[o:o + MEM_W]; o += MEM_W
    gates = feat[o:]
    qT_ref[...] = (_heads_norm_rope(q, _tile_lanes(qn_ref[...], tm), cos, sin) * SCALE).astype(BF16)
    ks_ref[...] = _heads_norm_rope(ks, _tile_lanes(kns_ref[...], tm), cos, sin).T.astype(BF16)
    kw_ref[...] = _heads_norm_rope(kw, _tile_lanes(knw_ref[...], tm), cos, sin).T.astype(BF16)
    vsT_ref[...] = vs.astype(BF16)
    vwT_ref[...] = vw.astype(BF16)
    gT_ref[...] = jax.nn.sigmoid(gates)
    qmT_ref[...] = (_heads_norm_rope(qm, _tile_lanes(mqn_ref[...], tm), None, None) * SCALE).astype(BF16)


def _nsa_proj(x, g, w_in, qn, kn, mem_qn, seq):
    t, d = x.shape
    tm = min(PROJ_TM, seq)
    npos = seq // tm
    wq, wkc, wvc, wks, wvs, wkw, wvw, wgt, wqm = jnp.split(w_in, np.cumsum(NSA_SIZES)[:-1].tolist(), axis=1)
    wtok = jnp.concatenate([wkc, wvc], axis=1).astype(BF16)
    wgt = jnp.pad(wgt, ((0, 0), (0, NSA_GATES_PAD)))
    wfeat = jnp.concatenate([wq, wks, wkw, wvs, wvw, wqm, wgt], axis=1).T.astype(BF16)
    cosF, sinF = _rope_tables(jnp.arange(seq))
    row = lambda n: pl.BlockSpec((tm, n), lambda i: (i, 0))
    col = lambda n: pl.BlockSpec((n, tm), lambda i: (0, i))
    tab = pl.BlockSpec((HALF, tm), lambda i: (0, i % npos))
    gain = _const_spec((HD, LANES))
    ngt = NSA_GATES + NSA_GATES_PAD
    return pl.pallas_call(
        _nsa_proj_body,
        out_shape=(
            jax.ShapeDtypeStruct((t, NSA_KV), F32), jax.ShapeDtypeStruct((t, NSA_KV), F32),
            jax.ShapeDtypeStruct((t, NSA_KV), BF16), jax.ShapeDtypeStruct((t, NSA_KV), BF16),
            jax.ShapeDtypeStruct((NSA_H * HD, t), BF16), jax.ShapeDtypeStruct((NSA_KV, t), BF16),
            jax.ShapeDtypeStruct((NSA_KV, t), BF16), jax.ShapeDtypeStruct((ngt, t), F32),
            jax.ShapeDtypeStruct((MEM_W, t), BF16),
        ),
        grid=(t // tm,),
        in_specs=[row(d), _const_spec((1, d)), _const_spec(wtok.shape), _const_spec(wfeat.shape),
                  tab, tab, gain, gain, gain, gain],
        out_specs=(row(NSA_KV), row(NSA_KV), row(NSA_KV), row(NSA_KV), col(NSA_H * HD), col(NSA_KV), col(NSA_KV),
                   col(ngt), col(MEM_W)),
        compiler_params=_cparams("parallel"),
        name="nsa_proj",
    )(x, g.reshape(1, d), wtok, wfeat, cosF, sinF, _lane_gain(qn), _lane_gain(kn[1]), _lane_gain(kn[2]),
      _lane_gain(mem_qn))


def _nsa_cmp_body(xk_ref, xv_ref, pk_ref, pv_ref, wk_ref, wv_ref, kn_ref, cos_ref, sin_ref, k_ref, vT_ref):
    def compress(x, pos, w_ref):
        n = x.shape[0]
        xa = (x + pos[0:1]).astype(BF16)
        xb = (pltpu.roll(x, n - 1, axis=0) + pos[1:2]).astype(BF16)
        return _dot_nt(w_ref[0], xa) + _dot_nt(w_ref[1], xb)

    kT = compress(xk_ref[0], pk_ref[...], wk_ref)
    kT = _heads_norm_rope(kT, _tile_lanes(kn_ref[...], kT.shape[-1]), cos_ref[...], sin_ref[...])
    k_ref[0] = kT.T.astype(BF16)
    vT_ref[0] = compress(xv_ref[0], pv_ref[...], wv_ref).astype(BF16)


def _nsa_cmp_weights(w, pos):
    eye = jnp.eye(NSA_G, dtype=F32)
    halves = w.reshape(CMP_L // CMP_S, CMP_S, HD, HD)
    wt = jnp.einsum('hg,alde->ahelgd', eye, halves).reshape(CMP_L // CMP_S, NSA_KV, CMP_S * NSA_KV)
    p = jnp.broadcast_to(pos.reshape(CMP_L // CMP_S, CMP_S, 1, HD), (CMP_L // CMP_S, CMP_S, NSA_G, HD))
    return wt.astype(BF16), p.reshape(CMP_L // CMP_S, CMP_S * NSA_KV).astype(F32)


def _nsa_cmp(kc, vc, pos_k, pos_v, wk, wv, kn0, batch, seq):
    n = seq // CMP_S
    width = CMP_S * NSA_KV
    xk = kc.reshape(batch, n, width)
    xv = vc.reshape(batch, n, width)
    wkt, pk = _nsa_cmp_weights(wk, pos_k)
    wvt, pv = _nsa_cmp_weights(wv, pos_v)
    cosE, sinE = _rope_tables(jnp.arange(n) * CMP_S + (CMP_L - 1))
    xspec = pl.BlockSpec((1, n, width), lambda b: (b, 0, 0))
    return pl.pallas_call(
        _nsa_cmp_body,
        out_shape=(jax.ShapeDtypeStruct((batch, n, NSA_KV), BF16), jax.ShapeDtypeStruct((batch, NSA_KV, n), BF16)),
        grid=(batch,),
        in_specs=[xspec, xspec, _const_spec(pk.shape), _const_spec(pv.shape), _const_spec(wkt.shape),
                  _const_spec(wvt.shape), _const_spec((HD, LANES)), _const_spec((HALF, n)), _const_spec((HALF, n))],
        out_specs=(pl.BlockSpec((1, n, NSA_KV), lambda b: (b, 0, 0)), pl.BlockSpec((1, NSA_KV, n), lambda b: (b, 0, 0))),
        compiler_params=_cparams("parallel"),
        name="nsa_cmp",
    )(xk, xv, pk, pv, wkt, wvt, _lane_gain(kn0), cosE, sinE)


NSA_TQ = 256
NSA_TK = 256


def _two_pass_attention(nheads, lo, hi, tk, logits, values, s_ref, m8_ref, m_ref, l8_ref, acc_ref):
    tq = s_ref.shape[-1]
    fold = lambda a: a.reshape(tk // SUBLANES, SUBLANES, tq)
    m8_ref[...] = jnp.full(m8_ref.shape, -jnp.inf, F32)
    l8_ref[...] = jnp.zeros(l8_ref.shape, F32)
    acc_ref[...] = jnp.zeros(acc_ref.shape, F32)

    def logits_chunk(c, carry):
        r0 = pl.multiple_of(c * tk, tk)
        for h in range(nheads):
            g8 = slice(h * SUBLANES, (h + 1) * SUBLANES)
            s = logits(h, r0) * LOG2E
            s_ref[h, pl.ds(r0, tk), :] = s
            m8_ref[g8, :] = jnp.maximum(m8_ref[g8, :], jnp.max(fold(s), axis=0))
        return carry

    lax.fori_loop(lo, hi, logits_chunk, 0)
    for h in range(nheads):
        m_ref[h:h + 1, :] = jnp.max(m8_ref[h * SUBLANES:(h + 1) * SUBLANES, :], axis=0, keepdims=True)

    def pv_chunk(c, carry):
        r0 = pl.multiple_of(c * tk, tk)
        for h in range(nheads):
            g8, hrows = slice(h * SUBLANES, (h + 1) * SUBLANES), slice(h * HD, (h + 1) * HD)
            p = jnp.exp2(s_ref[h, pl.ds(r0, tk), :] - m_ref[h:h + 1, :])
            l8_ref[g8, :] = l8_ref[g8, :] + jnp.sum(fold(p), axis=0)
            acc_ref[hrows, :] = acc_ref[hrows, :] + _dot(values(h, r0), p.astype(BF16))
        return carry

    lax.fori_loop(lo, hi, pv_chunk, 0)
    return [acc_ref[h * HD:(h + 1) * HD, :] / jnp.sum(l8_ref[h * SUBLANES:(h + 1) * SUBLANES, :], axis=0, keepdims=True)
            for h in range(nheads)]


def _nsa_attn_body(qT_ref, gT_ref, ks_ref, vsT_ref, kw_ref, vwT_ref, kc_ref, vcT_ref, cov_ref, y_ref,
                   bias_ref, out_ref, s_ref, m8_ref, m_ref, l8_ref, acc_ref, *, seq):
    tq, tk = NSA_TQ, NSA_TK
    qs = pl.program_id(1) * tq
    nkc = (qs + tq) // tk
    wlo = jnp.maximum((qs - WIN) // tk, 0)
    qpos = qs + lax.broadcasted_iota(I32, (1, tq), 1)
    ncr = seq // CMP_S
    nblk = seq // SLC_L
    n_sel = min(SLC_N_MAX, nblk)
    valid_c = CMP_S * lax.broadcasted_iota(I32, (ncr, tq), 0) + (CMP_L - 1) <= qpos
    any_c = jnp.where(qpos >= CMP_L - 1, 1.0, 0.0)
    jrow = lax.broadcasted_iota(I32, (nblk, tq), 0)
    qblk = qpos // SLC_L
    forced = (jrow == 0) | (jrow == qblk) | (jrow == qblk - 1)
    causal_blk = jrow * SLC_L <= qpos
    krow = lax.broadcasted_iota(I32, (tk, tq), 0)
    brow = lax.broadcasted_iota(I32, (SLC_L, tq), 0)
    group = lambda h: h // NSA_R
    pair = lambda h: slice((group(h) // 2) * LANES, (group(h) // 2 + 1) * LANES)
    grows = lambda h: slice(group(h) * HD, (group(h) + 1) * HD)
    q_of = lambda h: _pad_rows(qT_ref[h * HD:(h + 1) * HD, :], group(h) % 2)
    gate = lambda h, branch: gT_ref[h * 3 + branch:h * 3 + branch + 1, :]

    for g in range(NSA_G):
        heads = [g * NSA_R + r for r in range(NSA_R)]
        kcm = kc_ref[0][:, pair(heads[0])]
        vcm = vcT_ref[0][grows(heads[0]), :]
        psum = jnp.zeros((ncr, tq), F32)
        for h in heads:
            s = jnp.where(valid_c, _dot(kcm, q_of(h)), NEG)
            p = jnp.exp(s - jnp.max(s, axis=0, keepdims=True))
            p = p / jnp.sum(p, axis=0, keepdims=True) * any_c
            out_ref[h * HD:(h + 1) * HD, :] = gate(h, 0) * _dot(vcm, p.astype(BF16))
            psum = psum + p
        p_hi = psum.astype(BF16)
        p_lo = (psum - p_hi.astype(F32)).astype(BF16)
        imp = _dot(cov_ref[...], p_hi) + _dot(cov_ref[...], p_lo)
        imp = jnp.where(causal_blk, imp + jnp.where(forced, FORCE, 0.0), NEG)
        rank = jnp.zeros((nblk, tq), I32)
        for j2 in range(nblk):
            rj = imp[j2:j2 + 1]
            beats = (rj > imp) | ((rj == imp) & (j2 < jrow))
            rank = rank + beats.astype(I32)
        selb = jnp.where(rank < n_sel, 0.0, NEG)
        for j in range(nblk):
            bias_ref[g, j * SLC_L:(j + 1) * SLC_L, :] = jnp.where(j * SLC_L + brow <= qpos, selb[j:j + 1], NEG)

    stats = (s_ref, m8_ref, m_ref, l8_ref, acc_ref)
    slc = _two_pass_attention(
        NSA_H, 0, nkc, tk,
        lambda h, r0: _dot(ks_ref[pl.ds(r0, tk), pair(h)], q_of(h)) + bias_ref[group(h), pl.ds(r0, tk), :],
        lambda h, r0: vsT_ref[grows(h), pl.ds(r0, tk)], *stats)
    for h in range(NSA_H):
        out_ref[h * HD:(h + 1) * HD, :] = out_ref[h * HD:(h + 1) * HD, :] + gate(h, 1) * slc[h]

    def win_logits(h, r0):
        dist = qpos - (r0 + krow)
        return jnp.where((dist >= 0) & (dist < WIN), _dot(kw_ref[pl.ds(r0, tk), pair(h)], q_of(h)), NEG)

    win = _two_pass_attention(NSA_H, wlo, nkc, tk, win_logits, lambda h, r0: vwT_ref[grows(h), pl.ds(r0, tk)], *stats)
    outs = [out_ref[h * HD:(h + 1) * HD, :] + gate(h, 2) * win[h] for h in range(NSA_H)]
    y_ref[...] = jnp.concatenate(outs, axis=0).T.astype(BF16)


def _nsa_attn(qT, gT, ks, vsT, kw, vwT, kcmp, vcmpT, batch, seq):
    t = ks.shape[0]
    tq = NSA_TQ
    nq = seq // tq
    ncr, nblk = seq // CMP_S, seq // SLC_L
    starts = np.arange(ncr) * CMP_S
    sstart = np.arange(nblk) * SLC_L
    cover = (starts[None, :] < sstart[:, None] + SLC_L) & (starts[None, :] + CMP_L > sstart[:, None])
    cover[:, ncr - 1] = False
    colq = lambda n: pl.BlockSpec((n, tq), lambda b, i: (0, b * nq + i))
    tok = pl.BlockSpec((seq, NSA_KV), lambda b, i: (b, 0))
    feat = pl.BlockSpec((NSA_KV, seq), lambda b, i: (0, b))
    return pl.pallas_call(
        functools.partial(_nsa_attn_body, seq=seq),
        out_shape=jax.ShapeDtypeStruct((t, NSA_H * HD), BF16),
        grid=(batch, nq),
        in_specs=[
            colq(NSA_H * HD), colq(gT.shape[0]), tok, feat, tok, feat,
            pl.BlockSpec((1, ncr, NSA_KV), lambda b, i: (b, 0, 0)),
            pl.BlockSpec((1, NSA_KV, ncr), lambda b, i: (b, 0, 0)),
            _const_spec((nblk, ncr)),
        ],
        out_specs=pl.BlockSpec((tq, NSA_H * HD), lambda b, i: (b * nq + i, 0)),
        scratch_shapes=[pltpu.VMEM((NSA_G, seq, tq), F32), pltpu.VMEM((NSA_H * HD, tq), F32),
                        pltpu.VMEM((NSA_H, seq, tq), F32), pltpu.VMEM((NSA_H * SUBLANES, tq), F32),
                        pltpu.VMEM((2 * SUBLANES, tq), F32), pltpu.VMEM((NSA_H * SUBLANES, tq), F32),
                        pltpu.VMEM((NSA_H * HD, tq), F32)],
        compiler_params=_cparams("parallel", "arbitrary"),
        name="nsa_attn",
    )(qT, gT, ks, vsT, kw, vwT, kcmp, vcmpT, jnp.asarray(cover, BF16))


def _nsa_layer_mix(x, g, w_in, qn, kn, pos_k, pos_v, wk, wv, mem_qn, batch, seq):
    kc, vc, ks, kw, qT, vsT, vwT, gT, qmT = _nsa_proj(x, g, w_in, qn, kn, mem_qn, seq)
    kcmp, vcmpT = _nsa_cmp(kc, vc, pos_k, pos_v, wk, wv, kn[0], batch, seq)
    return _nsa_attn(qT, gT, ks, vsT, kw, vwT, kcmp, vcmpT, batch, seq), qmT


def kernel(x, mem, ffn_norm, ffn_w_gate, ffn_w_up, ffn_w_down, mix_norm, w_out, mem_norm, mem_w_kv, mem_qn, mem_kn, ret_w_in, dsa_w_in, dsa_qn, dsa_kn, nsa_w_in, nsa_qn, nsa_kn, nsa_cmp_pos_k, nsa_cmp_pos_v, nsa_cmp_wk, nsa_cmp_wv):
    batch, seq, d = x.shape
    x = x.reshape(batch * seq, d)
    for i in range(ffn_norm.shape[0]):
        x = _ffn(x, ffn_norm[i, 0], ffn_w_gate[i, 0], ffn_w_up[i, 0], ffn_w_down[i, 0])
        kind, j = i % N_MIXERS, i // N_MIXERS
        if kind == 0:
            q, v, gate, kT, qmT = _ret_proj(x, mix_norm[i], ret_w_in[j], mem_qn[i], seq)
            y_mix = _retention(q, kT, v, gate, batch, seq)
        elif kind == 1:
            kk, qT, iqT, vT, iwT, qmT = _dsa_proj(x, mix_norm[i], dsa_w_in[j], dsa_qn[j], dsa_kn[j], mem_qn[i], seq)
            y_mix = _dsa_attn(kk, qT, iqT, vT, iwT, batch, seq)
        else:
            y_mix, qmT = _nsa_layer_mix(x, mix_norm[i], nsa_w_in[j], nsa_qn[j], nsa_kn[j], nsa_cmp_pos_k[j],
                                        nsa_cmp_pos_v[j], nsa_cmp_wk[j], nsa_cmp_wv[j], mem_qn[i], batch, seq)
        mem_k, mem_vT = _mem_kv(mem, mem_norm[i], mem_w_kv[i], mem_kn[i])
        y_mem = _mem_attn(qmT, mem_k, mem_vT, seq)
        x = _mix_out_ffn(y_mix, y_mem, x, w_out[i], ffn_norm[i, 1], ffn_w_gate[i, 1], ffn_w_up[i, 1], ffn_w_down[i, 1])
    return x.reshape(batch, seq, d)
```

```python
import functools
import math

import jax
import jax.numpy as jnp
import numpy as np
from jax import lax
from jax.experimental import pallas as pl
from jax.experimental.pallas import tpu as pltpu

D_MODEL = 1024
HD = 64
HALF = HD // 2
H_MIX = 12
H_MEM = 4
MIX_W = H_MIX * HD
MEM_W = H_MEM * HD
D_FF = 2816
ROPE_THETA = 10000.0
EPS = 1e-6
NEG = -1e30
FORCE = 1e9
SCALE = HD ** -0.5
LOG2E = math.log2(math.e)

RET_H, RET_DK, RET_DV, RET_CHUNK = 6, 64, 128, 128
DSA_H, IDX_H, IDX_D, DSA_TOPK_MAX = 12, 8, 64, 256
NSA_H, NSA_G, CMP_L, CMP_S, SLC_L, SLC_N_MAX, WIN = 12, 4, 32, 16, 64, 16, 512
NSA_R = NSA_H // NSA_G
N_MIXERS = 3

RET_SIZES = [RET_H * RET_DK, RET_H * RET_DK, RET_H * RET_DV, RET_H * RET_DV, MEM_W]
DSA_SIZES = [DSA_H * HD, HD, HD, IDX_H * IDX_D, IDX_D, IDX_H, MEM_W]
NSA_SIZES = [NSA_H * HD] + [NSA_G * HD] * 6 + [NSA_H * 3, MEM_W]

LANES = 128
SUBLANES = 8
MXU_N = 256
VMEM_LIMIT_BYTES = 56 * 1024 * 1024

BF16 = jnp.bfloat16
F32 = jnp.float32
I32 = jnp.int32
I16 = jnp.int16


def _cparams(*sem):
    return pltpu.CompilerParams(dimension_semantics=sem, vmem_limit_bytes=VMEM_LIMIT_BYTES)


def _const_spec(shape):
    n = len(shape)
    return pl.BlockSpec(shape, lambda *_: (0,) * n, pipeline_mode=pl.Buffered(1))


def _rms(x, g):
    return x * lax.rsqrt(jnp.mean(x * x, axis=-1, keepdims=True) + EPS) * g


def _dot(a, b):
    return jnp.dot(a, b, preferred_element_type=F32)


def _dot_nt(a, b):
    return lax.dot_general(a, b, (((1,), (1,)), ((), ())), preferred_element_type=F32)


def _tile_lanes(a, n):
    reps = n // a.shape[-1]
    return a if reps == 1 else jnp.concatenate([a] * reps, axis=-1)


def _rope_tables(pos):
    inv = ROPE_THETA ** (-jnp.arange(HALF, dtype=F32) / HALF)
    ang = pos.astype(F32)[:, None] * inv[None, :]
    return jnp.cos(ang).T, jnp.sin(ang).T


def _lane_gain(g):
    return jnp.broadcast_to(g.astype(F32)[:, None], (g.shape[0], LANES))


def _heads_norm_rope(x, gain, cos, sin):
    t = x.shape[-1]
    x3 = x.reshape(x.shape[0] // HD, HD, t)
    if gain is not None:
        ms = jnp.sum(x3 * x3, axis=1, keepdims=True) * (1.0 / HD)
        x3 = x3 * lax.rsqrt(ms + EPS) * gain[None]
    if cos is not None:
        x1, x2 = x3[:, :HALF], x3[:, HALF:]
        x3 = jnp.concatenate([x1 * cos[None] - x2 * sin[None], x2 * cos[None] + x1 * sin[None]], axis=1)
    return x3.reshape(x.shape)


FFN_TM = 1024
FFN_TF = MXU_N


def _swiglu_half_step(x, g_ref, wg_ref, wu_ref, wd_ref):
    h = _rms(x, g_ref[...]).astype(BF16)
    acc = jnp.zeros(x.shape, F32)
    for c in range(D_FF // FFN_TF):
        sl = slice(c * FFN_TF, (c + 1) * FFN_TF)
        gate = _dot(h, wg_ref[:, sl])
        up = _dot(h, wu_ref[:, sl])
        act = (gate * jax.nn.sigmoid(gate) * up).astype(BF16)
        acc = acc + _dot(act, wd_ref[sl, :])
    return x + 0.5 * acc


def _ffn_body(x_ref, g_ref, wg_ref, wu_ref, wd_ref, o_ref):
    o_ref[...] = _swiglu_half_step(x_ref[...], g_ref, wg_ref, wu_ref, wd_ref)


def _mix_out_ffn_body(ymix_ref, ymem_ref, x_ref, wmix_ref, wmem_ref, g_ref, wg_ref, wu_ref, wd_ref, o_ref):
    x = x_ref[...] + _dot(ymix_ref[...], wmix_ref[...]) + _dot(ymem_ref[...], wmem_ref[...])
    o_ref[...] = _swiglu_half_step(x, g_ref, wg_ref, wu_ref, wd_ref)


def _ffn_specs(d):
    return [_const_spec((1, d)), _const_spec((d, D_FF)), _const_spec((d, D_FF)), _const_spec((D_FF, d))]


def _ffn(x, g, wg, wu, wd):
    t, d = x.shape
    tm = min(FFN_TM, t)
    return pl.pallas_call(
        _ffn_body,
        out_shape=jax.ShapeDtypeStruct((t, d), F32),
        grid=(t // tm,),
        in_specs=[pl.BlockSpec((tm, d), lambda i: (i, 0))] + _ffn_specs(d),
        out_specs=pl.BlockSpec((tm, d), lambda i: (i, 0)),
        compiler_params=_cparams("parallel"),
        name="ffn",
    )(x, g.reshape(1, d), wg.astype(BF16), wu.astype(BF16), wd.astype(BF16))


def _mix_out_ffn(y_mix, y_mem, x, w_out, g, wg, wu, wd):
    t, d = x.shape
    tm = min(FFN_TM, t)
    return pl.pallas_call(
        _mix_out_ffn_body,
        out_shape=jax.ShapeDtypeStruct((t, d), F32),
        grid=(t // tm,),
        in_specs=[
            pl.BlockSpec((tm, MIX_W), lambda i: (i, 0)),
            pl.BlockSpec((tm, MEM_W), lambda i: (i, 0)),
            pl.BlockSpec((tm, d), lambda i: (i, 0)),
            _const_spec((MIX_W, d)),
            _const_spec((MEM_W, d)),
        ] + _ffn_specs(d),
        out_specs=pl.BlockSpec((tm, d), lambda i: (i, 0)),
        compiler_params=_cparams("parallel"),
        name="mix_out_ffn",
    )(y_mix, y_mem, x, w_out[:MIX_W].astype(BF16), w_out[MIX_W:].astype(BF16), g.reshape(1, d), wg.astype(BF16),
      wu.astype(BF16), wd.astype(BF16))


PROJ_TM = 512


def _mem_kv_body(m_ref, g_ref, wk_ref, wvT_ref, kn_ref, k_ref, vT_ref):
    h = _rms(m_ref[0], g_ref[...]).astype(BF16)
    kT = _dot_nt(wk_ref[...], h)
    kT = _heads_norm_rope(kT, _tile_lanes(kn_ref[...], kT.shape[-1]), None, None)
    k_ref[0] = kT.T.astype(BF16)
    vT_ref[0] = _dot_nt(wvT_ref[...], h).astype(BF16)


def _mem_kv(mem, g, w_kv, kn):
    b, m, d = mem.shape
    return pl.pallas_call(
        _mem_kv_body,
        out_shape=(jax.ShapeDtypeStruct((b, m, MEM_W), BF16), jax.ShapeDtypeStruct((b, MEM_W, m), BF16)),
        grid=(b,),
        in_specs=[
            pl.BlockSpec((1, m, d), lambda i: (i, 0, 0)),
            _const_spec((1, d)),
            _const_spec((MEM_W, d)),
            _const_spec((MEM_W, d)),
            _const_spec((HD, LANES)),
        ],
        out_specs=(pl.BlockSpec((1, m, MEM_W), lambda i: (i, 0, 0)), pl.BlockSpec((1, MEM_W, m), lambda i: (i, 0, 0))),
        compiler_params=_cparams("parallel"),
        name="mem_kv",
    )(mem, g.reshape(1, d), w_kv[:, :MEM_W].T.astype(BF16), w_kv[:, MEM_W:].T.astype(BF16), _lane_gain(kn))


MEM_TQ = 512


def _pad_rows(blk, half, total=LANES):
    z = jnp.zeros_like(blk)
    parts = [z] * (total // HD)
    parts[half] = blk
    return jnp.concatenate(parts, axis=0)


def _mem_attn_body(qT_ref, k_ref, vT_ref, o_ref, s_ref, m8_ref, m_ref, l8_ref, acc_ref):
    n_mem = k_ref.shape[1]
    outs = _two_pass_attention(
        H_MEM, 0, 1, n_mem,
        lambda h, r0: _dot(k_ref[0, :, (h // 2) * LANES:(h // 2 + 1) * LANES],
                           _pad_rows(qT_ref[h * HD:(h + 1) * HD, :], h % 2)),
        lambda h, r0: vT_ref[0, h * HD:(h + 1) * HD, :], s_ref, m8_ref, m_ref, l8_ref, acc_ref)
    o_ref[...] = jnp.concatenate(outs, axis=0).T.astype(BF16)


def _mem_attn(qmT, k, vT, seq):
    t = qmT.shape[1]
    b, m, _ = k.shape
    tq = min(MEM_TQ, seq)
    nq = seq // tq
    return pl.pallas_call(
        _mem_attn_body,
        out_shape=jax.ShapeDtypeStruct((t, MEM_W), BF16),
        grid=(b, nq),
        in_specs=[
            pl.BlockSpec((MEM_W, tq), lambda bi, i: (0, bi * nq + i)),
            pl.BlockSpec((1, m, MEM_W), lambda bi, i: (bi, 0, 0)),
            pl.BlockSpec((1, MEM_W, m), lambda bi, i: (bi, 0, 0)),
        ],
        out_specs=pl.BlockSpec((tq, MEM_W), lambda bi, i: (bi * nq + i, 0)),
        scratch_shapes=[pltpu.VMEM((H_MEM, m, tq), F32), pltpu.VMEM((H_MEM * SUBLANES, tq), F32),
                        pltpu.VMEM((SUBLANES, tq), F32), pltpu.VMEM((H_MEM * SUBLANES, tq), F32),
                        pltpu.VMEM((MEM_W, tq), F32)],
        compiler_params=_cparams("parallel", "parallel"),
        name="mem_attn",
    )(qmT, k, vT)


def _ret_proj_body(x_ref, g_ref, wtok_ref, wfeat_ref, cosT_ref, sinT_ref, cosF_ref, sinF_ref, mqn_ref,
                   q_ref, v_ref, gate_ref, kT_ref, qmT_ref):
    h = _rms(x_ref[...], g_ref[...]).astype(BF16)
    tok = _dot(h, wtok_ref[...])
    nq, nv = RET_H * RET_DK, RET_H * RET_DV
    q = tok[:, :nq]
    lane = lax.broadcasted_iota(I32, q.shape, 1)
    rot = jnp.where(lane % HD < HALF, pltpu.roll(q, nq - HALF, axis=1), pltpu.roll(q, HALF, axis=1))
    q_ref[...] = (q * _tile_lanes(cosT_ref[...], nq) + rot * _tile_lanes(sinT_ref[...], nq)).astype(BF16)
    v_ref[...] = tok[:, nq:nq + nv].astype(BF16)
    gate_ref[...] = tok[:, nq + nv:]
    feat = _dot_nt(wfeat_ref[...], h)
    tm = feat.shape[-1]
    cos, sin = cosF_ref[...], sinF_ref[...]
    kT = _heads_norm_rope(feat[:nq], None, cos, sin) * (RET_DK ** -0.5)
    kT_ref[...] = kT.astype(BF16)
    qm = _heads_norm_rope(feat[nq:], _tile_lanes(mqn_ref[...], tm), None, None) * SCALE
    qmT_ref[...] = qm.astype(BF16)


def _token_rope_tables(seq):
    cosF, sinF = _rope_tables(jnp.arange(seq))
    cos = jnp.tile(cosF.T, (1, LANES // HALF))
    sign = jnp.where((jnp.arange(LANES) % HD) < HALF, -1.0, 1.0).astype(F32)
    sin = jnp.tile(sinF.T, (1, LANES // HALF)) * sign[None, :]
    return cos, sin


def _ret_proj(x, g, w_in, mem_qn, seq):
    t, d = x.shape
    tm = min(PROJ_TM, seq)
    npos = seq // tm
    wq, wk, wv, wg, wqm = jnp.split(w_in, np.cumsum(RET_SIZES)[:-1].tolist(), axis=1)
    wtok = jnp.concatenate([wq, wv, wg], axis=1).astype(BF16)
    wfeat = jnp.concatenate([wk, wqm], axis=1).T.astype(BF16)
    cosT, sinT = _token_rope_tables(seq)
    cosF, sinF = _rope_tables(jnp.arange(seq))
    nq, nv = RET_H * RET_DK, RET_H * RET_DV
    row = lambda n: pl.BlockSpec((tm, n), lambda i: (i, 0))
    col = lambda n: pl.BlockSpec((n, tm), lambda i: (0, i))
    return pl.pallas_call(
        _ret_proj_body,
        out_shape=(
            jax.ShapeDtypeStruct((t, nq), BF16), jax.ShapeDtypeStruct((t, nv), BF16),
            jax.ShapeDtypeStruct((t, nv), F32), jax.ShapeDtypeStruct((nq, t), BF16),
            jax.ShapeDtypeStruct((MEM_W, t), BF16),
        ),
        grid=(t // tm,),
        in_specs=[
            row(d), _const_spec((1, d)), _const_spec(wtok.shape), _const_spec(wfeat.shape),
            pl.BlockSpec((tm, LANES), lambda i: (i % npos, 0)), pl.BlockSpec((tm, LANES), lambda i: (i % npos, 0)),
            pl.BlockSpec((HALF, tm), lambda i: (0, i % npos)), pl.BlockSpec((HALF, tm), lambda i: (0, i % npos)),
            _const_spec((HD, LANES)),
        ],
        out_specs=(row(nq), row(nv), row(nv), col(nq), col(MEM_W)),
        compiler_params=_cparams("parallel"),
        name="ret_proj",
    )(x, g.reshape(1, d), wtok, wfeat, cosT, sinT, cosF, sinF, _lane_gain(mem_qn))


RET_TL = 512


def _ret_body(q_ref, kT_ref, v_ref, gate_ref, y_ref, state_ref):
    @pl.when(pl.program_id(1) == 0)
    def _():
        state_ref[...] = jnp.zeros_like(state_ref)

    c = RET_CHUNK
    ii = lax.broadcasted_iota(I32, (c, c), 0).astype(F32)
    jj = lax.broadcasted_iota(I32, (c, c), 1).astype(F32)
    diff = ii - jj
    jk = lax.broadcasted_iota(I32, (RET_DK, c), 1).astype(F32)
    log_gs = [math.log(1.0 - 2.0 ** (-5.0 - hh)) for hh in range(RET_H)]
    dmasks = [jnp.where(diff >= 0, jnp.exp(lg * jnp.maximum(diff, 0.0)), 0.0) for lg in log_gs]
    xis = [jnp.exp(lg * (ii + 1.0)) for lg in log_gs]
    zetas = [jnp.exp(lg * (c - 1.0 - jk)) for lg in log_gs]
    for n in range(q_ref.shape[0] // c):
        rows = slice(n * c, (n + 1) * c)
        for hh in range(RET_H):
            dmask, xi, zeta, decay = dmasks[hh], xis[hh], zetas[hh], math.exp(log_gs[hh] * c)
            pair = slice((hh // 2) * LANES, (hh // 2 + 1) * LANES)
            q2 = q_ref[rows, pair]
            kT = kT_ref[hh * RET_DK:(hh + 1) * RET_DK, rows]
            v = v_ref[rows, hh * RET_DV:(hh + 1) * RET_DV]
            state = state_ref[hh]
            inner = _dot(q2, _pad_rows(kT, hh % 2)) * dmask
            o = _dot(inner.astype(BF16), v)
            o = o + _dot(q2, _pad_rows(state.astype(BF16), hh % 2)) * xi
            kv = _dot((kT.astype(F32) * zeta).astype(BF16), v)
            state_ref[hh] = state * decay + kv
            mu = jnp.mean(o, axis=-1, keepdims=True)
            var = jnp.mean(jnp.square(o - mu), axis=-1, keepdims=True)
            o = (o - mu) * lax.rsqrt(var + EPS)
            gte = gate_ref[rows, hh * RET_DV:(hh + 1) * RET_DV]
            y_ref[rows, hh * RET_DV:(hh + 1) * RET_DV] = (gte * jax.nn.sigmoid(gte) * o).astype(BF16)


def _retention(q, kT, v, gate, batch, seq):
    t = q.shape[0]
    tl = min(RET_TL, seq)
    nl = seq // tl
    nq, nv = RET_H * RET_DK, RET_H * RET_DV
    return pl.pallas_call(
        _ret_body,
        out_shape=jax.ShapeDtypeStruct((t, nv), BF16),
        grid=(batch, nl),
        in_specs=[
            pl.BlockSpec((tl, nq), lambda b, i: (b * nl + i, 0)),
            pl.BlockSpec((nq, tl), lambda b, i: (0, b * nl + i)),
            pl.BlockSpec((tl, nv), lambda b, i: (b * nl + i, 0)),
            pl.BlockSpec((tl, nv), lambda b, i: (b * nl + i, 0)),
        ],
        out_specs=pl.BlockSpec((tl, nv), lambda b, i: (b * nl + i, 0)),
        scratch_shapes=[pltpu.VMEM((RET_H, RET_DK, RET_DV), F32)],
        compiler_params=_cparams("parallel", "arbitrary"),
        name="retention",
    )(q, kT, v, gate)


def _dsa_proj_body(x_ref, g_ref, w_ref, cos_ref, sin_ref, qn_ref, kn_ref, mqn_ref,
                   kk_ref, qT_ref, iqT_ref, vT_ref, iwT_ref, qmT_ref):
    h = _rms(x_ref[...], g_ref[...]).astype(BF16)
    feat = _dot_nt(w_ref[...], h)
    tm = feat.shape[-1]
    cos, sin = cos_ref[...], sin_ref[...]
    o = 0
    q = feat[o:o + DSA_H * HD]; o += DSA_H * HD
    iq = feat[o:o + IDX_H * IDX_D]; o += IDX_H * IDX_D
    k = feat[o:o + HD]; o += HD
    ik = feat[o:o + IDX_D]; o += IDX_D
    v = feat[o:o + HD]; o += HD
    qm = feat[o:o + MEM_W]; o += MEM_W
    iw = feat[o:o + IDX_H]
    qT_ref[...] = (_heads_norm_rope(q, _tile_lanes(qn_ref[...], tm), cos, sin) * SCALE).astype(BF16)
    iqT_ref[...] = _heads_norm_rope(iq, None, cos, sin).astype(BF16)
    k = _heads_norm_rope(k, _tile_lanes(kn_ref[...], tm), cos, sin)
    ik = _heads_norm_rope(ik, None, cos, sin)
    kk_ref[...] = jnp.concatenate([k, ik], axis=0).T.astype(BF16)
    vT_ref[...] = v.astype(BF16)
    iwT_ref[...] = iw
    qmT_ref[...] = (_heads_norm_rope(qm, _tile_lanes(mqn_ref[...], tm), None, None) * SCALE).astype(BF16)


def _dsa_proj(x, g, w_in, qn, kn, mem_qn, seq):
    t, d = x.shape
    tm = min(PROJ_TM, seq)
    npos = seq // tm
    wq, wk, wv, wiq, wik, wiw, wqm = jnp.split(w_in, np.cumsum(DSA_SIZES)[:-1].tolist(), axis=1)
    wfeat = jnp.concatenate([wq, wiq, wk, wik, wv, wqm, wiw], axis=1).T.astype(BF16)
    cosF, sinF = _rope_tables(jnp.arange(seq))
    col = lambda n: pl.BlockSpec((n, tm), lambda i: (0, i))
    tab = pl.BlockSpec((HALF, tm), lambda i: (0, i % npos))
    gain = _const_spec((HD, LANES))
    return pl.pallas_call(
        _dsa_proj_body,
        out_shape=(
            jax.ShapeDtypeStruct((t, LANES), BF16), jax.ShapeDtypeStruct((DSA_H * HD, t), BF16),
            jax.ShapeDtypeStruct((IDX_H * IDX_D, t), BF16), jax.ShapeDtypeStruct((HD, t), BF16),
            jax.ShapeDtypeStruct((IDX_H, t), F32), jax.ShapeDtypeStruct((MEM_W, t), BF16),
        ),
        grid=(t // tm,),
        in_specs=[pl.BlockSpec((tm, d), lambda i: (i, 0)), _const_spec((1, d)), _const_spec(wfeat.shape),
                  tab, tab, gain, gain, gain],
        out_specs=(pl.BlockSpec((tm, LANES), lambda i: (i, 0)), col(DSA_H * HD), col(IDX_H * IDX_D), col(HD),
                   col(IDX_H), col(MEM_W)),
        compiler_params=_cparams("parallel"),
        name="dsa_proj",
    )(x, g.reshape(1, d), wfeat, cosF, sinF, _lane_gain(qn), _lane_gain(kn), _lane_gain(mem_qn))


DSA_TQ = 256
DSA_TK = 256
HALF_BITS = 16
INT16_MIN = -2 ** (HALF_BITS - 1)
PACKED_SUBLANES = 2 * SUBLANES


def _dsa_attn_body(kk_ref, qT_ref, iqT_ref, vT_ref, iwT_ref, y_ref, keys_ref, hi_ref, lo_ref, bias_ref, s_ref, m8_ref,
                   m_ref, l8_ref, acc_ref, *, topk, seq):
    tq, tk = DSA_TQ, DSA_TK
    qs = pl.program_id(1) * tq
    nkc = (qs + tq) // tk
    qpos = qs + lax.broadcasted_iota(I32, (1, tq), 1)
    row = lax.broadcasted_iota(I32, (tk, tq), 0)

    def score_chunk(c, carry):
        r0 = pl.multiple_of(c * tk, tk)
        kkc = kk_ref[pl.ds(r0, tk), :]
        acc = jnp.zeros((tk, tq), F32)
        for h in range(IDX_H):
            r = _dot(kkc, _pad_rows(iqT_ref[h * IDX_D:(h + 1) * IDX_D, :], 1))
            acc = acc + jnp.maximum(r, 0.0) * iwT_ref[h:h + 1, :]
        sc = jnp.where(r0 + row <= qpos, acc + 0.0, -jnp.inf)
        bits = pltpu.bitcast(sc, I32)
        key = jnp.where(bits < 0, bits ^ jnp.int32(0x7FFFFFFF), bits)
        keys_ref[pl.ds(r0, tk), :] = key
        hi_ref[pl.ds(r0, tk), :] = lax.shift_right_arithmetic(key, HALF_BITS).astype(I16)
        return carry

    lax.fori_loop(0, nkc, score_chunk, 0)

    def count(pred):
        def body(c, acc):
            r0 = pl.multiple_of(c * tk, tk)
            m = pred(keys_ref[pl.ds(r0, tk), :], r0 + row)
            return acc + jnp.sum(m.astype(I32).reshape(tk // SUBLANES, SUBLANES, tq), axis=0)
        acc = lax.fori_loop(0, nkc, body, jnp.zeros((SUBLANES, tq), I32))
        return jnp.sum(acc, axis=0, keepdims=True)

    def count16(ref, pred):
        def body(c, acc):
            r0 = pl.multiple_of(c * tk, tk)
            ones = jnp.where(pred(ref[pl.ds(r0, tk), :]), jnp.int16(1), jnp.int16(0))
            parts = [ones[i:i + PACKED_SUBLANES] for i in range(0, tk, PACKED_SUBLANES)]
            while len(parts) > 1:
                parts = [a + b for a, b in zip(parts[::2], parts[1::2])]
            return acc + parts[0]
        acc = lax.fori_loop(0, nkc, body, jnp.zeros((PACKED_SUBLANES, tq), I16))
        return jnp.sum(acc.astype(I32), axis=0, keepdims=True)

    def kth_largest16(ref, k):
        thr = jnp.where(count16(ref, lambda v: v >= 0) >= k, jnp.int32(0), jnp.int32(INT16_MIN))

        def bit_step(b, thr):
            cand = thr | lax.shift_left(jnp.int32(1), (HALF_BITS - 2) - b)
            c16 = cand.astype(I16)
            return jnp.where(count16(ref, lambda v: v >= c16) >= k, cand, thr)

        return lax.fori_loop(0, HALF_BITS - 1, bit_step, thr)

    thr_hi = kth_largest16(hi_ref, topk)
    thr_hi16 = thr_hi.astype(I16)
    need_lo = topk - count16(hi_ref, lambda v: v > thr_hi16)

    def low_half_chunk(c, carry):
        r0 = pl.multiple_of(c * tk, tk)
        lo = ((keys_ref[pl.ds(r0, tk), :] & jnp.int32(0xFFFF)) + jnp.int32(INT16_MIN)).astype(I16)
        lo_ref[pl.ds(r0, tk), :] = jnp.where(hi_ref[pl.ds(r0, tk), :] == thr_hi16, lo, jnp.int16(INT16_MIN))
        return carry

    lax.fori_loop(0, nkc, low_half_chunk, 0)
    thr_lo = kth_largest16(lo_ref, need_lo)
    thr = lax.shift_left(thr_hi, HALF_BITS) | (thr_lo - jnp.int32(INT16_MIN))
    def tie_search():
        need = topk - count(lambda k, _: k > thr)

        def idx_step(b, q):
            cand = q | lax.shift_left(jnp.int32(1), (seq.bit_length() - 2) - b)
            return jnp.where(count(lambda k, idx: (k == thr) & (idx < cand)) < need, cand, q)

        return lax.fori_loop(0, seq.bit_length() - 1, idx_step, jnp.zeros((1, tq), I32))

    has_ties = jnp.max(count(lambda k, _: k >= thr)) > topk
    last = lax.cond(has_ties, tie_search, lambda: jnp.full((1, tq), seq, I32))

    def bias_chunk(c, carry):
        r0 = pl.multiple_of(c * tk, tk)
        k = keys_ref[pl.ds(r0, tk), :]
        idx = r0 + row
        sel = (k > thr) | ((k == thr) & (idx <= last))
        bias_ref[pl.ds(r0, tk), :] = jnp.where(sel & (idx <= qpos), 0.0, NEG)
        return carry

    lax.fori_loop(0, nkc, bias_chunk, 0)

    outs = _two_pass_attention(
        DSA_H, 0, nkc, tk,
        lambda h, r0: _dot(kk_ref[pl.ds(r0, tk), :], _pad_rows(qT_ref[h * HD:(h + 1) * HD, :], 0))
        + bias_ref[pl.ds(r0, tk), :],
        lambda h, r0: vT_ref[:, pl.ds(r0, tk)], s_ref, m8_ref, m_ref, l8_ref, acc_ref)
    y_ref[...] = jnp.concatenate(outs, axis=0).T.astype(BF16)


def _dsa_attn(kk, qT, iqT, vT, iwT, batch, seq):
    t = kk.shape[0]
    tq = DSA_TQ
    nq = seq // tq
    topk = min(DSA_TOPK_MAX, seq // 4)
    colq = lambda n: pl.BlockSpec((n, tq), lambda b, i: (0, b * nq + i))
    return pl.pallas_call(
        functools.partial(_dsa_attn_body, topk=topk, seq=seq),
        out_shape=jax.ShapeDtypeStruct((t, DSA_H * HD), BF16),
        grid=(batch, nq),
        in_specs=[
            pl.BlockSpec((seq, LANES), lambda b, i: (b, 0)),
            colq(DSA_H * HD), colq(IDX_H * IDX_D),
            pl.BlockSpec((HD, seq), lambda b, i: (0, b)),
            colq(IDX_H),
        ],
        out_specs=pl.BlockSpec((tq, DSA_H * HD), lambda b, i: (b * nq + i, 0)),
        scratch_shapes=[pltpu.VMEM((seq, tq), I32), pltpu.VMEM((seq, tq), I16), pltpu.VMEM((seq, tq), I16),
                        pltpu.VMEM((seq, tq), F32),
                        pltpu.VMEM((DSA_H, seq, tq), F32), pltpu.VMEM((DSA_H * SUBLANES, tq), F32),
                        pltpu.VMEM((2 * SUBLANES, tq), F32), pltpu.VMEM((DSA_H * SUBLANES, tq), F32),
                        pltpu.VMEM((DSA_H * HD, tq), F32)],
        compiler_params=_cparams("parallel", "arbitrary"),
        name="dsa_attn",
    )(kk, qT, iqT, vT, iwT)


NSA_KV = NSA_G * HD
NSA_GATES = NSA_H * 3
NSA_GATES_PAD = -NSA_GATES % SUBLANES


def _nsa_proj_body(x_ref, g_ref, wtok_ref, wfeat_ref, cos_ref, sin_ref, qn_ref, kns_ref, knw_ref, mqn_ref,
                   kc_ref, vc_ref, ks_ref, kw_ref, qT_ref, vsT_ref, vwT_ref, gT_ref, qmT_ref):
    h = _rms(x_ref[...], g_ref[...]).astype(BF16)
    tok = _dot(h, wtok_ref[...])
    kc_ref[...] = tok[:, :NSA_KV]
    vc_ref[...] = tok[:, NSA_KV:]
    feat = _dot_nt(wfeat_ref[...], h)
    tm = feat.shape[-1]
    cos, sin = cos_ref[...], sin_ref[...]
    o = 0
    q = feat[o:o + NSA_H * HD]; o += NSA_H * HD
    ks = feat[o:o + NSA_KV]; o += NSA_KV
    kw = feat[o:o + NSA_KV]; o += NSA_KV
    vs = feat[o:o + NSA_KV]; o += NSA_KV
    vw = feat[o:o + NSA_KV]; o += NSA_KV
    qm = feat[o:o + MEM_W]; o += MEM_W
    gates = feat[o:]
    qT_ref[...] = (_heads_norm_rope(q, _tile_lanes(qn_ref[...], tm), cos, sin) * SCALE).astype(BF16)
    ks_ref[...] = _heads_norm_rope(ks, _tile_lanes(kns_ref[...], tm), cos, sin).T.astype(BF16)
    kw_ref[...] = _heads_norm_rope(kw, _tile_lanes(knw_ref[...], tm), cos, sin).T.astype(BF16)
    vsT_ref[...] = vs.astype(BF16)
    vwT_ref[...] = vw.astype(BF16)
    gT_ref[...] = jax.nn.sigmoid(gates)
    qmT_ref[...] = (_heads_norm_rope(qm, _tile_lanes(mqn_ref[...], tm), None, None) * SCALE).astype(BF16)


def _nsa_proj(x, g, w_in, qn, kn, mem_qn, seq):
    t, d = x.shape
    tm = min(PROJ_TM, seq)
    npos = seq // tm
    wq, wkc, wvc, wks, wvs, wkw, wvw, wgt, wqm = jnp.split(w_in, np.cumsum(NSA_SIZES)[:-1].tolist(), axis=1)
    wtok = jnp.concatenate([wkc, wvc], axis=1).astype(BF16)
    wgt = jnp.pad(wgt, ((0, 0), (0, NSA_GATES_PAD)))
    wfeat = jnp.concatenate([wq, wks, wkw, wvs, wvw, wqm, wgt], axis=1).T.astype(BF16)
    cosF, sinF = _rope_tables(jnp.arange(seq))
    row = lambda n: pl.BlockSpec((tm, n), lambda i: (i, 0))
    col = lambda n: pl.BlockSpec((n, tm), lambda i: (0, i))
    tab = pl.BlockSpec((HALF, tm), lambda i: (0, i % npos))
    gain = _const_spec((HD, LANES))
    ngt = NSA_GATES + NSA_GATES_PAD
    return pl.pallas_call(
        _nsa_proj_body,
        out_shape=(
            jax.ShapeDtypeStruct((t, NSA_KV), F32), jax.ShapeDtypeStruct((t, NSA_KV), F32),
            jax.ShapeDtypeStruct((t, NSA_KV), BF16), jax.ShapeDtypeStruct((t, NSA_KV), BF16),
            jax.ShapeDtypeStruct((NSA_H * HD, t), BF16), jax.ShapeDtypeStruct((NSA_KV, t), BF16),
            jax.ShapeDtypeStruct((NSA_KV, t), BF16), jax.ShapeDtypeStruct((ngt, t), F32),
            jax.ShapeDtypeStruct((MEM_W, t), BF16),
        ),
        grid=(t // tm,),
        in_specs=[row(d), _const_spec((1, d)), _const_spec(wtok.shape), _const_spec(wfeat.shape),
                  tab, tab, gain, gain, gain, gain],
        out_specs=(row(NSA_KV), row(NSA_KV), row(NSA_KV), row(NSA_KV), col(NSA_H * HD), col(NSA_KV), col(NSA_KV),
                   col(ngt), col(MEM_W)),
        compiler_params=_cparams("parallel"),
        name="nsa_proj",
    )(x, g.reshape(1, d), wtok, wfeat, cosF, sinF, _lane_gain(qn), _lane_gain(kn[1]), _lane_gain(kn[2]),
      _lane_gain(mem_qn))


def _nsa_cmp_body(xk_ref, xv_ref, pk_ref, pv_ref, wk_ref, wv_ref, kn_ref, cos_ref, sin_ref, k_ref, vT_ref):
    def compress(x, pos, w_ref):
        n = x.shape[0]
        xa = (x + pos[0:1]).astype(BF16)
        xb = (pltpu.roll(x, n - 1, axis=0) + pos[1:2]).astype(BF16)
        return _dot_nt(w_ref[0], xa) + _dot_nt(w_ref[1], xb)

    kT = compress(xk_ref[0], pk_ref[...], wk_ref)
    kT = _heads_norm_rope(kT, _tile_lanes(kn_ref[...], kT.shape[-1]), cos_ref[...], sin_ref[...])
    k_ref[0] = kT.T.astype(BF16)
    vT_ref[0] = compress(xv_ref[0], pv_ref[...], wv_ref).astype(BF16)


def _nsa_cmp_weights(w, pos):
    eye = jnp.eye(NSA_G, dtype=F32)
    halves = w.reshape(CMP_L // CMP_S, CMP_S, HD, HD)
    wt = jnp.einsum('hg,alde->ahelgd', eye, halves).reshape(CMP_L // CMP_S, NSA_KV, CMP_S * NSA_KV)
    p = jnp.broadcast_to(pos.reshape(CMP_L // CMP_S, CMP_S, 1, HD), (CMP_L // CMP_S, CMP_S, NSA_G, HD))
    return wt.astype(BF16), p.reshape(CMP_L // CMP_S, CMP_S * NSA_KV).astype(F32)


def _nsa_cmp(kc, vc, pos_k, pos_v, wk, wv, kn0, batch, seq):
    n = seq // CMP_S
    width = CMP_S * NSA_KV
    xk = kc.reshape(batch, n, width)
    xv = vc.reshape(batch, n, width)
    wkt, pk = _nsa_cmp_weights(wk, pos_k)
    wvt, pv = _nsa_cmp_weights(wv, pos_v)
    cosE, sinE = _rope_tables(jnp.arange(n) * CMP_S + (CMP_L - 1))
    xspec = pl.BlockSpec((1, n, width), lambda b: (b, 0, 0))
    return pl.pallas_call(
        _nsa_cmp_body,
        out_shape=(jax.ShapeDtypeStruct((batch, n, NSA_KV), BF16), jax.ShapeDtypeStruct((batch, NSA_KV, n), BF16)),
        grid=(batch,),
        in_specs=[xspec, xspec, _const_spec(pk.shape), _const_spec(pv.shape), _const_spec(wkt.shape),
                  _const_spec(wvt.shape), _const_spec((HD, LANES)), _const_spec((HALF, n)), _const_spec((HALF, n))],
        out_specs=(pl.BlockSpec((1, n, NSA_KV), lambda b: (b, 0, 0)), pl.BlockSpec((1, NSA_KV, n), lambda b: (b, 0, 0))),
        compiler_params=_cparams("parallel"),
        name="nsa_cmp",
    )(xk, xv, pk, pv, wkt, wvt, _lane_gain(kn0), cosE, sinE)


NSA_TQ = 256
NSA_TK = 256


def _two_pass_attention(nheads, lo, hi, tk, logits, values, s_ref, m8_ref, m_ref, l8_ref, acc_ref):
    tq = s_ref.shape[-1]
    fold = lambda a: a.reshape(tk // SUBLANES, SUBLANES, tq)
    m8_ref[...] = jnp.full(m8_ref.shape, -jnp.inf, F32)
    l8_ref[...] = jnp.zeros(l8_ref.shape, F32)
    acc_ref[...] = jnp.zeros(acc_ref.shape, F32)

    def logits_chunk(c, carry):
        r0 = pl.multiple_of(c * tk, tk)
        for h in range(nheads):
            g8 = slice(h * SUBLANES, (h + 1) * SUBLANES)
            s = logits(h, r0) * LOG2E
            s_ref[h, pl.ds(r0, tk), :] = s
            m8_ref[g8, :] = jnp.maximum(m8_ref[g8, :], jnp.max(fold(s), axis=0))
        return carry

    lax.fori_loop(lo, hi, logits_chunk, 0)
    for h in range(nheads):
        m_ref[h:h + 1, :] = jnp.max(m8_ref[h * SUBLANES:(h + 1) * SUBLANES, :], axis=0, keepdims=True)

    def pv_chunk(c, carry):
        r0 = pl.multiple_of(c * tk, tk)
        for h in range(nheads):
            g8, hrows = slice(h * SUBLANES, (h + 1) * SUBLANES), slice(h * HD, (h + 1) * HD)
            p = jnp.exp2(s_ref[h, pl.ds(r0, tk), :] - m_ref[h:h + 1, :])
            l8_ref[g8, :] = l8_ref[g8, :] + jnp.sum(fold(p), axis=0)
            acc_ref[hrows, :] = acc_ref[hrows, :] + _dot(values(h, r0), p.astype(BF16))
        return carry

    lax.fori_loop(lo, hi, pv_chunk, 0)
    return [acc_ref[h * HD:(h + 1) * HD, :] / jnp.sum(l8_ref[h * SUBLANES:(h + 1) * SUBLANES, :], axis=0, keepdims=True)
            for h in range(nheads)]


def _nsa_attn_body(qT_ref, gT_ref, ks_ref, vsT_ref, kw_ref, vwT_ref, kc_ref, vcT_ref, cov_ref, y_ref,
                   bias_ref, out_ref, s_ref, m8_ref, m_ref, l8_ref, acc_ref, *, seq):
    tq, tk = NSA_TQ, NSA_TK
    qs = pl.program_id(1) * tq
    nkc = (qs + tq) // tk
    wlo = jnp.maximum((qs - WIN) // tk, 0)
    qpos = qs + lax.broadcasted_iota(I32, (1, tq), 1)
    ncr = seq // CMP_S
    nblk = seq // SLC_L
    n_sel = min(SLC_N_MAX, nblk)
    valid_c = CMP_S * lax.broadcasted_iota(I32, (ncr, tq), 0) + (CMP_L - 1) <= qpos
    any_c = jnp.where(qpos >= CMP_L - 1, 1.0, 0.0)
    jrow = lax.broadcasted_iota(I32, (nblk, tq), 0)
    qblk = qpos // SLC_L
    forced = (jrow == 0) | (jrow == qblk) | (jrow == qblk - 1)
    causal_blk = jrow * SLC_L <= qpos
    krow = lax.broadcasted_iota(I32, (tk, tq), 0)
    brow = lax.broadcasted_iota(I32, (SLC_L, tq), 0)
    group = lambda h: h // NSA_R
    pair = lambda h: slice((group(h) // 2) * LANES, (group(h) // 2 + 1) * LANES)
    grows = lambda h: slice(group(h) * HD, (group(h) + 1) * HD)
    q_of = lambda h: _pad_rows(qT_ref[h * HD:(h + 1) * HD, :], group(h) % 2)
    gate = lambda h, branch: gT_ref[h * 3 + branch:h * 3 + branch + 1, :]

    for g in range(NSA_G):
        heads = [g * NSA_R + r for r in range(NSA_R)]
        kcm = kc_ref[0][:, pair(heads[0])]
        vcm = vcT_ref[0][grows(heads[0]), :]
        psum = jnp.zeros((ncr, tq), F32)
        for h in heads:
            s = jnp.where(valid_c, _dot(kcm, q_of(h)), NEG)
            p = jnp.exp(s - jnp.max(s, axis=0, keepdims=True))
            p = p / jnp.sum(p, axis=0, keepdims=True) * any_c
            out_ref[h * HD:(h + 1) * HD, :] = gate(h, 0) * _dot(vcm, p.astype(BF16))
            psum = psum + p
        p_hi = psum.astype(BF16)
        p_lo = (psum - p_hi.astype(F32)).astype(BF16)
        imp = _dot(cov_ref[...], p_hi) + _dot(cov_ref[...], p_lo)
        imp = jnp.where(causal_blk, imp + jnp.where(forced, FORCE, 0.0), NEG)
        rank = jnp.zeros((nblk, tq), I32)
        for j2 in range(nblk):
            rj = imp[j2:j2 + 1]
            beats = (rj > imp) | ((rj == imp) & (j2 < jrow))
            rank = rank + beats.astype(I32)
        selb = jnp.where(rank < n_sel, 0.0, NEG)
        for j in range(nblk):
            bias_ref[g, j * SLC_L:(j + 1) * SLC_L, :] = jnp.where(j * SLC_L + brow <= qpos, selb[j:j + 1], NEG)

    stats = (s_ref, m8_ref, m_ref, l8_ref, acc_ref)
    slc = _two_pass_attention(
        NSA_H, 0, nkc, tk,
        lambda h, r0: _dot(ks_ref[pl.ds(r0, tk), pair(h)], q_of(h)) + bias_ref[group(h), pl.ds(r0, tk), :],
        lambda h, r0: vsT_ref[grows(h), pl.ds(r0, tk)], *stats)
    for h in range(NSA_H):
        out_ref[h * HD:(h + 1) * HD, :] = out_ref[h * HD:(h + 1) * HD, :] + gate(h, 1) * slc[h]

    def win_logits(h, r0):
        dist = qpos - (r0 + krow)
        return jnp.where((dist >= 0) & (dist < WIN), _dot(kw_ref[pl.ds(r0, tk), pair(h)], q_of(h)), NEG)

    win = _two_pass_attention(NSA_H, wlo, nkc, tk, win_logits, lambda h, r0: vwT_ref[grows(h), pl.ds(r0, tk)], *stats)
    outs = [out_ref[h * HD:(h + 1) * HD, :] + gate(h, 2) * win[h] for h in range(NSA_H)]
    y_ref[...] = jnp.concatenate(outs, axis=0).T.astype(BF16)


def _nsa_attn(qT, gT, ks, vsT, kw, vwT, kcmp, vcmpT, batch, seq):
    t = ks.shape[0]
    tq = NSA_TQ
    nq = seq // tq
    ncr, nblk = seq // CMP_S, seq // SLC_L
    starts = np.arange(ncr) * CMP_S
    sstart = np.arange(nblk) * SLC_L
    cover = (starts[None, :] < sstart[:, None] + SLC_L) & (starts[None, :] + CMP_L > sstart[:, None])
    cover[:, ncr - 1] = False
    colq = lambda n: pl.BlockSpec((n, tq), lambda b, i: (0, b * nq + i))
    tok = pl.BlockSpec((seq, NSA_KV), lambda b, i: (b, 0))
    feat = pl.BlockSpec((NSA_KV, seq), lambda b, i: (0, b))
    return pl.pallas_call(
        functools.partial(_nsa_attn_body, seq=seq),
        out_shape=jax.ShapeDtypeStruct((t, NSA_H * HD), BF16),
        grid=(batch, nq),
        in_specs=[
            colq(NSA_H * HD), colq(gT.shape[0]), tok, feat, tok, feat,
            pl.BlockSpec((1, ncr, NSA_KV), lambda b, i: (b, 0, 0)),
            pl.BlockSpec((1, NSA_KV, ncr), lambda b, i: (b, 0, 0)),
            _const_spec((nblk, ncr)),
        ],
        out_specs=pl.BlockSpec((tq, NSA_H * HD), lambda b, i: (b * nq + i, 0)),
        scratch_shapes=[pltpu.VMEM((NSA_G, seq, tq), F32), pltpu.VMEM((NSA_H * HD, tq), F32),
                        pltpu.VMEM((NSA_H, seq, tq), F32), pltpu.VMEM((NSA_H * SUBLANES, tq), F32),
                        pltpu.VMEM((2 * SUBLANES, tq), F32), pltpu.VMEM((NSA_H * SUBLANES, tq), F32),
                        pltpu.VMEM((NSA_H * HD, tq), F32)],
        compiler_params=_cparams("parallel", "arbitrary"),
        name="nsa_attn",
    )(qT, gT, ks, vsT, kw, vwT, kcmp, vcmpT, jnp.asarray(cover, BF16))


def _nsa_layer_mix(x, g, w_in, qn, kn, pos_k, pos_v, wk, wv, mem_qn, batch, seq):
    kc, vc, ks, kw, qT, vsT, vwT, gT, qmT = _nsa_proj(x, g, w_in, qn, kn, mem_qn, seq)
    kcmp, vcmpT = _nsa_cmp(kc, vc, pos_k, pos_v, wk, wv, kn[0], batch, seq)
    return _nsa_attn(qT, gT, ks, vsT, kw, vwT, kcmp, vcmpT, batch, seq), qmT


def kernel(x, mem, ffn_norm, ffn_w_gate, ffn_w_up, ffn_w_down, mix_norm, w_out, mem_norm, mem_w_kv, mem_qn, mem_kn, ret_w_in, dsa_w_in, dsa_qn, dsa_kn, nsa_w_in, nsa_qn, nsa_kn, nsa_cmp_pos_k, nsa_cmp_pos_v, nsa_cmp_wk, nsa_cmp_wv):
    batch, seq, d = x.shape
    x = x.reshape(batch * seq, d)
    for i in range(ffn_norm.shape[0]):
        x = _ffn(x, ffn_norm[i, 0], ffn_w_gate[i, 0], ffn_w_up[i, 0], ffn_w_down[i, 0])
        kind, j = i % N_MIXERS, i // N_MIXERS
        if kind == 0:
            q, v, gate, kT, qmT = _ret_proj(x, mix_norm[i], ret_w_in[j], mem_qn[i], seq)
            y_mix = _retention(q, kT, v, gate, batch, seq)
        elif kind == 1:
            kk, qT, iqT, vT, iwT, qmT = _dsa_proj(x, mix_norm[i], dsa_w_in[j], dsa_qn[j], dsa_kn[j], mem_qn[i], seq)
            y_mix = _dsa_attn(kk, qT, iqT, vT, iwT, batch, seq)
        else:
            y_mix, qmT = _nsa_layer_mix(x, mix_norm[i], nsa_w_in[j], nsa_qn[j], nsa_kn[j], nsa_cmp_pos_k[j],
                                        nsa_cmp_pos_v[j], nsa_cmp_wk[j], nsa_cmp_wv[j], mem_qn[i], batch, seq)
        mem_k, mem_vT = _mem_kv(mem, mem_norm[i], mem_w_kv[i], mem_kn[i])
        y_mem = _mem_attn(qmT, mem_k, mem_vT, seq)
        x = _mix_out_ffn(y_mix, y_mem, x, w_out[i], ffn_norm[i, 1], ffn_w_gate[i, 1], ffn_w_up[i, 1], ffn_w_down[i, 1])
    return x.reshape(batch, seq, d)
```

```python
import functools
import math

import jax
import jax.numpy as jnp
import numpy as np
from jax import lax
from jax.experimental import pallas as pl
from jax.experimental.pallas import tpu as pltpu

D_MODEL = 1024
HD = 64
HALF = HD // 2
H_MIX = 12
H_MEM = 4
MIX_W = H_MIX * HD
MEM_W = H_MEM * HD
D_FF = 2816
ROPE_THETA = 10000.0
EPS = 1e-6
NEG = -1e30
FORCE = 1e9
SCALE = HD ** -0.5
LOG2E = math.log2(math.e)

RET_H, RET_DK, RET_DV, RET_CHUNK = 6, 64, 128, 128
DSA_H, IDX_H, IDX_D, DSA_TOPK_MAX = 12, 8, 64, 256
NSA_H, NSA_G, CMP_L, CMP_S, SLC_L, SLC_N_MAX, WIN = 12, 4, 32, 16, 64, 16, 512
NSA_R = NSA_H // NSA_G
N_MIXERS = 3

RET_SIZES = [RET_H * RET_DK, RET_H * RET_DK, RET_H * RET_DV, RET_H * RET_DV, MEM_W]
DSA_SIZES = [DSA_H * HD, HD, HD, IDX_H * IDX_D, IDX_D, IDX_H, MEM_W]
NSA_SIZES = [NSA_H * HD] + [NSA_G * HD] * 6 + [NSA_H * 3, MEM_W]

LANES = 128
SUBLANES = 8
MXU_N = 256
VMEM_LIMIT_BYTES = 56 * 1024 * 1024

BF16 = jnp.bfloat16
F32 = jnp.float32
I32 = jnp.int32
I16 = jnp.int16


def _cparams(*sem):
    return pltpu.CompilerParams(dimension_semantics=sem, vmem_limit_bytes=VMEM_LIMIT_BYTES)


def _const_spec(shape):
    n = len(shape)
    return pl.BlockSpec(shape, lambda *_: (0,) * n, pipeline_mode=pl.Buffered(1))


def _rms(x, g):
    return x * lax.rsqrt(jnp.mean(x * x, axis=-1, keepdims=True) + EPS) * g


def _dot(a, b):
    return jnp.dot(a, b, preferred_element_type=F32)


def _dot_nt(a, b):
    return lax.dot_general(a, b, (((1,), (1,)), ((), ())), preferred_element_type=F32)


def _tile_lanes(a, n):
    reps = n // a.shape[-1]
    return a if reps == 1 else jnp.concatenate([a] * reps, axis=-1)


def _rope_tables(pos):
    inv = ROPE_THETA ** (-jnp.arange(HALF, dtype=F32) / HALF)
    ang = pos.astype(F32)[:, None] * inv[None, :]
    return jnp.cos(ang).T, jnp.sin(ang).T


def _lane_gain(g):
    return jnp.broadcast_to(g.astype(F32)[:, None], (g.shape[0], LANES))


def _heads_norm_rope(x, gain, cos, sin):
    t = x.shape[-1]
    x3 = x.reshape(x.shape[0] // HD, HD, t)
    if gain is not None:
        ms = jnp.sum(x3 * x3, axis=1, keepdims=True) * (1.0 / HD)
        x3 = x3 * lax.rsqrt(ms + EPS) * gain[None]
    if cos is not None:
        x1, x2 = x3[:, :HALF], x3[:, HALF:]
        x3 = jnp.concatenate([x1 * cos[None] - x2 * sin[None], x2 * cos[None] + x1 * sin[None]], axis=1)
    return x3.reshape(x.shape)


FFN_TM = 1024
FFN_TF = MXU_N


def _swiglu_half_step(x, g_ref, wg_ref, wu_ref, wd_ref):
    h = _rms(x, g_ref[...]).astype(BF16)
    acc = jnp.zeros(x.shape, F32)
    for c in range(D_FF // FFN_TF):
        sl = slice(c * FFN_TF, (c + 1) * FFN_TF)
        gate = _dot(h, wg_ref[:, sl])
        up = _dot(h, wu_ref[:, sl])
        act = (gate * jax.nn.sigmoid(gate) * up).astype(BF16)
        acc = acc + _dot(act, wd_ref[sl, :])
    return x + 0.5 * acc


def _ffn_body(x_ref, g_ref, wg_ref, wu_ref, wd_ref, o_ref):
    o_ref[...] = _swiglu_half_step(x_ref[...], g_ref, wg_ref, wu_ref, wd_ref)


def _mix_out_ffn_body(ymix_ref, ymem_ref, x_ref, wmix_ref, wmem_ref, g_ref, wg_ref, wu_ref, wd_ref, o_ref):
    x = x_ref[...] + _dot(ymix_ref[...], wmix_ref[...]) + _dot(ymem_ref[...], wmem_ref[...])
    o_ref[...] = _swiglu_half_step(x, g_ref, wg_ref, wu_ref, wd_ref)


def _ffn_specs(d):
    return [_const_spec((1, d)), _const_spec((d, D_FF)), _const_spec((d, D_FF)), _const_spec((D_FF, d))]


def _ffn(x, g, wg, wu, wd):
    t, d = x.shape
    tm = min(FFN_TM, t)
    return pl.pallas_call(
        _ffn_body,
        out_shape=jax.ShapeDtypeStruct((t, d), F32),
        grid=(t // tm,),
        in_specs=[pl.BlockSpec((tm, d), lambda i: (i, 0))] + _ffn_specs(d),
        out_specs=pl.BlockSpec((tm, d), lambda i: (i, 0)),
        compiler_params=_cparams("parallel"),
        name="ffn",
    )(x, g.reshape(1, d), wg.astype(BF16), wu.astype(BF16), wd.astype(BF16))


def _mix_out_ffn(y_mix, y_mem, x, w_out, g, wg, wu, wd):
    t, d = x.shape
    tm = min(FFN_TM, t)
    return pl.pallas_call(
        _mix_out_ffn_body,
        out_shape=jax.ShapeDtypeStruct((t, d), F32),
        grid=(t // tm,),
        in_specs=[
            pl.BlockSpec((tm, MIX_W), lambda i: (i, 0)),
            pl.BlockSpec((tm, MEM_W), lambda i: (i, 0)),
            pl.BlockSpec((tm, d), lambda i: (i, 0)),
            _const_spec((MIX_W, d)),
            _const_spec((MEM_W, d)),
        ] + _ffn_specs(d),
        out_specs=pl.BlockSpec((tm, d), lambda i: (i, 0)),
        compiler_params=_cparams("parallel"),
        name="mix_out_ffn",
    )(y_mix, y_mem, x, w_out[:MIX_W].astype(BF16), w_out[MIX_W:].astype(BF16), g.reshape(1, d), wg.astype(BF16),
      wu.astype(BF16), wd.astype(BF16))


PROJ_TM = 512


def _mem_kv_body(m_ref, g_ref, wk_ref, wvT_ref, kn_ref, k_ref, vT_ref):
    h = _rms(m_ref[0], g_ref[...]).astype(BF16)
    kT = _dot_nt(wk_ref[...], h)
    kT = _heads_norm_rope(kT, _tile_lanes(kn_ref[...], kT.shape[-1]), None, None)
    k_ref[0] = kT.T.astype(BF16)
    vT_ref[0] = _dot_nt(wvT_ref[...], h).astype(BF16)


def _mem_kv(mem, g, w_kv, kn):
    b, m, d = mem.shape
    return pl.pallas_call(
        _mem_kv_body,
        out_shape=(jax.ShapeDtypeStruct((b, m, MEM_W), BF16), jax.ShapeDtypeStruct((b, MEM_W, m), BF16)),
        grid=(b,),
        in_specs=[
            pl.BlockSpec((1, m, d), lambda i: (i, 0, 0)),
            _const_spec((1, d)),
            _const_spec((MEM_W, d)),
            _const_spec((MEM_W, d)),
            _const_spec((HD, LANES)),
        ],
        out_specs=(pl.BlockSpec((1, m, MEM_W), lambda i: (i, 0, 0)), pl.BlockSpec((1, MEM_W, m), lambda i: (i, 0, 0))),
        compiler_params=_cparams("parallel"),
        name="mem_kv",
    )(mem, g.reshape(1, d), w_kv[:, :MEM_W].T.astype(BF16), w_kv[:, MEM_W:].T.astype(BF16), _lane_gain(kn))


MEM_TQ = 512


def _pad_rows(blk, half, total=LANES):
    z = jnp.zeros_like(blk)
    parts = [z] * (total // HD)
    parts[half] = blk
    return jnp.concatenate(parts, axis=0)


def _mem_attn_body(qT_ref, k_ref, vT_ref, o_ref, s_ref, mc_ref, m_ref, l8_ref, acc_ref):
    n_mem = k_ref.shape[1]
    outs = _staged_attention(
        H_MEM, 0, 1, n_mem,
        lambda h, r0: _dot(k_ref[0, :, (h // 2) * LANES:(h // 2 + 1) * LANES],
                           _pad_rows(qT_ref[h * HD:(h + 1) * HD, :], h % 2)),
        lambda h, r0: vT_ref[0, h * HD:(h + 1) * HD, :], s_ref, mc_ref, m_ref, l8_ref, acc_ref)
    o_ref[...] = jnp.concatenate(outs, axis=0).T.astype(BF16)


def _mem_attn(qmT, k, vT, seq):
    t = qmT.shape[1]
    b, m, _ = k.shape
    tq = min(MEM_TQ, seq)
    nq = seq // tq
    return pl.pallas_call(
        _mem_attn_body,
        out_shape=jax.ShapeDtypeStruct((t, MEM_W), BF16),
        grid=(b, nq),
        in_specs=[
            pl.BlockSpec((MEM_W, tq), lambda bi, i: (0, bi * nq + i)),
            pl.BlockSpec((1, m, MEM_W), lambda bi, i: (bi, 0, 0)),
            pl.BlockSpec((1, MEM_W, m), lambda bi, i: (bi, 0, 0)),
        ],
        out_specs=pl.BlockSpec((tq, MEM_W), lambda bi, i: (bi * nq + i, 0)),
        scratch_shapes=[pltpu.VMEM((2, H_MEM, m, tq), F32), pltpu.VMEM((2, H_MEM * SUBLANES, tq), F32),
                        pltpu.VMEM((SUBLANES, tq), F32), pltpu.VMEM((H_MEM * SUBLANES, tq), F32),
                        pltpu.VMEM((MEM_W, tq), F32)],
        compiler_params=_cparams("parallel", "parallel"),
        name="mem_attn",
    )(qmT, k, vT)


def _ret_proj_body(x_ref, g_ref, wtok_ref, wfeat_ref, cosT_ref, sinT_ref, cosF_ref, sinF_ref, mqn_ref,
                   q_ref, v_ref, gate_ref, kT_ref, qmT_ref):
    h = _rms(x_ref[...], g_ref[...]).astype(BF16)
    tok = _dot(h, wtok_ref[...])
    nq, nv = RET_H * RET_DK, RET_H * RET_DV
    q = tok[:, :nq]
    lane = lax.broadcasted_iota(I32, q.shape, 1)
    rot = jnp.where(lane % HD < HALF, pltpu.roll(q, nq - HALF, axis=1), pltpu.roll(q, HALF, axis=1))
    q_ref[...] = (q * _tile_lanes(cosT_ref[...], nq) + rot * _tile_lanes(sinT_ref[...], nq)).astype(BF16)
    v_ref[...] = tok[:, nq:nq + nv].astype(BF16)
    gate_ref[...] = tok[:, nq + nv:]
    feat = _dot_nt(wfeat_ref[...], h)
    tm = feat.shape[-1]
    cos, sin = cosF_ref[...], sinF_ref[...]
    kT = _heads_norm_rope(feat[:nq], None, cos, sin) * (RET_DK ** -0.5)
    kT_ref[...] = kT.astype(BF16)
    qm = _heads_norm_rope(feat[nq:], _tile_lanes(mqn_ref[...], tm), None, None) * SCALE
    qmT_ref[...] = qm.astype(BF16)


def _token_rope_tables(seq):
    cosF, sinF = _rope_tables(jnp.arange(seq))
    cos = jnp.tile(cosF.T, (1, LANES // HALF))
    sign = jnp.where((jnp.arange(LANES) % HD) < HALF, -1.0, 1.0).astype(F32)
    sin = jnp.tile(sinF.T, (1, LANES // HALF)) * sign[None, :]
    return cos, sin


def _ret_proj(x, g, w_in, mem_qn, seq):
    t, d = x.shape
    tm = min(PROJ_TM, seq)
    npos = seq // tm
    wq, wk, wv, wg, wqm = jnp.split(w_in, np.cumsum(RET_SIZES)[:-1].tolist(), axis=1)
    wtok = jnp.concatenate([wq, wv, wg], axis=1).astype(BF16)
    wfeat = jnp.concatenate([wk, wqm], axis=1).T.astype(BF16)
    cosT, sinT = _token_rope_tables(seq)
    cosF, sinF = _rope_tables(jnp.arange(seq))
    nq, nv = RET_H * RET_DK, RET_H * RET_DV
    row = lambda n: pl.BlockSpec((tm, n), lambda i: (i, 0))
    col = lambda n: pl.BlockSpec((n, tm), lambda i: (0, i))
    return pl.pallas_call(
        _ret_proj_body,
        out_shape=(
            jax.ShapeDtypeStruct((t, nq), BF16), jax.ShapeDtypeStruct((t, nv), BF16),
            jax.ShapeDtypeStruct((t, nv), F32), jax.ShapeDtypeStruct((nq, t), BF16),
            jax.ShapeDtypeStruct((MEM_W, t), BF16),
        ),
        grid=(t // tm,),
        in_specs=[
            row(d), _const_spec((1, d)), _const_spec(wtok.shape), _const_spec(wfeat.shape),
            pl.BlockSpec((tm, LANES), lambda i: (i % npos, 0)), pl.BlockSpec((tm, LANES), lambda i: (i % npos, 0)),
            pl.BlockSpec((HALF, tm), lambda i: (0, i % npos)), pl.BlockSpec((HALF, tm), lambda i: (0, i % npos)),
            _const_spec((HD, LANES)),
        ],
        out_specs=(row(nq), row(nv), row(nv), col(nq), col(MEM_W)),
        compiler_params=_cparams("parallel"),
        name="ret_proj",
    )(x, g.reshape(1, d), wtok, wfeat, cosT, sinT, cosF, sinF, _lane_gain(mem_qn))


RET_TL = 512


def _ret_body(q_ref, kT_ref, v_ref, gate_ref, y_ref, state_ref):
    @pl.when(pl.program_id(1) == 0)
    def _():
        state_ref[...] = jnp.zeros_like(state_ref)

    c = RET_CHUNK
    ii = lax.broadcasted_iota(I32, (c, c), 0).astype(F32)
    jj = lax.broadcasted_iota(I32, (c, c), 1).astype(F32)
    diff = ii - jj
    jk = lax.broadcasted_iota(I32, (RET_DK, c), 1).astype(F32)
    log_gs = [math.log(1.0 - 2.0 ** (-5.0 - hh)) for hh in range(RET_H)]
    dmasks = [jnp.where(diff >= 0, jnp.exp(lg * jnp.maximum(diff, 0.0)), 0.0) for lg in log_gs]
    xis = [jnp.exp(lg * (ii + 1.0)) for lg in log_gs]
    zetas = [jnp.exp(lg * (c - 1.0 - jk)) for lg in log_gs]
    for n in range(q_ref.shape[0] // c):
        rows = slice(n * c, (n + 1) * c)
        for hh in range(RET_H):
            dmask, xi, zeta, decay = dmasks[hh], xis[hh], zetas[hh], math.exp(log_gs[hh] * c)
            pair = slice((hh // 2) * LANES, (hh // 2 + 1) * LANES)
            q2 = q_ref[rows, pair]
            kT = kT_ref[hh * RET_DK:(hh + 1) * RET_DK, rows]
            v = v_ref[rows, hh * RET_DV:(hh + 1) * RET_DV]
            state = state_ref[hh]
            inner = _dot(q2, _pad_rows(kT, hh % 2)) * dmask
            o = _dot(inner.astype(BF16), v)
            o = o + _dot(q2, _pad_rows(state.astype(BF16), hh % 2)) * xi
            kv = _dot((kT.astype(F32) * zeta).astype(BF16), v)
            state_ref[hh] = state * decay + kv
            mu = jnp.mean(o, axis=-1, keepdims=True)
            var = jnp.mean(jnp.square(o - mu), axis=-1, keepdims=True)
            o = (o - mu) * lax.rsqrt(var + EPS)
            gte = gate_ref[rows, hh * RET_DV:(hh + 1) * RET_DV]
            y_ref[rows, hh * RET_DV:(hh + 1) * RET_DV] = (gte * jax.nn.sigmoid(gte) * o).astype(BF16)


def _retention(q, kT, v, gate, batch, seq):
    t = q.shape[0]
    tl = min(RET_TL, seq)
    nl = seq // tl
    nq, nv = RET_H * RET_DK, RET_H * RET_DV
    return pl.pallas_call(
        _ret_body,
        out_shape=jax.ShapeDtypeStruct((t, nv), BF16),
        grid=(batch, nl),
        in_specs=[
            pl.BlockSpec((tl, nq), lambda b, i: (b * nl + i, 0)),
            pl.BlockSpec((nq, tl), lambda b, i: (0, b * nl + i)),
            pl.BlockSpec((tl, nv), lambda b, i: (b * nl + i, 0)),
            pl.BlockSpec((tl, nv), lambda b, i: (b * nl + i, 0)),
        ],
        out_specs=pl.BlockSpec((tl, nv), lambda b, i: (b * nl + i, 0)),
        scratch_shapes=[pltpu.VMEM((RET_H, RET_DK, RET_DV), F32)],
        compiler_params=_cparams("parallel", "arbitrary"),
        name="retention",
    )(q, kT, v, gate)


def _dsa_proj_body(x_ref, g_ref, w_ref, cos_ref, sin_ref, qn_ref, kn_ref, mqn_ref,
                   kk_ref, qT_ref, iqT_ref, vT_ref, iwT_ref, qmT_ref):
    h = _rms(x_ref[...], g_ref[...]).astype(BF16)
    feat = _dot_nt(w_ref[...], h)
    tm = feat.shape[-1]
    cos, sin = cos_ref[...], sin_ref[...]
    o = 0
    q = feat[o:o + DSA_H * HD]; o += DSA_H * HD
    iq = feat[o:o + IDX_H * IDX_D]; o += IDX_H * IDX_D
    k = feat[o:o + HD]; o += HD
    ik = feat[o:o + IDX_D]; o += IDX_D
    v = feat[o:o + HD]; o += HD
    qm = feat[o:o + MEM_W]; o += MEM_W
    iw = feat[o:o + IDX_H]
    qT_ref[...] = (_heads_norm_rope(q, _tile_lanes(qn_ref[...], tm), cos, sin) * SCALE).astype(BF16)
    iqT_ref[...] = _heads_norm_rope(iq, None, cos, sin).astype(BF16)
    k = _heads_norm_rope(k, _tile_lanes(kn_ref[...], tm), cos, sin)
    ik = _heads_norm_rope(ik, None, cos, sin)
    kk_ref[...] = jnp.concatenate([k, ik], axis=0).T.astype(BF16)
    vT_ref[...] = v.astype(BF16)
    iwT_ref[...] = iw
    qmT_ref[...] = (_heads_norm_rope(qm, _tile_lanes(mqn_ref[...], tm), None, None) * SCALE).astype(BF16)


def _dsa_proj(x, g, w_in, qn, kn, mem_qn, seq):
    t, d = x.shape
    tm = min(PROJ_TM, seq)
    npos = seq // tm
    wq, wk, wv, wiq, wik, wiw, wqm = jnp.split(w_in, np.cumsum(DSA_SIZES)[:-1].tolist(), axis=1)
    wfeat = jnp.concatenate([wq, wiq, wk, wik, wv, wqm, wiw], axis=1).T.astype(BF16)
    cosF, sinF = _rope_tables(jnp.arange(seq))
    col = lambda n: pl.BlockSpec((n, tm), lambda i: (0, i))
    tab = pl.BlockSpec((HALF, tm), lambda i: (0, i % npos))
    gain = _const_spec((HD, LANES))
    return pl.pallas_call(
        _dsa_proj_body,
        out_shape=(
            jax.ShapeDtypeStruct((t, LANES), BF16), jax.ShapeDtypeStruct((DSA_H * HD, t), BF16),
            jax.ShapeDtypeStruct((IDX_H * IDX_D, t), BF16), jax.ShapeDtypeStruct((HD, t), BF16),
            jax.ShapeDtypeStruct((IDX_H, t), F32), jax.ShapeDtypeStruct((MEM_W, t), BF16),
        ),
        grid=(t // tm,),
        in_specs=[pl.BlockSpec((tm, d), lambda i: (i, 0)), _const_spec((1, d)), _const_spec(wfeat.shape),
                  tab, tab, gain, gain, gain],
        out_specs=(pl.BlockSpec((tm, LANES), lambda i: (i, 0)), col(DSA_H * HD), col(IDX_H * IDX_D), col(HD),
                   col(IDX_H), col(MEM_W)),
        compiler_params=_cparams("parallel"),
        name="dsa_proj",
    )(x, g.reshape(1, d), wfeat, cosF, sinF, _lane_gain(qn), _lane_gain(kn), _lane_gain(mem_qn))


DSA_TQ = 256
DSA_TK = 256
HALF_BITS = 16
INT16_MIN = -2 ** (HALF_BITS - 1)
PACKED_SUBLANES = 2 * SUBLANES


def _dsa_attn_body(kk_ref, qT_ref, iqT_ref, vT_ref, iwT_ref, y_ref, keys_ref, hi_ref, lo_ref, bias_ref, s_ref, mc_ref,
                   m_ref, l8_ref, acc_ref, *, topk, seq):
    tq, tk = DSA_TQ, DSA_TK
    qs = pl.program_id(1) * tq
    nkc = (qs + tq) // tk
    qpos = qs + lax.broadcasted_iota(I32, (1, tq), 1)
    row = lax.broadcasted_iota(I32, (tk, tq), 0)

    def score_chunk(c, carry):
        r0 = pl.multiple_of(c * tk, tk)
        kkc = kk_ref[pl.ds(r0, tk), :]
        acc = jnp.zeros((tk, tq), F32)
        for h in range(IDX_H):
            r = _dot(kkc, _pad_rows(iqT_ref[h * IDX_D:(h + 1) * IDX_D, :], 1))
            acc = acc + jnp.maximum(r, 0.0) * iwT_ref[h:h + 1, :]
        sc = jnp.where(r0 + row <= qpos, acc + 0.0, -jnp.inf)
        bits = pltpu.bitcast(sc, I32)
        key = jnp.where(bits < 0, bits ^ jnp.int32(0x7FFFFFFF), bits)
        keys_ref[pl.ds(r0, tk), :] = key
        hi_ref[pl.ds(r0, tk), :] = lax.shift_right_arithmetic(key, HALF_BITS).astype(I16)
        return carry

    lax.fori_loop(0, nkc, score_chunk, 0)

    def count(pred):
        def body(c, acc):
            r0 = pl.multiple_of(c * tk, tk)
            m = pred(keys_ref[pl.ds(r0, tk), :], r0 + row)
            return acc + jnp.sum(m.astype(I32).reshape(tk // SUBLANES, SUBLANES, tq), axis=0)
        acc = lax.fori_loop(0, nkc, body, jnp.zeros((SUBLANES, tq), I32))
        return jnp.sum(acc, axis=0, keepdims=True)

    def count16(ref, pred):
        def body(c, acc):
            r0 = pl.multiple_of(c * tk, tk)
            ones = jnp.where(pred(ref[pl.ds(r0, tk), :]), jnp.int16(1), jnp.int16(0))
            parts = [ones[i:i + PACKED_SUBLANES] for i in range(0, tk, PACKED_SUBLANES)]
            while len(parts) > 1:
                parts = [a + b for a, b in zip(parts[::2], parts[1::2])]
            return acc + parts[0]
        acc = lax.fori_loop(0, nkc, body, jnp.zeros((PACKED_SUBLANES, tq), I16))
        return jnp.sum(acc.astype(I32), axis=0, keepdims=True)

    def kth_largest16(ref, k):
        thr = jnp.where(count16(ref, lambda v: v >= 0) >= k, jnp.int32(0), jnp.int32(INT16_MIN))

        def bit_step(b, thr):
            cand = thr | lax.shift_left(jnp.int32(1), (HALF_BITS - 2) - b)
            c16 = cand.astype(I16)
            return jnp.where(count16(ref, lambda v: v >= c16) >= k, cand, thr)

        return lax.fori_loop(0, HALF_BITS - 1, bit_step, thr)

    thr_hi = kth_largest16(hi_ref, topk)
    thr_hi16 = thr_hi.astype(I16)
    need_lo = topk - count16(hi_ref, lambda v: v > thr_hi16)

    def low_half_chunk(c, carry):
        r0 = pl.multiple_of(c * tk, tk)
        lo = ((keys_ref[pl.ds(r0, tk), :] & jnp.int32(0xFFFF)) + jnp.int32(INT16_MIN)).astype(I16)
        lo_ref[pl.ds(r0, tk), :] = jnp.where(hi_ref[pl.ds(r0, tk), :] == thr_hi16, lo, jnp.int16(INT16_MIN))
        return carry

    lax.fori_loop(0, nkc, low_half_chunk, 0)
    thr_lo = kth_largest16(lo_ref, need_lo)
    thr = lax.shift_left(thr_hi, HALF_BITS) | (thr_lo - jnp.int32(INT16_MIN))
    def tie_search():
        need = topk - count(lambda k, _: k > thr)

        def idx_step(b, q):
            cand = q | lax.shift_left(jnp.int32(1), (seq.bit_length() - 2) - b)
            return jnp.where(count(lambda k, idx: (k == thr) & (idx < cand)) < need, cand, q)

        return lax.fori_loop(0, seq.bit_length() - 1, idx_step, jnp.zeros((1, tq), I32))

    has_ties = jnp.max(count(lambda k, _: k >= thr)) > topk
    last = lax.cond(has_ties, tie_search, lambda: jnp.full((1, tq), seq, I32))

    def bias_chunk(c, carry):
        r0 = pl.multiple_of(c * tk, tk)
        k = keys_ref[pl.ds(r0, tk), :]
        idx = r0 + row
        sel = (k > thr) | ((k == thr) & (idx <= last))
        bias_ref[pl.ds(r0, tk), :] = jnp.where(sel & (idx <= qpos), 0.0, NEG)
        return carry

    lax.fori_loop(0, nkc, bias_chunk, 0)

    outs = _staged_attention(
        DSA_H, 0, nkc, tk,
        lambda h, r0: _dot(kk_ref[pl.ds(r0, tk), :], _pad_rows(qT_ref[h * HD:(h + 1) * HD, :], 0))
        + bias_ref[pl.ds(r0, tk), :],
        lambda h, r0: vT_ref[:, pl.ds(r0, tk)], s_ref, mc_ref, m_ref, l8_ref, acc_ref)
    y_ref[...] = jnp.concatenate(outs, axis=0).T.astype(BF16)


def _dsa_attn(kk, qT, iqT, vT, iwT, batch, seq):
    t = kk.shape[0]
    tq = DSA_TQ
    nq = seq // tq
    topk = min(DSA_TOPK_MAX, seq // 4)
    colq = lambda n: pl.BlockSpec((n, tq), lambda b, i: (0, b * nq + i))
    return pl.pallas_call(
        functools.partial(_dsa_attn_body, topk=topk, seq=seq),
        out_shape=jax.ShapeDtypeStruct((t, DSA_H * HD), BF16),
        grid=(batch, nq),
        in_specs=[
            pl.BlockSpec((seq, LANES), lambda b, i: (b, 0)),
            colq(DSA_H * HD), colq(IDX_H * IDX_D),
            pl.BlockSpec((HD, seq), lambda b, i: (0, b)),
            colq(IDX_H),
        ],
        out_specs=pl.BlockSpec((tq, DSA_H * HD), lambda b, i: (b * nq + i, 0)),
        scratch_shapes=[pltpu.VMEM((seq, tq), I32), pltpu.VMEM((seq, tq), I16), pltpu.VMEM((seq, tq), I16),
                        pltpu.VMEM((seq, tq), F32),
                        pltpu.VMEM((2, DSA_H, DSA_TK, tq), F32), pltpu.VMEM((2, DSA_H * SUBLANES, tq), F32),
                        pltpu.VMEM((2 * SUBLANES, tq), F32), pltpu.VMEM((DSA_H * SUBLANES, tq), F32),
                        pltpu.VMEM((DSA_H * HD, tq), F32)],
        compiler_params=_cparams("parallel", "arbitrary"),
        name="dsa_attn",
    )(kk, qT, iqT, vT, iwT)


NSA_KV = NSA_G * HD
NSA_GATES = NSA_H * 3
NSA_GATES_PAD = -NSA_GATES % SUBLANES


def _nsa_proj_body(x_ref, g_ref, wtok_ref, wfeat_ref, cos_ref, sin_ref, qn_ref, kns_ref, knw_ref, mqn_ref,
                   kc_ref, vc_ref, ks_ref, kw_ref, qT_ref, vsT_ref, vwT_ref, gT_ref, qmT_ref):
    h = _rms(x_ref[...], g_ref[...]).astype(BF16)
    tok = _dot(h, wtok_ref[...])
    kc_ref[...] = tok[:, :NSA_KV]
    vc_ref[...] = tok[:, NSA_KV:]
    feat = _dot_nt(wfeat_ref[...], h)
    tm = feat.shape[-1]
    cos, sin = cos_ref[...], sin_ref[...]
    o = 0
    q = feat[o:o + NSA_H * HD]; o += NSA_H * HD
    ks = feat[o:o + NSA_KV]; o += NSA_KV
    kw = feat[o:o + NSA_KV]; o += NSA_KV
    vs = feat[o:o + NSA_KV]; o += NSA_KV
    vw = feat[o:o + NSA_KV]; o += NSA_KV
    qm = feat[o:o + MEM_W]; o += MEM_W
    gates = feat[o:]
    qT_ref[...] = (_heads_norm_rope(q, _tile_lanes(qn_ref[...], tm), cos, sin) * SCALE).astype(BF16)
    ks_ref[...] = _heads_norm_rope(ks, _tile_lanes(kns_ref[...], tm), cos, sin).T.astype(BF16)
    kw_ref[...] = _heads_norm_rope(kw, _tile_lanes(knw_ref[...], tm), cos, sin).T.astype(BF16)
    vsT_ref[...] = vs.astype(BF16)
    vwT_ref[...] = vw.astype(BF16)
    gT_ref[...] = jax.nn.sigmoid(gates)
    qmT_ref[...] = (_heads_norm_rope(qm, _tile_lanes(mqn_ref[...], tm), None, None) * SCALE).astype(BF16)


def _nsa_proj(x, g, w_in, qn, kn, mem_qn, seq):
    t, d = x.shape
    tm = min(PROJ_TM, seq)
    npos = seq // tm
    wq, wkc, wvc, wks, wvs, wkw, wvw, wgt, wqm = jnp.split(w_in, np.cumsum(NSA_SIZES)[:-1].tolist(), axis=1)
    wtok = jnp.concatenate([wkc, wvc], axis=1).astype(BF16)
    wgt = jnp.pad(wgt, ((0, 0), (0, NSA_GATES_PAD)))
    wfeat = jnp.concatenate([wq, wks, wkw, wvs, wvw, wqm, wgt], axis=1).T.astype(BF16)
    cosF, sinF = _rope_tables(jnp.arange(seq))
    row = lambda n: pl.BlockSpec((tm, n), lambda i: (i, 0))
    col = lambda n: pl.BlockSpec((n, tm), lambda i: (0, i))
    tab = pl.BlockSpec((HALF, tm), lambda i: (0, i % npos))
    gain = _const_spec((HD, LANES))
    ngt = NSA_GATES + NSA_GATES_PAD
    return pl.pallas_call(
        _nsa_proj_body,
        out_shape=(
            jax.ShapeDtypeStruct((t, NSA_KV), F32), jax.ShapeDtypeStruct((t, NSA_KV), F32),
            jax.ShapeDtypeStruct((t, NSA_KV), BF16), jax.ShapeDtypeStruct((t, NSA_KV), BF16),
            jax.ShapeDtypeStruct((NSA_H * HD, t), BF16), jax.ShapeDtypeStruct((NSA_KV, t), BF16),
            jax.ShapeDtypeStruct((NSA_KV, t), BF16), jax.ShapeDtypeStruct((ngt, t), F32),
            jax.ShapeDtypeStruct((MEM_W, t), BF16),
        ),
        grid=(t // tm,),
        in_specs=[row(d), _const_spec((1, d)), _const_spec(wtok.shape), _const_spec(wfeat.shape),
                  tab, tab, gain, gain, gain, gain],
        out_specs=(row(NSA_KV), row(NSA_KV), row(NSA_KV), row(NSA_KV), col(NSA_H * HD), col(NSA_KV), col(NSA_KV),
                   col(ngt), col(MEM_W)),
        compiler_params=_cparams("parallel"),
        name="nsa_proj",
    )(x, g.reshape(1, d), wtok, wfeat, cosF, sinF, _lane_gain(qn), _lane_gain(kn[1]), _lane_gain(kn[2]),
      _lane_gain(mem_qn))


def _nsa_cmp_body(xk_ref, xv_ref, pk_ref, pv_ref, wk_ref, wv_ref, kn_ref, cos_ref, sin_ref, k_ref, vT_ref):
    def compress(x, pos, w_ref):
        n = x.shape[0]
        xa = (x + pos[0:1]).astype(BF16)
        xb = (pltpu.roll(x, n - 1, axis=0) + pos[1:2]).astype(BF16)
        return _dot_nt(w_ref[0], xa) + _dot_nt(w_ref[1], xb)

    kT = compress(xk_ref[0], pk_ref[...], wk_ref)
    kT = _heads_norm_rope(kT, _tile_lanes(kn_ref[...], kT.shape[-1]), cos_ref[...], sin_ref[...])
    k_ref[0] = kT.T.astype(BF16)
    vT_ref[0] = compress(xv_ref[0], pv_ref[...], wv_ref).astype(BF16)


def _nsa_cmp_weights(w, pos):
    eye = jnp.eye(NSA_G, dtype=F32)
    halves = w.reshape(CMP_L // CMP_S, CMP_S, HD, HD)
    wt = jnp.einsum('hg,alde->ahelgd', eye, halves).reshape(CMP_L // CMP_S, NSA_KV, CMP_S * NSA_KV)
    p = jnp.broadcast_to(pos.reshape(CMP_L // CMP_S, CMP_S, 1, HD), (CMP_L // CMP_S, CMP_S, NSA_G, HD))
    return wt.astype(BF16), p.reshape(CMP_L // CMP_S, CMP_S * NSA_KV).astype(F32)


def _nsa_cmp(kc, vc, pos_k, pos_v, wk, wv, kn0, batch, seq):
    n = seq // CMP_S
    width = CMP_S * NSA_KV
    xk = kc.reshape(batch, n, width)
    xv = vc.reshape(batch, n, width)
    wkt, pk = _nsa_cmp_weights(wk, pos_k)
    wvt, pv = _nsa_cmp_weights(wv, pos_v)
    cosE, sinE = _rope_tables(jnp.arange(n) * CMP_S + (CMP_L - 1))
    xspec = pl.BlockSpec((1, n, width), lambda b: (b, 0, 0))
    return pl.pallas_call(
        _nsa_cmp_body,
        out_shape=(jax.ShapeDtypeStruct((batch, n, NSA_KV), BF16), jax.ShapeDtypeStruct((batch, NSA_KV, n), BF16)),
        grid=(batch,),
        in_specs=[xspec, xspec, _const_spec(pk.shape), _const_spec(pv.shape), _const_spec(wkt.shape),
                  _const_spec(wvt.shape), _const_spec((HD, LANES)), _const_spec((HALF, n)), _const_spec((HALF, n))],
        out_specs=(pl.BlockSpec((1, n, NSA_KV), lambda b: (b, 0, 0)), pl.BlockSpec((1, NSA_KV, n), lambda b: (b, 0, 0))),
        compiler_params=_cparams("parallel"),
        name="nsa_cmp",
    )(xk, xv, pk, pv, wkt, wvt, _lane_gain(kn0), cosE, sinE)


NSA_TQ = 256
NSA_TK = 256


def _staged_attention(nheads, lo, hi, tk, logits, values, s_ref, mc_ref, m_ref, l8_ref, acc_ref):
    tq = s_ref.shape[-1]
    fold = lambda a: a.reshape(tk // SUBLANES, SUBLANES, tq)
    m_ref[...] = jnp.full(m_ref.shape, -jnp.inf, F32)
    l8_ref[...] = jnp.zeros(l8_ref.shape, F32)
    acc_ref[...] = jnp.zeros(acc_ref.shape, F32)

    start = lambda c: c * tk if isinstance(c, int) else pl.multiple_of(c * tk, tk)

    def stage(h, c, slot):
        s = logits(h, start(c)) * LOG2E
        s_ref[slot, h] = s
        mc_ref[slot, h * SUBLANES:(h + 1) * SUBLANES, :] = jnp.max(fold(s), axis=0)

    def consume(h, c, slot):
        g8, hrows = slice(h * SUBLANES, (h + 1) * SUBLANES), slice(h * HD, (h + 1) * HD)
        m_old = m_ref[h:h + 1, :]
        m_new = jnp.maximum(m_old, jnp.max(mc_ref[slot, g8, :], axis=0, keepdims=True))
        alpha = jnp.exp2(m_old - m_new)
        p = jnp.exp2(s_ref[slot, h] - m_new)
        l8_ref[g8, :] = l8_ref[g8, :] * alpha + jnp.sum(fold(p), axis=0)
        acc_ref[hrows, :] = acc_ref[hrows, :] * alpha + _dot(values(h, start(c)), p.astype(BF16))
        m_ref[h:h + 1, :] = m_new

    def consume_and_stage_next(c, slot):
        for h in range(nheads):
            consume(h, c, slot)
            stage(h, c + 1, 1 - slot)

    def finish(c, slot):
        for h in range(nheads):
            consume(h, c, slot)

    for h in range(nheads):
        stage(h, lo, 0)
    n_fused = hi - 1 - lo

    def two_chunks(j, carry):
        c = lo + 2 * j
        consume_and_stage_next(c, 0)
        consume_and_stage_next(c + 1, 1)
        return carry

    if isinstance(n_fused, int):
        for i in range(n_fused):
            consume_and_stage_next(lo + i, i % 2)
        finish(hi - 1, n_fused % 2)
    else:
        lax.fori_loop(0, n_fused // 2, two_chunks, 0)

        @pl.when(n_fused % 2 == 1)
        def _():
            consume_and_stage_next(hi - 2, 0)
            finish(hi - 1, 1)

        @pl.when(n_fused % 2 == 0)
        def _():
            finish(hi - 1, 0)

    return [acc_ref[h * HD:(h + 1) * HD, :] / jnp.sum(l8_ref[h * SUBLANES:(h + 1) * SUBLANES, :], axis=0, keepdims=True)
            for h in range(nheads)]


def _nsa_attn_body(qT_ref, gT_ref, ks_ref, vsT_ref, kw_ref, vwT_ref, kc_ref, vcT_ref, cov_ref, y_ref,
                   bias_ref, out_ref, s_ref, mc_ref, m_ref, l8_ref, acc_ref, *, seq):
    tq, tk = NSA_TQ, NSA_TK
    qs = pl.program_id(1) * tq
    nkc = (qs + tq) // tk
    wlo = jnp.maximum((qs - WIN) // tk, 0)
    qpos = qs + lax.broadcasted_iota(I32, (1, tq), 1)
    ncr = seq // CMP_S
    nblk = seq // SLC_L
    n_sel = min(SLC_N_MAX, nblk)
    valid_c = CMP_S * lax.broadcasted_iota(I32, (ncr, tq), 0) + (CMP_L - 1) <= qpos
    any_c = jnp.where(qpos >= CMP_L - 1, 1.0, 0.0)
    jrow = lax.broadcasted_iota(I32, (nblk, tq), 0)
    qblk = qpos // SLC_L
    forced = (jrow == 0) | (jrow == qblk) | (jrow == qblk - 1)
    causal_blk = jrow * SLC_L <= qpos
    krow = lax.broadcasted_iota(I32, (tk, tq), 0)
    brow = lax.broadcasted_iota(I32, (SLC_L, tq), 0)
    group = lambda h: h // NSA_R
    pair = lambda h: slice((group(h) // 2) * LANES, (group(h) // 2 + 1) * LANES)
    grows = lambda h: slice(group(h) * HD, (group(h) + 1) * HD)
    q_of = lambda h: _pad_rows(qT_ref[h * HD:(h + 1) * HD, :], group(h) % 2)
    gate = lambda h, branch: gT_ref[h * 3 + branch:h * 3 + branch + 1, :]

    for g in range(NSA_G):
        heads = [g * NSA_R + r for r in range(NSA_R)]
        kcm = kc_ref[0][:, pair(heads[0])]
        vcm = vcT_ref[0][grows(heads[0]), :]
        psum = jnp.zeros((ncr, tq), F32)
        for h in heads:
            s = jnp.where(valid_c, _dot(kcm, q_of(h)), NEG)
            p = jnp.exp(s - jnp.max(s, axis=0, keepdims=True))
            p = p / jnp.sum(p, axis=0, keepdims=True) * any_c
            out_ref[h * HD:(h + 1) * HD, :] = gate(h, 0) * _dot(vcm, p.astype(BF16))
            psum = psum + p
        p_hi = psum.astype(BF16)
        p_lo = (psum - p_hi.astype(F32)).astype(BF16)
        imp = _dot(cov_ref[...], p_hi) + _dot(cov_ref[...], p_lo)
        imp = jnp.where(causal_blk, imp + jnp.where(forced, FORCE, 0.0), NEG)
        rank = jnp.zeros((nblk, tq), I32)
        for j2 in range(nblk):
            rj = imp[j2:j2 + 1]
            beats = (rj > imp) | ((rj == imp) & (j2 < jrow))
            rank = rank + beats.astype(I32)
        selb = jnp.where(rank < n_sel, 0.0, NEG)
        for j in range(nblk):
            bias_ref[g, j * SLC_L:(j + 1) * SLC_L, :] = jnp.where(j * SLC_L + brow <= qpos, selb[j:j + 1], NEG)

    stats = (s_ref, mc_ref, m_ref, l8_ref, acc_ref)
    slc = _staged_attention(
        NSA_H, 0, nkc, tk,
        lambda h, r0: _dot(ks_ref[pl.ds(r0, tk), pair(h)], q_of(h)) + bias_ref[group(h), pl.ds(r0, tk), :],
        lambda h, r0: vsT_ref[grows(h), pl.ds(r0, tk)], *stats)
    for h in range(NSA_H):
        out_ref[h * HD:(h + 1) * HD, :] = out_ref[h * HD:(h + 1) * HD, :] + gate(h, 1) * slc[h]

    def win_logits(h, r0):
        dist = qpos - (r0 + krow)
        return jnp.where((dist >= 0) & (dist < WIN), _dot(kw_ref[pl.ds(r0, tk), pair(h)], q_of(h)), NEG)

    win = _staged_attention(NSA_H, wlo, nkc, tk, win_logits, lambda h, r0: vwT_ref[grows(h), pl.ds(r0, tk)], *stats)
    outs = [out_ref[h * HD:(h + 1) * HD, :] + gate(h, 2) * win[h] for h in range(NSA_H)]
    y_ref[...] = jnp.concatenate(outs, axis=0).T.astype(BF16)


def _nsa_attn(qT, gT, ks, vsT, kw, vwT, kcmp, vcmpT, batch, seq):
    t = ks.shape[0]
    tq = NSA_TQ
    nq = seq // tq
    ncr, nblk = seq // CMP_S, seq // SLC_L
    starts = np.arange(ncr) * CMP_S
    sstart = np.arange(nblk) * SLC_L
    cover = (starts[None, :] < sstart[:, None] + SLC_L) & (starts[None, :] + CMP_L > sstart[:, None])
    cover[:, ncr - 1] = False
    colq = lambda n: pl.BlockSpec((n, tq), lambda b, i: (0, b * nq + i))
    tok = pl.BlockSpec((seq, NSA_KV), lambda b, i: (b, 0))
    feat = pl.BlockSpec((NSA_KV, seq), lambda b, i: (0, b))
    return pl.pallas_call(
        functools.partial(_nsa_attn_body, seq=seq),
        out_shape=jax.ShapeDtypeStruct((t, NSA_H * HD), BF16),
        grid=(batch, nq),
        in_specs=[
            colq(NSA_H * HD), colq(gT.shape[0]), tok, feat, tok, feat,
            pl.BlockSpec((1, ncr, NSA_KV), lambda b, i: (b, 0, 0)),
            pl.BlockSpec((1, NSA_KV, ncr), lambda b, i: (b, 0, 0)),
            _const_spec((nblk, ncr)),
        ],
        out_specs=pl.BlockSpec((tq, NSA_H * HD), lambda b, i: (b * nq + i, 0)),
        scratch_shapes=[pltpu.VMEM((NSA_G, seq, tq), F32), pltpu.VMEM((NSA_H * HD, tq), F32),
                        pltpu.VMEM((2, NSA_H, NSA_TK, tq), F32), pltpu.VMEM((2, NSA_H * SUBLANES, tq), F32),
                        pltpu.VMEM((2 * SUBLANES, tq), F32), pltpu.VMEM((NSA_H * SUBLANES, tq), F32),
                        pltpu.VMEM((NSA_H * HD, tq), F32)],
        compiler_params=_cparams("parallel", "arbitrary"),
        name="nsa_attn",
    )(qT, gT, ks, vsT, kw, vwT, kcmp, vcmpT, jnp.asarray(cover, BF16))


def _nsa_layer_mix(x, g, w_in, qn, kn, pos_k, pos_v, wk, wv, mem_qn, batch, seq):
    kc, vc, ks, kw, qT, vsT, vwT, gT, qmT = _nsa_proj(x, g, w_in, qn, kn, mem_qn, seq)
    kcmp, vcmpT = _nsa_cmp(kc, vc, pos_k, pos_v, wk, wv, kn[0], batch, seq)
    return _nsa_attn(qT, gT, ks, vsT, kw, vwT, kcmp, vcmpT, batch, seq), qmT


def kernel(x, mem, ffn_norm, ffn_w_gate, ffn_w_up, ffn_w_down, mix_norm, w_out, mem_norm, mem_w_kv, mem_qn, mem_kn, ret_w_in, dsa_w_in, dsa_qn, dsa_kn, nsa_w_in, nsa_qn, nsa_kn, nsa_cmp_pos_k, nsa_cmp_pos_v, nsa_cmp_wk, nsa_cmp_wv):
    batch, seq, d = x.shape
    x = x.reshape(batch * seq, d)
    for i in range(ffn_norm.shape[0]):
        x = _ffn(x, ffn_norm[i, 0], ffn_w_gate[i, 0], ffn_w_up[i, 0], ffn_w_down[i, 0])
        kind, j = i % N_MIXERS, i // N_MIXERS
        if kind == 0:
            q, v, gate, kT, qmT = _ret_proj(x, mix_norm[i], ret_w_in[j], mem_qn[i], seq)
            y_mix = _retention(q, kT, v, gate, batch, seq)
        elif kind == 1:
            kk, qT, iqT, vT, iwT, qmT = _dsa_proj(x, mix_norm[i], dsa_w_in[j], dsa_qn[j], dsa_kn[j], mem_qn[i], seq)
            y_mix = _dsa_attn(kk, qT, iqT, vT, iwT, batch, seq)
        else:
            y_mix, qmT = _nsa_layer_mix(x, mix_norm[i], nsa_w_in[j], nsa_qn[j], nsa_kn[j], nsa_cmp_pos_k[j],
                                        nsa_cmp_pos_v[j], nsa_cmp_wk[j], nsa_cmp_wv[j], mem_qn[i], batch, seq)
        mem_k, mem_vT = _mem_kv(mem, mem_norm[i], mem_w_kv[i], mem_kn[i])
        y_mem = _mem_attn(qmT, mem_k, mem_vT, seq)
        x = _mix_out_ffn(y_mix, y_mem, x, w_out[i], ffn_norm[i, 1], ffn_w_gate[i, 1], ffn_w_up[i, 1], ffn_w_down[i, 1])
    return x.reshape(batch, seq, d)
```

```python
import functools
import math

import jax
import jax.numpy as jnp
import numpy as np
from jax import lax
from jax.experimental import pallas as pl
from jax.experimental.pallas import tpu as pltpu

D_MODEL = 1024
HD = 64
HALF = HD // 2
H_MIX = 12
H_MEM = 4
MIX_W = H_MIX * HD
MEM_W = H_MEM * HD
D_FF = 2816
ROPE_THETA = 10000.0
EPS = 1e-6
NEG = -1e30
FORCE = 1e9
SCALE = HD ** -0.5
LOG2E = math.log2(math.e)

RET_H, RET_DK, RET_DV, RET_CHUNK = 6, 64, 128, 128
DSA_H, IDX_H, IDX_D, DSA_TOPK_MAX = 12, 8, 64, 256
NSA_H, NSA_G, CMP_L, CMP_S, SLC_L, SLC_N_MAX, WIN = 12, 4, 32, 16, 64, 16, 512
NSA_R = NSA_H // NSA_G
N_MIXERS = 3

RET_SIZES = [RET_H * RET_DK, RET_H * RET_DK, RET_H * RET_DV, RET_H * RET_DV, MEM_W]
DSA_SIZES = [DSA_H * HD, HD, HD, IDX_H * IDX_D, IDX_D, IDX_H, MEM_W]
NSA_SIZES = [NSA_H * HD] + [NSA_G * HD] * 6 + [NSA_H * 3, MEM_W]

LANES = 128
SUBLANES = 8
PACKED_SUBLANES = 2 * SUBLANES
VAUG = HD + PACKED_SUBLANES
MXU_N = 256
VMEM_LIMIT_BYTES = 56 * 1024 * 1024

BF16 = jnp.bfloat16
F32 = jnp.float32
I32 = jnp.int32
I16 = jnp.int16


def _cparams(*sem):
    return pltpu.CompilerParams(dimension_semantics=sem, vmem_limit_bytes=VMEM_LIMIT_BYTES)


def _const_spec(shape):
    n = len(shape)
    return pl.BlockSpec(shape, lambda *_: (0,) * n, pipeline_mode=pl.Buffered(1))


def _rms(x, g):
    return x * lax.rsqrt(jnp.mean(x * x, axis=-1, keepdims=True) + EPS) * g


def _dot(a, b):
    return jnp.dot(a, b, preferred_element_type=F32)


def _dot_nt(a, b):
    return lax.dot_general(a, b, (((1,), (1,)), ((), ())), preferred_element_type=F32)


def _tile_lanes(a, n):
    reps = n // a.shape[-1]
    return a if reps == 1 else jnp.concatenate([a] * reps, axis=-1)


def _rope_tables(pos):
    inv = ROPE_THETA ** (-jnp.arange(HALF, dtype=F32) / HALF)
    ang = pos.astype(F32)[:, None] * inv[None, :]
    return jnp.cos(ang).T, jnp.sin(ang).T


def _lane_gain(g):
    return jnp.broadcast_to(g.astype(F32)[:, None], (g.shape[0], LANES))


def _augment_values(v):
    t = v.shape[-1]
    v3 = v.reshape(v.shape[0] // HD, HD, t)
    tail = jnp.where(lax.broadcasted_iota(I32, (v3.shape[0], VAUG - HD, t), 1) == 0, 1.0, 0.0).astype(v.dtype)
    return jnp.concatenate([v3, tail], axis=1).reshape(v3.shape[0] * VAUG, t)


def _heads_norm_rope(x, gain, cos, sin):
    t = x.shape[-1]
    x3 = x.reshape(x.shape[0] // HD, HD, t)
    if gain is not None:
        ms = jnp.sum(x3 * x3, axis=1, keepdims=True) * (1.0 / HD)
        x3 = x3 * lax.rsqrt(ms + EPS) * gain[None]
    if cos is not None:
        x1, x2 = x3[:, :HALF], x3[:, HALF:]
        x3 = jnp.concatenate([x1 * cos[None] - x2 * sin[None], x2 * cos[None] + x1 * sin[None]], axis=1)
    return x3.reshape(x.shape)


FFN_TM = 1024
FFN_TF = MXU_N


def _swiglu_half_step(x, g_ref, wg_ref, wu_ref, wd_ref):
    h = _rms(x, g_ref[...]).astype(BF16)
    acc = jnp.zeros(x.shape, F32)
    for c in range(D_FF // FFN_TF):
        sl = slice(c * FFN_TF, (c + 1) * FFN_TF)
        gate = _dot(h, wg_ref[:, sl])
        up = _dot(h, wu_ref[:, sl])
        act = (gate * jax.nn.sigmoid(gate) * up).astype(BF16)
        acc = acc + _dot(act, wd_ref[sl, :])
    return x + 0.5 * acc


def _ffn_body(x_ref, g_ref, wg_ref, wu_ref, wd_ref, o_ref):
    o_ref[...] = _swiglu_half_step(x_ref[...], g_ref, wg_ref, wu_ref, wd_ref)


def _mix_out_ffn_body(ymix_ref, ymem_ref, x_ref, wmix_ref, wmem_ref, g_ref, wg_ref, wu_ref, wd_ref, o_ref):
    x = x_ref[...] + _dot(ymix_ref[...], wmix_ref[...]) + _dot(ymem_ref[...], wmem_ref[...])
    o_ref[...] = _swiglu_half_step(x, g_ref, wg_ref, wu_ref, wd_ref)


def _ffn_specs(d, layer, half):
    pick = lambda r, c: pl.BlockSpec((None, None, r, c), lambda i: (layer, half, 0, 0), pipeline_mode=pl.Buffered(1))
    return [_const_spec((1, d)), pick(d, D_FF), pick(d, D_FF), pick(D_FF, d)]


def _ffn(x, g, wg, wu, wd, layer, half):
    t, d = x.shape
    tm = min(FFN_TM, t)
    return pl.pallas_call(
        _ffn_body,
        out_shape=jax.ShapeDtypeStruct((t, d), F32),
        grid=(t // tm,),
        in_specs=[pl.BlockSpec((tm, d), lambda i: (i, 0))] + _ffn_specs(d, layer, half),
        out_specs=pl.BlockSpec((tm, d), lambda i: (i, 0)),
        compiler_params=_cparams("parallel"),
        name="ffn",
    )(x, g.reshape(1, d), wg, wu, wd)


def _mix_out_ffn(y_mix, y_mem, x, w_out, g, wg, wu, wd, layer, half):
    t, d = x.shape
    tm = min(FFN_TM, t)
    return pl.pallas_call(
        _mix_out_ffn_body,
        out_shape=jax.ShapeDtypeStruct((t, d), F32),
        grid=(t // tm,),
        in_specs=[
            pl.BlockSpec((tm, MIX_W), lambda i: (i, 0)),
            pl.BlockSpec((tm, MEM_W), lambda i: (i, 0)),
            pl.BlockSpec((tm, d), lambda i: (i, 0)),
            _const_spec((MIX_W, d)),
            _const_spec((MEM_W, d)),
        ] + _ffn_specs(d, layer, half),
        out_specs=pl.BlockSpec((tm, d), lambda i: (i, 0)),
        compiler_params=_cparams("parallel"),
        name="mix_out_ffn",
    )(y_mix, y_mem, x, w_out[:MIX_W].astype(BF16), w_out[MIX_W:].astype(BF16), g.reshape(1, d), wg, wu, wd)


PROJ_TM = 512


def _mem_kv_body(m_ref, g_ref, wk_ref, wvT_ref, kn_ref, k_ref, vT_ref):
    h = _rms(m_ref[0], g_ref[...]).astype(BF16)
    kT = _dot_nt(wk_ref[...], h)
    kT = _heads_norm_rope(kT, _tile_lanes(kn_ref[...], kT.shape[-1]), None, None)
    k_ref[0] = kT.T.astype(BF16)
    vT_ref[0] = _augment_values(_dot_nt(wvT_ref[...], h)).astype(BF16)


def _mem_kv(mem, g, w_kv, kn):
    b, m, d = mem.shape
    return pl.pallas_call(
        _mem_kv_body,
        out_shape=(jax.ShapeDtypeStruct((b, m, MEM_W), BF16), jax.ShapeDtypeStruct((b, H_MEM * VAUG, m), BF16)),
        grid=(b,),
        in_specs=[
            pl.BlockSpec((1, m, d), lambda i: (i, 0, 0)),
            _const_spec((1, d)),
            _const_spec((MEM_W, d)),
            _const_spec((MEM_W, d)),
            _const_spec((HD, LANES)),
        ],
        out_specs=(pl.BlockSpec((1, m, MEM_W), lambda i: (i, 0, 0)),
                   pl.BlockSpec((1, H_MEM * VAUG, m), lambda i: (i, 0, 0))),
        compiler_params=_cparams("parallel"),
        name="mem_kv",
    )(mem, g.reshape(1, d), w_kv[:, :MEM_W].T.astype(BF16), w_kv[:, MEM_W:].T.astype(BF16), _lane_gain(kn))


MEM_TQ = 512


def _pad_rows(blk, half, total=LANES):
    z = jnp.zeros_like(blk)
    parts = [z] * (total // HD)
    parts[half] = blk
    return jnp.concatenate(parts, axis=0)


def _mem_attn_body(qT_ref, k_ref, vT_ref, o_ref, s_ref, mc_ref, m_ref, acc_ref):
    n_mem = k_ref.shape[1]
    outs = _staged_attention(
        H_MEM, 0, 1, n_mem,
        lambda h, r0: _dot(k_ref[0, :, (h // 2) * LANES:(h // 2 + 1) * LANES],
                           _pad_rows(qT_ref[h * HD:(h + 1) * HD, :], h % 2)),
        lambda h, r0: vT_ref[0, h * VAUG:(h + 1) * VAUG, :], s_ref, mc_ref, m_ref, acc_ref)
    o_ref[...] = jnp.concatenate(outs, axis=0).T.astype(BF16)


def _mem_attn(qmT, k, vT, seq):
    t = qmT.shape[1]
    b, m, _ = k.shape
    tq = min(MEM_TQ, seq)
    nq = seq // tq
    return pl.pallas_call(
        _mem_attn_body,
        out_shape=jax.ShapeDtypeStruct((t, MEM_W), BF16),
        grid=(b, nq),
        in_specs=[
            pl.BlockSpec((MEM_W, tq), lambda bi, i: (0, bi * nq + i)),
            pl.BlockSpec((1, m, MEM_W), lambda bi, i: (bi, 0, 0)),
            pl.BlockSpec((1, H_MEM * VAUG, m), lambda bi, i: (bi, 0, 0)),
        ],
        out_specs=pl.BlockSpec((tq, MEM_W), lambda bi, i: (bi * nq + i, 0)),
        scratch_shapes=[pltpu.VMEM((2, H_MEM, m, tq), F32), pltpu.VMEM((2, H_MEM * SUBLANES, tq), F32),
                        pltpu.VMEM((SUBLANES, tq), F32), pltpu.VMEM((H_MEM * VAUG, tq), F32)],
        compiler_params=_cparams("parallel", "parallel"),
        name="mem_attn",
    )(qmT, k, vT)


def _ret_proj_body(x_ref, g_ref, wtok_ref, wfeat_ref, cosT_ref, sinT_ref, cosF_ref, sinF_ref, mqn_ref,
                   q_ref, v_ref, gate_ref, kT_ref, qmT_ref):
    h = _rms(x_ref[...], g_ref[...]).astype(BF16)
    tok = _dot(h, wtok_ref[...])
    nq, nv = RET_H * RET_DK, RET_H * RET_DV
    q = tok[:, :nq]
    lane = lax.broadcasted_iota(I32, q.shape, 1)
    rot = jnp.where(lane % HD < HALF, pltpu.roll(q, nq - HALF, axis=1), pltpu.roll(q, HALF, axis=1))
    q_ref[...] = (q * _tile_lanes(cosT_ref[...], nq) + rot * _tile_lanes(sinT_ref[...], nq)).astype(BF16)
    v_ref[...] = tok[:, nq:nq + nv].astype(BF16)
    gate_ref[...] = tok[:, nq + nv:]
    feat = _dot_nt(wfeat_ref[...], h)
    tm = feat.shape[-1]
    cos, sin = cosF_ref[...], sinF_ref[...]
    kT = _heads_norm_rope(feat[:nq], None, cos, sin) * (RET_DK ** -0.5)
    kT_ref[...] = kT.astype(BF16)
    qm = _heads_norm_rope(feat[nq:], _tile_lanes(mqn_ref[...], tm), None, None) * SCALE
    qmT_ref[...] = qm.astype(BF16)


def _token_rope_tables(seq):
    cosF, sinF = _rope_tables(jnp.arange(seq))
    cos = jnp.tile(cosF.T, (1, LANES // HALF))
    sign = jnp.where((jnp.arange(LANES) % HD) < HALF, -1.0, 1.0).astype(F32)
    sin = jnp.tile(sinF.T, (1, LANES // HALF)) * sign[None, :]
    return cos, sin


def _ret_proj(x, g, w_in, mem_qn, seq):
    t, d = x.shape
    tm = min(PROJ_TM, seq)
    npos = seq // tm
    wq, wk, wv, wg, wqm = jnp.split(w_in, np.cumsum(RET_SIZES)[:-1].tolist(), axis=1)
    wtok = jnp.concatenate([wq, wv, wg], axis=1).astype(BF16)
    wfeat = jnp.concatenate([wk, wqm], axis=1).T.astype(BF16)
    cosT, sinT = _token_rope_tables(seq)
    cosF, sinF = _rope_tables(jnp.arange(seq))
    nq, nv = RET_H * RET_DK, RET_H * RET_DV
    row = lambda n: pl.BlockSpec((tm, n), lambda i: (i, 0))
    col = lambda n: pl.BlockSpec((n, tm), lambda i: (0, i))
    return pl.pallas_call(
        _ret_proj_body,
        out_shape=(
            jax.ShapeDtypeStruct((t, nq), BF16), jax.ShapeDtypeStruct((t, nv), BF16),
            jax.ShapeDtypeStruct((t, nv), F32), jax.ShapeDtypeStruct((nq, t), BF16),
            jax.ShapeDtypeStruct((MEM_W, t), BF16),
        ),
        grid=(t // tm,),
        in_specs=[
            row(d), _const_spec((1, d)), _const_spec(wtok.shape), _const_spec(wfeat.shape),
            pl.BlockSpec((tm, LANES), lambda i: (i % npos, 0)), pl.BlockSpec((tm, LANES), lambda i: (i % npos, 0)),
            pl.BlockSpec((HALF, tm), lambda i: (0, i % npos)), pl.BlockSpec((HALF, tm), lambda i: (0, i % npos)),
            _const_spec((HD, LANES)),
        ],
        out_specs=(row(nq), row(nv), row(nv), col(nq), col(MEM_W)),
        compiler_params=_cparams("parallel"),
        name="ret_proj",
    )(x, g.reshape(1, d), wtok, wfeat, cosT, sinT, cosF, sinF, _lane_gain(mem_qn))


RET_TL = 512


def _ret_body(q_ref, kT_ref, v_ref, gate_ref, y_ref, state_ref):
    @pl.when(pl.program_id(1) == 0)
    def _():
        state_ref[...] = jnp.zeros_like(state_ref)

    c = RET_CHUNK
    ii = lax.broadcasted_iota(I32, (c, c), 0).astype(F32)
    jj = lax.broadcasted_iota(I32, (c, c), 1).astype(F32)
    diff = ii - jj
    jk = lax.broadcasted_iota(I32, (RET_DK, c), 1).astype(F32)
    log_gs = [math.log(1.0 - 2.0 ** (-5.0 - hh)) for hh in range(RET_H)]
    dmasks = [jnp.where(diff >= 0, jnp.exp(lg * jnp.maximum(diff, 0.0)), 0.0) for lg in log_gs]
    xis = [jnp.exp(lg * (ii + 1.0)) for lg in log_gs]
    zetas = [jnp.exp(lg * (c - 1.0 - jk)) for lg in log_gs]
    for n in range(q_ref.shape[0] // c):
        rows = slice(n * c, (n + 1) * c)
        for hh in range(RET_H):
            dmask, xi, zeta, decay = dmasks[hh], xis[hh], zetas[hh], math.exp(log_gs[hh] * c)
            pair = slice((hh // 2) * LANES, (hh // 2 + 1) * LANES)
            q2 = q_ref[rows, pair]
            kT = kT_ref[hh * RET_DK:(hh + 1) * RET_DK, rows]
            v = v_ref[rows, hh * RET_DV:(hh + 1) * RET_DV]
            state = state_ref[hh]
            inner = _dot(q2, _pad_rows(kT, hh % 2)) * dmask
            o = _dot(inner.astype(BF16), v)
            o = o + _dot(q2, _pad_rows(state.astype(BF16), hh % 2)) * xi
            kv = _dot((kT.astype(F32) * zeta).astype(BF16), v)
            state_ref[hh] = state * decay + kv
            mu = jnp.mean(o, axis=-1, keepdims=True)
            var = jnp.mean(jnp.square(o - mu), axis=-1, keepdims=True)
            o = (o - mu) * lax.rsqrt(var + EPS)
            gte = gate_ref[rows, hh * RET_DV:(hh + 1) * RET_DV]
            y_ref[rows, hh * RET_DV:(hh + 1) * RET_DV] = (gte * jax.nn.sigmoid(gte) * o).astype(BF16)


def _retention(q, kT, v, gate, batch, seq):
    t = q.shape[0]
    tl = min(RET_TL, seq)
    nl = seq // tl
    nq, nv = RET_H * RET_DK, RET_H * RET_DV
    return pl.pallas_call(
        _ret_body,
        out_shape=jax.ShapeDtypeStruct((t, nv), BF16),
        grid=(batch, nl),
        in_specs=[
            pl.BlockSpec((tl, nq), lambda b, i: (b * nl + i, 0)),
            pl.BlockSpec((nq, tl), lambda b, i: (0, b * nl + i)),
            pl.BlockSpec((tl, nv), lambda b, i: (b * nl + i, 0)),
            pl.BlockSpec((tl, nv), lambda b, i: (b * nl + i, 0)),
        ],
        out_specs=pl.BlockSpec((tl, nv), lambda b, i: (b * nl + i, 0)),
        scratch_shapes=[pltpu.VMEM((RET_H, RET_DK, RET_DV), F32)],
        compiler_params=_cparams("parallel", "arbitrary"),
        name="retention",
    )(q, kT, v, gate)


def _dsa_proj_body(x_ref, g_ref, w_ref, cos_ref, sin_ref, qn_ref, kn_ref, mqn_ref,
                   kk_ref, qT_ref, iqT_ref, vT_ref, iwT_ref, qmT_ref):
    h = _rms(x_ref[...], g_ref[...]).astype(BF16)
    feat = _dot_nt(w_ref[...], h)
    tm = feat.shape[-1]
    cos, sin = cos_ref[...], sin_ref[...]
    o = 0
    q = feat[o:o + DSA_H * HD]; o += DSA_H * HD
    iq = feat[o:o + IDX_H * IDX_D]; o += IDX_H * IDX_D
    k = feat[o:o + HD]; o += HD
    ik = feat[o:o + IDX_D]; o += IDX_D
    v = feat[o:o + HD]; o += HD
    qm = feat[o:o + MEM_W]; o += MEM_W
    iw = feat[o:o + IDX_H]
    qT_ref[...] = (_heads_norm_rope(q, _tile_lanes(qn_ref[...], tm), cos, sin) * SCALE).astype(BF16)
    iqT_ref[...] = _heads_norm_rope(iq, None, cos, sin).astype(BF16)
    k = _heads_norm_rope(k, _tile_lanes(kn_ref[...], tm), cos, sin)
    ik = _heads_norm_rope(ik, None, cos, sin)
    kk_ref[...] = jnp.concatenate([k, ik], axis=0).T.astype(BF16)
    vT_ref[...] = _augment_values(v).astype(BF16)
    iwT_ref[...] = iw
    qmT_ref[...] = (_heads_norm_rope(qm, _tile_lanes(mqn_ref[...], tm), None, None) * SCALE).astype(BF16)


def _dsa_proj(x, g, w_in, qn, kn, mem_qn, seq):
    t, d = x.shape
    tm = min(PROJ_TM, seq)
    npos = seq // tm
    wq, wk, wv, wiq, wik, wiw, wqm = jnp.split(w_in, np.cumsum(DSA_SIZES)[:-1].tolist(), axis=1)
    wfeat = jnp.concatenate([wq, wiq, wk, wik, wv, wqm, wiw], axis=1).T.astype(BF16)
    cosF, sinF = _rope_tables(jnp.arange(seq))
    col = lambda n: pl.BlockSpec((n, tm), lambda i: (0, i))
    tab = pl.BlockSpec((HALF, tm), lambda i: (0, i % npos))
    gain = _const_spec((HD, LANES))
    return pl.pallas_call(
        _dsa_proj_body,
        out_shape=(
            jax.ShapeDtypeStruct((t, LANES), BF16), jax.ShapeDtypeStruct((DSA_H * HD, t), BF16),
            jax.ShapeDtypeStruct((IDX_H * IDX_D, t), BF16), jax.ShapeDtypeStruct((VAUG, t), BF16),
            jax.ShapeDtypeStruct((IDX_H, t), F32), jax.ShapeDtypeStruct((MEM_W, t), BF16),
        ),
        grid=(t // tm,),
        in_specs=[pl.BlockSpec((tm, d), lambda i: (i, 0)), _const_spec((1, d)), _const_spec(wfeat.shape),
                  tab, tab, gain, gain, gain],
        out_specs=(pl.BlockSpec((tm, LANES), lambda i: (i, 0)), col(DSA_H * HD), col(IDX_H * IDX_D), col(VAUG),
                   col(IDX_H), col(MEM_W)),
        compiler_params=_cparams("parallel"),
        name="dsa_proj",
    )(x, g.reshape(1, d), wfeat, cosF, sinF, _lane_gain(qn), _lane_gain(kn), _lane_gain(mem_qn))


DSA_TQ = 256
DSA_TK = 256
HALF_BITS = 16
INT16_MIN = -2 ** (HALF_BITS - 1)


def _dsa_attn_body(kk_ref, qT_ref, iqT_ref, vT_ref, iwT_ref, y_ref, keys_ref, hi_ref, lo_ref, bias_ref, s_ref, mc_ref,
                   m_ref, acc_ref, *, topk, seq):
    tq, tk = DSA_TQ, DSA_TK
    qs = pl.program_id(1) * tq
    nkc = (qs + tq) // tk
    qpos = qs + lax.broadcasted_iota(I32, (1, tq), 1)
    row = lax.broadcasted_iota(I32, (tk, tq), 0)

    def score_chunk(c, carry):
        r0 = pl.multiple_of(c * tk, tk)
        kkc = kk_ref[pl.ds(r0, tk), :]
        acc = jnp.zeros((tk, tq), F32)
        for h in range(IDX_H):
            r = _dot(kkc, _pad_rows(iqT_ref[h * IDX_D:(h + 1) * IDX_D, :], 1))
            acc = acc + jnp.maximum(r, 0.0) * iwT_ref[h:h + 1, :]
        sc = jnp.where(r0 + row <= qpos, acc + 0.0, -jnp.inf)
        bits = pltpu.bitcast(sc, I32)
        key = jnp.where(bits < 0, bits ^ jnp.int32(0x7FFFFFFF), bits)
        keys_ref[pl.ds(r0, tk), :] = key
        hi_ref[pl.ds(r0, tk), :] = lax.shift_right_arithmetic(key, HALF_BITS).astype(I16)
        return carry

    lax.fori_loop(0, nkc, score_chunk, 0)

    def count(pred):
        def body(c, acc):
            r0 = pl.multiple_of(c * tk, tk)
            m = pred(keys_ref[pl.ds(r0, tk), :], r0 + row)
            return acc + jnp.sum(m.astype(I32).reshape(tk // SUBLANES, SUBLANES, tq), axis=0)
        acc = lax.fori_loop(0, nkc, body, jnp.zeros((SUBLANES, tq), I32))
        return jnp.sum(acc, axis=0, keepdims=True)

    def count16(ref, pred):
        def body(c, acc):
            r0 = pl.multiple_of(c * tk, tk)
            ones = jnp.where(pred(ref[pl.ds(r0, tk), :]), jnp.int16(1), jnp.int16(0))
            parts = [ones[i:i + PACKED_SUBLANES] for i in range(0, tk, PACKED_SUBLANES)]
            while len(parts) > 1:
                parts = [a + b for a, b in zip(parts[::2], parts[1::2])]
            return acc + parts[0]
        acc = lax.fori_loop(0, nkc, body, jnp.zeros((PACKED_SUBLANES, tq), I16))
        return jnp.sum(acc.astype(I32), axis=0, keepdims=True)

    def kth_largest16(ref, k):
        thr = jnp.where(count16(ref, lambda v: v >= 0) >= k, jnp.int32(0), jnp.int32(INT16_MIN))

        def bit_step(b, thr):
            cand = thr | lax.shift_left(jnp.int32(1), (HALF_BITS - 2) - b)
            c16 = cand.astype(I16)
            return jnp.where(count16(ref, lambda v: v >= c16) >= k, cand, thr)

        return lax.fori_loop(0, HALF_BITS - 1, bit_step, thr)

    thr_hi = kth_largest16(hi_ref, topk)
    thr_hi16 = thr_hi.astype(I16)
    need_lo = topk - count16(hi_ref, lambda v: v > thr_hi16)

    def low_half_chunk(c, carry):
        r0 = pl.multiple_of(c * tk, tk)
        lo = ((keys_ref[pl.ds(r0, tk), :] & jnp.int32(0xFFFF)) + jnp.int32(INT16_MIN)).astype(I16)
        lo_ref[pl.ds(r0, tk), :] = jnp.where(hi_ref[pl.ds(r0, tk), :] == thr_hi16, lo, jnp.int16(INT16_MIN))
        return carry

    lax.fori_loop(0, nkc, low_half_chunk, 0)
    thr_lo = kth_largest16(lo_ref, need_lo)
    thr = lax.shift_left(thr_hi, HALF_BITS) | (thr_lo - jnp.int32(INT16_MIN))
    def tie_search():
        need = topk - count(lambda k, _: k > thr)

        def idx_step(b, q):
            cand = q | lax.shift_left(jnp.int32(1), (seq.bit_length() - 2) - b)
            return jnp.where(count(lambda k, idx: (k == thr) & (idx < cand)) < need, cand, q)

        return lax.fori_loop(0, seq.bit_length() - 1, idx_step, jnp.zeros((1, tq), I32))

    has_ties = jnp.max(count(lambda k, _: k >= thr)) > topk
    last = lax.cond(has_ties, tie_search, lambda: jnp.full((1, tq), seq, I32))

    def bias_chunk(c, carry):
        r0 = pl.multiple_of(c * tk, tk)
        k = keys_ref[pl.ds(r0, tk), :]
        idx = r0 + row
        sel = (k > thr) | ((k == thr) & (idx <= last))
        bias_ref[pl.ds(r0, tk), :] = jnp.where(sel & (idx <= qpos), 0.0, NEG)
        return carry

    lax.fori_loop(0, nkc, bias_chunk, 0)

    outs = _staged_attention(
        DSA_H, 0, nkc, tk,
        lambda h, r0: _dot(kk_ref[pl.ds(r0, tk), :], _pad_rows(qT_ref[h * HD:(h + 1) * HD, :], 0))
        + bias_ref[pl.ds(r0, tk), :],
        lambda h, r0: vT_ref[:, pl.ds(r0, tk)], s_ref, mc_ref, m_ref, acc_ref)
    y_ref[...] = jnp.concatenate(outs, axis=0).T.astype(BF16)


def _dsa_attn(kk, qT, iqT, vT, iwT, batch, seq):
    t = kk.shape[0]
    tq = DSA_TQ
    nq = seq // tq
    topk = min(DSA_TOPK_MAX, seq // 4)
    colq = lambda n: pl.BlockSpec((n, tq), lambda b, i: (0, b * nq + i))
    return pl.pallas_call(
        functools.partial(_dsa_attn_body, topk=topk, seq=seq),
        out_shape=jax.ShapeDtypeStruct((t, DSA_H * HD), BF16),
        grid=(batch, nq),
        in_specs=[
            pl.BlockSpec((seq, LANES), lambda b, i: (b, 0)),
            colq(DSA_H * HD), colq(IDX_H * IDX_D),
            pl.BlockSpec((VAUG, seq), lambda b, i: (0, b)),
            colq(IDX_H),
        ],
        out_specs=pl.BlockSpec((tq, DSA_H * HD), lambda b, i: (b * nq + i, 0)),
        scratch_shapes=[pltpu.VMEM((seq, tq), I32), pltpu.VMEM((seq, tq), I16), pltpu.VMEM((seq, tq), I16),
                        pltpu.VMEM((seq, tq), F32),
                        pltpu.VMEM((2, DSA_H, DSA_TK, tq), F32), pltpu.VMEM((2, DSA_H * SUBLANES, tq), F32),
                        pltpu.VMEM((2 * SUBLANES, tq), F32), pltpu.VMEM((DSA_H * VAUG, tq), F32)],
        compiler_params=_cparams("parallel", "arbitrary"),
        name="dsa_attn",
    )(kk, qT, iqT, vT, iwT)


NSA_KV = NSA_G * HD
NSA_GATES = NSA_H * 3
NSA_GATES_PAD = -NSA_GATES % SUBLANES


def _nsa_proj_body(x_ref, g_ref, wtok_ref, wfeat_ref, cos_ref, sin_ref, qn_ref, kns_ref, knw_ref, mqn_ref,
                   kc_ref, vc_ref, ks_ref, kw_ref, qT_ref, vsT_ref, vwT_ref, gT_ref, qmT_ref):
    h = _rms(x_ref[...], g_ref[...]).astype(BF16)
    tok = _dot(h, wtok_ref[...])
    kc_ref[...] = tok[:, :NSA_KV]
    vc_ref[...] = tok[:, NSA_KV:]
    feat = _dot_nt(wfeat_ref[...], h)
    tm = feat.shape[-1]
    cos, sin = cos_ref[...], sin_ref[...]
    o = 0
    q = feat[o:o + NSA_H * HD]; o += NSA_H * HD
    ks = feat[o:o + NSA_KV]; o += NSA_KV
    kw = feat[o:o + NSA_KV]; o += NSA_KV
    vs = feat[o:o + NSA_KV]; o += NSA_KV
    vw = feat[o:o + NSA_KV]; o += NSA_KV
    qm = feat[o:o + MEM_W]; o += MEM_W
    gates = feat[o:]
    qT_ref[...] = (_heads_norm_rope(q, _tile_lanes(qn_ref[...], tm), cos, sin) * SCALE).astype(BF16)
    ks_ref[...] = _heads_norm_rope(ks, _tile_lanes(kns_ref[...], tm), cos, sin).T.astype(BF16)
    kw_ref[...] = _heads_norm_rope(kw, _tile_lanes(knw_ref[...], tm), cos, sin).T.astype(BF16)
    vsT_ref[...] = _augment_values(vs).astype(BF16)
    vwT_ref[...] = _augment_values(vw).astype(BF16)
    gT_ref[...] = jax.nn.sigmoid(gates)
    qmT_ref[...] = (_heads_norm_rope(qm, _tile_lanes(mqn_ref[...], tm), None, None) * SCALE).astype(BF16)


def _nsa_proj(x, g, w_in, qn, kn, mem_qn, seq):
    t, d = x.shape
    tm = min(PROJ_TM, seq)
    npos = seq // tm
    wq, wkc, wvc, wks, wvs, wkw, wvw, wgt, wqm = jnp.split(w_in, np.cumsum(NSA_SIZES)[:-1].tolist(), axis=1)
    wtok = jnp.concatenate([wkc, wvc], axis=1).astype(BF16)
    wgt = jnp.pad(wgt, ((0, 0), (0, NSA_GATES_PAD)))
    wfeat = jnp.concatenate([wq, wks, wkw, wvs, wvw, wqm, wgt], axis=1).T.astype(BF16)
    cosF, sinF = _rope_tables(jnp.arange(seq))
    row = lambda n: pl.BlockSpec((tm, n), lambda i: (i, 0))
    col = lambda n: pl.BlockSpec((n, tm), lambda i: (0, i))
    tab = pl.BlockSpec((HALF, tm), lambda i: (0, i % npos))
    gain = _const_spec((HD, LANES))
    ngt = NSA_GATES + NSA_GATES_PAD
    return pl.pallas_call(
        _nsa_proj_body,
        out_shape=(
            jax.ShapeDtypeStruct((t, NSA_KV), F32), jax.ShapeDtypeStruct((t, NSA_KV), F32),
            jax.ShapeDtypeStruct((t, NSA_KV), BF16), jax.ShapeDtypeStruct((t, NSA_KV), BF16),
            jax.ShapeDtypeStruct((NSA_H * HD, t), BF16), jax.ShapeDtypeStruct((NSA_G * VAUG, t), BF16),
            jax.ShapeDtypeStruct((NSA_G * VAUG, t), BF16), jax.ShapeDtypeStruct((ngt, t), F32),
            jax.ShapeDtypeStruct((MEM_W, t), BF16),
        ),
        grid=(t // tm,),
        in_specs=[row(d), _const_spec((1, d)), _const_spec(wtok.shape), _const_spec(wfeat.shape),
                  tab, tab, gain, gain, gain, gain],
        out_specs=(row(NSA_KV), row(NSA_KV), row(NSA_KV), row(NSA_KV), col(NSA_H * HD), col(NSA_G * VAUG),
                   col(NSA_G * VAUG), col(ngt), col(MEM_W)),
        compiler_params=_cparams("parallel"),
        name="nsa_proj",
    )(x, g.reshape(1, d), wtok, wfeat, cosF, sinF, _lane_gain(qn), _lane_gain(kn[1]), _lane_gain(kn[2]),
      _lane_gain(mem_qn))


def _nsa_cmp_body(xk_ref, xv_ref, pk_ref, pv_ref, wk_ref, wv_ref, kn_ref, cos_ref, sin_ref, k_ref, vT_ref):
    def compress(x, pos, w_ref):
        n = x.shape[0]
        xa = (x + pos[0:1]).astype(BF16)
        xb = (pltpu.roll(x, n - 1, axis=0) + pos[1:2]).astype(BF16)
        return _dot_nt(w_ref[0], xa) + _dot_nt(w_ref[1], xb)

    kT = compress(xk_ref[0], pk_ref[...], wk_ref)
    kT = _heads_norm_rope(kT, _tile_lanes(kn_ref[...], kT.shape[-1]), cos_ref[...], sin_ref[...])
    k_ref[0] = kT.T.astype(BF16)
    vT_ref[0] = compress(xv_ref[0], pv_ref[...], wv_ref).astype(BF16)


def _nsa_cmp_weights(w, pos):
    eye = jnp.eye(NSA_G, dtype=F32)
    halves = w.reshape(CMP_L // CMP_S, CMP_S, HD, HD)
    wt = jnp.einsum('hg,alde->ahelgd', eye, halves).reshape(CMP_L // CMP_S, NSA_KV, CMP_S * NSA_KV)
    p = jnp.broadcast_to(pos.reshape(CMP_L // CMP_S, CMP_S, 1, HD), (CMP_L // CMP_S, CMP_S, NSA_G, HD))
    return wt.astype(BF16), p.reshape(CMP_L // CMP_S, CMP_S * NSA_KV).astype(F32)


def _nsa_cmp(kc, vc, pos_k, pos_v, wk, wv, kn0, batch, seq):
    n = seq // CMP_S
    width = CMP_S * NSA_KV
    xk = kc.reshape(batch, n, width)
    xv = vc.reshape(batch, n, width)
    wkt, pk = _nsa_cmp_weights(wk, pos_k)
    wvt, pv = _nsa_cmp_weights(wv, pos_v)
    cosE, sinE = _rope_tables(jnp.arange(n) * CMP_S + (CMP_L - 1))
    xspec = pl.BlockSpec((1, n, width), lambda b: (b, 0, 0))
    return pl.pallas_call(
        _nsa_cmp_body,
        out_shape=(jax.ShapeDtypeStruct((batch, n, NSA_KV), BF16), jax.ShapeDtypeStruct((batch, NSA_KV, n), BF16)),
        grid=(batch,),
        in_specs=[xspec, xspec, _const_spec(pk.shape), _const_spec(pv.shape), _const_spec(wkt.shape),
                  _const_spec(wvt.shape), _const_spec((HD, LANES)), _const_spec((HALF, n)), _const_spec((HALF, n))],
        out_specs=(pl.BlockSpec((1, n, NSA_KV), lambda b: (b, 0, 0)), pl.BlockSpec((1, NSA_KV, n), lambda b: (b, 0, 0))),
        compiler_params=_cparams("parallel"),
        name="nsa_cmp",
    )(xk, xv, pk, pv, wkt, wvt, _lane_gain(kn0), cosE, sinE)


NSA_TQ = 256
NSA_TK = 256


def _staged_attention(nheads, lo, hi, tk, logits, values, s_ref, mc_ref, m_ref, acc_ref):
    tq = s_ref.shape[-1]
    fold = lambda a: a.reshape(tk // SUBLANES, SUBLANES, tq)
    m_ref[...] = jnp.full(m_ref.shape, -jnp.inf, F32)
    acc_ref[...] = jnp.zeros(acc_ref.shape, F32)

    start = lambda c: c * tk if isinstance(c, int) else pl.multiple_of(c * tk, tk)

    def stage(h, c, slot):
        s = logits(h, start(c)) * LOG2E
        s_ref[slot, h] = s
        mc_ref[slot, h * SUBLANES:(h + 1) * SUBLANES, :] = jnp.max(fold(s), axis=0)

    def consume(h, c, slot):
        g8, hrows = slice(h * SUBLANES, (h + 1) * SUBLANES), slice(h * VAUG, (h + 1) * VAUG)
        m_old = m_ref[h:h + 1, :]
        m_new = jnp.maximum(m_old, jnp.max(mc_ref[slot, g8, :], axis=0, keepdims=True))
        alpha = jnp.exp2(m_old - m_new)
        p = jnp.exp2(s_ref[slot, h] - m_new)
        acc_ref[hrows, :] = acc_ref[hrows, :] * alpha + _dot(values(h, start(c)), p.astype(BF16))
        m_ref[h:h + 1, :] = m_new

    def consume_and_stage_next(c, slot):
        for h in range(nheads):
            consume(h, c, slot)
            stage(h, c + 1, 1 - slot)

    def finish(c, slot):
        for h in range(nheads):
            consume(h, c, slot)

    for h in range(nheads):
        stage(h, lo, 0)
    n_fused = hi - 1 - lo

    def two_chunks(j, carry):
        c = lo + 2 * j
        consume_and_stage_next(c, 0)
        consume_and_stage_next(c + 1, 1)
        return carry

    if isinstance(n_fused, int):
        for i in range(n_fused):
            consume_and_stage_next(lo + i, i % 2)
        finish(hi - 1, n_fused % 2)
    else:
        lax.fori_loop(0, n_fused // 2, two_chunks, 0)

        @pl.when(n_fused % 2 == 1)
        def _():
            consume_and_stage_next(hi - 2, 0)
            finish(hi - 1, 1)

        @pl.when(n_fused % 2 == 0)
        def _():
            finish(hi - 1, 0)

    return [acc_ref[h * VAUG:h * VAUG + HD, :] / acc_ref[h * VAUG + HD:h * VAUG + HD + 1, :] for h in range(nheads)]


def _nsa_attn_body(qT_ref, gT_ref, ks_ref, vsT_ref, kw_ref, vwT_ref, kc_ref, vcT_ref, cov_ref, y_ref,
                   bias_ref, out_ref, s_ref, mc_ref, m_ref, acc_ref, *, seq):
    tq, tk = NSA_TQ, NSA_TK
    qs = pl.program_id(1) * tq
    nkc = (qs + tq) // tk
    wlo = jnp.maximum((qs - WIN) // tk, 0)
    qpos = qs + lax.broadcasted_iota(I32, (1, tq), 1)
    ncr = seq // CMP_S
    nblk = seq // SLC_L
    n_sel = min(SLC_N_MAX, nblk)
    valid_c = CMP_S * lax.broadcasted_iota(I32, (ncr, tq), 0) + (CMP_L - 1) <= qpos
    any_c = jnp.where(qpos >= CMP_L - 1, 1.0, 0.0)
    jrow = lax.broadcasted_iota(I32, (nblk, tq), 0)
    qblk = qpos // SLC_L
    forced = (jrow == 0) | (jrow == qblk) | (jrow == qblk - 1)
    causal_blk = jrow * SLC_L <= qpos
    krow = lax.broadcasted_iota(I32, (tk, tq), 0)
    brow = lax.broadcasted_iota(I32, (SLC_L, tq), 0)
    group = lambda h: h // NSA_R
    pair = lambda h: slice((group(h) // 2) * LANES, (group(h) // 2 + 1) * LANES)
    grows = lambda h: slice(group(h) * HD, (group(h) + 1) * HD)
    vrows = lambda h: slice(group(h) * VAUG, (group(h) + 1) * VAUG)
    q_of = lambda h: _pad_rows(qT_ref[h * HD:(h + 1) * HD, :], group(h) % 2)
    gate = lambda h, branch: gT_ref[h * 3 + branch:h * 3 + branch + 1, :]

    for g in range(NSA_G):
        heads = [g * NSA_R + r for r in range(NSA_R)]
        kcm = kc_ref[0][:, pair(heads[0])]
        vcm = vcT_ref[0][grows(heads[0]), :]
        psum = jnp.zeros((ncr, tq), F32)
        for h in heads:
            s = jnp.where(valid_c, _dot(kcm, q_of(h)), NEG)
            p = jnp.exp(s - jnp.max(s, axis=0, keepdims=True))
            p = p / jnp.sum(p, axis=0, keepdims=True) * any_c
            out_ref[h * HD:(h + 1) * HD, :] = gate(h, 0) * _dot(vcm, p.astype(BF16))
            psum = psum + p
        p_hi = psum.astype(BF16)
        p_lo = (psum - p_hi.astype(F32)).astype(BF16)
        imp = _dot(cov_ref[...], p_hi) + _dot(cov_ref[...], p_lo)
        imp = jnp.where(causal_blk, imp + jnp.where(forced, FORCE, 0.0), NEG)
        rank = jnp.zeros((nblk, tq), I32)
        for j2 in range(nblk):
            rj = imp[j2:j2 + 1]
            beats = (rj > imp) | ((rj == imp) & (j2 < jrow))
            rank = rank + beats.astype(I32)
        selb = jnp.where(rank < n_sel, 0.0, NEG)
        for j in range(nblk):
            bias_ref[g, j * SLC_L:(j + 1) * SLC_L, :] = jnp.where(j * SLC_L + brow <= qpos, selb[j:j + 1], NEG)

    stats = (s_ref, mc_ref, m_ref, acc_ref)
    slc = _staged_attention(
        NSA_H, 0, nkc, tk,
        lambda h, r0: _dot(ks_ref[pl.ds(r0, tk), pair(h)], q_of(h)) + bias_ref[group(h), pl.ds(r0, tk), :],
        lambda h, r0: vsT_ref[vrows(h), pl.ds(r0, tk)], *stats)
    for h in range(NSA_H):
        out_ref[h * HD:(h + 1) * HD, :] = out_ref[h * HD:(h + 1) * HD, :] + gate(h, 1) * slc[h]

    def win_logits(h, r0):
        dist = qpos - (r0 + krow)
        return jnp.where((dist >= 0) & (dist < WIN), _dot(kw_ref[pl.ds(r0, tk), pair(h)], q_of(h)), NEG)

    win = _staged_attention(NSA_H, wlo, nkc, tk, win_logits, lambda h, r0: vwT_ref[vrows(h), pl.ds(r0, tk)], *stats)
    outs = [out_ref[h * HD:(h + 1) * HD, :] + gate(h, 2) * win[h] for h in range(NSA_H)]
    y_ref[...] = jnp.concatenate(outs, axis=0).T.astype(BF16)


def _nsa_attn(qT, gT, ks, vsT, kw, vwT, kcmp, vcmpT, batch, seq):
    t = ks.shape[0]
    tq = NSA_TQ
    nq = seq // tq
    ncr, nblk = seq // CMP_S, seq // SLC_L
    starts = np.arange(ncr) * CMP_S
    sstart = np.arange(nblk) * SLC_L
    cover = (starts[None, :] < sstart[:, None] + SLC_L) & (starts[None, :] + CMP_L > sstart[:, None])
    cover[:, ncr - 1] = False
    colq = lambda n: pl.BlockSpec((n, tq), lambda b, i: (0, b * nq + i))
    tok = pl.BlockSpec((seq, NSA_KV), lambda b, i: (b, 0))
    feat = pl.BlockSpec((NSA_G * VAUG, seq), lambda b, i: (0, b))
    return pl.pallas_call(
        functools.partial(_nsa_attn_body, seq=seq),
        out_shape=jax.ShapeDtypeStruct((t, NSA_H * HD), BF16),
        grid=(batch, nq),
        in_specs=[
            colq(NSA_H * HD), colq(gT.shape[0]), tok, feat, tok, feat,
            pl.BlockSpec((1, ncr, NSA_KV), lambda b, i: (b, 0, 0)),
            pl.BlockSpec((1, NSA_KV, ncr), lambda b, i: (b, 0, 0)),
            _const_spec((nblk, ncr)),
        ],
        out_specs=pl.BlockSpec((tq, NSA_H * HD), lambda b, i: (b * nq + i, 0)),
        scratch_shapes=[pltpu.VMEM((NSA_G, seq, tq), F32), pltpu.VMEM((NSA_H * HD, tq), F32),
                        pltpu.VMEM((2, NSA_H, NSA_TK, tq), F32), pltpu.VMEM((2, NSA_H * SUBLANES, tq), F32),
                        pltpu.VMEM((2 * SUBLANES, tq), F32), pltpu.VMEM((NSA_H * VAUG, tq), F32)],
        compiler_params=_cparams("parallel", "arbitrary"),
        name="nsa_attn",
    )(qT, gT, ks, vsT, kw, vwT, kcmp, vcmpT, jnp.asarray(cover, BF16))


def _nsa_layer_mix(x, g, w_in, qn, kn, pos_k, pos_v, wk, wv, mem_qn, batch, seq):
    kc, vc, ks, kw, qT, vsT, vwT, gT, qmT = _nsa_proj(x, g, w_in, qn, kn, mem_qn, seq)
    kcmp, vcmpT = _nsa_cmp(kc, vc, pos_k, pos_v, wk, wv, kn[0], batch, seq)
    return _nsa_attn(qT, gT, ks, vsT, kw, vwT, kcmp, vcmpT, batch, seq), qmT


def kernel(x, mem, ffn_norm, ffn_w_gate, ffn_w_up, ffn_w_down, mix_norm, w_out, mem_norm, mem_w_kv, mem_qn, mem_kn, ret_w_in, dsa_w_in, dsa_qn, dsa_kn, nsa_w_in, nsa_qn, nsa_kn, nsa_cmp_pos_k, nsa_cmp_pos_v, nsa_cmp_wk, nsa_cmp_wv):
    batch, seq, d = x.shape
    x = x.reshape(batch * seq, d)
    ffn_w = (ffn_w_gate.astype(BF16), ffn_w_up.astype(BF16), ffn_w_down.astype(BF16))
    for i in range(ffn_norm.shape[0]):
        x = _ffn(x, ffn_norm[i, 0], *ffn_w, i, 0)
        kind, j = i % N_MIXERS, i // N_MIXERS
        if kind == 0:
            q, v, gate, kT, qmT = _ret_proj(x, mix_norm[i], ret_w_in[j], mem_qn[i], seq)
            y_mix = _retention(q, kT, v, gate, batch, seq)
        elif kind == 1:
            kk, qT, iqT, vT, iwT, qmT = _dsa_proj(x, mix_norm[i], dsa_w_in[j], dsa_qn[j], dsa_kn[j], mem_qn[i], seq)
            y_mix = _dsa_attn(kk, qT, iqT, vT, iwT, batch, seq)
        else:
            y_mix, qmT = _nsa_layer_mix(x, mix_norm[i], nsa_w_in[j], nsa_qn[j], nsa_kn[j], nsa_cmp_pos_k[j],
                                        nsa_cmp_pos_v[j], nsa_cmp_wk[j], nsa_cmp_wv[j], mem_qn[i], batch, seq)
        mem_k, mem_vT = _mem_kv(mem, mem_norm[i], mem_w_kv[i], mem_kn[i])
        y_mem = _mem_attn(qmT, mem_k, mem_vT, seq)
        x = _mix_out_ffn(y_mix, y_mem, x, w_out[i], ffn_norm[i, 1], *ffn_w, i, 1)
    return x.reshape(batch, seq, d)
```

```python
import functools
import math

import jax
import jax.numpy as jnp
import numpy as np
from jax import lax
from jax.experimental import pallas as pl
from jax.experimental.pallas import tpu as pltpu

D_MODEL = 1024
HD = 64
HALF = HD // 2
H_MIX = 12
H_MEM = 4
MIX_W = H_MIX * HD
MEM_W = H_MEM * HD
D_FF = 2816
ROPE_THETA = 10000.0
EPS = 1e-6
NEG = -1e30
FORCE = 1e9
SCALE = HD ** -0.5
QSCALE = SCALE * math.log2(math.e)

RET_H, RET_DK, RET_DV, RET_CHUNK = 6, 64, 128, 128
DSA_H, IDX_H, IDX_D, DSA_TOPK_MAX = 12, 8, 64, 256
NSA_H, NSA_G, CMP_L, CMP_S, SLC_L, SLC_N_MAX, WIN = 12, 4, 32, 16, 64, 16, 512
NSA_R = NSA_H // NSA_G
N_MIXERS = 3

RET_SIZES = [RET_H * RET_DK, RET_H * RET_DK, RET_H * RET_DV, RET_H * RET_DV, MEM_W]
DSA_SIZES = [DSA_H * HD, HD, HD, IDX_H * IDX_D, IDX_D, IDX_H, MEM_W]
NSA_SIZES = [NSA_H * HD] + [NSA_G * HD] * 6 + [NSA_H * 3, MEM_W]

LANES = 128
SUBLANES = 8
PACKED_SUBLANES = 2 * SUBLANES
VAUG = HD + PACKED_SUBLANES
MXU_N = 256
VMEM_LIMIT_BYTES = 56 * 1024 * 1024

BF16 = jnp.bfloat16
F32 = jnp.float32
I32 = jnp.int32
I16 = jnp.int16


def _cparams(*sem):
    return pltpu.CompilerParams(dimension_semantics=sem, vmem_limit_bytes=VMEM_LIMIT_BYTES)


def _const_spec(shape):
    n = len(shape)
    return pl.BlockSpec(shape, lambda *_: (0,) * n, pipeline_mode=pl.Buffered(1))


def _rms(x, g):
    return x * lax.rsqrt(jnp.mean(x * x, axis=-1, keepdims=True) + EPS) * g


def _dot(a, b):
    return jnp.dot(a, b, preferred_element_type=F32)


def _dot_nt(a, b):
    return lax.dot_general(a, b, (((1,), (1,)), ((), ())), preferred_element_type=F32)


def _tile_lanes(a, n):
    reps = n // a.shape[-1]
    return a if reps == 1 else jnp.concatenate([a] * reps, axis=-1)


def _rope_tables(pos):
    inv = ROPE_THETA ** (-jnp.arange(HALF, dtype=F32) / HALF)
    ang = pos.astype(F32)[:, None] * inv[None, :]
    return jnp.cos(ang).T, jnp.sin(ang).T


def _lane_gain(g):
    return jnp.broadcast_to(g.astype(F32)[:, None], (g.shape[0], LANES))


def _augment_values(v):
    t = v.shape[-1]
    v3 = v.reshape(v.shape[0] // HD, HD, t)
    tail = jnp.where(lax.broadcasted_iota(I32, (v3.shape[0], VAUG - HD, t), 1) == 0, 1.0, 0.0).astype(v.dtype)
    return jnp.concatenate([v3, tail], axis=1).reshape(v3.shape[0] * VAUG, t)


def _heads_norm_rope(x, gain, cos, sin):
    t = x.shape[-1]
    x3 = x.reshape(x.shape[0] // HD, HD, t)
    if gain is not None:
        ms = jnp.sum(x3 * x3, axis=1, keepdims=True) * (1.0 / HD)
        x3 = x3 * lax.rsqrt(ms + EPS) * gain[None]
    if cos is not None:
        x1, x2 = x3[:, :HALF], x3[:, HALF:]
        x3 = jnp.concatenate([x1 * cos[None] - x2 * sin[None], x2 * cos[None] + x1 * sin[None]], axis=1)
    return x3.reshape(x.shape)


FFN_TM = 1024
FFN_TF = MXU_N


def _swiglu_half_step(x, g_ref, wg_ref, wu_ref, wd_ref):
    h = _rms(x, g_ref[...]).astype(BF16)
    acc = jnp.zeros(x.shape, F32)
    for c in range(D_FF // FFN_TF):
        sl = slice(c * FFN_TF, (c + 1) * FFN_TF)
        gate = _dot(h, wg_ref[:, sl])
        up = _dot(h, wu_ref[:, sl])
        act = (gate * jax.nn.sigmoid(gate) * up).astype(BF16)
        acc = acc + _dot(act, wd_ref[sl, :])
    return x + 0.5 * acc


def _ffn_body(x_ref, g_ref, wg_ref, wu_ref, wd_ref, o_ref):
    o_ref[...] = _swiglu_half_step(x_ref[...], g_ref, wg_ref, wu_ref, wd_ref)


def _mix_out_ffn_body(ymix_ref, ymem_ref, x_ref, wmix_ref, wmem_ref, g_ref, wg_ref, wu_ref, wd_ref, o_ref):
    x = x_ref[...] + _dot(ymix_ref[...], wmix_ref[...]) + _dot(ymem_ref[...], wmem_ref[...])
    o_ref[...] = _swiglu_half_step(x, g_ref, wg_ref, wu_ref, wd_ref)


def _ffn_specs(d, layer, half):
    pick = lambda r, c: pl.BlockSpec((None, None, r, c), lambda i: (layer, half, 0, 0), pipeline_mode=pl.Buffered(1))
    return [_const_spec((1, d)), pick(d, D_FF), pick(d, D_FF), pick(D_FF, d)]


def _ffn(x, g, wg, wu, wd, layer, half):
    t, d = x.shape
    tm = min(FFN_TM, t)
    return pl.pallas_call(
        _ffn_body,
        out_shape=jax.ShapeDtypeStruct((t, d), F32),
        grid=(t // tm,),
        in_specs=[pl.BlockSpec((tm, d), lambda i: (i, 0))] + _ffn_specs(d, layer, half),
        out_specs=pl.BlockSpec((tm, d), lambda i: (i, 0)),
        compiler_params=_cparams("parallel"),
        name="ffn",
    )(x, g.reshape(1, d), wg, wu, wd)


def _mix_out_ffn(y_mix, y_mem, x, w_out, g, wg, wu, wd, layer, half):
    t, d = x.shape
    tm = min(FFN_TM, t)
    return pl.pallas_call(
        _mix_out_ffn_body,
        out_shape=jax.ShapeDtypeStruct((t, d), F32),
        grid=(t // tm,),
        in_specs=[
            pl.BlockSpec((tm, MIX_W), lambda i: (i, 0)),
            pl.BlockSpec((tm, MEM_W), lambda i: (i, 0)),
            pl.BlockSpec((tm, d), lambda i: (i, 0)),
            _const_spec((MIX_W, d)),
            _const_spec((MEM_W, d)),
        ] + _ffn_specs(d, layer, half),
        out_specs=pl.BlockSpec((tm, d), lambda i: (i, 0)),
        compiler_params=_cparams("parallel"),
        name="mix_out_ffn",
    )(y_mix, y_mem, x, w_out[:MIX_W].astype(BF16), w_out[MIX_W:].astype(BF16), g.reshape(1, d), wg, wu, wd)


PROJ_TM = 1024


def _mem_kv_body(m_ref, g_ref, wk_ref, wvT_ref, kn_ref, k_ref, vT_ref):
    h = _rms(m_ref[0], g_ref[...]).astype(BF16)
    kT = _dot_nt(wk_ref[...], h)
    kT = _heads_norm_rope(kT, _tile_lanes(kn_ref[...], kT.shape[-1]), None, None)
    k_ref[0] = kT.T.astype(BF16)
    vT_ref[0] = _augment_values(_dot_nt(wvT_ref[...], h)).astype(BF16)


def _mem_kv(mem, g, w_kv, kn):
    b, m, d = mem.shape
    return pl.pallas_call(
        _mem_kv_body,
        out_shape=(jax.ShapeDtypeStruct((b, m, MEM_W), BF16), jax.ShapeDtypeStruct((b, H_MEM * VAUG, m), BF16)),
        grid=(b,),
        in_specs=[
            pl.BlockSpec((1, m, d), lambda i: (i, 0, 0)),
            _const_spec((1, d)),
            _const_spec((MEM_W, d)),
            _const_spec((MEM_W, d)),
            _const_spec((HD, LANES)),
        ],
        out_specs=(pl.BlockSpec((1, m, MEM_W), lambda i: (i, 0, 0)),
                   pl.BlockSpec((1, H_MEM * VAUG, m), lambda i: (i, 0, 0))),
        compiler_params=_cparams("parallel"),
        name="mem_kv",
    )(mem, g.reshape(1, d), w_kv[:, :MEM_W].T.astype(BF16), w_kv[:, MEM_W:].T.astype(BF16), _lane_gain(kn))


MEM_TQ = 512


def _pad_rows(blk, half, total=LANES):
    z = jnp.zeros_like(blk)
    parts = [z] * (total // HD)
    parts[half] = blk
    return jnp.concatenate(parts, axis=0)


def _mem_attn_body(qT_ref, k_ref, vT_ref, o_ref, s_ref, mc_ref, m_ref, acc_ref):
    n_mem = k_ref.shape[1]
    outs = _staged_attention(
        H_MEM, 0, 1, n_mem,
        lambda h, r0: _dot(k_ref[0, :, (h // 2) * LANES:(h // 2 + 1) * LANES],
                           _pad_rows(qT_ref[h * HD:(h + 1) * HD, :], h % 2)),
        lambda h, r0: vT_ref[0, h * VAUG:(h + 1) * VAUG, :], s_ref, mc_ref, m_ref, acc_ref)
    o_ref[...] = jnp.concatenate(outs, axis=0).T.astype(BF16)


def _mem_attn(qmT, k, vT, seq):
    t = qmT.shape[1]
    b, m, _ = k.shape
    tq = min(MEM_TQ, seq)
    nq = seq // tq
    return pl.pallas_call(
        _mem_attn_body,
        out_shape=jax.ShapeDtypeStruct((t, MEM_W), BF16),
        grid=(b, nq),
        in_specs=[
            pl.BlockSpec((MEM_W, tq), lambda bi, i: (0, bi * nq + i)),
            pl.BlockSpec((1, m, MEM_W), lambda bi, i: (bi, 0, 0)),
            pl.BlockSpec((1, H_MEM * VAUG, m), lambda bi, i: (bi, 0, 0)),
        ],
        out_specs=pl.BlockSpec((tq, MEM_W), lambda bi, i: (bi * nq + i, 0)),
        scratch_shapes=[pltpu.VMEM((2, H_MEM, m, tq), F32), pltpu.VMEM((2, H_MEM * SUBLANES, tq), F32),
                        pltpu.VMEM((SUBLANES, tq), F32), pltpu.VMEM((H_MEM * VAUG, tq), F32)],
        compiler_params=_cparams("parallel", "parallel"),
        name="mem_attn",
    )(qmT, k, vT)


def _ret_proj_body(x_ref, g_ref, wtok_ref, wfeat_ref, cosT_ref, sinT_ref, cosF_ref, sinF_ref, mqn_ref,
                   q_ref, v_ref, gate_ref, kT_ref, qmT_ref):
    h = _rms(x_ref[...], g_ref[...]).astype(BF16)
    tok = _dot(h, wtok_ref[...])
    nq, nv = RET_H * RET_DK, RET_H * RET_DV
    q = tok[:, :nq]
    lane = lax.broadcasted_iota(I32, q.shape, 1)
    rot = jnp.where(lane % HD < HALF, pltpu.roll(q, nq - HALF, axis=1), pltpu.roll(q, HALF, axis=1))
    q_ref[...] = (q * _tile_lanes(cosT_ref[...], nq) + rot * _tile_lanes(sinT_ref[...], nq)).astype(BF16)
    v_ref[...] = tok[:, nq:nq + nv].astype(BF16)
    gate_ref[...] = tok[:, nq + nv:]
    feat = _dot_nt(wfeat_ref[...], h)
    tm = feat.shape[-1]
    cos, sin = cosF_ref[...], sinF_ref[...]
    kT = _heads_norm_rope(feat[:nq], None, cos, sin) * (RET_DK ** -0.5)
    kT_ref[...] = kT.astype(BF16)
    qm = _heads_norm_rope(feat[nq:], _tile_lanes(mqn_ref[...], tm), None, None) * QSCALE
    qmT_ref[...] = qm.astype(BF16)


def _token_rope_tables(seq):
    cosF, sinF = _rope_tables(jnp.arange(seq))
    cos = jnp.tile(cosF.T, (1, LANES // HALF))
    sign = jnp.where((jnp.arange(LANES) % HD) < HALF, -1.0, 1.0).astype(F32)
    sin = jnp.tile(sinF.T, (1, LANES // HALF)) * sign[None, :]
    return cos, sin


def _ret_proj(x, g, w_in, mem_qn, seq):
    t, d = x.shape
    tm = min(PROJ_TM, seq)
    npos = seq // tm
    wq, wk, wv, wg, wqm = jnp.split(w_in, np.cumsum(RET_SIZES)[:-1].tolist(), axis=1)
    wtok = jnp.concatenate([wq, wv, wg], axis=1).astype(BF16)
    wfeat = jnp.concatenate([wk, wqm], axis=1).T.astype(BF16)
    cosT, sinT = _token_rope_tables(seq)
    cosF, sinF = _rope_tables(jnp.arange(seq))
    nq, nv = RET_H * RET_DK, RET_H * RET_DV
    row = lambda n: pl.BlockSpec((tm, n), lambda i: (i, 0))
    col = lambda n: pl.BlockSpec((n, tm), lambda i: (0, i))
    return pl.pallas_call(
        _ret_proj_body,
        out_shape=(
            jax.ShapeDtypeStruct((t, nq), BF16), jax.ShapeDtypeStruct((t, nv), BF16),
            jax.ShapeDtypeStruct((t, nv), F32), jax.ShapeDtypeStruct((nq, t), BF16),
            jax.ShapeDtypeStruct((MEM_W, t), BF16),
        ),
        grid=(t // tm,),
        in_specs=[
            row(d), _const_spec((1, d)), _const_spec(wtok.shape), _const_spec(wfeat.shape),
            pl.BlockSpec((tm, LANES), lambda i: (i % npos, 0)), pl.BlockSpec((tm, LANES), lambda i: (i % npos, 0)),
            pl.BlockSpec((HALF, tm), lambda i: (0, i % npos)), pl.BlockSpec((HALF, tm), lambda i: (0, i % npos)),
            _const_spec((HD, LANES)),
        ],
        out_specs=(row(nq), row(nv), row(nv), col(nq), col(MEM_W)),
        compiler_params=_cparams("parallel"),
        name="ret_proj",
    )(x, g.reshape(1, d), wtok, wfeat, cosT, sinT, cosF, sinF, _lane_gain(mem_qn))


RET_TL = 512


def _ret_body(q_ref, kT_ref, v_ref, gate_ref, y_ref, state_ref):
    @pl.when(pl.program_id(1) == 0)
    def _():
        state_ref[...] = jnp.zeros_like(state_ref)

    c = RET_CHUNK
    ii = lax.broadcasted_iota(I32, (c, c), 0).astype(F32)
    jj = lax.broadcasted_iota(I32, (c, c), 1).astype(F32)
    diff = ii - jj
    jk = lax.broadcasted_iota(I32, (RET_DK, c), 1).astype(F32)
    log_gs = [math.log(1.0 - 2.0 ** (-5.0 - hh)) for hh in range(RET_H)]
    dmasks = [jnp.where(diff >= 0, jnp.exp(lg * jnp.maximum(diff, 0.0)), 0.0) for lg in log_gs]
    xis = [jnp.exp(lg * (ii + 1.0)) for lg in log_gs]
    zetas = [jnp.exp(lg * (c - 1.0 - jk)) for lg in log_gs]
    for n in range(q_ref.shape[0] // c):
        rows = slice(n * c, (n + 1) * c)
        for hh in range(RET_H):
            dmask, xi, zeta, decay = dmasks[hh], xis[hh], zetas[hh], math.exp(log_gs[hh] * c)
            pair = slice((hh // 2) * LANES, (hh // 2 + 1) * LANES)
            q2 = q_ref[rows, pair]
            kT = kT_ref[hh * RET_DK:(hh + 1) * RET_DK, rows]
            v = v_ref[rows, hh * RET_DV:(hh + 1) * RET_DV]
            state = state_ref[hh]
            inner = _dot(q2, _pad_rows(kT, hh % 2)) * dmask
            o = _dot(inner.astype(BF16), v)
            o = o + _dot(q2, _pad_rows(state.astype(BF16), hh % 2)) * xi
            kv = _dot((kT.astype(F32) * zeta).astype(BF16), v)
            state_ref[hh] = state * decay + kv
            mu = jnp.mean(o, axis=-1, keepdims=True)
            var = jnp.mean(jnp.square(o - mu), axis=-1, keepdims=True)
            o = (o - mu) * lax.rsqrt(var + EPS)
            gte = gate_ref[rows, hh * RET_DV:(hh + 1) * RET_DV]
            y_ref[rows, hh * RET_DV:(hh + 1) * RET_DV] = (gte * jax.nn.sigmoid(gte) * o).astype(BF16)


def _retention(q, kT, v, gate, batch, seq):
    t = q.shape[0]
    tl = min(RET_TL, seq)
    nl = seq // tl
    nq, nv = RET_H * RET_DK, RET_H * RET_DV
    return pl.pallas_call(
        _ret_body,
        out_shape=jax.ShapeDtypeStruct((t, nv), BF16),
        grid=(batch, nl),
        in_specs=[
            pl.BlockSpec((tl, nq), lambda b, i: (b * nl + i, 0)),
            pl.BlockSpec((nq, tl), lambda b, i: (0, b * nl + i)),
            pl.BlockSpec((tl, nv), lambda b, i: (b * nl + i, 0)),
            pl.BlockSpec((tl, nv), lambda b, i: (b * nl + i, 0)),
        ],
        out_specs=pl.BlockSpec((tl, nv), lambda b, i: (b * nl + i, 0)),
        scratch_shapes=[pltpu.VMEM((RET_H, RET_DK, RET_DV), F32)],
        compiler_params=_cparams("parallel", "arbitrary"),
        name="retention",
    )(q, kT, v, gate)


def _dsa_proj_body(x_ref, g_ref, w_ref, cos_ref, sin_ref, qn_ref, kn_ref, mqn_ref,
                   kk_ref, qT_ref, iqT_ref, vT_ref, iwT_ref, qmT_ref):
    h = _rms(x_ref[...], g_ref[...]).astype(BF16)
    feat = _dot_nt(w_ref[...], h)
    tm = feat.shape[-1]
    cos, sin = cos_ref[...], sin_ref[...]
    o = 0
    q = feat[o:o + DSA_H * HD]; o += DSA_H * HD
    iq = feat[o:o + IDX_H * IDX_D]; o += IDX_H * IDX_D
    k = feat[o:o + HD]; o += HD
    ik = feat[o:o + IDX_D]; o += IDX_D
    v = feat[o:o + HD]; o += HD
    qm = feat[o:o + MEM_W]; o += MEM_W
    iw = feat[o:o + IDX_H]
    qT_ref[...] = (_heads_norm_rope(q, _tile_lanes(qn_ref[...], tm), cos, sin) * QSCALE).astype(BF16)
    iqT_ref[...] = _heads_norm_rope(iq, None, cos, sin).astype(BF16)
    k = _heads_norm_rope(k, _tile_lanes(kn_ref[...], tm), cos, sin)
    ik = _heads_norm_rope(ik, None, cos, sin)
    kk_ref[...] = jnp.concatenate([k, ik], axis=0).T.astype(BF16)
    vT_ref[...] = _augment_values(v).astype(BF16)
    iwT_ref[...] = iw
    qmT_ref[...] = (_heads_norm_rope(qm, _tile_lanes(mqn_ref[...], tm), None, None) * QSCALE).astype(BF16)


def _dsa_proj(x, g, w_in, qn, kn, mem_qn, seq):
    t, d = x.shape
    tm = min(PROJ_TM, seq)
    npos = seq // tm
    wq, wk, wv, wiq, wik, wiw, wqm = jnp.split(w_in, np.cumsum(DSA_SIZES)[:-1].tolist(), axis=1)
    wfeat = jnp.concatenate([wq, wiq, wk, wik, wv, wqm, wiw], axis=1).T.astype(BF16)
    cosF, sinF = _rope_tables(jnp.arange(seq))
    col = lambda n: pl.BlockSpec((n, tm), lambda i: (0, i))
    tab = pl.BlockSpec((HALF, tm), lambda i: (0, i % npos))
    gain = _const_spec((HD, LANES))
    return pl.pallas_call(
        _dsa_proj_body,
        out_shape=(
            jax.ShapeDtypeStruct((t, LANES), BF16), jax.ShapeDtypeStruct((DSA_H * HD, t), BF16),
            jax.ShapeDtypeStruct((IDX_H * IDX_D, t), BF16), jax.ShapeDtypeStruct((VAUG, t), BF16),
            jax.ShapeDtypeStruct((IDX_H, t), F32), jax.ShapeDtypeStruct((MEM_W, t), BF16),
        ),
        grid=(t // tm,),
        in_specs=[pl.BlockSpec((tm, d), lambda i: (i, 0)), _const_spec((1, d)), _const_spec(wfeat.shape),
                  tab, tab, gain, gain, gain],
        out_specs=(pl.BlockSpec((tm, LANES), lambda i: (i, 0)), col(DSA_H * HD), col(IDX_H * IDX_D), col(VAUG),
                   col(IDX_H), col(MEM_W)),
        compiler_params=_cparams("parallel"),
        name="dsa_proj",
    )(x, g.reshape(1, d), wfeat, cosF, sinF, _lane_gain(qn), _lane_gain(kn), _lane_gain(mem_qn))


DSA_TQ = 256
DSA_TK = 256
HALF_BITS = 16
INT16_MIN = -2 ** (HALF_BITS - 1)


def _dsa_attn_body(kk_ref, qT_ref, iqT_ref, vT_ref, iwT_ref, y_ref, keys_ref, hi_ref, lo_ref, bias_ref, s_ref, mc_ref,
                   m_ref, acc_ref, *, topk, seq):
    tq, tk = DSA_TQ, DSA_TK
    qs = pl.program_id(1) * tq
    nkc = (qs + tq) // tk
    qpos = qs + lax.broadcasted_iota(I32, (1, tq), 1)
    row = lax.broadcasted_iota(I32, (tk, tq), 0)

    def score_chunk(c, carry):
        r0 = pl.multiple_of(c * tk, tk)
        kkc = kk_ref[pl.ds(r0, tk), :]
        acc = jnp.zeros((tk, tq), F32)
        for h in range(IDX_H):
            r = _dot(kkc, _pad_rows(iqT_ref[h * IDX_D:(h + 1) * IDX_D, :], 1))
            acc = acc + jnp.maximum(r, 0.0) * iwT_ref[h:h + 1, :]
        sc = jnp.where(r0 + row <= qpos, acc + 0.0, -jnp.inf)
        bits = pltpu.bitcast(sc, I32)
        key = jnp.where(bits < 0, bits ^ jnp.int32(0x7FFFFFFF), bits)
        keys_ref[pl.ds(r0, tk), :] = key
        hi_ref[pl.ds(r0, tk), :] = lax.shift_right_arithmetic(key, HALF_BITS).astype(I16)
        return carry

    lax.fori_loop(0, nkc, score_chunk, 0)

    def count(pred):
        def body(c, acc):
            r0 = pl.multiple_of(c * tk, tk)
            m = pred(keys_ref[pl.ds(r0, tk), :], r0 + row)
            return acc + jnp.sum(m.astype(I32).reshape(tk // SUBLANES, SUBLANES, tq), axis=0)
        acc = lax.fori_loop(0, nkc, body, jnp.zeros((SUBLANES, tq), I32))
        return jnp.sum(acc, axis=0, keepdims=True)

    def count16(ref, pred):
        def body(c, acc):
            r0 = pl.multiple_of(c * tk, tk)
            ones = jnp.where(pred(ref[pl.ds(r0, tk), :]), jnp.int16(1), jnp.int16(0))
            parts = [ones[i:i + PACKED_SUBLANES] for i in range(0, tk, PACKED_SUBLANES)]
            while len(parts) > 1:
                parts = [a + b for a, b in zip(parts[::2], parts[1::2])]
            return acc + parts[0]
        acc = lax.fori_loop(0, nkc, body, jnp.zeros((PACKED_SUBLANES, tq), I16))
        return jnp.sum(acc.astype(I32), axis=0, keepdims=True)

    def kth_largest16(ref, k):
        thr = jnp.where(count16(ref, lambda v: v >= 0) >= k, jnp.int32(0), jnp.int32(INT16_MIN))

        def bit_step(b, thr):
            cand = thr | lax.shift_left(jnp.int32(1), (HALF_BITS - 2) - b)
            c16 = cand.astype(I16)
            return jnp.where(count16(ref, lambda v: v >= c16) >= k, cand, thr)

        return lax.fori_loop(0, HALF_BITS - 1, bit_step, thr)

    thr_hi = kth_largest16(hi_ref, topk)
    thr_hi16 = thr_hi.astype(I16)
    need_lo = topk - count16(hi_ref, lambda v: v > thr_hi16)

    def low_half_chunk(c, carry):
        r0 = pl.multiple_of(c * tk, tk)
        lo = ((keys_ref[pl.ds(r0, tk), :] & jnp.int32(0xFFFF)) + jnp.int32(INT16_MIN)).astype(I16)
        lo_ref[pl.ds(r0, tk), :] = jnp.where(hi_ref[pl.ds(r0, tk), :] == thr_hi16, lo, jnp.int16(INT16_MIN))
        return carry

    lax.fori_loop(0, nkc, low_half_chunk, 0)
    thr_lo = kth_largest16(lo_ref, need_lo)
    thr = lax.shift_left(thr_hi, HALF_BITS) | (thr_lo - jnp.int32(INT16_MIN))
    def tie_search():
        need = topk - count(lambda k, _: k > thr)

        def idx_step(b, q):
            cand = q | lax.shift_left(jnp.int32(1), (seq.bit_length() - 2) - b)
            return jnp.where(count(lambda k, idx: (k == thr) & (idx < cand)) < need, cand, q)

        return lax.fori_loop(0, seq.bit_length() - 1, idx_step, jnp.zeros((1, tq), I32))

    has_ties = jnp.max(count(lambda k, _: k >= thr)) > topk
    last = lax.cond(has_ties, tie_search, lambda: jnp.full((1, tq), seq, I32))

    def bias_chunk(c, carry):
        r0 = pl.multiple_of(c * tk, tk)
        k = keys_ref[pl.ds(r0, tk), :]
        idx = r0 + row
        sel = (k > thr) | ((k == thr) & (idx <= last))
        bias_ref[pl.ds(r0, tk), :] = jnp.where(sel & (idx <= qpos), 0.0, NEG)
        return carry

    lax.fori_loop(0, nkc, bias_chunk, 0)

    outs = _staged_attention(
        DSA_H, 0, nkc, tk,
        lambda h, r0: _dot(kk_ref[pl.ds(r0, tk), :], _pad_rows(qT_ref[h * HD:(h + 1) * HD, :], 0))
        + bias_ref[pl.ds(r0, tk), :],
        lambda h, r0: vT_ref[:, pl.ds(r0, tk)], s_ref, mc_ref, m_ref, acc_ref)
    y_ref[...] = jnp.concatenate(outs, axis=0).T.astype(BF16)


def _dsa_attn(kk, qT, iqT, vT, iwT, batch, seq):
    t = kk.shape[0]
    tq = DSA_TQ
    nq = seq // tq
    topk = min(DSA_TOPK_MAX, seq // 4)
    colq = lambda n: pl.BlockSpec((n, tq), lambda b, i: (0, b * nq + i))
    return pl.pallas_call(
        functools.partial(_dsa_attn_body, topk=topk, seq=seq),
        out_shape=jax.ShapeDtypeStruct((t, DSA_H * HD), BF16),
        grid=(batch, nq),
        in_specs=[
            pl.BlockSpec((seq, LANES), lambda b, i: (b, 0)),
            colq(DSA_H * HD), colq(IDX_H * IDX_D),
            pl.BlockSpec((VAUG, seq), lambda b, i: (0, b)),
            colq(IDX_H),
        ],
        out_specs=pl.BlockSpec((tq, DSA_H * HD), lambda b, i: (b * nq + i, 0)),
        scratch_shapes=[pltpu.VMEM((seq, tq), I32), pltpu.VMEM((seq, tq), I16), pltpu.VMEM((seq, tq), I16),
                        pltpu.VMEM((seq, tq), F32),
                        pltpu.VMEM((2, DSA_H, DSA_TK, tq), F32), pltpu.VMEM((2, DSA_H * SUBLANES, tq), F32),
                        pltpu.VMEM((2 * SUBLANES, tq), F32), pltpu.VMEM((DSA_H * VAUG, tq), F32)],
        compiler_params=_cparams("parallel", "arbitrary"),
        name="dsa_attn",
    )(kk, qT, iqT, vT, iwT)


NSA_KV = NSA_G * HD
NSA_GATES = NSA_H * 3
NSA_GATES_PAD = -NSA_GATES % SUBLANES


def _nsa_proj_body(x_ref, g_ref, wtok_ref, wfeat_ref, cos_ref, sin_ref, qn_ref, kns_ref, knw_ref, mqn_ref,
                   kc_ref, vc_ref, ks_ref, kw_ref, qT_ref, vsT_ref, vwT_ref, gT_ref, qmT_ref):
    h = _rms(x_ref[...], g_ref[...]).astype(BF16)
    tok = _dot(h, wtok_ref[...])
    kc_ref[...] = tok[:, :NSA_KV]
    vc_ref[...] = tok[:, NSA_KV:]
    feat = _dot_nt(wfeat_ref[...], h)
    tm = feat.shape[-1]
    cos, sin = cos_ref[...], sin_ref[...]
    o = 0
    q = feat[o:o + NSA_H * HD]; o += NSA_H * HD
    ks = feat[o:o + NSA_KV]; o += NSA_KV
    kw = feat[o:o + NSA_KV]; o += NSA_KV
    vs = feat[o:o + NSA_KV]; o += NSA_KV
    vw = feat[o:o + NSA_KV]; o += NSA_KV
    qm = feat[o:o + MEM_W]; o += MEM_W
    gates = feat[o:]
    qT_ref[...] = (_heads_norm_rope(q, _tile_lanes(qn_ref[...], tm), cos, sin) * QSCALE).astype(BF16)
    ks_ref[...] = _heads_norm_rope(ks, _tile_lanes(kns_ref[...], tm), cos, sin).T.astype(BF16)
    kw_ref[...] = _heads_norm_rope(kw, _tile_lanes(knw_ref[...], tm), cos, sin).T.astype(BF16)
    vsT_ref[...] = _augment_values(vs).astype(BF16)
    vwT_ref[...] = _augment_values(vw).astype(BF16)
    gT_ref[...] = jax.nn.sigmoid(gates)
    qmT_ref[...] = (_heads_norm_rope(qm, _tile_lanes(mqn_ref[...], tm), None, None) * QSCALE).astype(BF16)


def _nsa_proj(x, g, w_in, qn, kn, mem_qn, seq):
    t, d = x.shape
    tm = min(PROJ_TM, seq)
    npos = seq // tm
    wq, wkc, wvc, wks, wvs, wkw, wvw, wgt, wqm = jnp.split(w_in, np.cumsum(NSA_SIZES)[:-1].tolist(), axis=1)
    wtok = jnp.concatenate([wkc, wvc], axis=1).astype(BF16)
    wgt = jnp.pad(wgt, ((0, 0), (0, NSA_GATES_PAD)))
    wfeat = jnp.concatenate([wq, wks, wkw, wvs, wvw, wqm, wgt], axis=1).T.astype(BF16)
    cosF, sinF = _rope_tables(jnp.arange(seq))
    row = lambda n: pl.BlockSpec((tm, n), lambda i: (i, 0))
    col = lambda n: pl.BlockSpec((n, tm), lambda i: (0, i))
    tab = pl.BlockSpec((HALF, tm), lambda i: (0, i % npos))
    gain = _const_spec((HD, LANES))
    ngt = NSA_GATES + NSA_GATES_PAD
    return pl.pallas_call(
        _nsa_proj_body,
        out_shape=(
            jax.ShapeDtypeStruct((t, NSA_KV), F32), jax.ShapeDtypeStruct((t, NSA_KV), F32),
            jax.ShapeDtypeStruct((t, NSA_KV), BF16), jax.ShapeDtypeStruct((t, NSA_KV), BF16),
            jax.ShapeDtypeStruct((NSA_H * HD, t), BF16), jax.ShapeDtypeStruct((NSA_G * VAUG, t), BF16),
            jax.ShapeDtypeStruct((NSA_G * VAUG, t), BF16), jax.ShapeDtypeStruct((ngt, t), F32),
            jax.ShapeDtypeStruct((MEM_W, t), BF16),
        ),
        grid=(t // tm,),
        in_specs=[row(d), _const_spec((1, d)), _const_spec(wtok.shape), _const_spec(wfeat.shape),
                  tab, tab, gain, gain, gain, gain],
        out_specs=(row(NSA_KV), row(NSA_KV), row(NSA_KV), row(NSA_KV), col(NSA_H * HD), col(NSA_G * VAUG),
                   col(NSA_G * VAUG), col(ngt), col(MEM_W)),
        compiler_params=_cparams("parallel"),
        name="nsa_proj",
    )(x, g.reshape(1, d), wtok, wfeat, cosF, sinF, _lane_gain(qn), _lane_gain(kn[1]), _lane_gain(kn[2]),
      _lane_gain(mem_qn))


def _nsa_cmp_body(xk_ref, xv_ref, pk_ref, pv_ref, wk_ref, wv_ref, kn_ref, cos_ref, sin_ref, k_ref, vT_ref):
    def compress(x, pos, w_ref):
        n = x.shape[0]
        xa = (x + pos[0:1]).astype(BF16)
        xb = (pltpu.roll(x, n - 1, axis=0) + pos[1:2]).astype(BF16)
        return _dot_nt(w_ref[0], xa) + _dot_nt(w_ref[1], xb)

    kT = compress(xk_ref[0], pk_ref[...], wk_ref)
    kT = _heads_norm_rope(kT, _tile_lanes(kn_ref[...], kT.shape[-1]), cos_ref[...], sin_ref[...])
    k_ref[0] = kT.T.astype(BF16)
    vT_ref[0] = compress(xv_ref[0], pv_ref[...], wv_ref).astype(BF16)


def _nsa_cmp_weights(w, pos):
    eye = jnp.eye(NSA_G, dtype=F32)
    halves = w.reshape(CMP_L // CMP_S, CMP_S, HD, HD)
    wt = jnp.einsum('hg,alde->ahelgd', eye, halves).reshape(CMP_L // CMP_S, NSA_KV, CMP_S * NSA_KV)
    p = jnp.broadcast_to(pos.reshape(CMP_L // CMP_S, CMP_S, 1, HD), (CMP_L // CMP_S, CMP_S, NSA_G, HD))
    return wt.astype(BF16), p.reshape(CMP_L // CMP_S, CMP_S * NSA_KV).astype(F32)


def _nsa_cmp(kc, vc, pos_k, pos_v, wk, wv, kn0, batch, seq):
    n = seq // CMP_S
    width = CMP_S * NSA_KV
    xk = kc.reshape(batch, n, width)
    xv = vc.reshape(batch, n, width)
    wkt, pk = _nsa_cmp_weights(wk, pos_k)
    wvt, pv = _nsa_cmp_weights(wv, pos_v)
    cosE, sinE = _rope_tables(jnp.arange(n) * CMP_S + (CMP_L - 1))
    xspec = pl.BlockSpec((1, n, width), lambda b: (b, 0, 0))
    return pl.pallas_call(
        _nsa_cmp_body,
        out_shape=(jax.ShapeDtypeStruct((batch, n, NSA_KV), BF16), jax.ShapeDtypeStruct((batch, NSA_KV, n), BF16)),
        grid=(batch,),
        in_specs=[xspec, xspec, _const_spec(pk.shape), _const_spec(pv.shape), _const_spec(wkt.shape),
                  _const_spec(wvt.shape), _const_spec((HD, LANES)), _const_spec((HALF, n)), _const_spec((HALF, n))],
        out_specs=(pl.BlockSpec((1, n, NSA_KV), lambda b: (b, 0, 0)), pl.BlockSpec((1, NSA_KV, n), lambda b: (b, 0, 0))),
        compiler_params=_cparams("parallel"),
        name="nsa_cmp",
    )(xk, xv, pk, pv, wkt, wvt, _lane_gain(kn0), cosE, sinE)


NSA_TQ = 256
NSA_TK = 256


def _staged_attention(nheads, lo, hi, tk, logits, values, s_ref, mc_ref, m_ref, acc_ref):
    tq = s_ref.shape[-1]
    fold = lambda a: a.reshape(tk // SUBLANES, SUBLANES, tq)
    m_ref[...] = jnp.full(m_ref.shape, -jnp.inf, F32)
    acc_ref[...] = jnp.zeros(acc_ref.shape, F32)

    start = lambda c: c * tk if isinstance(c, int) else pl.multiple_of(c * tk, tk)

    def stage(h, c, slot):
        s = logits(h, start(c))
        s_ref[slot, h] = s
        mc_ref[slot, h * SUBLANES:(h + 1) * SUBLANES, :] = jnp.max(fold(s), axis=0)

    def consume(h, c, slot):
        g8, hrows = slice(h * SUBLANES, (h + 1) * SUBLANES), slice(h * VAUG, (h + 1) * VAUG)
        m_old = m_ref[h:h + 1, :]
        m_new = jnp.maximum(m_old, jnp.max(mc_ref[slot, g8, :], axis=0, keepdims=True))
        alpha = jnp.exp2(m_old - m_new)
        p = jnp.exp2(s_ref[slot, h] - m_new)
        acc_ref[hrows, :] = acc_ref[hrows, :] * alpha + _dot(values(h, start(c)), p.astype(BF16))
        m_ref[h:h + 1, :] = m_new

    def consume_and_stage_next(c, slot):
        for h in range(nheads):
            consume(h, c, slot)
            stage(h, c + 1, 1 - slot)

    def finish(c, slot):
        for h in range(nheads):
            consume(h, c, slot)

    for h in range(nheads):
        stage(h, lo, 0)
    n_fused = hi - 1 - lo

    def two_chunks(j, carry):
        c = lo + 2 * j
        consume_and_stage_next(c, 0)
        consume_and_stage_next(c + 1, 1)
        return carry

    if isinstance(n_fused, int):
        for i in range(n_fused):
            consume_and_stage_next(lo + i, i % 2)
        finish(hi - 1, n_fused % 2)
    else:
        lax.fori_loop(0, n_fused // 2, two_chunks, 0)

        @pl.when(n_fused % 2 == 1)
        def _():
            consume_and_stage_next(hi - 2, 0)
            finish(hi - 1, 1)

        @pl.when(n_fused % 2 == 0)
        def _():
            finish(hi - 1, 0)

    return [acc_ref[h * VAUG:h * VAUG + HD, :] / acc_ref[h * VAUG + HD:h * VAUG + HD + 1, :] for h in range(nheads)]


def _nsa_attn_body(qT_ref, gT_ref, ks_ref, vsT_ref, kw_ref, vwT_ref, kc_ref, vcT_ref, cov_ref, y_ref,
                   bias_ref, out_ref, s_ref, mc_ref, m_ref, acc_ref, *, seq):
    tq, tk = NSA_TQ, NSA_TK
    qs = pl.program_id(1) * tq
    nkc = (qs + tq) // tk
    wlo = jnp.maximum((qs - WIN) // tk, 0)
    qpos = qs + lax.broadcasted_iota(I32, (1, tq), 1)
    ncr = seq // CMP_S
    nblk = seq // SLC_L
    n_sel = min(SLC_N_MAX, nblk)
    valid_c = CMP_S * lax.broadcasted_iota(I32, (ncr, tq), 0) + (CMP_L - 1) <= qpos
    any_c = jnp.where(qpos >= CMP_L - 1, 1.0, 0.0)
    jrow = lax.broadcasted_iota(I32, (nblk, tq), 0)
    qblk = qpos // SLC_L
    forced = (jrow == 0) | (jrow == qblk) | (jrow == qblk - 1)
    causal_blk = jrow * SLC_L <= qpos
    krow = lax.broadcasted_iota(I32, (tk, tq), 0)
    brow = lax.broadcasted_iota(I32, (SLC_L, tq), 0)
    group = lambda h: h // NSA_R
    pair = lambda h: slice((group(h) // 2) * LANES, (group(h) // 2 + 1) * LANES)
    grows = lambda h: slice(group(h) * HD, (group(h) + 1) * HD)
    vrows = lambda h: slice(group(h) * VAUG, (group(h) + 1) * VAUG)
    q_of = lambda h: _pad_rows(qT_ref[h * HD:(h + 1) * HD, :], group(h) % 2)
    gate = lambda h, branch: gT_ref[h * 3 + branch:h * 3 + branch + 1, :]

    for g in range(NSA_G):
        heads = [g * NSA_R + r for r in range(NSA_R)]
        kcm = kc_ref[0][:, pair(heads[0])]
        vcm = vcT_ref[0][grows(heads[0]), :]
        psum = jnp.zeros((ncr, tq), F32)
        for h in heads:
            s = jnp.where(valid_c, _dot(kcm, q_of(h)), NEG)
            p = jnp.exp2(s - jnp.max(s, axis=0, keepdims=True))
            p = p / jnp.sum(p, axis=0, keepdims=True) * any_c
            out_ref[h * HD:(h + 1) * HD, :] = gate(h, 0) * _dot(vcm, p.astype(BF16))
            psum = psum + p
        p_hi = psum.astype(BF16)
        p_lo = (psum - p_hi.astype(F32)).astype(BF16)
        imp = _dot(cov_ref[...], p_hi) + _dot(cov_ref[...], p_lo)
        imp = jnp.where(causal_blk, imp + jnp.where(forced, FORCE, 0.0), NEG)
        rank = jnp.zeros((nblk, tq), I32)
        for j2 in range(nblk):
            rj = imp[j2:j2 + 1]
            beats = (rj > imp) | ((rj == imp) & (j2 < jrow))
            rank = rank + beats.astype(I32)
        selb = jnp.where(rank < n_sel, 0.0, NEG)
        for j in range(nblk):
            bias_ref[g, j * SLC_L:(j + 1) * SLC_L, :] = jnp.where(j * SLC_L + brow <= qpos, selb[j:j + 1], NEG)

    stats = (s_ref, mc_ref, m_ref, acc_ref)
    slc = _staged_attention(
        NSA_H, 0, nkc, tk,
        lambda h, r0: _dot(ks_ref[pl.ds(r0, tk), pair(h)], q_of(h)) + bias_ref[group(h), pl.ds(r0, tk), :],
        lambda h, r0: vsT_ref[vrows(h), pl.ds(r0, tk)], *stats)
    for h in range(NSA_H):
        out_ref[h * HD:(h + 1) * HD, :] = out_ref[h * HD:(h + 1) * HD, :] + gate(h, 1) * slc[h]

    def win_logits(h, r0):
        dist = qpos - (r0 + krow)
        return jnp.where((dist >= 0) & (dist < WIN), _dot(kw_ref[pl.ds(r0, tk), pair(h)], q_of(h)), NEG)

    win = _staged_attention(NSA_H, wlo, nkc, tk, win_logits, lambda h, r0: vwT_ref[vrows(h), pl.ds(r0, tk)], *stats)
    outs = [out_ref[h * HD:(h + 1) * HD, :] + gate(h, 2) * win[h] for h in range(NSA_H)]
    y_ref[...] = jnp.concatenate(outs, axis=0).T.astype(BF16)


def _nsa_attn(qT, gT, ks, vsT, kw, vwT, kcmp, vcmpT, batch, seq):
    t = ks.shape[0]
    tq = NSA_TQ
    nq = seq // tq
    ncr, nblk = seq // CMP_S, seq // SLC_L
    starts = np.arange(ncr) * CMP_S
    sstart = np.arange(nblk) * SLC_L
    cover = (starts[None, :] < sstart[:, None] + SLC_L) & (starts[None, :] + CMP_L > sstart[:, None])
    cover[:, ncr - 1] = False
    colq = lambda n: pl.BlockSpec((n, tq), lambda b, i: (0, b * nq + i))
    tok = pl.BlockSpec((seq, NSA_KV), lambda b, i: (b, 0))
    feat = pl.BlockSpec((NSA_G * VAUG, seq), lambda b, i: (0, b))
    return pl.pallas_call(
        functools.partial(_nsa_attn_body, seq=seq),
        out_shape=jax.ShapeDtypeStruct((t, NSA_H * HD), BF16),
        grid=(batch, nq),
        in_specs=[
            colq(NSA_H * HD), colq(gT.shape[0]), tok, feat, tok, feat,
            pl.BlockSpec((1, ncr, NSA_KV), lambda b, i: (b, 0, 0)),
            pl.BlockSpec((1, NSA_KV, ncr), lambda b, i: (b, 0, 0)),
            _const_spec((nblk, ncr)),
        ],
        out_specs=pl.BlockSpec((tq, NSA_H * HD), lambda b, i: (b * nq + i, 0)),
        scratch_shapes=[pltpu.VMEM((NSA_G, seq, tq), F32), pltpu.VMEM((NSA_H * HD, tq), F32),
                        pltpu.VMEM((2, NSA_H, NSA_TK, tq), F32), pltpu.VMEM((2, NSA_H * SUBLANES, tq), F32),
                        pltpu.VMEM((2 * SUBLANES, tq), F32), pltpu.VMEM((NSA_H * VAUG, tq), F32)],
        compiler_params=_cparams("parallel", "arbitrary"),
        name="nsa_attn",
    )(qT, gT, ks, vsT, kw, vwT, kcmp, vcmpT, jnp.asarray(cover, BF16))


def _nsa_layer_mix(x, g, w_in, qn, kn, pos_k, pos_v, wk, wv, mem_qn, batch, seq):
    kc, vc, ks, kw, qT, vsT, vwT, gT, qmT = _nsa_proj(x, g, w_in, qn, kn, mem_qn, seq)
    kcmp, vcmpT = _nsa_cmp(kc, vc, pos_k, pos_v, wk, wv, kn[0], batch, seq)
    return _nsa_attn(qT, gT, ks, vsT, kw, vwT, kcmp, vcmpT, batch, seq), qmT


def kernel(x, mem, ffn_norm, ffn_w_gate, ffn_w_up, ffn_w_down, mix_norm, w_out, mem_norm, mem_w_kv, mem_qn, mem_kn, ret_w_in, dsa_w_in, dsa_qn, dsa_kn, nsa_w_in, nsa_qn, nsa_kn, nsa_cmp_pos_k, nsa_cmp_pos_v, nsa_cmp_wk, nsa_cmp_wv):
    batch, seq, d = x.shape
    x = x.reshape(batch * seq, d)
    ffn_w = (ffn_w_gate.astype(BF16), ffn_w_up.astype(BF16), ffn_w_down.astype(BF16))
    for i in range(ffn_norm.shape[0]):
        x = _ffn(x, ffn_norm[i, 0], *ffn_w, i, 0)
        kind, j = i % N_MIXERS, i // N_MIXERS
        if kind == 0:
            q, v, gate, kT, qmT = _ret_proj(x, mix_norm[i], ret_w_in[j], mem_qn[i], seq)
            y_mix = _retention(q, kT, v, gate, batch, seq)
        elif kind == 1:
            kk, qT, iqT, vT, iwT, qmT = _dsa_proj(x, mix_norm[i], dsa_w_in[j], dsa_qn[j], dsa_kn[j], mem_qn[i], seq)
            y_mix = _dsa_attn(kk, qT, iqT, vT, iwT, batch, seq)
        else:
            y_mix, qmT = _nsa_layer_mix(x, mix_norm[i], nsa_w_in[j], nsa_qn[j], nsa_kn[j], nsa_cmp_pos_k[j],
                                        nsa_cmp_pos_v[j], nsa_cmp_wk[j], nsa_cmp_wv[j], mem_qn[i], batch, seq)
        mem_k, mem_vT = _mem_kv(mem, mem_norm[i], mem_w_kv[i], mem_kn[i])
        y_mem = _mem_attn(qmT, mem_k, mem_vT, seq)
        x = _mix_out_ffn(y_mix, y_mem, x, w_out[i], ffn_norm[i, 1], *ffn_w, i, 1)
    return x.reshape(batch, seq, d)
```

```python
import functools
import math

import jax
import jax.numpy as jnp
import numpy as np
from jax import lax
from jax.experimental import pallas as pl
from jax.experimental.pallas import tpu as pltpu

D_MODEL = 1024
HD = 64
HALF = HD // 2
H_MIX = 12
H_MEM = 4
MIX_W = H_MIX * HD
MEM_W = H_MEM * HD
D_FF = 2816
ROPE_THETA = 10000.0
EPS = 1e-6
NEG = -1e30
FORCE = 1e9
SCALE = HD ** -0.5
QSCALE = SCALE * math.log2(math.e)

RET_H, RET_DK, RET_DV, RET_CHUNK = 6, 64, 128, 128
DSA_H, IDX_H, IDX_D, DSA_TOPK_MAX = 12, 8, 64, 256
NSA_H, NSA_G, CMP_L, CMP_S, SLC_L, SLC_N_MAX, WIN = 12, 4, 32, 16, 64, 16, 512
NSA_R = NSA_H // NSA_G
N_MIXERS = 3

RET_SIZES = [RET_H * RET_DK, RET_H * RET_DK, RET_H * RET_DV, RET_H * RET_DV, MEM_W]
DSA_SIZES = [DSA_H * HD, HD, HD, IDX_H * IDX_D, IDX_D, IDX_H, MEM_W]
NSA_SIZES = [NSA_H * HD] + [NSA_G * HD] * 6 + [NSA_H * 3, MEM_W]

LANES = 128
SUBLANES = 8
PACKED_SUBLANES = 2 * SUBLANES
VAUG = HD + PACKED_SUBLANES
MXU_N = 256
VMEM_LIMIT_BYTES = 56 * 1024 * 1024

BF16 = jnp.bfloat16
F32 = jnp.float32
I32 = jnp.int32
I16 = jnp.int16


def _cparams(*sem):
    return pltpu.CompilerParams(dimension_semantics=sem, vmem_limit_bytes=VMEM_LIMIT_BYTES)


def _const_spec(shape):
    n = len(shape)
    return pl.BlockSpec(shape, lambda *_: (0,) * n, pipeline_mode=pl.Buffered(1))


def _rms(x, g):
    return x * lax.rsqrt(jnp.mean(x * x, axis=-1, keepdims=True) + EPS) * g


def _dot(a, b):
    return jnp.dot(a, b, preferred_element_type=F32)


def _dot_nt(a, b):
    return lax.dot_general(a, b, (((1,), (1,)), ((), ())), preferred_element_type=F32)


def _tile_lanes(a, n):
    reps = n // a.shape[-1]
    return a if reps == 1 else jnp.concatenate([a] * reps, axis=-1)


def _rope_tables(pos):
    inv = ROPE_THETA ** (-jnp.arange(HALF, dtype=F32) / HALF)
    ang = pos.astype(F32)[:, None] * inv[None, :]
    return jnp.cos(ang).T, jnp.sin(ang).T


def _lane_gain(g):
    return jnp.broadcast_to(g.astype(F32)[:, None], (g.shape[0], LANES))


def _augment_values(v):
    t = v.shape[-1]
    v3 = v.reshape(v.shape[0] // HD, HD, t)
    tail = jnp.where(lax.broadcasted_iota(I32, (v3.shape[0], VAUG - HD, t), 1) == 0, 1.0, 0.0).astype(v.dtype)
    return jnp.concatenate([v3, tail], axis=1).reshape(v3.shape[0] * VAUG, t)


def _heads_norm_rope(x, gain, cos, sin):
    t = x.shape[-1]
    x3 = x.reshape(x.shape[0] // HD, HD, t)
    if gain is not None:
        ms = jnp.sum(x3 * x3, axis=1, keepdims=True) * (1.0 / HD)
        x3 = x3 * lax.rsqrt(ms + EPS) * gain[None]
    if cos is not None:
        x1, x2 = x3[:, :HALF], x3[:, HALF:]
        x3 = jnp.concatenate([x1 * cos[None] - x2 * sin[None], x2 * cos[None] + x1 * sin[None]], axis=1)
    return x3.reshape(x.shape)


FFN_TM = 1024
FFN_TF = MXU_N


def _swiglu_half_step(x, g_ref, wg_ref, wu_ref, wd_ref):
    h = _rms(x, g_ref[...]).astype(BF16)
    acc = jnp.zeros(x.shape, F32)
    for c in range(D_FF // FFN_TF):
        sl = slice(c * FFN_TF, (c + 1) * FFN_TF)
        gate = _dot(h, wg_ref[:, sl])
        up = _dot(h, wu_ref[:, sl])
        act = (gate * jax.nn.sigmoid(gate) * up).astype(BF16)
        acc = acc + _dot(act, wd_ref[sl, :])
    return x + 0.5 * acc


def _ffn_body(x_ref, g_ref, wg_ref, wu_ref, wd_ref, o_ref):
    o_ref[...] = _swiglu_half_step(x_ref[...], g_ref, wg_ref, wu_ref, wd_ref)


def _mix_out_ffn_body(ymix_ref, ymem_ref, x_ref, wmix_ref, wmem_ref, g_ref, wg_ref, wu_ref, wd_ref, o_ref):
    x = x_ref[...] + _dot(ymix_ref[...], wmix_ref[...]) + _dot(ymem_ref[...], wmem_ref[...])
    o_ref[...] = _swiglu_half_step(x, g_ref, wg_ref, wu_ref, wd_ref)


def _ffn_specs(d, layer, half):
    pick = lambda r, c: pl.BlockSpec((None, None, r, c), lambda i: (layer, half, 0, 0), pipeline_mode=pl.Buffered(1))
    return [_const_spec((1, d)), pick(d, D_FF), pick(d, D_FF), pick(D_FF, d)]


def _ffn(x, g, wg, wu, wd, layer, half):
    t, d = x.shape
    tm = min(FFN_TM, t)
    return pl.pallas_call(
        _ffn_body,
        out_shape=jax.ShapeDtypeStruct((t, d), F32),
        grid=(t // tm,),
        in_specs=[pl.BlockSpec((tm, d), lambda i: (i, 0))] + _ffn_specs(d, layer, half),
        out_specs=pl.BlockSpec((tm, d), lambda i: (i, 0)),
        compiler_params=_cparams("parallel"),
        name="ffn",
    )(x, g.reshape(1, d), wg, wu, wd)


def _mix_out_ffn(y_mix, y_mem, x, w_out, g, wg, wu, wd, layer, half):
    t, d = x.shape
    tm = min(FFN_TM, t)
    return pl.pallas_call(
        _mix_out_ffn_body,
        out_shape=jax.ShapeDtypeStruct((t, d), F32),
        grid=(t // tm,),
        in_specs=[
            pl.BlockSpec((tm, MIX_W), lambda i: (i, 0)),
            pl.BlockSpec((tm, MEM_W), lambda i: (i, 0)),
            pl.BlockSpec((tm, d), lambda i: (i, 0)),
            _const_spec((MIX_W, d)),
            _const_spec((MEM_W, d)),
        ] + _ffn_specs(d, layer, half),
        out_specs=pl.BlockSpec((tm, d), lambda i: (i, 0)),
        compiler_params=_cparams("parallel"),
        name="mix_out_ffn",
    )(y_mix, y_mem, x, w_out[:MIX_W].astype(BF16), w_out[MIX_W:].astype(BF16), g.reshape(1, d), wg, wu, wd)


PROJ_TM = 1024


def _mem_kv_body(m_ref, g_ref, wk_ref, wvT_ref, kn_ref, k_ref, vT_ref):
    h = _rms(m_ref[0], g_ref[...]).astype(BF16)
    kT = _dot_nt(wk_ref[...], h)
    kT = _heads_norm_rope(kT, _tile_lanes(kn_ref[...], kT.shape[-1]), None, None)
    k_ref[0] = kT.T.astype(BF16)
    vT_ref[0] = _augment_values(_dot_nt(wvT_ref[...], h)).astype(BF16)


def _mem_kv(mem, g, w_kv, kn):
    b, m, d = mem.shape
    return pl.pallas_call(
        _mem_kv_body,
        out_shape=(jax.ShapeDtypeStruct((b, m, MEM_W), BF16), jax.ShapeDtypeStruct((b, H_MEM * VAUG, m), BF16)),
        grid=(b,),
        in_specs=[
            pl.BlockSpec((1, m, d), lambda i: (i, 0, 0)),
            _const_spec((1, d)),
            _const_spec((MEM_W, d)),
            _const_spec((MEM_W, d)),
            _const_spec((HD, LANES)),
        ],
        out_specs=(pl.BlockSpec((1, m, MEM_W), lambda i: (i, 0, 0)),
                   pl.BlockSpec((1, H_MEM * VAUG, m), lambda i: (i, 0, 0))),
        compiler_params=_cparams("parallel"),
        name="mem_kv",
    )(mem, g.reshape(1, d), w_kv[:, :MEM_W].T.astype(BF16), w_kv[:, MEM_W:].T.astype(BF16), _lane_gain(kn))


MEM_TQ = 512


def _pad_rows(blk, half, total=LANES):
    z = jnp.zeros_like(blk)
    parts = [z] * (total // HD)
    parts[half] = blk
    return jnp.concatenate(parts, axis=0)


def _mem_attn_body(qT_ref, k_ref, vT_ref, o_ref, s_ref, mc_ref, m_ref, acc_ref):
    n_mem = k_ref.shape[1]
    outs = _staged_attention(
        H_MEM, 0, 1, n_mem,
        lambda h, r0: _dot(k_ref[0, :, (h // 2) * LANES:(h // 2 + 1) * LANES],
                           _pad_rows(qT_ref[h * HD:(h + 1) * HD, :], h % 2)),
        lambda h, r0: vT_ref[0, h * VAUG:(h + 1) * VAUG, :], s_ref, mc_ref, m_ref, acc_ref)
    o_ref[...] = jnp.concatenate(outs, axis=0).T.astype(BF16)


def _mem_attn(qmT, k, vT, seq):
    t = qmT.shape[1]
    b, m, _ = k.shape
    tq = min(MEM_TQ, seq)
    nq = seq // tq
    return pl.pallas_call(
        _mem_attn_body,
        out_shape=jax.ShapeDtypeStruct((t, MEM_W), BF16),
        grid=(b, nq),
        in_specs=[
            pl.BlockSpec((MEM_W, tq), lambda bi, i: (0, bi * nq + i)),
            pl.BlockSpec((1, m, MEM_W), lambda bi, i: (bi, 0, 0)),
            pl.BlockSpec((1, H_MEM * VAUG, m), lambda bi, i: (bi, 0, 0)),
        ],
        out_specs=pl.BlockSpec((tq, MEM_W), lambda bi, i: (bi * nq + i, 0)),
        scratch_shapes=[pltpu.VMEM((2, H_MEM, m, tq), F32), pltpu.VMEM((2, H_MEM * SUBLANES, tq), F32),
                        pltpu.VMEM((SUBLANES, tq), F32), pltpu.VMEM((H_MEM * VAUG, tq), F32)],
        compiler_params=_cparams("parallel", "parallel"),
        name="mem_attn",
    )(qmT, k, vT)


def _ret_proj_body(x_ref, g_ref, wtok_ref, wfeat_ref, cosT_ref, sinT_ref, cosF_ref, sinF_ref, mqn_ref,
                   q_ref, v_ref, gate_ref, kT_ref, qmT_ref):
    h = _rms(x_ref[...], g_ref[...]).astype(BF16)
    tok = _dot(h, wtok_ref[...])
    nq, nv = RET_H * RET_DK, RET_H * RET_DV
    q = tok[:, :nq]
    lane = lax.broadcasted_iota(I32, q.shape, 1)
    rot = jnp.where(lane % HD < HALF, pltpu.roll(q, nq - HALF, axis=1), pltpu.roll(q, HALF, axis=1))
    q_ref[...] = (q * _tile_lanes(cosT_ref[...], nq) + rot * _tile_lanes(sinT_ref[...], nq)).astype(BF16)
    v_ref[...] = tok[:, nq:nq + nv].astype(BF16)
    gate_ref[...] = tok[:, nq + nv:]
    feat = _dot_nt(wfeat_ref[...], h)
    tm = feat.shape[-1]
    cos, sin = cosF_ref[...], sinF_ref[...]
    kT = _heads_norm_rope(feat[:nq], None, cos, sin) * (RET_DK ** -0.5)
    kT_ref[...] = kT.astype(BF16)
    qm = _heads_norm_rope(feat[nq:], _tile_lanes(mqn_ref[...], tm), None, None) * QSCALE
    qmT_ref[...] = qm.astype(BF16)


def _token_rope_tables(seq):
    cosF, sinF = _rope_tables(jnp.arange(seq))
    cos = jnp.tile(cosF.T, (1, LANES // HALF))
    sign = jnp.where((jnp.arange(LANES) % HD) < HALF, -1.0, 1.0).astype(F32)
    sin = jnp.tile(sinF.T, (1, LANES // HALF)) * sign[None, :]
    return cos, sin


def _ret_proj(x, g, w_in, mem_qn, seq):
    t, d = x.shape
    tm = min(PROJ_TM, seq)
    npos = seq // tm
    wq, wk, wv, wg, wqm = jnp.split(w_in, np.cumsum(RET_SIZES)[:-1].tolist(), axis=1)
    wtok = jnp.concatenate([wq, wv, wg], axis=1).astype(BF16)
    wfeat = jnp.concatenate([wk, wqm], axis=1).T.astype(BF16)
    cosT, sinT = _token_rope_tables(seq)
    cosF, sinF = _rope_tables(jnp.arange(seq))
    nq, nv = RET_H * RET_DK, RET_H * RET_DV
    row = lambda n: pl.BlockSpec((tm, n), lambda i: (i, 0))
    col = lambda n: pl.BlockSpec((n, tm), lambda i: (0, i))
    return pl.pallas_call(
        _ret_proj_body,
        out_shape=(
            jax.ShapeDtypeStruct((t, nq), BF16), jax.ShapeDtypeStruct((t, nv), BF16),
            jax.ShapeDtypeStruct((t, nv), F32), jax.ShapeDtypeStruct((nq, t), BF16),
            jax.ShapeDtypeStruct((MEM_W, t), BF16),
        ),
        grid=(t // tm,),
        in_specs=[
            row(d), _const_spec((1, d)), _const_spec(wtok.shape), _const_spec(wfeat.shape),
            pl.BlockSpec((tm, LANES), lambda i: (i % npos, 0)), pl.BlockSpec((tm, LANES), lambda i: (i % npos, 0)),
            pl.BlockSpec((HALF, tm), lambda i: (0, i % npos)), pl.BlockSpec((HALF, tm), lambda i: (0, i % npos)),
            _const_spec((HD, LANES)),
        ],
        out_specs=(row(nq), row(nv), row(nv), col(nq), col(MEM_W)),
        compiler_params=_cparams("parallel"),
        name="ret_proj",
    )(x, g.reshape(1, d), wtok, wfeat, cosT, sinT, cosF, sinF, _lane_gain(mem_qn))


RET_TL = 512


def _ret_body(q_ref, kT_ref, v_ref, gate_ref, y_ref, state_ref):
    @pl.when(pl.program_id(1) == 0)
    def _():
        state_ref[...] = jnp.zeros_like(state_ref)

    c = RET_CHUNK
    ii = lax.broadcasted_iota(I32, (c, c), 0).astype(F32)
    jj = lax.broadcasted_iota(I32, (c, c), 1).astype(F32)
    diff = ii - jj
    jk = lax.broadcasted_iota(I32, (RET_DK, c), 1).astype(F32)
    log_gs = [math.log(1.0 - 2.0 ** (-5.0 - hh)) for hh in range(RET_H)]
    dmasks = [jnp.where(diff >= 0, jnp.exp(lg * jnp.maximum(diff, 0.0)), 0.0) for lg in log_gs]
    xis = [jnp.exp(lg * (ii + 1.0)) for lg in log_gs]
    zetas = [jnp.exp(lg * (c - 1.0 - jk)) for lg in log_gs]
    for n in range(q_ref.shape[0] // c):
        rows = slice(n * c, (n + 1) * c)
        for hh in range(RET_H):
            dmask, xi, zeta, decay = dmasks[hh], xis[hh], zetas[hh], math.exp(log_gs[hh] * c)
            pair = slice((hh // 2) * LANES, (hh // 2 + 1) * LANES)
            q2 = q_ref[rows, pair]
            kT = kT_ref[hh * RET_DK:(hh + 1) * RET_DK, rows]
            v = v_ref[rows, hh * RET_DV:(hh + 1) * RET_DV]
            state = state_ref[hh]
            inner = _dot(q2, _pad_rows(kT, hh % 2)) * dmask
            o = _dot(inner.astype(BF16), v)
            o = o + _dot(q2, _pad_rows(state.astype(BF16), hh % 2)) * xi
            kv = _dot((kT.astype(F32) * zeta).astype(BF16), v)
            state_ref[hh] = state * decay + kv
            mu = jnp.mean(o, axis=-1, keepdims=True)
            var = jnp.mean(jnp.square(o - mu), axis=-1, keepdims=True)
            o = (o - mu) * lax.rsqrt(var + EPS)
            gte = gate_ref[rows, hh * RET_DV:(hh + 1) * RET_DV]
            y_ref[rows, hh * RET_DV:(hh + 1) * RET_DV] = (gte * jax.nn.sigmoid(gte) * o).astype(BF16)


def _retention(q, kT, v, gate, batch, seq):
    t = q.shape[0]
    tl = min(RET_TL, seq)
    nl = seq // tl
    nq, nv = RET_H * RET_DK, RET_H * RET_DV
    return pl.pallas_call(
        _ret_body,
        out_shape=jax.ShapeDtypeStruct((t, nv), BF16),
        grid=(batch, nl),
        in_specs=[
            pl.BlockSpec((tl, nq), lambda b, i: (b * nl + i, 0)),
            pl.BlockSpec((nq, tl), lambda b, i: (0, b * nl + i)),
            pl.BlockSpec((tl, nv), lambda b, i: (b * nl + i, 0)),
            pl.BlockSpec((tl, nv), lambda b, i: (b * nl + i, 0)),
        ],
        out_specs=pl.BlockSpec((tl, nv), lambda b, i: (b * nl + i, 0)),
        scratch_shapes=[pltpu.VMEM((RET_H, RET_DK, RET_DV), F32)],
        compiler_params=_cparams("parallel", "arbitrary"),
        name="retention",
    )(q, kT, v, gate)


def _dsa_proj_body(x_ref, g_ref, w_ref, cos_ref, sin_ref, qn_ref, kn_ref, mqn_ref,
                   kk_ref, qT_ref, iqT_ref, vT_ref, iwT_ref, qmT_ref):
    h = _rms(x_ref[...], g_ref[...]).astype(BF16)
    feat = _dot_nt(w_ref[...], h)
    tm = feat.shape[-1]
    cos, sin = cos_ref[...], sin_ref[...]
    o = 0
    q = feat[o:o + DSA_H * HD]; o += DSA_H * HD
    iq = feat[o:o + IDX_H * IDX_D]; o += IDX_H * IDX_D
    k = feat[o:o + HD]; o += HD
    ik = feat[o:o + IDX_D]; o += IDX_D
    v = feat[o:o + HD]; o += HD
    qm = feat[o:o + MEM_W]; o += MEM_W
    iw = feat[o:o + IDX_H]
    qT_ref[...] = (_heads_norm_rope(q, _tile_lanes(qn_ref[...], tm), cos, sin) * QSCALE).astype(BF16)
    iqT_ref[...] = _heads_norm_rope(iq, None, cos, sin).astype(BF16)
    k = _heads_norm_rope(k, _tile_lanes(kn_ref[...], tm), cos, sin)
    ik = _heads_norm_rope(ik, None, cos, sin)
    kk_ref[...] = jnp.concatenate([k, ik], axis=0).T.astype(BF16)
    vT_ref[...] = _augment_values(v).astype(BF16)
    iwT_ref[...] = iw
    qmT_ref[...] = (_heads_norm_rope(qm, _tile_lanes(mqn_ref[...], tm), None, None) * QSCALE).astype(BF16)


def _dsa_proj(x, g, w_in, qn, kn, mem_qn, seq):
    t, d = x.shape
    tm = min(PROJ_TM, seq)
    npos = seq // tm
    wq, wk, wv, wiq, wik, wiw, wqm = jnp.split(w_in, np.cumsum(DSA_SIZES)[:-1].tolist(), axis=1)
    wfeat = jnp.concatenate([wq, wiq, wk, wik, wv, wqm, wiw], axis=1).T.astype(BF16)
    cosF, sinF = _rope_tables(jnp.arange(seq))
    col = lambda n: pl.BlockSpec((n, tm), lambda i: (0, i))
    tab = pl.BlockSpec((HALF, tm), lambda i: (0, i % npos))
    gain = _const_spec((HD, LANES))
    return pl.pallas_call(
        _dsa_proj_body,
        out_shape=(
            jax.ShapeDtypeStruct((t, LANES), BF16), jax.ShapeDtypeStruct((DSA_H * HD, t), BF16),
            jax.ShapeDtypeStruct((IDX_H * IDX_D, t), BF16), jax.ShapeDtypeStruct((VAUG, t), BF16),
            jax.ShapeDtypeStruct((IDX_H, t), F32), jax.ShapeDtypeStruct((MEM_W, t), BF16),
        ),
        grid=(t // tm,),
        in_specs=[pl.BlockSpec((tm, d), lambda i: (i, 0)), _const_spec((1, d)), _const_spec(wfeat.shape),
                  tab, tab, gain, gain, gain],
        out_specs=(pl.BlockSpec((tm, LANES), lambda i: (i, 0)), col(DSA_H * HD), col(IDX_H * IDX_D), col(VAUG),
                   col(IDX_H), col(MEM_W)),
        compiler_params=_cparams("parallel"),
        name="dsa_proj",
    )(x, g.reshape(1, d), wfeat, cosF, sinF, _lane_gain(qn), _lane_gain(kn), _lane_gain(mem_qn))


DSA_TQ = 256
DSA_TK = 256
HALF_BITS = 16
INT16_MIN = -2 ** (HALF_BITS - 1)


def _dsa_attn_body(kk_ref, qT_ref, iqT_ref, vT_ref, iwT_ref, y_ref, keys_ref, hi_ref, lo_ref, bias_ref, s_ref, mc_ref,
                   m_ref, acc_ref, *, topk, seq):
    tq, tk = DSA_TQ, DSA_TK
    qs = pl.program_id(1) * tq
    nkc = (qs + tq) // tk
    qpos = qs + lax.broadcasted_iota(I32, (1, tq), 1)
    row = lax.broadcasted_iota(I32, (tk, tq), 0)

    def score_chunk(c, carry):
        r0 = pl.multiple_of(c * tk, tk)
        kkc = kk_ref[pl.ds(r0, tk), :]
        acc = jnp.zeros((tk, tq), F32)
        for h in range(IDX_H):
            r = _dot(kkc, _pad_rows(iqT_ref[h * IDX_D:(h + 1) * IDX_D, :], 1))
            acc = acc + jnp.maximum(r, 0.0) * iwT_ref[h:h + 1, :]
        sc = jnp.where(r0 + row <= qpos, acc + 0.0, -jnp.inf)
        bits = pltpu.bitcast(sc, I32)
        key = jnp.where(bits < 0, bits ^ jnp.int32(0x7FFFFFFF), bits)
        keys_ref[pl.ds(r0, tk), :] = key
        hi_ref[pl.ds(r0, tk), :] = lax.shift_right_arithmetic(key, HALF_BITS).astype(I16)
        return carry

    lax.fori_loop(0, nkc, score_chunk, 0)

    def count(pred):
        def body(c, acc):
            r0 = pl.multiple_of(c * tk, tk)
            m = pred(keys_ref[pl.ds(r0, tk), :], r0 + row)
            return acc + jnp.sum(m.astype(I32).reshape(tk // SUBLANES, SUBLANES, tq), axis=0)
        acc = lax.fori_loop(0, nkc, body, jnp.zeros((SUBLANES, tq), I32))
        return jnp.sum(acc, axis=0, keepdims=True)

    def count16(ref, pred):
        def body(c, acc):
            r0 = pl.multiple_of(c * tk, tk)
            ones = jnp.where(pred(ref[pl.ds(r0, tk), :]), jnp.int16(1), jnp.int16(0))
            parts = [ones[i:i + PACKED_SUBLANES] for i in range(0, tk, PACKED_SUBLANES)]
            while len(parts) > 1:
                parts = [a + b for a, b in zip(parts[::2], parts[1::2])]
            return acc + parts[0]
        acc = lax.fori_loop(0, nkc, body, jnp.zeros((PACKED_SUBLANES, tq), I16))
        return jnp.sum(acc.astype(I32), axis=0, keepdims=True)

    def kth_largest16(ref, k):
        thr = jnp.where(count16(ref, lambda v: v >= 0) >= k, jnp.int32(0), jnp.int32(INT16_MIN))

        def bit_step(b, thr):
            cand = thr | lax.shift_left(jnp.int32(1), (HALF_BITS - 2) - b)
            c16 = cand.astype(I16)
            return jnp.where(count16(ref, lambda v: v >= c16) >= k, cand, thr)

        return lax.fori_loop(0, HALF_BITS - 1, bit_step, thr)

    thr_hi = kth_largest16(hi_ref, topk)
    thr_hi16 = thr_hi.astype(I16)
    need_lo = topk - count16(hi_ref, lambda v: v > thr_hi16)

    def low_half_chunk(c, carry):
        r0 = pl.multiple_of(c * tk, tk)
        lo = ((keys_ref[pl.ds(r0, tk), :] & jnp.int32(0xFFFF)) + jnp.int32(INT16_MIN)).astype(I16)
        lo_ref[pl.ds(r0, tk), :] = jnp.where(hi_ref[pl.ds(r0, tk), :] == thr_hi16, lo, jnp.int16(INT16_MIN))
        return carry

    lax.fori_loop(0, nkc, low_half_chunk, 0)
    thr_lo = kth_largest16(lo_ref, need_lo)
    thr = lax.shift_left(thr_hi, HALF_BITS) | (thr_lo - jnp.int32(INT16_MIN))
    def tie_search():
        need = topk - count(lambda k, _: k > thr)

        def idx_step(b, q):
            cand = q | lax.shift_left(jnp.int32(1), (seq.bit_length() - 2) - b)
            return jnp.where(count(lambda k, idx: (k == thr) & (idx < cand)) < need, cand, q)

        return lax.fori_loop(0, seq.bit_length() - 1, idx_step, jnp.zeros((1, tq), I32))

    has_ties = jnp.max(count(lambda k, _: k >= thr)) > topk
    last = lax.cond(has_ties, tie_search, lambda: jnp.full((1, tq), seq, I32))

    def bias_chunk(c, carry):
        r0 = pl.multiple_of(c * tk, tk)
        k = keys_ref[pl.ds(r0, tk), :]
        idx = r0 + row
        sel = (k > thr) | ((k == thr) & (idx <= last))
        bias_ref[pl.ds(r0, tk), :] = jnp.where(sel & (idx <= qpos), 0.0, NEG)
        return carry

    lax.fori_loop(0, nkc, bias_chunk, 0)

    outs = _staged_attention(
        DSA_H, 0, nkc, tk,
        lambda h, r0: _dot(kk_ref[pl.ds(r0, tk), :], _pad_rows(qT_ref[h * HD:(h + 1) * HD, :], 0))
        + bias_ref[pl.ds(r0, tk), :],
        lambda h, r0: vT_ref[:, pl.ds(r0, tk)], s_ref, mc_ref, m_ref, acc_ref)
    y_ref[...] = jnp.concatenate(outs, axis=0).T.astype(BF16)


def _dsa_attn(kk, qT, iqT, vT, iwT, batch, seq):
    t = kk.shape[0]
    tq = DSA_TQ
    nq = seq // tq
    topk = min(DSA_TOPK_MAX, seq // 4)
    colq = lambda n: pl.BlockSpec((n, tq), lambda b, i: (0, b * nq + i))
    return pl.pallas_call(
        functools.partial(_dsa_attn_body, topk=topk, seq=seq),
        out_shape=jax.ShapeDtypeStruct((t, DSA_H * HD), BF16),
        grid=(batch, nq),
        in_specs=[
            pl.BlockSpec((seq, LANES), lambda b, i: (b, 0)),
            colq(DSA_H * HD), colq(IDX_H * IDX_D),
            pl.BlockSpec((VAUG, seq), lambda b, i: (0, b)),
            colq(IDX_H),
        ],
        out_specs=pl.BlockSpec((tq, DSA_H * HD), lambda b, i: (b * nq + i, 0)),
        scratch_shapes=[pltpu.VMEM((seq, tq), I32), pltpu.VMEM((seq, tq), I16), pltpu.VMEM((seq, tq), I16),
                        pltpu.VMEM((seq, tq), F32),
                        pltpu.VMEM((2, DSA_H, DSA_TK, tq), F32), pltpu.VMEM((2, DSA_H * SUBLANES, tq), F32),
                        pltpu.VMEM((2 * SUBLANES, tq), F32), pltpu.VMEM((DSA_H * VAUG, tq), F32)],
        compiler_params=_cparams("parallel", "arbitrary"),
        name="dsa_attn",
    )(kk, qT, iqT, vT, iwT)


NSA_KV = NSA_G * HD
NSA_GATES = NSA_H * 3
NSA_GATES_PAD = -NSA_GATES % SUBLANES


def _nsa_proj_body(x_ref, g_ref, wtok_ref, wfeat_ref, cos_ref, sin_ref, qn_ref, kns_ref, knw_ref, mqn_ref,
                   kc_ref, vc_ref, ks_ref, kw_ref, qT_ref, vsT_ref, vwT_ref, gT_ref, qmT_ref):
    h = _rms(x_ref[...], g_ref[...]).astype(BF16)
    tok = _dot(h, wtok_ref[...])
    kc_ref[...] = tok[:, :NSA_KV]
    vc_ref[...] = tok[:, NSA_KV:]
    feat = _dot_nt(wfeat_ref[...], h)
    tm = feat.shape[-1]
    cos, sin = cos_ref[...], sin_ref[...]
    o = 0
    q = feat[o:o + NSA_H * HD]; o += NSA_H * HD
    ks = feat[o:o + NSA_KV]; o += NSA_KV
    kw = feat[o:o + NSA_KV]; o += NSA_KV
    vs = feat[o:o + NSA_KV]; o += NSA_KV
    vw = feat[o:o + NSA_KV]; o += NSA_KV
    qm = feat[o:o + MEM_W]; o += MEM_W
    gates = feat[o:]
    qT_ref[...] = (_heads_norm_rope(q, _tile_lanes(qn_ref[...], tm), cos, sin) * QSCALE).astype(BF16)
    ks_ref[...] = _heads_norm_rope(ks, _tile_lanes(kns_ref[...], tm), cos, sin).T.astype(BF16)
    kw_ref[...] = _heads_norm_rope(kw, _tile_lanes(knw_ref[...], tm), cos, sin).T.astype(BF16)
    vsT_ref[...] = _augment_values(vs).astype(BF16)
    vwT_ref[...] = _augment_values(vw).astype(BF16)
    gT_ref[...] = jax.nn.sigmoid(gates)
    qmT_ref[...] = (_heads_norm_rope(qm, _tile_lanes(mqn_ref[...], tm), None, None) * QSCALE).astype(BF16)


def _nsa_proj(x, g, w_in, qn, kn, mem_qn, seq):
    t, d = x.shape
    tm = min(PROJ_TM, seq)
    npos = seq // tm
    wq, wkc, wvc, wks, wvs, wkw, wvw, wgt, wqm = jnp.split(w_in, np.cumsum(NSA_SIZES)[:-1].tolist(), axis=1)
    wtok = jnp.concatenate([wkc, wvc], axis=1).astype(BF16)
    wgt = jnp.pad(wgt, ((0, 0), (0, NSA_GATES_PAD)))
    wfeat = jnp.concatenate([wq, wks, wkw, wvs, wvw, wqm, wgt], axis=1).T.astype(BF16)
    cosF, sinF = _rope_tables(jnp.arange(seq))
    row = lambda n: pl.BlockSpec((tm, n), lambda i: (i, 0))
    col = lambda n: pl.BlockSpec((n, tm), lambda i: (0, i))
    tab = pl.BlockSpec((HALF, tm), lambda i: (0, i % npos))
    gain = _const_spec((HD, LANES))
    ngt = NSA_GATES + NSA_GATES_PAD
    return pl.pallas_call(
        _nsa_proj_body,
        out_shape=(
            jax.ShapeDtypeStruct((t, NSA_KV), F32), jax.ShapeDtypeStruct((t, NSA_KV), F32),
            jax.ShapeDtypeStruct((t, NSA_KV), BF16), jax.ShapeDtypeStruct((t, NSA_KV), BF16),
            jax.ShapeDtypeStruct((NSA_H * HD, t), BF16), jax.ShapeDtypeStruct((NSA_G * VAUG, t), BF16),
            jax.ShapeDtypeStruct((NSA_G * VAUG, t), BF16), jax.ShapeDtypeStruct((ngt, t), F32),
            jax.ShapeDtypeStruct((MEM_W, t), BF16),
        ),
        grid=(t // tm,),
        in_specs=[row(d), _const_spec((1, d)), _const_spec(wtok.shape), _const_spec(wfeat.shape),
                  tab, tab, gain, gain, gain, gain],
        out_specs=(row(NSA_KV), row(NSA_KV), row(NSA_KV), row(NSA_KV), col(NSA_H * HD), col(NSA_G * VAUG),
                   col(NSA_G * VAUG), col(ngt), col(MEM_W)),
        compiler_params=_cparams("parallel"),
        name="nsa_proj",
    )(x, g.reshape(1, d), wtok, wfeat, cosF, sinF, _lane_gain(qn), _lane_gain(kn[1]), _lane_gain(kn[2]),
      _lane_gain(mem_qn))


def _nsa_cmp_body(xk_ref, xv_ref, pk_ref, pv_ref, wk_ref, wv_ref, kn_ref, cos_ref, sin_ref, k_ref, vT_ref):
    def compress(x, pos, w_ref):
        n = x.shape[0]
        xa = (x + pos[0:1]).astype(BF16)
        xb = (pltpu.roll(x, n - 1, axis=0) + pos[1:2]).astype(BF16)
        return _dot_nt(w_ref[0], xa) + _dot_nt(w_ref[1], xb)

    kT = compress(xk_ref[0], pk_ref[...], wk_ref)
    kT = _heads_norm_rope(kT, _tile_lanes(kn_ref[...], kT.shape[-1]), cos_ref[...], sin_ref[...])
    k_ref[0] = kT.T.astype(BF16)
    vT_ref[0] = compress(xv_ref[0], pv_ref[...], wv_ref).astype(BF16)


def _nsa_cmp_weights(w, pos):
    eye = jnp.eye(NSA_G, dtype=F32)
    halves = w.reshape(CMP_L // CMP_S, CMP_S, HD, HD)
    wt = jnp.einsum('hg,alde->ahelgd', eye, halves).reshape(CMP_L // CMP_S, NSA_KV, CMP_S * NSA_KV)
    p = jnp.broadcast_to(pos.reshape(CMP_L // CMP_S, CMP_S, 1, HD), (CMP_L // CMP_S, CMP_S, NSA_G, HD))
    return wt.astype(BF16), p.reshape(CMP_L // CMP_S, CMP_S * NSA_KV).astype(F32)


def _nsa_cmp(kc, vc, pos_k, pos_v, wk, wv, kn0, batch, seq):
    n = seq // CMP_S
    width = CMP_S * NSA_KV
    xk = kc.reshape(batch, n, width)
    xv = vc.reshape(batch, n, width)
    wkt, pk = _nsa_cmp_weights(wk, pos_k)
    wvt, pv = _nsa_cmp_weights(wv, pos_v)
    cosE, sinE = _rope_tables(jnp.arange(n) * CMP_S + (CMP_L - 1))
    xspec = pl.BlockSpec((1, n, width), lambda b: (b, 0, 0))
    return pl.pallas_call(
        _nsa_cmp_body,
        out_shape=(jax.ShapeDtypeStruct((batch, n, NSA_KV), BF16), jax.ShapeDtypeStruct((batch, NSA_KV, n), BF16)),
        grid=(batch,),
        in_specs=[xspec, xspec, _const_spec(pk.shape), _const_spec(pv.shape), _const_spec(wkt.shape),
                  _const_spec(wvt.shape), _const_spec((HD, LANES)), _const_spec((HALF, n)), _const_spec((HALF, n))],
        out_specs=(pl.BlockSpec((1, n, NSA_KV), lambda b: (b, 0, 0)), pl.BlockSpec((1, NSA_KV, n), lambda b: (b, 0, 0))),
        compiler_params=_cparams("parallel"),
        name="nsa_cmp",
    )(xk, xv, pk, pv, wkt, wvt, _lane_gain(kn0), cosE, sinE)


NSA_TQ = 256
NSA_TK = 256


def _staged_attention(nheads, lo, hi, tk, logits, values, s_ref, mc_ref, m_ref, acc_ref):
    tq = s_ref.shape[-1]
    fold = lambda a: a.reshape(tk // SUBLANES, SUBLANES, tq)
    m_ref[...] = jnp.full(m_ref.shape, -jnp.inf, F32)
    acc_ref[...] = jnp.zeros(acc_ref.shape, F32)

    start = lambda c: c * tk if isinstance(c, int) else pl.multiple_of(c * tk, tk)

    def stage(h, c, slot):
        s = logits(h, start(c))
        s_ref[slot, h] = s
        mc_ref[slot, h * SUBLANES:(h + 1) * SUBLANES, :] = jnp.max(fold(s), axis=0)

    def consume(h, c, slot):
        g8, hrows = slice(h * SUBLANES, (h + 1) * SUBLANES), slice(h * VAUG, (h + 1) * VAUG)
        m_old = m_ref[h:h + 1, :]
        m_new = jnp.maximum(m_old, jnp.max(mc_ref[slot, g8, :], axis=0, keepdims=True))
        alpha = jnp.exp2(m_old - m_new)
        p = jnp.exp2(s_ref[slot, h] - m_new)
        acc_ref[hrows, :] = acc_ref[hrows, :] * alpha + _dot(values(h, start(c)), p.astype(BF16))
        m_ref[h:h + 1, :] = m_new

    def consume_and_stage_next(c, slot):
        for h in range(nheads):
            consume(h, c, slot)
            stage(h, c + 1, 1 - slot)

    def finish(c, slot):
        for h in range(nheads):
            consume(h, c, slot)

    for h in range(nheads):
        stage(h, lo, 0)
    n_fused = hi - 1 - lo

    def two_chunks(j, carry):
        c = lo + 2 * j
        consume_and_stage_next(c, 0)
        consume_and_stage_next(c + 1, 1)
        return carry

    if isinstance(n_fused, int):
        for i in range(n_fused):
            consume_and_stage_next(lo + i, i % 2)
        finish(hi - 1, n_fused % 2)
    else:
        lax.fori_loop(0, n_fused // 2, two_chunks, 0)

        @pl.when(n_fused % 2 == 1)
        def _():
            consume_and_stage_next(hi - 2, 0)
            finish(hi - 1, 1)

        @pl.when(n_fused % 2 == 0)
        def _():
            finish(hi - 1, 0)

    return [acc_ref[h * VAUG:h * VAUG + HD, :] / acc_ref[h * VAUG + HD:h * VAUG + HD + 1, :] for h in range(nheads)]


def _nsa_attn_body(qT_ref, gT_ref, ks_ref, vsT_ref, kw_ref, vwT_ref, kc_ref, vcT_ref, cov_ref, y_ref,
                   bias_ref, out_ref, s_ref, mc_ref, m_ref, acc_ref, *, seq):
    tq, tk = NSA_TQ, NSA_TK
    qs = pl.program_id(1) * tq
    nkc = (qs + tq) // tk
    wlo = jnp.maximum((qs - WIN) // tk, 0)
    qpos = qs + lax.broadcasted_iota(I32, (1, tq), 1)
    ncr = seq // CMP_S
    nblk = seq // SLC_L
    n_sel = min(SLC_N_MAX, nblk)
    valid_c = CMP_S * lax.broadcasted_iota(I32, (ncr, tq), 0) + (CMP_L - 1) <= qpos
    any_c = jnp.where(qpos >= CMP_L - 1, 1.0, 0.0)
    jrow = lax.broadcasted_iota(I32, (nblk, tq), 0)
    qblk = qpos // SLC_L
    forced = (jrow == 0) | (jrow == qblk) | (jrow == qblk - 1)
    causal_blk = jrow * SLC_L <= qpos
    krow = lax.broadcasted_iota(I32, (tk, tq), 0)
    brow = lax.broadcasted_iota(I32, (SLC_L, tq), 0)
    group = lambda h: h // NSA_R
    pair = lambda h: slice((group(h) // 2) * LANES, (group(h) // 2 + 1) * LANES)
    grows = lambda h: slice(group(h) * HD, (group(h) + 1) * HD)
    vrows = lambda h: slice(group(h) * VAUG, (group(h) + 1) * VAUG)
    q_of = lambda h: _pad_rows(qT_ref[h * HD:(h + 1) * HD, :], group(h) % 2)
    gate = lambda h, branch: gT_ref[h * 3 + branch:h * 3 + branch + 1, :]

    for g in range(NSA_G):
        heads = [g * NSA_R + r for r in range(NSA_R)]
        kcm = kc_ref[0][:, pair(heads[0])]
        vcm = vcT_ref[0][grows(heads[0]), :]
        psum = jnp.zeros((ncr, tq), F32)
        for h in heads:
            s = jnp.where(valid_c, _dot(kcm, q_of(h)), NEG)
            p = jnp.exp2(s - jnp.max(s, axis=0, keepdims=True))
            p = p / jnp.sum(p, axis=0, keepdims=True) * any_c
            out_ref[h * HD:(h + 1) * HD, :] = gate(h, 0) * _dot(vcm, p.astype(BF16))
            psum = psum + p
        p_hi = psum.astype(BF16)
        p_lo = (psum - p_hi.astype(F32)).astype(BF16)
        imp = _dot(cov_ref[...], p_hi) + _dot(cov_ref[...], p_lo)
        imp = jnp.where(causal_blk, imp + jnp.where(forced, FORCE, 0.0), NEG)
        rank = jnp.zeros((nblk, tq), I32)
        for j2 in range(nblk):
            rj = imp[j2:j2 + 1]
            beats = (rj > imp) | ((rj == imp) & (j2 < jrow))
            rank = rank + beats.astype(I32)
        selb = jnp.where(rank < n_sel, 0.0, NEG)
        for j in range(nblk):
            bias_ref[g, j * SLC_L:(j + 1) * SLC_L, :] = jnp.where(j * SLC_L + brow <= qpos, selb[j:j + 1], NEG)

    stats = (s_ref, mc_ref, m_ref, acc_ref)
    slc = _staged_attention(
        NSA_H, 0, nkc, tk,
        lambda h, r0: _dot(ks_ref[pl.ds(r0, tk), pair(h)], q_of(h)) + bias_ref[group(h), pl.ds(r0, tk), :],
        lambda h, r0: vsT_ref[vrows(h), pl.ds(r0, tk)], *stats)
    for h in range(NSA_H):
        out_ref[h * HD:(h + 1) * HD, :] = out_ref[h * HD:(h + 1) * HD, :] + gate(h, 1) * slc[h]

    def win_mask_chunk(c, carry):
        r0 = pl.multiple_of(c * tk, tk)
        dist = qpos - (r0 + krow)
        bias_ref[0, pl.ds(r0, tk), :] = jnp.where((dist >= 0) & (dist < WIN), 0.0, NEG)
        return carry

    lax.fori_loop(wlo, nkc, win_mask_chunk, 0)
    win = _staged_attention(
        NSA_H, wlo, nkc, tk,
        lambda h, r0: _dot(kw_ref[pl.ds(r0, tk), pair(h)], q_of(h)) + bias_ref[0, pl.ds(r0, tk), :],
        lambda h, r0: vwT_ref[vrows(h), pl.ds(r0, tk)], *stats)
    outs = [out_ref[h * HD:(h + 1) * HD, :] + gate(h, 2) * win[h] for h in range(NSA_H)]
    y_ref[...] = jnp.concatenate(outs, axis=0).T.astype(BF16)


def _nsa_attn(qT, gT, ks, vsT, kw, vwT, kcmp, vcmpT, batch, seq):
    t = ks.shape[0]
    tq = NSA_TQ
    nq = seq // tq
    ncr, nblk = seq // CMP_S, seq // SLC_L
    starts = np.arange(ncr) * CMP_S
    sstart = np.arange(nblk) * SLC_L
    cover = (starts[None, :] < sstart[:, None] + SLC_L) & (starts[None, :] + CMP_L > sstart[:, None])
    cover[:, ncr - 1] = False
    colq = lambda n: pl.BlockSpec((n, tq), lambda b, i: (0, b * nq + i))
    tok = pl.BlockSpec((seq, NSA_KV), lambda b, i: (b, 0))
    feat = pl.BlockSpec((NSA_G * VAUG, seq), lambda b, i: (0, b))
    return pl.pallas_call(
        functools.partial(_nsa_attn_body, seq=seq),
        out_shape=jax.ShapeDtypeStruct((t, NSA_H * HD), BF16),
        grid=(batch, nq),
        in_specs=[
            colq(NSA_H * HD), colq(gT.shape[0]), tok, feat, tok, feat,
            pl.BlockSpec((1, ncr, NSA_KV), lambda b, i: (b, 0, 0)),
            pl.BlockSpec((1, NSA_KV, ncr), lambda b, i: (b, 0, 0)),
            _const_spec((nblk, ncr)),
        ],
        out_specs=pl.BlockSpec((tq, NSA_H * HD), lambda b, i: (b * nq + i, 0)),
        scratch_shapes=[pltpu.VMEM((NSA_G, seq, tq), F32), pltpu.VMEM((NSA_H * HD, tq), F32),
                        pltpu.VMEM((2, NSA_H, NSA_TK, tq), F32), pltpu.VMEM((2, NSA_H * SUBLANES, tq), F32),
                        pltpu.VMEM((2 * SUBLANES, tq), F32), pltpu.VMEM((NSA_H * VAUG, tq), F32)],
        compiler_params=_cparams("parallel", "arbitrary"),
        name="nsa_attn",
    )(qT, gT, ks, vsT, kw, vwT, kcmp, vcmpT, jnp.asarray(cover, BF16))


def _nsa_layer_mix(x, g, w_in, qn, kn, pos_k, pos_v, wk, wv, mem_qn, batch, seq):
    kc, vc, ks, kw, qT, vsT, vwT, gT, qmT = _nsa_proj(x, g, w_in, qn, kn, mem_qn, seq)
    kcmp, vcmpT = _nsa_cmp(kc, vc, pos_k, pos_v, wk, wv, kn[0], batch, seq)
    return _nsa_attn(qT, gT, ks, vsT, kw, vwT, kcmp, vcmpT, batch, seq), qmT


def kernel(x, mem, ffn_norm, ffn_w_gate, ffn_w_up, ffn_w_down, mix_norm, w_out, mem_norm, mem_w_kv, mem_qn, mem_kn, ret_w_in, dsa_w_in, dsa_qn, dsa_kn, nsa_w_in, nsa_qn, nsa_kn, nsa_cmp_pos_k, nsa_cmp_pos_v, nsa_cmp_wk, nsa_cmp_wv):
    batch, seq, d = x.shape
    x = x.reshape(batch * seq, d)
    ffn_w = (ffn_w_gate.astype(BF16), ffn_w_up.astype(BF16), ffn_w_down.astype(BF16))
    for i in range(ffn_norm.shape[0]):
        x = _ffn(x, ffn_norm[i, 0], *ffn_w, i, 0)
        kind, j = i % N_MIXERS, i // N_MIXERS
        if kind == 0:
            q, v, gate, kT, qmT = _ret_proj(x, mix_norm[i], ret_w_in[j], mem_qn[i], seq)
            y_mix = _retention(q, kT, v, gate, batch, seq)
        elif kind == 1:
            kk, qT, iqT, vT, iwT, qmT = _dsa_proj(x, mix_norm[i], dsa_w_in[j], dsa_qn[j], dsa_kn[j], mem_qn[i], seq)
            y_mix = _dsa_attn(kk, qT, iqT, vT, iwT, batch, seq)
        else:
            y_mix, qmT = _nsa_layer_mix(x, mix_norm[i], nsa_w_in[j], nsa_qn[j], nsa_kn[j], nsa_cmp_pos_k[j],
                                        nsa_cmp_pos_v[j], nsa_cmp_wk[j], nsa_cmp_wv[j], mem_qn[i], batch, seq)
        mem_k, mem_vT = _mem_kv(mem, mem_norm[i], mem_w_kv[i], mem_kn[i])
        y_mem = _mem_attn(qmT, mem_k, mem_vT, seq)
        x = _mix_out_ffn(y_mix, y_mem, x, w_out[i], ffn_norm[i, 1], *ffn_w, i, 1)
    return x.reshape(batch, seq, d)
```

```python
import functools
import math

import jax
import jax.numpy as jnp
import numpy as np
from jax import lax
from jax.experimental import pallas as pl
from jax.experimental.pallas import tpu as pltpu

D_MODEL = 1024
HD = 64
HALF = HD // 2
H_MIX = 12
H_MEM = 4
MIX_W = H_MIX * HD
MEM_W = H_MEM * HD
D_FF = 2816
ROPE_THETA = 10000.0
EPS = 1e-6
NEG = -1e30
FORCE = 1e9
SCALE = HD ** -0.5
QSCALE = SCALE * math.log2(math.e)

RET_H, RET_DK, RET_DV, RET_CHUNK = 6, 64, 128, 128
DSA_H, IDX_H, IDX_D, DSA_TOPK_MAX = 12, 8, 64, 256
NSA_H, NSA_G, CMP_L, CMP_S, SLC_L, SLC_N_MAX, WIN = 12, 4, 32, 16, 64, 16, 512
NSA_R = NSA_H // NSA_G
N_MIXERS = 3

RET_SIZES = [RET_H * RET_DK, RET_H * RET_DK, RET_H * RET_DV, RET_H * RET_DV, MEM_W]
DSA_SIZES = [DSA_H * HD, HD, HD, IDX_H * IDX_D, IDX_D, IDX_H, MEM_W]
NSA_SIZES = [NSA_H * HD] + [NSA_G * HD] * 6 + [NSA_H * 3, MEM_W]

LANES = 128
SUBLANES = 8
PACKED_SUBLANES = 2 * SUBLANES
VAUG = HD + PACKED_SUBLANES
MXU_N = 256
VMEM_LIMIT_BYTES = 56 * 1024 * 1024

BF16 = jnp.bfloat16
F32 = jnp.float32
I32 = jnp.int32
I16 = jnp.int16


def _cparams(*sem):
    return pltpu.CompilerParams(dimension_semantics=sem, vmem_limit_bytes=VMEM_LIMIT_BYTES)


def _const_spec(shape):
    n = len(shape)
    return pl.BlockSpec(shape, lambda *_: (0,) * n, pipeline_mode=pl.Buffered(1))


def _rms(x, g):
    return x * lax.rsqrt(jnp.mean(x * x, axis=-1, keepdims=True) + EPS) * g


def _dot(a, b):
    return jnp.dot(a, b, preferred_element_type=F32)


def _dot_nt(a, b):
    return lax.dot_general(a, b, (((1,), (1,)), ((), ())), preferred_element_type=F32)


def _tile_lanes(a, n):
    reps = n // a.shape[-1]
    return a if reps == 1 else jnp.concatenate([a] * reps, axis=-1)


def _rope_tables(pos):
    inv = ROPE_THETA ** (-jnp.arange(HALF, dtype=F32) / HALF)
    ang = pos.astype(F32)[:, None] * inv[None, :]
    return jnp.cos(ang).T, jnp.sin(ang).T


def _lane_gain(g):
    return jnp.broadcast_to(g.astype(F32)[:, None], (g.shape[0], LANES))


def _augment_values(v):
    t = v.shape[-1]
    v3 = v.reshape(v.shape[0] // HD, HD, t)
    tail = jnp.where(lax.broadcasted_iota(I32, (v3.shape[0], VAUG - HD, t), 1) == 0, 1.0, 0.0).astype(v.dtype)
    return jnp.concatenate([v3, tail], axis=1).reshape(v3.shape[0] * VAUG, t)


def _heads_norm_rope(x, gain, cos, sin):
    t = x.shape[-1]
    x3 = x.reshape(x.shape[0] // HD, HD, t)
    if gain is not None:
        ms = jnp.sum(x3 * x3, axis=1, keepdims=True) * (1.0 / HD)
        x3 = x3 * lax.rsqrt(ms + EPS) * gain[None]
    if cos is not None:
        x1, x2 = x3[:, :HALF], x3[:, HALF:]
        x3 = jnp.concatenate([x1 * cos[None] - x2 * sin[None], x2 * cos[None] + x1 * sin[None]], axis=1)
    return x3.reshape(x.shape)


FFN_TM = 1024
FFN_TF = MXU_N


def _swiglu_half_step(x, g_ref, wg_ref, wu_ref, wd_ref):
    h = _rms(x, g_ref[...]).astype(BF16)
    acc = jnp.zeros(x.shape, F32)
    for c in range(D_FF // FFN_TF):
        sl = slice(c * FFN_TF, (c + 1) * FFN_TF)
        gate = _dot(h, wg_ref[:, sl])
        up = _dot(h, wu_ref[:, sl])
        act = (gate * jax.nn.sigmoid(gate) * up).astype(BF16)
        acc = acc + _dot(act, wd_ref[sl, :])
    return x + 0.5 * acc


def _ffn_body(x_ref, g_ref, wg_ref, wu_ref, wd_ref, o_ref):
    o_ref[...] = _swiglu_half_step(x_ref[...], g_ref, wg_ref, wu_ref, wd_ref)


def _mix_out_ffn_body(ymix_ref, ymem_ref, x_ref, wmix_ref, wmem_ref, g_ref, wg_ref, wu_ref, wd_ref, o_ref):
    x = x_ref[...] + _dot(ymix_ref[...], wmix_ref[...]) + _dot(ymem_ref[...], wmem_ref[...])
    o_ref[...] = _swiglu_half_step(x, g_ref, wg_ref, wu_ref, wd_ref)


def _ffn_specs(d, layer, half):
    pick = lambda r, c: pl.BlockSpec((None, None, r, c), lambda i: (layer, half, 0, 0), pipeline_mode=pl.Buffered(1))
    return [_const_spec((1, d)), pick(d, D_FF), pick(d, D_FF), pick(D_FF, d)]


def _ffn(x, g, wg, wu, wd, layer, half):
    t, d = x.shape
    tm = min(FFN_TM, t)
    return pl.pallas_call(
        _ffn_body,
        out_shape=jax.ShapeDtypeStruct((t, d), F32),
        grid=(t // tm,),
        in_specs=[pl.BlockSpec((tm, d), lambda i: (i, 0))] + _ffn_specs(d, layer, half),
        out_specs=pl.BlockSpec((tm, d), lambda i: (i, 0)),
        compiler_params=_cparams("parallel"),
        name="ffn",
    )(x, g.reshape(1, d), wg, wu, wd)


def _mix_out_ffn(y_mix, y_mem, x, w_out, g, wg, wu, wd, layer, half):
    t, d = x.shape
    tm = min(FFN_TM, t)
    return pl.pallas_call(
        _mix_out_ffn_body,
        out_shape=jax.ShapeDtypeStruct((t, d), F32),
        grid=(t // tm,),
        in_specs=[
            pl.BlockSpec((tm, MIX_W), lambda i: (i, 0)),
            pl.BlockSpec((tm, MEM_W), lambda i: (i, 0)),
            pl.BlockSpec((tm, d), lambda i: (i, 0)),
            _const_spec((MIX_W, d)),
            _const_spec((MEM_W, d)),
        ] + _ffn_specs(d, layer, half),
        out_specs=pl.BlockSpec((tm, d), lambda i: (i, 0)),
        compiler_params=_cparams("parallel"),
        name="mix_out_ffn",
    )(y_mix, y_mem, x, w_out[:MIX_W].astype(BF16), w_out[MIX_W:].astype(BF16), g.reshape(1, d), wg, wu, wd)


PROJ_TM = 1024


def _mem_kv_body(m_ref, g_ref, wk_ref, wvT_ref, kn_ref, k_ref, vT_ref):
    h = _rms(m_ref[0], g_ref[...]).astype(BF16)
    kT = _dot_nt(wk_ref[...], h)
    kT = _heads_norm_rope(kT, _tile_lanes(kn_ref[...], kT.shape[-1]), None, None)
    k_ref[0] = kT.T.astype(BF16)
    vT_ref[0] = _augment_values(_dot_nt(wvT_ref[...], h)).astype(BF16)


def _mem_kv(mem, g, w_kv, kn):
    b, m, d = mem.shape
    return pl.pallas_call(
        _mem_kv_body,
        out_shape=(jax.ShapeDtypeStruct((b, m, MEM_W), BF16), jax.ShapeDtypeStruct((b, H_MEM * VAUG, m), BF16)),
        grid=(b,),
        in_specs=[
            pl.BlockSpec((1, m, d), lambda i: (i, 0, 0)),
            _const_spec((1, d)),
            _const_spec((MEM_W, d)),
            _const_spec((MEM_W, d)),
            _const_spec((HD, LANES)),
        ],
        out_specs=(pl.BlockSpec((1, m, MEM_W), lambda i: (i, 0, 0)),
                   pl.BlockSpec((1, H_MEM * VAUG, m), lambda i: (i, 0, 0))),
        compiler_params=_cparams("parallel"),
        name="mem_kv",
    )(mem, g.reshape(1, d), w_kv[:, :MEM_W].T.astype(BF16), w_kv[:, MEM_W:].T.astype(BF16), _lane_gain(kn))


MEM_TQ = 512


def _pad_rows(blk, half, total=LANES):
    z = jnp.zeros_like(blk)
    parts = [z] * (total // HD)
    parts[half] = blk
    return jnp.concatenate(parts, axis=0)


def _mem_attn_body(qT_ref, k_ref, vT_ref, o_ref, s_ref, mc_ref, m_ref, acc_ref):
    n_mem = k_ref.shape[1]
    outs = _staged_attention(
        H_MEM, 0, 1, n_mem,
        lambda h, r0: _dot(k_ref[0, :, (h // 2) * LANES:(h // 2 + 1) * LANES],
                           _pad_rows(qT_ref[h * HD:(h + 1) * HD, :], h % 2)),
        lambda h, r0: vT_ref[0, h * VAUG:(h + 1) * VAUG, :], s_ref, mc_ref, m_ref, acc_ref)
    o_ref[...] = jnp.concatenate(outs, axis=0).T.astype(BF16)


def _mem_attn(qmT, k, vT, seq):
    t = qmT.shape[1]
    b, m, _ = k.shape
    tq = min(MEM_TQ, seq)
    nq = seq // tq
    return pl.pallas_call(
        _mem_attn_body,
        out_shape=jax.ShapeDtypeStruct((t, MEM_W), BF16),
        grid=(b, nq),
        in_specs=[
            pl.BlockSpec((MEM_W, tq), lambda bi, i: (0, bi * nq + i)),
            pl.BlockSpec((1, m, MEM_W), lambda bi, i: (bi, 0, 0)),
            pl.BlockSpec((1, H_MEM * VAUG, m), lambda bi, i: (bi, 0, 0)),
        ],
        out_specs=pl.BlockSpec((tq, MEM_W), lambda bi, i: (bi * nq + i, 0)),
        scratch_shapes=[pltpu.VMEM((2, H_MEM, m, tq), F32), pltpu.VMEM((2, H_MEM * SUBLANES, tq), F32),
                        pltpu.VMEM((SUBLANES, tq), F32), pltpu.VMEM((H_MEM * VAUG, tq), F32)],
        compiler_params=_cparams("parallel", "parallel"),
        name="mem_attn",
    )(qmT, k, vT)


def _ret_proj_body(x_ref, g_ref, wtok_ref, wfeat_ref, cosT_ref, sinT_ref, cosF_ref, sinF_ref, mqn_ref,
                   q_ref, v_ref, gate_ref, kT_ref, qmT_ref):
    h = _rms(x_ref[...], g_ref[...]).astype(BF16)
    tok = _dot(h, wtok_ref[...])
    nq, nv = RET_H * RET_DK, RET_H * RET_DV
    q = tok[:, :nq]
    lane = lax.broadcasted_iota(I32, q.shape, 1)
    rot = jnp.where(lane % HD < HALF, pltpu.roll(q, nq - HALF, axis=1), pltpu.roll(q, HALF, axis=1))
    q_ref[...] = (q * _tile_lanes(cosT_ref[...], nq) + rot * _tile_lanes(sinT_ref[...], nq)).astype(BF16)
    v_ref[...] = tok[:, nq:nq + nv].astype(BF16)
    gate_ref[...] = tok[:, nq + nv:]
    feat = _dot_nt(wfeat_ref[...], h)
    tm = feat.shape[-1]
    cos, sin = cosF_ref[...], sinF_ref[...]
    kT = _heads_norm_rope(feat[:nq], None, cos, sin) * (RET_DK ** -0.5)
    kT_ref[...] = kT.astype(BF16)
    qm = _heads_norm_rope(feat[nq:], _tile_lanes(mqn_ref[...], tm), None, None) * QSCALE
    qmT_ref[...] = qm.astype(BF16)


def _token_rope_tables(seq):
    cosF, sinF = _rope_tables(jnp.arange(seq))
    cos = jnp.tile(cosF.T, (1, LANES // HALF))
    sign = jnp.where((jnp.arange(LANES) % HD) < HALF, -1.0, 1.0).astype(F32)
    sin = jnp.tile(sinF.T, (1, LANES // HALF)) * sign[None, :]
    return cos, sin


def _ret_proj(x, g, w_in, mem_qn, seq):
    t, d = x.shape
    tm = min(PROJ_TM, seq)
    npos = seq // tm
    wq, wk, wv, wg, wqm = jnp.split(w_in, np.cumsum(RET_SIZES)[:-1].tolist(), axis=1)
    wtok = jnp.concatenate([wq, wv, wg], axis=1).astype(BF16)
    wfeat = jnp.concatenate([wk, wqm], axis=1).T.astype(BF16)
    cosT, sinT = _token_rope_tables(seq)
    cosF, sinF = _rope_tables(jnp.arange(seq))
    nq, nv = RET_H * RET_DK, RET_H * RET_DV
    row = lambda n: pl.BlockSpec((tm, n), lambda i: (i, 0))
    col = lambda n: pl.BlockSpec((n, tm), lambda i: (0, i))
    return pl.pallas_call(
        _ret_proj_body,
        out_shape=(
            jax.ShapeDtypeStruct((t, nq), BF16), jax.ShapeDtypeStruct((t, nv), BF16),
            jax.ShapeDtypeStruct((t, nv), F32), jax.ShapeDtypeStruct((nq, t), BF16),
            jax.ShapeDtypeStruct((MEM_W, t), BF16),
        ),
        grid=(t // tm,),
        in_specs=[
            row(d), _const_spec((1, d)), _const_spec(wtok.shape), _const_spec(wfeat.shape),
            pl.BlockSpec((tm, LANES), lambda i: (i % npos, 0)), pl.BlockSpec((tm, LANES), lambda i: (i % npos, 0)),
            pl.BlockSpec((HALF, tm), lambda i: (0, i % npos)), pl.BlockSpec((HALF, tm), lambda i: (0, i % npos)),
            _const_spec((HD, LANES)),
        ],
        out_specs=(row(nq), row(nv), row(nv), col(nq), col(MEM_W)),
        compiler_params=_cparams("parallel"),
        name="ret_proj",
    )(x, g.reshape(1, d), wtok, wfeat, cosT, sinT, cosF, sinF, _lane_gain(mem_qn))


RET_TL = 512


def _ret_body(q_ref, kT_ref, v_ref, gate_ref, y_ref, state_ref):
    @pl.when(pl.program_id(1) == 0)
    def _():
        state_ref[...] = jnp.zeros_like(state_ref)

    c = RET_CHUNK
    ii = lax.broadcasted_iota(I32, (c, c), 0).astype(F32)
    jj = lax.broadcasted_iota(I32, (c, c), 1).astype(F32)
    diff = ii - jj
    jk = lax.broadcasted_iota(I32, (RET_DK, c), 1).astype(F32)
    log_gs = [math.log(1.0 - 2.0 ** (-5.0 - hh)) for hh in range(RET_H)]
    dmasks = [jnp.where(diff >= 0, jnp.exp(lg * jnp.maximum(diff, 0.0)), 0.0) for lg in log_gs]
    xis = [jnp.exp(lg * (ii + 1.0)) for lg in log_gs]
    zetas = [jnp.exp(lg * (c - 1.0 - jk)) for lg in log_gs]
    for n in range(q_ref.shape[0] // c):
        rows = slice(n * c, (n + 1) * c)
        for hh in range(RET_H):
            dmask, xi, zeta, decay = dmasks[hh], xis[hh], zetas[hh], math.exp(log_gs[hh] * c)
            pair = slice((hh // 2) * LANES, (hh // 2 + 1) * LANES)
            q2 = q_ref[rows, pair]
            kT = kT_ref[hh * RET_DK:(hh + 1) * RET_DK, rows]
            v = v_ref[rows, hh * RET_DV:(hh + 1) * RET_DV]
            state = state_ref[hh]
            inner = _dot(q2, _pad_rows(kT, hh % 2)) * dmask
            o = _dot(inner.astype(BF16), v)
            o = o + _dot(q2, _pad_rows(state.astype(BF16), hh % 2)) * xi
            kv = _dot((kT.astype(F32) * zeta).astype(BF16), v)
            state_ref[hh] = state * decay + kv
            mu = jnp.mean(o, axis=-1, keepdims=True)
            var = jnp.mean(jnp.square(o - mu), axis=-1, keepdims=True)
            o = (o - mu) * lax.rsqrt(var + EPS)
            gte = gate_ref[rows, hh * RET_DV:(hh + 1) * RET_DV]
            y_ref[rows, hh * RET_DV:(hh + 1) * RET_DV] = (gte * jax.nn.sigmoid(gte) * o).astype(BF16)


def _retention(q, kT, v, gate, batch, seq):
    t = q.shape[0]
    tl = min(RET_TL, seq)
    nl = seq // tl
    nq, nv = RET_H * RET_DK, RET_H * RET_DV
    return pl.pallas_call(
        _ret_body,
        out_shape=jax.ShapeDtypeStruct((t, nv), BF16),
        grid=(batch, nl),
        in_specs=[
            pl.BlockSpec((tl, nq), lambda b, i: (b * nl + i, 0)),
            pl.BlockSpec((nq, tl), lambda b, i: (0, b * nl + i)),
            pl.BlockSpec((tl, nv), lambda b, i: (b * nl + i, 0)),
            pl.BlockSpec((tl, nv), lambda b, i: (b * nl + i, 0)),
        ],
        out_specs=pl.BlockSpec((tl, nv), lambda b, i: (b * nl + i, 0)),
        scratch_shapes=[pltpu.VMEM((RET_H, RET_DK, RET_DV), F32)],
        compiler_params=_cparams("parallel", "arbitrary"),
        name="retention",
    )(q, kT, v, gate)


def _dsa_proj_body(x_ref, g_ref, w_ref, cos_ref, sin_ref, qn_ref, kn_ref, mqn_ref,
                   kk_ref, qT_ref, iqT_ref, vT_ref, iwT_ref, qmT_ref):
    h = _rms(x_ref[...], g_ref[...]).astype(BF16)
    feat = _dot_nt(w_ref[...], h)
    tm = feat.shape[-1]
    cos, sin = cos_ref[...], sin_ref[...]
    o = 0
    q = feat[o:o + DSA_H * HD]; o += DSA_H * HD
    iq = feat[o:o + IDX_H * IDX_D]; o += IDX_H * IDX_D
    k = feat[o:o + HD]; o += HD
    ik = feat[o:o + IDX_D]; o += IDX_D
    v = feat[o:o + HD]; o += HD
    qm = feat[o:o + MEM_W]; o += MEM_W
    iw = feat[o:o + IDX_H]
    qT_ref[...] = (_heads_norm_rope(q, _tile_lanes(qn_ref[...], tm), cos, sin) * QSCALE).astype(BF16)
    iqT_ref[...] = _heads_norm_rope(iq, None, cos, sin).astype(BF16)
    k = _heads_norm_rope(k, _tile_lanes(kn_ref[...], tm), cos, sin)
    ik = _heads_norm_rope(ik, None, cos, sin)
    kk_ref[...] = jnp.concatenate([k, ik], axis=0).T.astype(BF16)
    vT_ref[...] = _augment_values(v).astype(BF16)
    iwT_ref[...] = iw
    qmT_ref[...] = (_heads_norm_rope(qm, _tile_lanes(mqn_ref[...], tm), None, None) * QSCALE).astype(BF16)


def _dsa_proj(x, g, w_in, qn, kn, mem_qn, seq):
    t, d = x.shape
    tm = min(PROJ_TM, seq)
    npos = seq // tm
    wq, wk, wv, wiq, wik, wiw, wqm = jnp.split(w_in, np.cumsum(DSA_SIZES)[:-1].tolist(), axis=1)
    wfeat = jnp.concatenate([wq, wiq, wk, wik, wv, wqm, wiw], axis=1).T.astype(BF16)
    cosF, sinF = _rope_tables(jnp.arange(seq))
    col = lambda n: pl.BlockSpec((n, tm), lambda i: (0, i))
    tab = pl.BlockSpec((HALF, tm), lambda i: (0, i % npos))
    gain = _const_spec((HD, LANES))
    return pl.pallas_call(
        _dsa_proj_body,
        out_shape=(
            jax.ShapeDtypeStruct((t, LANES), BF16), jax.ShapeDtypeStruct((DSA_H * HD, t), BF16),
            jax.ShapeDtypeStruct((IDX_H * IDX_D, t), BF16), jax.ShapeDtypeStruct((VAUG, t), BF16),
            jax.ShapeDtypeStruct((IDX_H, t), F32), jax.ShapeDtypeStruct((MEM_W, t), BF16),
        ),
        grid=(t // tm,),
        in_specs=[pl.BlockSpec((tm, d), lambda i: (i, 0)), _const_spec((1, d)), _const_spec(wfeat.shape),
                  tab, tab, gain, gain, gain],
        out_specs=(pl.BlockSpec((tm, LANES), lambda i: (i, 0)), col(DSA_H * HD), col(IDX_H * IDX_D), col(VAUG),
                   col(IDX_H), col(MEM_W)),
        compiler_params=_cparams("parallel"),
        name="dsa_proj",
    )(x, g.reshape(1, d), wfeat, cosF, sinF, _lane_gain(qn), _lane_gain(kn), _lane_gain(mem_qn))


DSA_TQ = 256
DSA_TK = 256
HALF_BITS = 16
INT16_MIN = -2 ** (HALF_BITS - 1)


def _dsa_attn_body(kk_ref, qT_ref, iqT_ref, vT_ref, iwT_ref, y_ref, keys_ref, hi_ref, lo_ref, bias_ref, s_ref, mc_ref,
                   m_ref, acc_ref, *, topk, seq):
    tq, tk = DSA_TQ, DSA_TK
    qs = pl.program_id(1) * tq
    nkc = (qs + tq) // tk
    qpos = qs + lax.broadcasted_iota(I32, (1, tq), 1)
    row = lax.broadcasted_iota(I32, (tk, tq), 0)

    def score_chunk(c, carry):
        r0 = pl.multiple_of(c * tk, tk)
        kkc = kk_ref[pl.ds(r0, tk), :]
        acc = jnp.zeros((tk, tq), F32)
        for h in range(IDX_H):
            r = _dot(kkc, _pad_rows(iqT_ref[h * IDX_D:(h + 1) * IDX_D, :], 1))
            acc = acc + jnp.maximum(r, 0.0) * iwT_ref[h:h + 1, :]
        sc = jnp.where(r0 + row <= qpos, acc + 0.0, -jnp.inf)
        bits = pltpu.bitcast(sc, I32)
        key = jnp.where(bits < 0, bits ^ jnp.int32(0x7FFFFFFF), bits)
        keys_ref[pl.ds(r0, tk), :] = key
        hi_ref[pl.ds(r0, tk), :] = lax.shift_right_arithmetic(key, HALF_BITS).astype(I16)
        return carry

    lax.fori_loop(0, nkc, score_chunk, 0)

    def count(pred):
        def body(c, acc):
            r0 = pl.multiple_of(c * tk, tk)
            m = pred(keys_ref[pl.ds(r0, tk), :], r0 + row)
            return acc + jnp.sum(m.astype(I32).reshape(tk // SUBLANES, SUBLANES, tq), axis=0)
        acc = lax.fori_loop(0, nkc, body, jnp.zeros((SUBLANES, tq), I32))
        return jnp.sum(acc, axis=0, keepdims=True)

    def count16(ref, pred):
        def body(c, acc):
            r0 = pl.multiple_of(c * tk, tk)
            ones = jnp.where(pred(ref[pl.ds(r0, tk), :]), jnp.int16(1), jnp.int16(0))
            parts = [ones[i:i + PACKED_SUBLANES] for i in range(0, tk, PACKED_SUBLANES)]
            while len(parts) > 1:
                parts = [a + b for a, b in zip(parts[::2], parts[1::2])]
            return acc + parts[0]
        acc = lax.fori_loop(0, nkc, body, jnp.zeros((PACKED_SUBLANES, tq), I16))
        return jnp.sum(acc.astype(I32), axis=0, keepdims=True)

    def kth_largest16(ref, k):
        thr = jnp.where(count16(ref, lambda v: v >= 0) >= k, jnp.int32(0), jnp.int32(INT16_MIN))

        def bit_step(b, thr):
            cand = thr | lax.shift_left(jnp.int32(1), (HALF_BITS - 2) - b)
            c16 = cand.astype(I16)
            return jnp.where(count16(ref, lambda v: v >= c16) >= k, cand, thr)

        return lax.fori_loop(0, HALF_BITS - 1, bit_step, thr)

    thr_hi = kth_largest16(hi_ref, topk)
    thr_hi16 = thr_hi.astype(I16)
    need_lo = topk - count16(hi_ref, lambda v: v > thr_hi16)

    def low_half_chunk(c, carry):
        r0 = pl.multiple_of(c * tk, tk)
        lo = ((keys_ref[pl.ds(r0, tk), :] & jnp.int32(0xFFFF)) + jnp.int32(INT16_MIN)).astype(I16)
        lo_ref[pl.ds(r0, tk), :] = jnp.where(hi_ref[pl.ds(r0, tk), :] == thr_hi16, lo, jnp.int16(INT16_MIN))
        return carry

    lax.fori_loop(0, nkc, low_half_chunk, 0)
    thr_lo = kth_largest16(lo_ref, need_lo)
    thr = lax.shift_left(thr_hi, HALF_BITS) | (thr_lo - jnp.int32(INT16_MIN))
    def tie_search():
        need = topk - count(lambda k, _: k > thr)

        def idx_step(b, q):
            cand = q | lax.shift_left(jnp.int32(1), (seq.bit_length() - 2) - b)
            return jnp.where(count(lambda k, idx: (k == thr) & (idx < cand)) < need, cand, q)

        return lax.fori_loop(0, seq.bit_length() - 1, idx_step, jnp.zeros((1, tq), I32))

    has_ties = jnp.max(count(lambda k, _: k >= thr)) > topk
    last = lax.cond(has_ties, tie_search, lambda: jnp.full((1, tq), seq, I32))

    def bias_chunk(c, carry):
        r0 = pl.multiple_of(c * tk, tk)
        k = keys_ref[pl.ds(r0, tk), :]
        idx = r0 + row
        sel = (k > thr) | ((k == thr) & (idx <= last))
        bias_ref[pl.ds(r0, tk), :] = jnp.where(sel & (idx <= qpos), 0.0, NEG)
        return carry

    lax.fori_loop(0, nkc, bias_chunk, 0)

    outs = _staged_attention(
        DSA_H, 0, nkc, tk,
        lambda h, r0: _dot(kk_ref[pl.ds(r0, tk), :], _pad_rows(qT_ref[h * HD:(h + 1) * HD, :], 0))
        + bias_ref[pl.ds(r0, tk), :],
        lambda h, r0: vT_ref[:, pl.ds(r0, tk)], s_ref, mc_ref, m_ref, acc_ref)
    y_ref[...] = jnp.concatenate(outs, axis=0).T.astype(BF16)


def _dsa_attn(kk, qT, iqT, vT, iwT, batch, seq):
    t = kk.shape[0]
    tq = DSA_TQ
    nq = seq // tq
    topk = min(DSA_TOPK_MAX, seq // 4)
    colq = lambda n: pl.BlockSpec((n, tq), lambda b, i: (0, b * nq + i))
    return pl.pallas_call(
        functools.partial(_dsa_attn_body, topk=topk, seq=seq),
        out_shape=jax.ShapeDtypeStruct((t, DSA_H * HD), BF16),
        grid=(batch, nq),
        in_specs=[
            pl.BlockSpec((seq, LANES), lambda b, i: (b, 0)),
            colq(DSA_H * HD), colq(IDX_H * IDX_D),
            pl.BlockSpec((VAUG, seq), lambda b, i: (0, b)),
            colq(IDX_H),
        ],
        out_specs=pl.BlockSpec((tq, DSA_H * HD), lambda b, i: (b * nq + i, 0)),
        scratch_shapes=[pltpu.VMEM((seq, tq), I32), pltpu.VMEM((seq, tq), I16), pltpu.VMEM((seq, tq), I16),
                        pltpu.VMEM((seq, tq), F32),
                        pltpu.VMEM((2, DSA_H, DSA_TK, tq), F32), pltpu.VMEM((2, DSA_H * SUBLANES, tq), F32),
                        pltpu.VMEM((2 * SUBLANES, tq), F32), pltpu.VMEM((DSA_H * VAUG, tq), F32)],
        compiler_params=_cparams("parallel", "arbitrary"),
        name="dsa_attn",
    )(kk, qT, iqT, vT, iwT)


NSA_KV = NSA_G * HD
NSA_GATES = NSA_H * 3
NSA_GATES_PAD = -NSA_GATES % SUBLANES


def _nsa_proj_body(x_ref, g_ref, wtok_ref, wfeat_ref, cos_ref, sin_ref, qn_ref, kns_ref, knw_ref, mqn_ref,
                   kc_ref, vc_ref, ks_ref, kw_ref, qT_ref, vsT_ref, vwT_ref, gT_ref, qmT_ref):
    h = _rms(x_ref[...], g_ref[...]).astype(BF16)
    tok = _dot(h, wtok_ref[...])
    kc_ref[...] = tok[:, :NSA_KV]
    vc_ref[...] = tok[:, NSA_KV:]
    feat = _dot_nt(wfeat_ref[...], h)
    tm = feat.shape[-1]
    cos, sin = cos_ref[...], sin_ref[...]
    o = 0
    q = feat[o:o + NSA_H * HD]; o += NSA_H * HD
    ks = feat[o:o + NSA_KV]; o += NSA_KV
    kw = feat[o:o + NSA_KV]; o += NSA_KV
    vs = feat[o:o + NSA_KV]; o += NSA_KV
    vw = feat[o:o + NSA_KV]; o += NSA_KV
    qm = feat[o:o + MEM_W]; o += MEM_W
    gates = feat[o:]
    qT_ref[...] = (_heads_norm_rope(q, _tile_lanes(qn_ref[...], tm), cos, sin) * QSCALE).astype(BF16)
    ks_ref[...] = _heads_norm_rope(ks, _tile_lanes(kns_ref[...], tm), cos, sin).T.astype(BF16)
    kw_ref[...] = _heads_norm_rope(kw, _tile_lanes(knw_ref[...], tm), cos, sin).T.astype(BF16)
    vsT_ref[...] = _augment_values(vs).astype(BF16)
    vwT_ref[...] = _augment_values(vw).astype(BF16)
    gT_ref[...] = jax.nn.sigmoid(gates)
    qmT_ref[...] = (_heads_norm_rope(qm, _tile_lanes(mqn_ref[...], tm), None, None) * QSCALE).astype(BF16)


def _nsa_proj(x, g, w_in, qn, kn, mem_qn, seq):
    t, d = x.shape
    tm = min(PROJ_TM, seq)
    npos = seq // tm
    wq, wkc, wvc, wks, wvs, wkw, wvw, wgt, wqm = jnp.split(w_in, np.cumsum(NSA_SIZES)[:-1].tolist(), axis=1)
    wtok = jnp.concatenate([wkc, wvc], axis=1).astype(BF16)
    wgt = jnp.pad(wgt, ((0, 0), (0, NSA_GATES_PAD)))
    wfeat = jnp.concatenate([wq, wks, wkw, wvs, wvw, wqm, wgt], axis=1).T.astype(BF16)
    cosF, sinF = _rope_tables(jnp.arange(seq))
    row = lambda n: pl.BlockSpec((tm, n), lambda i: (i, 0))
    col = lambda n: pl.BlockSpec((n, tm), lambda i: (0, i))
    tab = pl.BlockSpec((HALF, tm), lambda i: (0, i % npos))
    gain = _const_spec((HD, LANES))
    ngt = NSA_GATES + NSA_GATES_PAD
    return pl.pallas_call(
        _nsa_proj_body,
        out_shape=(
            jax.ShapeDtypeStruct((t, NSA_KV), F32), jax.ShapeDtypeStruct((t, NSA_KV), F32),
            jax.ShapeDtypeStruct((t, NSA_KV), BF16), jax.ShapeDtypeStruct((t, NSA_KV), BF16),
            jax.ShapeDtypeStruct((NSA_H * HD, t), BF16), jax.ShapeDtypeStruct((NSA_G * VAUG, t), BF16),
            jax.ShapeDtypeStruct((NSA_G * VAUG, t), BF16), jax.ShapeDtypeStruct((ngt, t), F32),
            jax.ShapeDtypeStruct((MEM_W, t), BF16),
        ),
        grid=(t // tm,),
        in_specs=[row(d), _const_spec((1, d)), _const_spec(wtok.shape), _const_spec(wfeat.shape),
                  tab, tab, gain, gain, gain, gain],
        out_specs=(row(NSA_KV), row(NSA_KV), row(NSA_KV), row(NSA_KV), col(NSA_H * HD), col(NSA_G * VAUG),
                   col(NSA_G * VAUG), col(ngt), col(MEM_W)),
        compiler_params=_cparams("parallel"),
        name="nsa_proj",
    )(x, g.reshape(1, d), wtok, wfeat, cosF, sinF, _lane_gain(qn), _lane_gain(kn[1]), _lane_gain(kn[2]),
      _lane_gain(mem_qn))


def _nsa_cmp_body(k01_ref, k23_ref, v01_ref, v23_ref, pk_ref, pv_ref, wk_ref, wv_ref, kn_ref, cos_ref, sin_ref,
                  k_ref, vT_ref):
    n = k_ref.shape[1]

    def compress(lo_ref, hi_ref, pos, w_ref):
        parts = []
        for l in range(CMP_S):
            parts += [lo_ref[pl.ds(l, n, stride=CMP_S), :], hi_ref[pl.ds(l, n, stride=CMP_S), :]]
        x = jnp.concatenate(parts, axis=1)
        xa = (x + pos[0:1]).astype(BF16)
        xb = (pltpu.roll(x, n - 1, axis=0) + pos[1:2]).astype(BF16)
        return _dot_nt(w_ref[0], xa) + _dot_nt(w_ref[1], xb)

    kT = compress(k01_ref, k23_ref, pk_ref[...], wk_ref)
    kT = _heads_norm_rope(kT, _tile_lanes(kn_ref[...], kT.shape[-1]), cos_ref[...], sin_ref[...])
    k_ref[0] = kT.T.astype(BF16)
    vT_ref[0] = compress(v01_ref, v23_ref, pv_ref[...], wv_ref).astype(BF16)


def _nsa_cmp_weights(w, pos):
    halves, width = CMP_L // CMP_S, CMP_S * NSA_KV
    w_eld = w.reshape(halves, CMP_S, HD, HD).transpose(0, 3, 1, 2)
    per_group = jnp.broadcast_to(w_eld[:, :, :, None, :], (halves, HD, CMP_S, NSA_G, HD)).reshape(halves, HD, width)
    rows = np.arange(NSA_KV)[:, None] // HD
    cols = (np.arange(width)[None, :] // HD) % NSA_G
    wt = jnp.where(jnp.asarray(rows == cols)[None], jnp.tile(per_group, (1, NSA_G, 1)), 0.0)
    p = jnp.broadcast_to(pos.reshape(halves, CMP_S, 1, HD), (halves, CMP_S, NSA_G, HD))
    return wt.astype(BF16), p.reshape(halves, width).astype(F32)


def _nsa_cmp(kc, vc, pos_k, pos_v, wk, wv, kn0, batch, seq):
    n = seq // CMP_S
    wkt, pk = _nsa_cmp_weights(wk, pos_k)
    wvt, pv = _nsa_cmp_weights(wv, pos_v)
    cosE, sinE = _rope_tables(jnp.arange(n) * CMP_S + (CMP_L - 1))
    lanes_lo = pl.BlockSpec((seq, LANES), lambda b: (b, 0))
    lanes_hi = pl.BlockSpec((seq, LANES), lambda b: (b, 1))
    return pl.pallas_call(
        _nsa_cmp_body,
        out_shape=(jax.ShapeDtypeStruct((batch, n, NSA_KV), BF16), jax.ShapeDtypeStruct((batch, NSA_KV, n), BF16)),
        grid=(batch,),
        in_specs=[lanes_lo, lanes_hi, lanes_lo, lanes_hi, _const_spec(pk.shape), _const_spec(pv.shape),
                  _const_spec(wkt.shape), _const_spec(wvt.shape), _const_spec((HD, LANES)), _const_spec((HALF, n)),
                  _const_spec((HALF, n))],
        out_specs=(pl.BlockSpec((1, n, NSA_KV), lambda b: (b, 0, 0)), pl.BlockSpec((1, NSA_KV, n), lambda b: (b, 0, 0))),
        compiler_params=_cparams("parallel"),
        name="nsa_cmp",
    )(kc, kc, vc, vc, pk, pv, wkt, wvt, _lane_gain(kn0), cosE, sinE)


NSA_TQ = 256
NSA_TK = 256


def _staged_attention(nheads, lo, hi, tk, logits, values, s_ref, mc_ref, m_ref, acc_ref):
    tq = s_ref.shape[-1]
    fold = lambda a: a.reshape(tk // SUBLANES, SUBLANES, tq)
    m_ref[...] = jnp.full(m_ref.shape, -jnp.inf, F32)
    acc_ref[...] = jnp.zeros(acc_ref.shape, F32)

    start = lambda c: c * tk if isinstance(c, int) else pl.multiple_of(c * tk, tk)

    def stage(h, c, slot):
        s = logits(h, start(c))
        s_ref[slot, h] = s
        mc_ref[slot, h * SUBLANES:(h + 1) * SUBLANES, :] = jnp.max(fold(s), axis=0)

    def consume(h, c, slot):
        g8, hrows = slice(h * SUBLANES, (h + 1) * SUBLANES), slice(h * VAUG, (h + 1) * VAUG)
        m_old = m_ref[h:h + 1, :]
        m_new = jnp.maximum(m_old, jnp.max(mc_ref[slot, g8, :], axis=0, keepdims=True))
        alpha = jnp.exp2(m_old - m_new)
        p = jnp.exp2(s_ref[slot, h] - m_new)
        acc_ref[hrows, :] = acc_ref[hrows, :] * alpha + _dot(values(h, start(c)), p.astype(BF16))
        m_ref[h:h + 1, :] = m_new

    def consume_and_stage_next(c, slot):
        for h in range(nheads):
            consume(h, c, slot)
            stage(h, c + 1, 1 - slot)

    def finish(c, slot):
        for h in range(nheads):
            consume(h, c, slot)

    for h in range(nheads):
        stage(h, lo, 0)
    n_fused = hi - 1 - lo

    def two_chunks(j, carry):
        c = lo + 2 * j
        consume_and_stage_next(c, 0)
        consume_and_stage_next(c + 1, 1)
        return carry

    if isinstance(n_fused, int):
        for i in range(n_fused):
            consume_and_stage_next(lo + i, i % 2)
        finish(hi - 1, n_fused % 2)
    else:
        lax.fori_loop(0, n_fused // 2, two_chunks, 0)

        @pl.when(n_fused % 2 == 1)
        def _():
            consume_and_stage_next(hi - 2, 0)
            finish(hi - 1, 1)

        @pl.when(n_fused % 2 == 0)
        def _():
            finish(hi - 1, 0)

    return [acc_ref[h * VAUG:h * VAUG + HD, :] / acc_ref[h * VAUG + HD:h * VAUG + HD + 1, :] for h in range(nheads)]


def _nsa_attn_body(qT_ref, gT_ref, ks_ref, vsT_ref, kw_ref, vwT_ref, kc_ref, vcT_ref, cov_ref, y_ref,
                   bias_ref, out_ref, s_ref, mc_ref, m_ref, acc_ref, *, seq):
    tq, tk = NSA_TQ, NSA_TK
    qs = pl.program_id(1) * tq
    nkc = (qs + tq) // tk
    wlo = jnp.maximum((qs - WIN) // tk, 0)
    qpos = qs + lax.broadcasted_iota(I32, (1, tq), 1)
    ncr = seq // CMP_S
    nblk = seq // SLC_L
    n_sel = min(SLC_N_MAX, nblk)
    valid_c = CMP_S * lax.broadcasted_iota(I32, (ncr, tq), 0) + (CMP_L - 1) <= qpos
    any_c = jnp.where(qpos >= CMP_L - 1, 1.0, 0.0)
    jrow = lax.broadcasted_iota(I32, (nblk, tq), 0)
    qblk = qpos // SLC_L
    forced = (jrow == 0) | (jrow == qblk) | (jrow == qblk - 1)
    causal_blk = jrow * SLC_L <= qpos
    krow = lax.broadcasted_iota(I32, (tk, tq), 0)
    brow = lax.broadcasted_iota(I32, (SLC_L, tq), 0)
    group = lambda h: h // NSA_R
    pair = lambda h: slice((group(h) // 2) * LANES, (group(h) // 2 + 1) * LANES)
    grows = lambda h: slice(group(h) * HD, (group(h) + 1) * HD)
    vrows = lambda h: slice(group(h) * VAUG, (group(h) + 1) * VAUG)
    q_of = lambda h: _pad_rows(qT_ref[h * HD:(h + 1) * HD, :], group(h) % 2)
    gate = lambda h, branch: gT_ref[h * 3 + branch:h * 3 + branch + 1, :]

    for g in range(NSA_G):
        heads = [g * NSA_R + r for r in range(NSA_R)]
        kcm = kc_ref[0][:, pair(heads[0])]
        vcm = vcT_ref[0][grows(heads[0]), :]
        psum = jnp.zeros((ncr, tq), F32)
        for h in heads:
            s = jnp.where(valid_c, _dot(kcm, q_of(h)), NEG)
            p = jnp.exp2(s - jnp.max(s, axis=0, keepdims=True))
            p = p / jnp.sum(p, axis=0, keepdims=True) * any_c
            out_ref[h * HD:(h + 1) * HD, :] = gate(h, 0) * _dot(vcm, p.astype(BF16))
            psum = psum + p
        p_hi = psum.astype(BF16)
        p_lo = (psum - p_hi.astype(F32)).astype(BF16)
        imp = _dot(cov_ref[...], p_hi) + _dot(cov_ref[...], p_lo)
        imp = jnp.where(causal_blk, imp + jnp.where(forced, FORCE, 0.0), NEG)
        rank = jnp.zeros((nblk, tq), I32)
        for j2 in range(nblk):
            rj = imp[j2:j2 + 1]
            beats = (rj > imp) | ((rj == imp) & (j2 < jrow))
            rank = rank + beats.astype(I32)
        selb = jnp.where(rank < n_sel, 0.0, NEG)
        for j in range(nblk):
            bias_ref[g, j * SLC_L:(j + 1) * SLC_L, :] = jnp.where(j * SLC_L + brow <= qpos, selb[j:j + 1], NEG)

    stats = (s_ref, mc_ref, m_ref, acc_ref)
    slc = _staged_attention(
        NSA_H, 0, nkc, tk,
        lambda h, r0: _dot(ks_ref[pl.ds(r0, tk), pair(h)], q_of(h)) + bias_ref[group(h), pl.ds(r0, tk), :],
        lambda h, r0: vsT_ref[vrows(h), pl.ds(r0, tk)], *stats)
    for h in range(NSA_H):
        out_ref[h * HD:(h + 1) * HD, :] = out_ref[h * HD:(h + 1) * HD, :] + gate(h, 1) * slc[h]

    def win_mask_chunk(c, carry):
        r0 = pl.multiple_of(c * tk, tk)
        dist = qpos - (r0 + krow)
        bias_ref[0, pl.ds(r0, tk), :] = jnp.where((dist >= 0) & (dist < WIN), 0.0, NEG)
        return carry

    lax.fori_loop(wlo, nkc, win_mask_chunk, 0)
    win = _staged_attention(
        NSA_H, wlo, nkc, tk,
        lambda h, r0: _dot(kw_ref[pl.ds(r0, tk), pair(h)], q_of(h)) + bias_ref[0, pl.ds(r0, tk), :],
        lambda h, r0: vwT_ref[vrows(h), pl.ds(r0, tk)], *stats)
    outs = [out_ref[h * HD:(h + 1) * HD, :] + gate(h, 2) * win[h] for h in range(NSA_H)]
    y_ref[...] = jnp.concatenate(outs, axis=0).T.astype(BF16)


def _nsa_attn(qT, gT, ks, vsT, kw, vwT, kcmp, vcmpT, batch, seq):
    t = ks.shape[0]
    tq = NSA_TQ
    nq = seq // tq
    ncr, nblk = seq // CMP_S, seq // SLC_L
    starts = np.arange(ncr) * CMP_S
    sstart = np.arange(nblk) * SLC_L
    cover = (starts[None, :] < sstart[:, None] + SLC_L) & (starts[None, :] + CMP_L > sstart[:, None])
    cover[:, ncr - 1] = False
    colq = lambda n: pl.BlockSpec((n, tq), lambda b, i: (0, b * nq + i))
    tok = pl.BlockSpec((seq, NSA_KV), lambda b, i: (b, 0))
    feat = pl.BlockSpec((NSA_G * VAUG, seq), lambda b, i: (0, b))
    return pl.pallas_call(
        functools.partial(_nsa_attn_body, seq=seq),
        out_shape=jax.ShapeDtypeStruct((t, NSA_H * HD), BF16),
        grid=(batch, nq),
        in_specs=[
            colq(NSA_H * HD), colq(gT.shape[0]), tok, feat, tok, feat,
            pl.BlockSpec((1, ncr, NSA_KV), lambda b, i: (b, 0, 0)),
            pl.BlockSpec((1, NSA_KV, ncr), lambda b, i: (b, 0, 0)),
            _const_spec((nblk, ncr)),
        ],
        out_specs=pl.BlockSpec((tq, NSA_H * HD), lambda b, i: (b * nq + i, 0)),
        scratch_shapes=[pltpu.VMEM((NSA_G, seq, tq), F32), pltpu.VMEM((NSA_H * HD, tq), F32),
                        pltpu.VMEM((2, NSA_H, NSA_TK, tq), F32), pltpu.VMEM((2, NSA_H * SUBLANES, tq), F32),
                        pltpu.VMEM((2 * SUBLANES, tq), F32), pltpu.VMEM((NSA_H * VAUG, tq), F32)],
        compiler_params=_cparams("parallel", "arbitrary"),
        name="nsa_attn",
    )(qT, gT, ks, vsT, kw, vwT, kcmp, vcmpT, jnp.asarray(cover, BF16))


def _nsa_layer_mix(x, g, w_in, qn, kn, pos_k, pos_v, wk, wv, mem_qn, batch, seq):
    kc, vc, ks, kw, qT, vsT, vwT, gT, qmT = _nsa_proj(x, g, w_in, qn, kn, mem_qn, seq)
    kcmp, vcmpT = _nsa_cmp(kc, vc, pos_k, pos_v, wk, wv, kn[0], batch, seq)
    return _nsa_attn(qT, gT, ks, vsT, kw, vwT, kcmp, vcmpT, batch, seq), qmT


def kernel(x, mem, ffn_norm, ffn_w_gate, ffn_w_up, ffn_w_down, mix_norm, w_out, mem_norm, mem_w_kv, mem_qn, mem_kn, ret_w_in, dsa_w_in, dsa_qn, dsa_kn, nsa_w_in, nsa_qn, nsa_kn, nsa_cmp_pos_k, nsa_cmp_pos_v, nsa_cmp_wk, nsa_cmp_wv):
    batch, seq, d = x.shape
    x = x.reshape(batch * seq, d)
    ffn_w = (ffn_w_gate.astype(BF16), ffn_w_up.astype(BF16), ffn_w_down.astype(BF16))
    for i in range(ffn_norm.shape[0]):
        x = _ffn(x, ffn_norm[i, 0], *ffn_w, i, 0)
        kind, j = i % N_MIXERS, i // N_MIXERS
        if kind == 0:
            q, v, gate, kT, qmT = _ret_proj(x, mix_norm[i], ret_w_in[j], mem_qn[i], seq)
            y_mix = _retention(q, kT, v, gate, batch, seq)
        elif kind == 1:
            kk, qT, iqT, vT, iwT, qmT = _dsa_proj(x, mix_norm[i], dsa_w_in[j], dsa_qn[j], dsa_kn[j], mem_qn[i], seq)
            y_mix = _dsa_attn(kk, qT, iqT, vT, iwT, batch, seq)
        else:
            y_mix, qmT = _nsa_layer_mix(x, mix_norm[i], nsa_w_in[j], nsa_qn[j], nsa_kn[j], nsa_cmp_pos_k[j],
                                        nsa_cmp_pos_v[j], nsa_cmp_wk[j], nsa_cmp_wv[j], mem_qn[i], batch, seq)
        mem_k, mem_vT = _mem_kv(mem, mem_norm[i], mem_w_kv[i], mem_kn[i])
        y_mem = _mem_attn(qmT, mem_k, mem_vT, seq)
        x = _mix_out_ffn(y_mix, y_mem, x, w_out[i], ffn_norm[i, 1], *ffn_w, i, 1)
    return x.reshape(batch, seq, d)
```

```python
import functools
import math

import jax
import jax.numpy as jnp
import numpy as np
from jax import lax
from jax.experimental import pallas as pl
from jax.experimental.pallas import tpu as pltpu

D_MODEL = 1024
HD = 64
HALF = HD // 2
H_MIX = 12
H_MEM = 4
MIX_W = H_MIX * HD
MEM_W = H_MEM * HD
D_FF = 2816
ROPE_THETA = 10000.0
EPS = 1e-6
NEG = -1e30
FORCE = 1e9
SCALE = HD ** -0.5
QSCALE = SCALE * math.log2(math.e)

RET_H, RET_DK, RET_DV, RET_CHUNK = 6, 64, 128, 128
DSA_H, IDX_H, IDX_D, DSA_TOPK_MAX = 12, 8, 64, 256
NSA_H, NSA_G, CMP_L, CMP_S, SLC_L, SLC_N_MAX, WIN = 12, 4, 32, 16, 64, 16, 512
NSA_R = NSA_H // NSA_G
N_MIXERS = 3

RET_SIZES = [RET_H * RET_DK, RET_H * RET_DK, RET_H * RET_DV, RET_H * RET_DV, MEM_W]
DSA_SIZES = [DSA_H * HD, HD, HD, IDX_H * IDX_D, IDX_D, IDX_H, MEM_W]
NSA_SIZES = [NSA_H * HD] + [NSA_G * HD] * 6 + [NSA_H * 3, MEM_W]

LANES = 128
SUBLANES = 8
PACKED_SUBLANES = 2 * SUBLANES
VAUG = HD + PACKED_SUBLANES
MXU_N = 256
VMEM_LIMIT_BYTES = 56 * 1024 * 1024

BF16 = jnp.bfloat16
F32 = jnp.float32
I32 = jnp.int32
I16 = jnp.int16


def _cparams(*sem):
    return pltpu.CompilerParams(dimension_semantics=sem, vmem_limit_bytes=VMEM_LIMIT_BYTES)


def _const_spec(shape):
    n = len(shape)
    return pl.BlockSpec(shape, lambda *_: (0,) * n, pipeline_mode=pl.Buffered(1))


def _rms(x, g):
    return x * lax.rsqrt(jnp.mean(x * x, axis=-1, keepdims=True) + EPS) * g


def _dot(a, b):
    return jnp.dot(a, b, preferred_element_type=F32)


def _dot_nt(a, b):
    return lax.dot_general(a, b, (((1,), (1,)), ((), ())), preferred_element_type=F32)


def _tile_lanes(a, n):
    reps = n // a.shape[-1]
    return a if reps == 1 else jnp.concatenate([a] * reps, axis=-1)


def _rope_tables(pos):
    inv = ROPE_THETA ** (-jnp.arange(HALF, dtype=F32) / HALF)
    ang = pos.astype(F32)[:, None] * inv[None, :]
    return jnp.cos(ang).T, jnp.sin(ang).T


def _lane_gain(g):
    return jnp.broadcast_to(g.astype(F32)[:, None], (g.shape[0], LANES))


def _augment_values(v):
    t = v.shape[-1]
    v3 = v.reshape(v.shape[0] // HD, HD, t)
    tail = jnp.where(lax.broadcasted_iota(I32, (v3.shape[0], VAUG - HD, t), 1) == 0, 1.0, 0.0).astype(v.dtype)
    return jnp.concatenate([v3, tail], axis=1).reshape(v3.shape[0] * VAUG, t)


def _heads_norm_rope(x, gain, cos, sin):
    t = x.shape[-1]
    x3 = x.reshape(x.shape[0] // HD, HD, t)
    if gain is not None:
        ms = jnp.sum(x3 * x3, axis=1, keepdims=True) * (1.0 / HD)
        x3 = x3 * lax.rsqrt(ms + EPS) * gain[None]
    if cos is not None:
        x1, x2 = x3[:, :HALF], x3[:, HALF:]
        x3 = jnp.concatenate([x1 * cos[None] - x2 * sin[None], x2 * cos[None] + x1 * sin[None]], axis=1)
    return x3.reshape(x.shape)


FFN_TM = 1024
FFN_TF = MXU_N


def _swiglu_half_step(x, g_ref, wg_ref, wu_ref, wd_ref):
    h = _rms(x, g_ref[...]).astype(BF16)
    acc = jnp.zeros(x.shape, F32)
    for c in range(D_FF // FFN_TF):
        sl = slice(c * FFN_TF, (c + 1) * FFN_TF)
        gate = _dot(h, wg_ref[:, sl])
        up = _dot(h, wu_ref[:, sl])
        act = (gate * jax.nn.sigmoid(gate) * up).astype(BF16)
        acc = acc + _dot(act, wd_ref[sl, :])
    return x + 0.5 * acc


def _ffn_body(x_ref, g_ref, wg_ref, wu_ref, wd_ref, o_ref):
    o_ref[...] = _swiglu_half_step(x_ref[...], g_ref, wg_ref, wu_ref, wd_ref)


def _mix_out_ffn_body(ymix_ref, ymem_ref, x_ref, wmix_ref, wmem_ref, g_ref, wg_ref, wu_ref, wd_ref, o_ref):
    x = x_ref[...] + _dot(ymix_ref[...], wmix_ref[...]) + _dot(ymem_ref[...], wmem_ref[...])
    o_ref[...] = _swiglu_half_step(x, g_ref, wg_ref, wu_ref, wd_ref)


def _ffn_specs(d, layer, half):
    pick = lambda r, c: pl.BlockSpec((None, None, r, c), lambda i: (layer, half, 0, 0), pipeline_mode=pl.Buffered(1))
    return [_const_spec((1, d)), pick(d, D_FF), pick(d, D_FF), pick(D_FF, d)]


def _ffn(x, g, wg, wu, wd, layer, half):
    t, d = x.shape
    tm = min(FFN_TM, t)
    return pl.pallas_call(
        _ffn_body,
        out_shape=jax.ShapeDtypeStruct((t, d), F32),
        grid=(t // tm,),
        in_specs=[pl.BlockSpec((tm, d), lambda i: (i, 0))] + _ffn_specs(d, layer, half),
        out_specs=pl.BlockSpec((tm, d), lambda i: (i, 0)),
        compiler_params=_cparams("parallel"),
        name="ffn",
    )(x, g.reshape(1, d), wg, wu, wd)


def _mix_out_ffn(y_mix, y_mem, x, w_out, g, wg, wu, wd, layer, half):
    t, d = x.shape
    tm = min(FFN_TM, t)
    return pl.pallas_call(
        _mix_out_ffn_body,
        out_shape=jax.ShapeDtypeStruct((t, d), F32),
        grid=(t // tm,),
        in_specs=[
            pl.BlockSpec((tm, MIX_W), lambda i: (i, 0)),
            pl.BlockSpec((tm, MEM_W), lambda i: (i, 0)),
            pl.BlockSpec((tm, d), lambda i: (i, 0)),
            _const_spec((MIX_W, d)),
            _const_spec((MEM_W, d)),
        ] + _ffn_specs(d, layer, half),
        out_specs=pl.BlockSpec((tm, d), lambda i: (i, 0)),
        compiler_params=_cparams("parallel"),
        name="mix_out_ffn",
    )(y_mix, y_mem, x, w_out[:MIX_W].astype(BF16), w_out[MIX_W:].astype(BF16), g.reshape(1, d), wg, wu, wd)


PROJ_TM = 1024


def _mem_kv_body(m_ref, g_ref, wk_ref, wvT_ref, kn_ref, k_ref, vT_ref):
    h = _rms(m_ref[0], g_ref[...]).astype(BF16)
    kT = _dot_nt(wk_ref[...], h)
    kT = _heads_norm_rope(kT, _tile_lanes(kn_ref[...], kT.shape[-1]), None, None)
    k_ref[0] = kT.T.astype(BF16)
    vT_ref[0] = _augment_values(_dot_nt(wvT_ref[...], h)).astype(BF16)


def _mem_kv(mem, g, w_kv, kn):
    b, m, d = mem.shape
    return pl.pallas_call(
        _mem_kv_body,
        out_shape=(jax.ShapeDtypeStruct((b, m, MEM_W), BF16), jax.ShapeDtypeStruct((b, H_MEM * VAUG, m), BF16)),
        grid=(b,),
        in_specs=[
            pl.BlockSpec((1, m, d), lambda i: (i, 0, 0)),
            _const_spec((1, d)),
            _const_spec((MEM_W, d)),
            _const_spec((MEM_W, d)),
            _const_spec((HD, LANES)),
        ],
        out_specs=(pl.BlockSpec((1, m, MEM_W), lambda i: (i, 0, 0)),
                   pl.BlockSpec((1, H_MEM * VAUG, m), lambda i: (i, 0, 0))),
        compiler_params=_cparams("parallel"),
        name="mem_kv",
    )(mem, g.reshape(1, d), w_kv[:, :MEM_W].T.astype(BF16), w_kv[:, MEM_W:].T.astype(BF16), _lane_gain(kn))


MEM_TQ = 1024


def _pad_rows(blk, half, total=LANES):
    z = jnp.zeros_like(blk)
    parts = [z] * (total // HD)
    parts[half] = blk
    return jnp.concatenate(parts, axis=0)


def _mem_attn_body(qT_ref, k_ref, vT_ref, o_ref, s_ref, mc_ref, m_ref, acc_ref):
    n_mem = k_ref.shape[1]
    outs = _staged_attention(
        H_MEM, 0, 1, n_mem,
        lambda h, r0: _dot(k_ref[0, :, (h // 2) * LANES:(h // 2 + 1) * LANES],
                           _pad_rows(qT_ref[h * HD:(h + 1) * HD, :], h % 2)),
        lambda h, r0: vT_ref[0, h * VAUG:(h + 1) * VAUG, :], s_ref, mc_ref, m_ref, acc_ref)
    o_ref[...] = jnp.concatenate(outs, axis=0).T.astype(BF16)


def _mem_attn(qmT, k, vT, seq):
    t = qmT.shape[1]
    b, m, _ = k.shape
    tq = min(MEM_TQ, seq)
    nq = seq // tq
    return pl.pallas_call(
        _mem_attn_body,
        out_shape=jax.ShapeDtypeStruct((t, MEM_W), BF16),
        grid=(b, nq),
        in_specs=[
            pl.BlockSpec((MEM_W, tq), lambda bi, i: (0, bi * nq + i)),
            pl.BlockSpec((1, m, MEM_W), lambda bi, i: (bi, 0, 0)),
            pl.BlockSpec((1, H_MEM * VAUG, m), lambda bi, i: (bi, 0, 0)),
        ],
        out_specs=pl.BlockSpec((tq, MEM_W), lambda bi, i: (bi * nq + i, 0)),
        scratch_shapes=[pltpu.VMEM((2, H_MEM, m, tq), F32), pltpu.VMEM((2, H_MEM * SUBLANES, tq), F32),
                        pltpu.VMEM((SUBLANES, tq), F32), pltpu.VMEM((H_MEM * VAUG, tq), F32)],
        compiler_params=_cparams("parallel", "parallel"),
        name="mem_attn",
    )(qmT, k, vT)


def _ret_proj_body(x_ref, g_ref, wtok_ref, wfeat_ref, cosT_ref, sinT_ref, cosF_ref, sinF_ref, mqn_ref,
                   q_ref, v_ref, gate_ref, kT_ref, qmT_ref):
    h = _rms(x_ref[...], g_ref[...]).astype(BF16)
    tok = _dot(h, wtok_ref[...])
    nq, nv = RET_H * RET_DK, RET_H * RET_DV
    q = tok[:, :nq]
    lane = lax.broadcasted_iota(I32, q.shape, 1)
    rot = jnp.where(lane % HD < HALF, pltpu.roll(q, nq - HALF, axis=1), pltpu.roll(q, HALF, axis=1))
    q_ref[...] = (q * _tile_lanes(cosT_ref[...], nq) + rot * _tile_lanes(sinT_ref[...], nq)).astype(BF16)
    v_ref[...] = tok[:, nq:nq + nv].astype(BF16)
    gate_ref[...] = tok[:, nq + nv:]
    feat = _dot_nt(wfeat_ref[...], h)
    tm = feat.shape[-1]
    cos, sin = cosF_ref[...], sinF_ref[...]
    kT = _heads_norm_rope(feat[:nq], None, cos, sin) * (RET_DK ** -0.5)
    kT_ref[...] = kT.astype(BF16)
    qm = _heads_norm_rope(feat[nq:], _tile_lanes(mqn_ref[...], tm), None, None) * QSCALE
    qmT_ref[...] = qm.astype(BF16)


def _token_rope_tables(seq):
    cosF, sinF = _rope_tables(jnp.arange(seq))
    cos = jnp.tile(cosF.T, (1, LANES // HALF))
    sign = jnp.where((jnp.arange(LANES) % HD) < HALF, -1.0, 1.0).astype(F32)
    sin = jnp.tile(sinF.T, (1, LANES // HALF)) * sign[None, :]
    return cos, sin


def _ret_proj(x, g, w_in, mem_qn, seq):
    t, d = x.shape
    tm = min(PROJ_TM, seq)
    npos = seq // tm
    wq, wk, wv, wg, wqm = jnp.split(w_in, np.cumsum(RET_SIZES)[:-1].tolist(), axis=1)
    wtok = jnp.concatenate([wq, wv, wg], axis=1).astype(BF16)
    wfeat = jnp.concatenate([wk, wqm], axis=1).T.astype(BF16)
    cosT, sinT = _token_rope_tables(seq)
    cosF, sinF = _rope_tables(jnp.arange(seq))
    nq, nv = RET_H * RET_DK, RET_H * RET_DV
    row = lambda n: pl.BlockSpec((tm, n), lambda i: (i, 0))
    col = lambda n: pl.BlockSpec((n, tm), lambda i: (0, i))
    return pl.pallas_call(
        _ret_proj_body,
        out_shape=(
            jax.ShapeDtypeStruct((t, nq), BF16), jax.ShapeDtypeStruct((t, nv), BF16),
            jax.ShapeDtypeStruct((t, nv), F32), jax.ShapeDtypeStruct((nq, t), BF16),
            jax.ShapeDtypeStruct((MEM_W, t), BF16),
        ),
        grid=(t // tm,),
        in_specs=[
            row(d), _const_spec((1, d)), _const_spec(wtok.shape), _const_spec(wfeat.shape),
            pl.BlockSpec((tm, LANES), lambda i: (i % npos, 0)), pl.BlockSpec((tm, LANES), lambda i: (i % npos, 0)),
            pl.BlockSpec((HALF, tm), lambda i: (0, i % npos)), pl.BlockSpec((HALF, tm), lambda i: (0, i % npos)),
            _const_spec((HD, LANES)),
        ],
        out_specs=(row(nq), row(nv), row(nv), col(nq), col(MEM_W)),
        compiler_params=_cparams("parallel"),
        name="ret_proj",
    )(x, g.reshape(1, d), wtok, wfeat, cosT, sinT, cosF, sinF, _lane_gain(mem_qn))


RET_TL = 1024


def _ret_body(q_ref, kT_ref, v_ref, gate_ref, y_ref, state_ref):
    @pl.when(pl.program_id(1) == 0)
    def _():
        state_ref[...] = jnp.zeros_like(state_ref)

    c = RET_CHUNK
    ii = lax.broadcasted_iota(I32, (c, c), 0).astype(F32)
    jj = lax.broadcasted_iota(I32, (c, c), 1).astype(F32)
    diff = ii - jj
    jk = lax.broadcasted_iota(I32, (RET_DK, c), 1).astype(F32)
    log_gs = [math.log(1.0 - 2.0 ** (-5.0 - hh)) for hh in range(RET_H)]
    dmasks = [jnp.where(diff >= 0, jnp.exp(lg * jnp.maximum(diff, 0.0)), 0.0) for lg in log_gs]
    xis = [jnp.exp(lg * (ii + 1.0)) for lg in log_gs]
    zetas = [jnp.exp(lg * (c - 1.0 - jk)) for lg in log_gs]
    for n in range(q_ref.shape[0] // c):
        rows = slice(n * c, (n + 1) * c)
        for hh in range(RET_H):
            dmask, xi, zeta, decay = dmasks[hh], xis[hh], zetas[hh], math.exp(log_gs[hh] * c)
            pair = slice((hh // 2) * LANES, (hh // 2 + 1) * LANES)
            q2 = q_ref[rows, pair]
            kT = kT_ref[hh * RET_DK:(hh + 1) * RET_DK, rows]
            v = v_ref[rows, hh * RET_DV:(hh + 1) * RET_DV]
            state = state_ref[hh]
            inner = _dot(q2, _pad_rows(kT, hh % 2)) * dmask
            o = _dot(inner.astype(BF16), v)
            o = o + _dot(q2, _pad_rows(state.astype(BF16), hh % 2)) * xi
            kv = _dot((kT.astype(F32) * zeta).astype(BF16), v)
            state_ref[hh] = state * decay + kv
            mu = jnp.mean(o, axis=-1, keepdims=True)
            var = jnp.mean(jnp.square(o - mu), axis=-1, keepdims=True)
            o = (o - mu) * lax.rsqrt(var + EPS)
            gte = gate_ref[rows, hh * RET_DV:(hh + 1) * RET_DV]
            y_ref[rows, hh * RET_DV:(hh + 1) * RET_DV] = (gte * jax.nn.sigmoid(gte) * o).astype(BF16)


def _retention(q, kT, v, gate, batch, seq):
    t = q.shape[0]
    tl = min(RET_TL, seq)
    nl = seq // tl
    nq, nv = RET_H * RET_DK, RET_H * RET_DV
    return pl.pallas_call(
        _ret_body,
        out_shape=jax.ShapeDtypeStruct((t, nv), BF16),
        grid=(batch, nl),
        in_specs=[
            pl.BlockSpec((tl, nq), lambda b, i: (b * nl + i, 0)),
            pl.BlockSpec((nq, tl), lambda b, i: (0, b * nl + i)),
            pl.BlockSpec((tl, nv), lambda b, i: (b * nl + i, 0)),
            pl.BlockSpec((tl, nv), lambda b, i: (b * nl + i, 0)),
        ],
        out_specs=pl.BlockSpec((tl, nv), lambda b, i: (b * nl + i, 0)),
        scratch_shapes=[pltpu.VMEM((RET_H, RET_DK, RET_DV), F32)],
        compiler_params=_cparams("parallel", "arbitrary"),
        name="retention",
    )(q, kT, v, gate)


def _dsa_proj_body(x_ref, g_ref, w_ref, cos_ref, sin_ref, qn_ref, kn_ref, mqn_ref,
                   kk_ref, qT_ref, iqT_ref, vT_ref, iwT_ref, qmT_ref):
    h = _rms(x_ref[...], g_ref[...]).astype(BF16)
    feat = _dot_nt(w_ref[...], h)
    tm = feat.shape[-1]
    cos, sin = cos_ref[...], sin_ref[...]
    o = 0
    q = feat[o:o + DSA_H * HD]; o += DSA_H * HD
    iq = feat[o:o + IDX_H * IDX_D]; o += IDX_H * IDX_D
    k = feat[o:o + HD]; o += HD
    ik = feat[o:o + IDX_D]; o += IDX_D
    v = feat[o:o + HD]; o += HD
    qm = feat[o:o + MEM_W]; o += MEM_W
    iw = feat[o:o + IDX_H]
    qT_ref[...] = (_heads_norm_rope(q, _tile_lanes(qn_ref[...], tm), cos, sin) * QSCALE).astype(BF16)
    iqT_ref[...] = _heads_norm_rope(iq, None, cos, sin).astype(BF16)
    k = _heads_norm_rope(k, _tile_lanes(kn_ref[...], tm), cos, sin)
    ik = _heads_norm_rope(ik, None, cos, sin)
    kk_ref[...] = jnp.concatenate([k, ik], axis=0).T.astype(BF16)
    vT_ref[...] = _augment_values(v).astype(BF16)
    iwT_ref[...] = iw
    qmT_ref[...] = (_heads_norm_rope(qm, _tile_lanes(mqn_ref[...], tm), None, None) * QSCALE).astype(BF16)


def _dsa_proj(x, g, w_in, qn, kn, mem_qn, seq):
    t, d = x.shape
    tm = min(PROJ_TM, seq)
    npos = seq // tm
    wq, wk, wv, wiq, wik, wiw, wqm = jnp.split(w_in, np.cumsum(DSA_SIZES)[:-1].tolist(), axis=1)
    wfeat = jnp.concatenate([wq, wiq, wk, wik, wv, wqm, wiw], axis=1).T.astype(BF16)
    cosF, sinF = _rope_tables(jnp.arange(seq))
    col = lambda n: pl.BlockSpec((n, tm), lambda i: (0, i))
    tab = pl.BlockSpec((HALF, tm), lambda i: (0, i % npos))
    gain = _const_spec((HD, LANES))
    return pl.pallas_call(
        _dsa_proj_body,
        out_shape=(
            jax.ShapeDtypeStruct((t, LANES), BF16), jax.ShapeDtypeStruct((DSA_H * HD, t), BF16),
            jax.ShapeDtypeStruct((IDX_H * IDX_D, t), BF16), jax.ShapeDtypeStruct((VAUG, t), BF16),
            jax.ShapeDtypeStruct((IDX_H, t), F32), jax.ShapeDtypeStruct((MEM_W, t), BF16),
        ),
        grid=(t // tm,),
        in_specs=[pl.BlockSpec((tm, d), lambda i: (i, 0)), _const_spec((1, d)), _const_spec(wfeat.shape),
                  tab, tab, gain, gain, gain],
        out_specs=(pl.BlockSpec((tm, LANES), lambda i: (i, 0)), col(DSA_H * HD), col(IDX_H * IDX_D), col(VAUG),
                   col(IDX_H), col(MEM_W)),
        compiler_params=_cparams("parallel"),
        name="dsa_proj",
    )(x, g.reshape(1, d), wfeat, cosF, sinF, _lane_gain(qn), _lane_gain(kn), _lane_gain(mem_qn))


DSA_TQ = 256
DSA_TK = 256
HALF_BITS = 16
INT16_MIN = -2 ** (HALF_BITS - 1)


def _dsa_attn_body(kk_ref, qT_ref, iqT_ref, vT_ref, iwT_ref, y_ref, keys_ref, hi_ref, lo_ref, bias_ref, s_ref, mc_ref,
                   m_ref, acc_ref, *, topk, seq):
    tq, tk = DSA_TQ, DSA_TK
    qs = pl.program_id(1) * tq
    nkc = (qs + tq) // tk
    qpos = qs + lax.broadcasted_iota(I32, (1, tq), 1)
    row = lax.broadcasted_iota(I32, (tk, tq), 0)

    def score_chunk(c, carry):
        r0 = pl.multiple_of(c * tk, tk)
        kkc = kk_ref[pl.ds(r0, tk), :]
        acc = jnp.zeros((tk, tq), F32)
        for h in range(IDX_H):
            r = _dot(kkc, _pad_rows(iqT_ref[h * IDX_D:(h + 1) * IDX_D, :], 1))
            acc = acc + jnp.maximum(r, 0.0) * iwT_ref[h:h + 1, :]
        sc = jnp.where(r0 + row <= qpos, acc + 0.0, -jnp.inf)
        bits = pltpu.bitcast(sc, I32)
        key = jnp.where(bits < 0, bits ^ jnp.int32(0x7FFFFFFF), bits)
        keys_ref[pl.ds(r0, tk), :] = key
        hi_ref[pl.ds(r0, tk), :] = lax.shift_right_arithmetic(key, HALF_BITS).astype(I16)
        return carry

    lax.fori_loop(0, nkc, score_chunk, 0)

    def count(pred):
        def body(c, acc):
            r0 = pl.multiple_of(c * tk, tk)
            m = pred(keys_ref[pl.ds(r0, tk), :], r0 + row)
            return acc + jnp.sum(m.astype(I32).reshape(tk // SUBLANES, SUBLANES, tq), axis=0)
        acc = lax.fori_loop(0, nkc, body, jnp.zeros((SUBLANES, tq), I32))
        return jnp.sum(acc, axis=0, keepdims=True)

    def count16(ref, pred):
        def body(c, acc):
            r0 = pl.multiple_of(c * tk, tk)
            ones = jnp.where(pred(ref[pl.ds(r0, tk), :]), jnp.int16(1), jnp.int16(0))
            parts = [ones[i:i + PACKED_SUBLANES] for i in range(0, tk, PACKED_SUBLANES)]
            while len(parts) > 1:
                parts = [a + b for a, b in zip(parts[::2], parts[1::2])]
            return acc + parts[0]
        acc = lax.fori_loop(0, nkc, body, jnp.zeros((PACKED_SUBLANES, tq), I16))
        return jnp.sum(acc.astype(I32), axis=0, keepdims=True)

    def kth_largest16(ref, k):
        thr = jnp.where(count16(ref, lambda v: v >= 0) >= k, jnp.int32(0), jnp.int32(INT16_MIN))

        def bit_step(b, thr):
            cand = thr | lax.shift_left(jnp.int32(1), (HALF_BITS - 2) - b)
            c16 = cand.astype(I16)
            return jnp.where(count16(ref, lambda v: v >= c16) >= k, cand, thr)

        return lax.fori_loop(0, HALF_BITS - 1, bit_step, thr)

    thr_hi = kth_largest16(hi_ref, topk)
    thr_hi16 = thr_hi.astype(I16)
    need_lo = topk - count16(hi_ref, lambda v: v > thr_hi16)

    def low_half_chunk(c, carry):
        r0 = pl.multiple_of(c * tk, tk)
        lo = ((keys_ref[pl.ds(r0, tk), :] & jnp.int32(0xFFFF)) + jnp.int32(INT16_MIN)).astype(I16)
        lo_ref[pl.ds(r0, tk), :] = jnp.where(hi_ref[pl.ds(r0, tk), :] == thr_hi16, lo, jnp.int16(INT16_MIN))
        return carry

    lax.fori_loop(0, nkc, low_half_chunk, 0)
    thr_lo = kth_largest16(lo_ref, need_lo)
    thr = lax.shift_left(thr_hi, HALF_BITS) | (thr_lo - jnp.int32(INT16_MIN))
    def tie_search():
        need = topk - count(lambda k, _: k > thr)

        def idx_step(b, q):
            cand = q | lax.shift_left(jnp.int32(1), (seq.bit_length() - 2) - b)
            return jnp.where(count(lambda k, idx: (k == thr) & (idx < cand)) < need, cand, q)

        return lax.fori_loop(0, seq.bit_length() - 1, idx_step, jnp.zeros((1, tq), I32))

    has_ties = jnp.max(count(lambda k, _: k >= thr)) > topk
    last = lax.cond(has_ties, tie_search, lambda: jnp.full((1, tq), seq, I32))

    def bias_chunk(c, carry):
        r0 = pl.multiple_of(c * tk, tk)
        k = keys_ref[pl.ds(r0, tk), :]
        idx = r0 + row
        sel = (k > thr) | ((k == thr) & (idx <= last))
        bias_ref[pl.ds(r0, tk), :] = jnp.where(sel & (idx <= qpos), 0.0, NEG)
        return carry

    lax.fori_loop(0, nkc, bias_chunk, 0)

    outs = _staged_attention(
        DSA_H, 0, nkc, tk,
        lambda h, r0: _dot(kk_ref[pl.ds(r0, tk), :], _pad_rows(qT_ref[h * HD:(h + 1) * HD, :], 0))
        + bias_ref[pl.ds(r0, tk), :],
        lambda h, r0: vT_ref[:, pl.ds(r0, tk)], s_ref, mc_ref, m_ref, acc_ref)
    y_ref[...] = jnp.concatenate(outs, axis=0).T.astype(BF16)


def _dsa_attn(kk, qT, iqT, vT, iwT, batch, seq):
    t = kk.shape[0]
    tq = DSA_TQ
    nq = seq // tq
    topk = min(DSA_TOPK_MAX, seq // 4)
    colq = lambda n: pl.BlockSpec((n, tq), lambda b, i: (0, b * nq + i))
    return pl.pallas_call(
        functools.partial(_dsa_attn_body, topk=topk, seq=seq),
        out_shape=jax.ShapeDtypeStruct((t, DSA_H * HD), BF16),
        grid=(batch, nq),
        in_specs=[
            pl.BlockSpec((seq, LANES), lambda b, i: (b, 0)),
            colq(DSA_H * HD), colq(IDX_H * IDX_D),
            pl.BlockSpec((VAUG, seq), lambda b, i: (0, b)),
            colq(IDX_H),
        ],
        out_specs=pl.BlockSpec((tq, DSA_H * HD), lambda b, i: (b * nq + i, 0)),
        scratch_shapes=[pltpu.VMEM((seq, tq), I32), pltpu.VMEM((seq, tq), I16), pltpu.VMEM((seq, tq), I16),
                        pltpu.VMEM((seq, tq), F32),
                        pltpu.VMEM((2, DSA_H, DSA_TK, tq), F32), pltpu.VMEM((2, DSA_H * SUBLANES, tq), F32),
                        pltpu.VMEM((2 * SUBLANES, tq), F32), pltpu.VMEM((DSA_H * VAUG, tq), F32)],
        compiler_params=_cparams("parallel", "arbitrary"),
        name="dsa_attn",
    )(kk, qT, iqT, vT, iwT)


NSA_KV = NSA_G * HD
NSA_GATES = NSA_H * 3
NSA_GATES_PAD = -NSA_GATES % SUBLANES


def _nsa_proj_body(x_ref, g_ref, wtok_ref, wfeat_ref, cos_ref, sin_ref, qn_ref, kns_ref, knw_ref, mqn_ref,
                   kc_ref, vc_ref, ks_ref, kw_ref, qT_ref, vsT_ref, vwT_ref, gT_ref, qmT_ref):
    h = _rms(x_ref[...], g_ref[...]).astype(BF16)
    tok = _dot(h, wtok_ref[...])
    kc_ref[...] = tok[:, :NSA_KV]
    vc_ref[...] = tok[:, NSA_KV:]
    feat = _dot_nt(wfeat_ref[...], h)
    tm = feat.shape[-1]
    cos, sin = cos_ref[...], sin_ref[...]
    o = 0
    q = feat[o:o + NSA_H * HD]; o += NSA_H * HD
    ks = feat[o:o + NSA_KV]; o += NSA_KV
    kw = feat[o:o + NSA_KV]; o += NSA_KV
    vs = feat[o:o + NSA_KV]; o += NSA_KV
    vw = feat[o:o + NSA_KV]; o += NSA_KV
    qm = feat[o:o + MEM_W]; o += MEM_W
    gates = feat[o:]
    qT_ref[...] = (_heads_norm_rope(q, _tile_lanes(qn_ref[...], tm), cos, sin) * QSCALE).astype(BF16)
    ks_ref[...] = _heads_norm_rope(ks, _tile_lanes(kns_ref[...], tm), cos, sin).T.astype(BF16)
    kw_ref[...] = _heads_norm_rope(kw, _tile_lanes(knw_ref[...], tm), cos, sin).T.astype(BF16)
    vsT_ref[...] = _augment_values(vs).astype(BF16)
    vwT_ref[...] = _augment_values(vw).astype(BF16)
    gT_ref[...] = jax.nn.sigmoid(gates)
    qmT_ref[...] = (_heads_norm_rope(qm, _tile_lanes(mqn_ref[...], tm), None, None) * QSCALE).astype(BF16)


def _nsa_proj(x, g, w_in, qn, kn, mem_qn, seq):
    t, d = x.shape
    tm = min(PROJ_TM, seq)
    npos = seq // tm
    wq, wkc, wvc, wks, wvs, wkw, wvw, wgt, wqm = jnp.split(w_in, np.cumsum(NSA_SIZES)[:-1].tolist(), axis=1)
    wtok = jnp.concatenate([wkc, wvc], axis=1).astype(BF16)
    wgt = jnp.pad(wgt, ((0, 0), (0, NSA_GATES_PAD)))
    wfeat = jnp.concatenate([wq, wks, wkw, wvs, wvw, wqm, wgt], axis=1).T.astype(BF16)
    cosF, sinF = _rope_tables(jnp.arange(seq))
    row = lambda n: pl.BlockSpec((tm, n), lambda i: (i, 0))
    col = lambda n: pl.BlockSpec((n, tm), lambda i: (0, i))
    tab = pl.BlockSpec((HALF, tm), lambda i: (0, i % npos))
    gain = _const_spec((HD, LANES))
    ngt = NSA_GATES + NSA_GATES_PAD
    return pl.pallas_call(
        _nsa_proj_body,
        out_shape=(
            jax.ShapeDtypeStruct((t, NSA_KV), F32), jax.ShapeDtypeStruct((t, NSA_KV), F32),
            jax.ShapeDtypeStruct((t, NSA_KV), BF16), jax.ShapeDtypeStruct((t, NSA_KV), BF16),
            jax.ShapeDtypeStruct((NSA_H * HD, t), BF16), jax.ShapeDtypeStruct((NSA_G * VAUG, t), BF16),
            jax.ShapeDtypeStruct((NSA_G * VAUG, t), BF16), jax.ShapeDtypeStruct((ngt, t), F32),
            jax.ShapeDtypeStruct((MEM_W, t), BF16),
        ),
        grid=(t // tm,),
        in_specs=[row(d), _const_spec((1, d)), _const_spec(wtok.shape), _const_spec(wfeat.shape),
                  tab, tab, gain, gain, gain, gain],
        out_specs=(row(NSA_KV), row(NSA_KV), row(NSA_KV), row(NSA_KV), col(NSA_H * HD), col(NSA_G * VAUG),
                   col(NSA_G * VAUG), col(ngt), col(MEM_W)),
        compiler_params=_cparams("parallel"),
        name="nsa_proj",
    )(x, g.reshape(1, d), wtok, wfeat, cosF, sinF, _lane_gain(qn), _lane_gain(kn[1]), _lane_gain(kn[2]),
      _lane_gain(mem_qn))


def _nsa_cmp_body(k01_ref, k23_ref, v01_ref, v23_ref, pk_ref, pv_ref, wk_ref, wv_ref, kn_ref, cos_ref, sin_ref,
                  k_ref, vT_ref):
    n = k_ref.shape[1]

    def compress(lo_ref, hi_ref, pos, w_ref):
        parts = []
        for l in range(CMP_S):
            parts += [lo_ref[pl.ds(l, n, stride=CMP_S), :], hi_ref[pl.ds(l, n, stride=CMP_S), :]]
        x = jnp.concatenate(parts, axis=1)
        xa = (x + pos[0:1]).astype(BF16)
        xb = (pltpu.roll(x, n - 1, axis=0) + pos[1:2]).astype(BF16)
        return _dot_nt(w_ref[0], xa) + _dot_nt(w_ref[1], xb)

    kT = compress(k01_ref, k23_ref, pk_ref[...], wk_ref)
    kT = _heads_norm_rope(kT, _tile_lanes(kn_ref[...], kT.shape[-1]), cos_ref[...], sin_ref[...])
    k_ref[0] = kT.T.astype(BF16)
    vT_ref[0] = compress(v01_ref, v23_ref, pv_ref[...], wv_ref).astype(BF16)


def _nsa_cmp_weights(w, pos):
    halves, width = CMP_L // CMP_S, CMP_S * NSA_KV
    w_eld = w.reshape(halves, CMP_S, HD, HD).transpose(0, 3, 1, 2)
    per_group = jnp.broadcast_to(w_eld[:, :, :, None, :], (halves, HD, CMP_S, NSA_G, HD)).reshape(halves, HD, width)
    rows = np.arange(NSA_KV)[:, None] // HD
    cols = (np.arange(width)[None, :] // HD) % NSA_G
    wt = jnp.where(jnp.asarray(rows == cols)[None], jnp.tile(per_group, (1, NSA_G, 1)), 0.0)
    p = jnp.broadcast_to(pos.reshape(halves, CMP_S, 1, HD), (halves, CMP_S, NSA_G, HD))
    return wt.astype(BF16), p.reshape(halves, width).astype(F32)


def _nsa_cmp(kc, vc, pos_k, pos_v, wk, wv, kn0, batch, seq):
    n = seq // CMP_S
    wkt, pk = _nsa_cmp_weights(wk, pos_k)
    wvt, pv = _nsa_cmp_weights(wv, pos_v)
    cosE, sinE = _rope_tables(jnp.arange(n) * CMP_S + (CMP_L - 1))
    lanes_lo = pl.BlockSpec((seq, LANES), lambda b: (b, 0))
    lanes_hi = pl.BlockSpec((seq, LANES), lambda b: (b, 1))
    return pl.pallas_call(
        _nsa_cmp_body,
        out_shape=(jax.ShapeDtypeStruct((batch, n, NSA_KV), BF16), jax.ShapeDtypeStruct((batch, NSA_KV, n), BF16)),
        grid=(batch,),
        in_specs=[lanes_lo, lanes_hi, lanes_lo, lanes_hi, _const_spec(pk.shape), _const_spec(pv.shape),
                  _const_spec(wkt.shape), _const_spec(wvt.shape), _const_spec((HD, LANES)), _const_spec((HALF, n)),
                  _const_spec((HALF, n))],
        out_specs=(pl.BlockSpec((1, n, NSA_KV), lambda b: (b, 0, 0)), pl.BlockSpec((1, NSA_KV, n), lambda b: (b, 0, 0))),
        compiler_params=_cparams("parallel"),
        name="nsa_cmp",
    )(kc, kc, vc, vc, pk, pv, wkt, wvt, _lane_gain(kn0), cosE, sinE)


NSA_TQ = 256
NSA_TK = 256


def _staged_attention(nheads, lo, hi, tk, logits, values, s_ref, mc_ref, m_ref, acc_ref):
    tq = s_ref.shape[-1]
    fold = lambda a: a.reshape(tk // SUBLANES, SUBLANES, tq)
    m_ref[...] = jnp.full(m_ref.shape, -jnp.inf, F32)
    acc_ref[...] = jnp.zeros(acc_ref.shape, F32)

    start = lambda c: c * tk if isinstance(c, int) else pl.multiple_of(c * tk, tk)

    def stage(h, c, slot):
        s = logits(h, start(c))
        s_ref[slot, h] = s
        mc_ref[slot, h * SUBLANES:(h + 1) * SUBLANES, :] = jnp.max(fold(s), axis=0)

    def consume(h, c, slot):
        g8, hrows = slice(h * SUBLANES, (h + 1) * SUBLANES), slice(h * VAUG, (h + 1) * VAUG)
        m_old = m_ref[h:h + 1, :]
        m_new = jnp.maximum(m_old, jnp.max(mc_ref[slot, g8, :], axis=0, keepdims=True))
        alpha = jnp.exp2(m_old - m_new)
        p = jnp.exp2(s_ref[slot, h] - m_new)
        acc_ref[hrows, :] = acc_ref[hrows, :] * alpha + _dot(values(h, start(c)), p.astype(BF16))
        m_ref[h:h + 1, :] = m_new

    def consume_and_stage_next(c, slot):
        for h in range(nheads):
            consume(h, c, slot)
            stage(h, c + 1, 1 - slot)

    def finish(c, slot):
        for h in range(nheads):
            consume(h, c, slot)

    for h in range(nheads):
        stage(h, lo, 0)
    n_fused = hi - 1 - lo

    def two_chunks(j, carry):
        c = lo + 2 * j
        consume_and_stage_next(c, 0)
        consume_and_stage_next(c + 1, 1)
        return carry

    if isinstance(n_fused, int):
        for i in range(n_fused):
            consume_and_stage_next(lo + i, i % 2)
        finish(hi - 1, n_fused % 2)
    else:
        lax.fori_loop(0, n_fused // 2, two_chunks, 0)

        @pl.when(n_fused % 2 == 1)
        def _():
            consume_and_stage_next(hi - 2, 0)
            finish(hi - 1, 1)

        @pl.when(n_fused % 2 == 0)
        def _():
            finish(hi - 1, 0)

    return [acc_ref[h * VAUG:h * VAUG + HD, :] / acc_ref[h * VAUG + HD:h * VAUG + HD + 1, :] for h in range(nheads)]


def _nsa_attn_body(qT_ref, gT_ref, ks_ref, vsT_ref, kw_ref, vwT_ref, kc_ref, vcT_ref, cov_ref, y_ref,
                   bias_ref, out_ref, s_ref, mc_ref, m_ref, acc_ref, *, seq):
    tq, tk = NSA_TQ, NSA_TK
    qs = pl.program_id(1) * tq
    nkc = (qs + tq) // tk
    wlo = jnp.maximum((qs - WIN) // tk, 0)
    qpos = qs + lax.broadcasted_iota(I32, (1, tq), 1)
    ncr = seq // CMP_S
    nblk = seq // SLC_L
    n_sel = min(SLC_N_MAX, nblk)
    valid_c = CMP_S * lax.broadcasted_iota(I32, (ncr, tq), 0) + (CMP_L - 1) <= qpos
    any_c = jnp.where(qpos >= CMP_L - 1, 1.0, 0.0)
    jrow = lax.broadcasted_iota(I32, (nblk, tq), 0)
    qblk = qpos // SLC_L
    forced = (jrow == 0) | (jrow == qblk) | (jrow == qblk - 1)
    causal_blk = jrow * SLC_L <= qpos
    krow = lax.broadcasted_iota(I32, (tk, tq), 0)
    brow = lax.broadcasted_iota(I32, (SLC_L, tq), 0)
    group = lambda h: h // NSA_R
    pair = lambda h: slice((group(h) // 2) * LANES, (group(h) // 2 + 1) * LANES)
    grows = lambda h: slice(group(h) * HD, (group(h) + 1) * HD)
    vrows = lambda h: slice(group(h) * VAUG, (group(h) + 1) * VAUG)
    q_of = lambda h: _pad_rows(qT_ref[h * HD:(h + 1) * HD, :], group(h) % 2)
    gate = lambda h, branch: gT_ref[h * 3 + branch:h * 3 + branch + 1, :]

    for g in range(NSA_G):
        heads = [g * NSA_R + r for r in range(NSA_R)]
        kcm = kc_ref[0][:, pair(heads[0])]
        vcm = vcT_ref[0][grows(heads[0]), :]
        psum = jnp.zeros((ncr, tq), F32)
        for h in heads:
            s = jnp.where(valid_c, _dot(kcm, q_of(h)), NEG)
            p = jnp.exp2(s - jnp.max(s, axis=0, keepdims=True))
            p = p / jnp.sum(p, axis=0, keepdims=True) * any_c
            out_ref[h * HD:(h + 1) * HD, :] = gate(h, 0) * _dot(vcm, p.astype(BF16))
            psum = psum + p
        p_hi = psum.astype(BF16)
        p_lo = (psum - p_hi.astype(F32)).astype(BF16)
        imp = _dot(cov_ref[...], p_hi) + _dot(cov_ref[...], p_lo)
        imp = jnp.where(causal_blk, imp + jnp.where(forced, FORCE, 0.0), NEG)
        rank = jnp.zeros((nblk, tq), I32)
        for j2 in range(nblk):
            rj = imp[j2:j2 + 1]
            beats = (rj > imp) | ((rj == imp) & (j2 < jrow))
            rank = rank + beats.astype(I32)
        selb = jnp.where(rank < n_sel, 0.0, NEG)
        for j in range(nblk):
            bias_ref[g, j * SLC_L:(j + 1) * SLC_L, :] = jnp.where(j * SLC_L + brow <= qpos, selb[j:j + 1], NEG)

    stats = (s_ref, mc_ref, m_ref, acc_ref)
    slc = _staged_attention(
        NSA_H, 0, nkc, tk,
        lambda h, r0: _dot(ks_ref[pl.ds(r0, tk), pair(h)], q_of(h)) + bias_ref[group(h), pl.ds(r0, tk), :],
        lambda h, r0: vsT_ref[vrows(h), pl.ds(r0, tk)], *stats)
    for h in range(NSA_H):
        out_ref[h * HD:(h + 1) * HD, :] = out_ref[h * HD:(h + 1) * HD, :] + gate(h, 1) * slc[h]

    def win_mask_chunk(c, carry):
        r0 = pl.multiple_of(c * tk, tk)
        dist = qpos - (r0 + krow)
        bias_ref[0, pl.ds(r0, tk), :] = jnp.where((dist >= 0) & (dist < WIN), 0.0, NEG)
        return carry

    lax.fori_loop(wlo, nkc, win_mask_chunk, 0)
    win = _staged_attention(
        NSA_H, wlo, nkc, tk,
        lambda h, r0: _dot(kw_ref[pl.ds(r0, tk), pair(h)], q_of(h)) + bias_ref[0, pl.ds(r0, tk), :],
        lambda h, r0: vwT_ref[vrows(h), pl.ds(r0, tk)], *stats)
    outs = [out_ref[h * HD:(h + 1) * HD, :] + gate(h, 2) * win[h] for h in range(NSA_H)]
    y_ref[...] = jnp.concatenate(outs, axis=0).T.astype(BF16)


def _nsa_attn(qT, gT, ks, vsT, kw, vwT, kcmp, vcmpT, batch, seq):
    t = ks.shape[0]
    tq = NSA_TQ
    nq = seq // tq
    ncr, nblk = seq // CMP_S, seq // SLC_L
    starts = np.arange(ncr) * CMP_S
    sstart = np.arange(nblk) * SLC_L
    cover = (starts[None, :] < sstart[:, None] + SLC_L) & (starts[None, :] + CMP_L > sstart[:, None])
    cover[:, ncr - 1] = False
    colq = lambda n: pl.BlockSpec((n, tq), lambda b, i: (0, b * nq + i))
    tok = pl.BlockSpec((seq, NSA_KV), lambda b, i: (b, 0))
    feat = pl.BlockSpec((NSA_G * VAUG, seq), lambda b, i: (0, b))
    return pl.pallas_call(
        functools.partial(_nsa_attn_body, seq=seq),
        out_shape=jax.ShapeDtypeStruct((t, NSA_H * HD), BF16),
        grid=(batch, nq),
        in_specs=[
            colq(NSA_H * HD), colq(gT.shape[0]), tok, feat, tok, feat,
            pl.BlockSpec((1, ncr, NSA_KV), lambda b, i: (b, 0, 0)),
            pl.BlockSpec((1, NSA_KV, ncr), lambda b, i: (b, 0, 0)),
            _const_spec((nblk, ncr)),
        ],
        out_specs=pl.BlockSpec((tq, NSA_H * HD), lambda b, i: (b * nq + i, 0)),
        scratch_shapes=[pltpu.VMEM((NSA_G, seq, tq), F32), pltpu.VMEM((NSA_H * HD, tq), F32),
                        pltpu.VMEM((2, NSA_H, NSA_TK, tq), F32), pltpu.VMEM((2, NSA_H * SUBLANES, tq), F32),
                        pltpu.VMEM((2 * SUBLANES, tq), F32), pltpu.VMEM((NSA_H * VAUG, tq), F32)],
        compiler_params=_cparams("parallel", "arbitrary"),
        name="nsa_attn",
    )(qT, gT, ks, vsT, kw, vwT, kcmp, vcmpT, jnp.asarray(cover, BF16))


def _nsa_layer_mix(x, g, w_in, qn, kn, pos_k, pos_v, wk, wv, mem_qn, batch, seq):
    kc, vc, ks, kw, qT, vsT, vwT, gT, qmT = _nsa_proj(x, g, w_in, qn, kn, mem_qn, seq)
    kcmp, vcmpT = _nsa_cmp(kc, vc, pos_k, pos_v, wk, wv, kn[0], batch, seq)
    return _nsa_attn(qT, gT, ks, vsT, kw, vwT, kcmp, vcmpT, batch, seq), qmT


def kernel(x, mem, ffn_norm, ffn_w_gate, ffn_w_up, ffn_w_down, mix_norm, w_out, mem_norm, mem_w_kv, mem_qn, mem_kn, ret_w_in, dsa_w_in, dsa_qn, dsa_kn, nsa_w_in, nsa_qn, nsa_kn, nsa_cmp_pos_k, nsa_cmp_pos_v, nsa_cmp_wk, nsa_cmp_wv):
    batch, seq, d = x.shape
    x = x.reshape(batch * seq, d)
    ffn_w = (ffn_w_gate.astype(BF16), ffn_w_up.astype(BF16), ffn_w_down.astype(BF16))
    for i in range(ffn_norm.shape[0]):
        x = _ffn(x, ffn_norm[i, 0], *ffn_w, i, 0)
        kind, j = i % N_MIXERS, i // N_MIXERS
        if kind == 0:
            q, v, gate, kT, qmT = _ret_proj(x, mix_norm[i], ret_w_in[j], mem_qn[i], seq)
            y_mix = _retention(q, kT, v, gate, batch, seq)
        elif kind == 1:
            kk, qT, iqT, vT, iwT, qmT = _dsa_proj(x, mix_norm[i], dsa_w_in[j], dsa_qn[j], dsa_kn[j], mem_qn[i], seq)
            y_mix = _dsa_attn(kk, qT, iqT, vT, iwT, batch, seq)
        else:
            y_mix, qmT = _nsa_layer_mix(x, mix_norm[i], nsa_w_in[j], nsa_qn[j], nsa_kn[j], nsa_cmp_pos_k[j],
                                        nsa_cmp_pos_v[j], nsa_cmp_wk[j], nsa_cmp_wv[j], mem_qn[i], batch, seq)
        mem_k, mem_vT = _mem_kv(mem, mem_norm[i], mem_w_kv[i], mem_kn[i])
        y_mem = _mem_attn(qmT, mem_k, mem_vT, seq)
        x = _mix_out_ffn(y_mix, y_mem, x, w_out[i], ffn_norm[i, 1], *ffn_w, i, 1)
    return x.reshape(batch, seq, d)
```

```python
import functools
import math

import jax
import jax.numpy as jnp
import numpy as np
from jax import lax
from jax.experimental import pallas as pl
from jax.experimental.pallas import tpu as pltpu

D_MODEL = 1024
HD = 64
HALF = HD // 2
H_MIX = 12
H_MEM = 4
MIX_W = H_MIX * HD
MEM_W = H_MEM * HD
D_FF = 2816
ROPE_THETA = 10000.0
EPS = 1e-6
NEG = -1e30
FORCE = 1e9
SCALE = HD ** -0.5
QSCALE = SCALE * math.log2(math.e)

RET_H, RET_DK, RET_DV, RET_CHUNK = 6, 64, 128, 128
DSA_H, IDX_H, IDX_D, DSA_TOPK_MAX = 12, 8, 64, 256
NSA_H, NSA_G, CMP_L, CMP_S, SLC_L, SLC_N_MAX, WIN = 12, 4, 32, 16, 64, 16, 512
NSA_R = NSA_H // NSA_G
N_MIXERS = 3

RET_SIZES = [RET_H * RET_DK, RET_H * RET_DK, RET_H * RET_DV, RET_H * RET_DV, MEM_W]
DSA_SIZES = [DSA_H * HD, HD, HD, IDX_H * IDX_D, IDX_D, IDX_H, MEM_W]
NSA_SIZES = [NSA_H * HD] + [NSA_G * HD] * 6 + [NSA_H * 3, MEM_W]

LANES = 128
SUBLANES = 8
PACKED_SUBLANES = 2 * SUBLANES
VAUG = HD + PACKED_SUBLANES
MXU_N = 256
VMEM_LIMIT_BYTES = 56 * 1024 * 1024

BF16 = jnp.bfloat16
F32 = jnp.float32
I32 = jnp.int32
I16 = jnp.int16


def _cparams(*sem):
    return pltpu.CompilerParams(dimension_semantics=sem, vmem_limit_bytes=VMEM_LIMIT_BYTES)


def _const_spec(shape):
    n = len(shape)
    return pl.BlockSpec(shape, lambda *_: (0,) * n, pipeline_mode=pl.Buffered(1))


def _rms(x, g):
    return x * lax.rsqrt(jnp.mean(x * x, axis=-1, keepdims=True) + EPS) * g


def _dot(a, b):
    return jnp.dot(a, b, preferred_element_type=F32)


def _dot_nt(a, b):
    return lax.dot_general(a, b, (((1,), (1,)), ((), ())), preferred_element_type=F32)


def _tile_lanes(a, n):
    reps = n // a.shape[-1]
    return a if reps == 1 else jnp.concatenate([a] * reps, axis=-1)


def _rope_tables(pos):
    inv = ROPE_THETA ** (-jnp.arange(HALF, dtype=F32) / HALF)
    ang = pos.astype(F32)[:, None] * inv[None, :]
    return jnp.cos(ang).T, jnp.sin(ang).T


def _lane_gain(g):
    return jnp.broadcast_to(g.astype(F32)[:, None], (g.shape[0], LANES))


def _augment_values(v):
    t = v.shape[-1]
    v3 = v.reshape(v.shape[0] // HD, HD, t)
    tail = jnp.where(lax.broadcasted_iota(I32, (v3.shape[0], VAUG - HD, t), 1) == 0, 1.0, 0.0).astype(v.dtype)
    return jnp.concatenate([v3, tail], axis=1).reshape(v3.shape[0] * VAUG, t)


def _heads_norm_rope(x, gain, cos, sin):
    t = x.shape[-1]
    x3 = x.reshape(x.shape[0] // HD, HD, t)
    if gain is not None:
        ms = jnp.sum(x3 * x3, axis=1, keepdims=True) * (1.0 / HD)
        x3 = x3 * lax.rsqrt(ms + EPS) * gain[None]
    if cos is not None:
        x1, x2 = x3[:, :HALF], x3[:, HALF:]
        x3 = jnp.concatenate([x1 * cos[None] - x2 * sin[None], x2 * cos[None] + x1 * sin[None]], axis=1)
    return x3.reshape(x.shape)


FFN_TM = 1024
FFN_TF = MXU_N


def _swiglu_half_step(x, g_ref, wg_ref, wu_ref, wd_ref):
    h = _rms(x, g_ref[...]).astype(BF16)
    acc = jnp.zeros(x.shape, F32)
    for c in range(D_FF // FFN_TF):
        sl = slice(c * FFN_TF, (c + 1) * FFN_TF)
        gate = _dot(h, wg_ref[:, sl])
        up = _dot(h, wu_ref[:, sl])
        act = (gate * jax.nn.sigmoid(gate) * up).astype(BF16)
        acc = acc + _dot(act, wd_ref[sl, :])
    return x + 0.5 * acc


def _ffn_body(x_ref, g_ref, wg_ref, wu_ref, wd_ref, o_ref):
    o_ref[...] = _swiglu_half_step(x_ref[...], g_ref, wg_ref, wu_ref, wd_ref)


def _mix_out_ffn_body(ymix_ref, ymem_ref, x_ref, wmix_ref, wmem_ref, g_ref, wg_ref, wu_ref, wd_ref, o_ref):
    x = x_ref[...] + _dot(ymix_ref[...], wmix_ref[...]) + _dot(ymem_ref[...], wmem_ref[...])
    o_ref[...] = _swiglu_half_step(x, g_ref, wg_ref, wu_ref, wd_ref)


def _ffn_specs(d, layer, half):
    pick = lambda r, c: pl.BlockSpec((None, None, r, c), lambda i: (layer, half, 0, 0), pipeline_mode=pl.Buffered(1))
    return [_const_spec((1, d)), pick(d, D_FF), pick(d, D_FF), pick(D_FF, d)]


def _ffn(x, g, wg, wu, wd, layer, half):
    t, d = x.shape
    tm = min(FFN_TM, t)
    return pl.pallas_call(
        _ffn_body,
        out_shape=jax.ShapeDtypeStruct((t, d), F32),
        grid=(t // tm,),
        in_specs=[pl.BlockSpec((tm, d), lambda i: (i, 0))] + _ffn_specs(d, layer, half),
        out_specs=pl.BlockSpec((tm, d), lambda i: (i, 0)),
        compiler_params=_cparams("parallel"),
        name="ffn",
    )(x, g.reshape(1, d), wg, wu, wd)


def _mix_out_ffn(y_mix, y_mem, x, w_out, g, wg, wu, wd, layer, half):
    t, d = x.shape
    tm = min(FFN_TM, t)
    return pl.pallas_call(
        _mix_out_ffn_body,
        out_shape=jax.ShapeDtypeStruct((t, d), F32),
        grid=(t // tm,),
        in_specs=[
            pl.BlockSpec((tm, MIX_W), lambda i: (i, 0)),
            pl.BlockSpec((tm, MEM_W), lambda i: (i, 0)),
            pl.BlockSpec((tm, d), lambda i: (i, 0)),
            _const_spec((MIX_W, d)),
            _const_spec((MEM_W, d)),
        ] + _ffn_specs(d, layer, half),
        out_specs=pl.BlockSpec((tm, d), lambda i: (i, 0)),
        compiler_params=_cparams("parallel"),
        name="mix_out_ffn",
    )(y_mix, y_mem, x, w_out[:MIX_W].astype(BF16), w_out[MIX_W:].astype(BF16), g.reshape(1, d), wg, wu, wd)


PROJ_TM = 1024


def _mem_kv_body(m_ref, g_ref, wk_ref, wvT_ref, kn_ref, k_ref, vT_ref):
    h = _rms(m_ref[0], g_ref[...]).astype(BF16)
    kT = _dot_nt(wk_ref[...], h)
    kT = _heads_norm_rope(kT, _tile_lanes(kn_ref[...], kT.shape[-1]), None, None)
    k_ref[0] = kT.T.astype(BF16)
    vT_ref[0] = _augment_values(_dot_nt(wvT_ref[...], h)).astype(BF16)


def _mem_kv(mem, g, w_kv, kn):
    b, m, d = mem.shape
    return pl.pallas_call(
        _mem_kv_body,
        out_shape=(jax.ShapeDtypeStruct((b, m, MEM_W), BF16), jax.ShapeDtypeStruct((b, H_MEM * VAUG, m), BF16)),
        grid=(b,),
        in_specs=[
            pl.BlockSpec((1, m, d), lambda i: (i, 0, 0)),
            _const_spec((1, d)),
            _const_spec((MEM_W, d)),
            _const_spec((MEM_W, d)),
            _const_spec((HD, LANES)),
        ],
        out_specs=(pl.BlockSpec((1, m, MEM_W), lambda i: (i, 0, 0)),
                   pl.BlockSpec((1, H_MEM * VAUG, m), lambda i: (i, 0, 0))),
        compiler_params=_cparams("parallel"),
        name="mem_kv",
    )(mem, g.reshape(1, d), w_kv[:, :MEM_W].T.astype(BF16), w_kv[:, MEM_W:].T.astype(BF16), _lane_gain(kn))


MEM_TQ = 1024


def _pad_rows(blk, half, total=LANES):
    z = jnp.zeros_like(blk)
    parts = [z] * (total // HD)
    parts[half] = blk
    return jnp.concatenate(parts, axis=0)


def _mem_attn_body(qT_ref, k_ref, vT_ref, o_ref, s_ref, mc_ref, m_ref, acc_ref):
    n_mem = k_ref.shape[1]
    outs = _staged_attention(
        H_MEM, 0, 1, n_mem,
        lambda h, r0: _dot(k_ref[0, :, (h // 2) * LANES:(h // 2 + 1) * LANES],
                           _pad_rows(qT_ref[h * HD:(h + 1) * HD, :], h % 2)),
        lambda h, r0: vT_ref[0, h * VAUG:(h + 1) * VAUG, :], s_ref, mc_ref, m_ref, acc_ref)
    o_ref[...] = jnp.concatenate(outs, axis=0).T.astype(BF16)


def _mem_attn(qmT, k, vT, seq):
    t = qmT.shape[1]
    b, m, _ = k.shape
    tq = min(MEM_TQ, seq)
    nq = seq // tq
    return pl.pallas_call(
        _mem_attn_body,
        out_shape=jax.ShapeDtypeStruct((t, MEM_W), BF16),
        grid=(b, nq),
        in_specs=[
            pl.BlockSpec((MEM_W, tq), lambda bi, i: (0, bi * nq + i)),
            pl.BlockSpec((1, m, MEM_W), lambda bi, i: (bi, 0, 0)),
            pl.BlockSpec((1, H_MEM * VAUG, m), lambda bi, i: (bi, 0, 0)),
        ],
        out_specs=pl.BlockSpec((tq, MEM_W), lambda bi, i: (bi * nq + i, 0)),
        scratch_shapes=[pltpu.VMEM((2, H_MEM, m, tq), F32), pltpu.VMEM((2, H_MEM * SUBLANES, tq), F32),
                        pltpu.VMEM((SUBLANES, tq), F32), pltpu.VMEM((H_MEM * VAUG, tq), F32)],
        compiler_params=_cparams("parallel", "parallel"),
        name="mem_attn",
    )(qmT, k, vT)


def _ret_proj_body(x_ref, g_ref, wtok_ref, wfeat_ref, cosT_ref, sinT_ref, cosF_ref, sinF_ref, mqn_ref,
                   q_ref, v_ref, gate_ref, kT_ref, qmT_ref):
    h = _rms(x_ref[...], g_ref[...]).astype(BF16)
    tok = _dot(h, wtok_ref[...])
    nq, nv = RET_H * RET_DK, RET_H * RET_DV
    q = tok[:, :nq]
    lane = lax.broadcasted_iota(I32, q.shape, 1)
    rot = jnp.where(lane % HD < HALF, pltpu.roll(q, nq - HALF, axis=1), pltpu.roll(q, HALF, axis=1))
    q_ref[...] = (q * _tile_lanes(cosT_ref[...], nq) + rot * _tile_lanes(sinT_ref[...], nq)).astype(BF16)
    v_ref[...] = tok[:, nq:nq + nv].astype(BF16)
    gate_ref[...] = tok[:, nq + nv:]
    feat = _dot_nt(wfeat_ref[...], h)
    tm = feat.shape[-1]
    cos, sin = cosF_ref[...], sinF_ref[...]
    kT = _heads_norm_rope(feat[:nq], None, cos, sin) * (RET_DK ** -0.5)
    kT_ref[...] = kT.astype(BF16)
    qm = _heads_norm_rope(feat[nq:], _tile_lanes(mqn_ref[...], tm), None, None) * QSCALE
    qmT_ref[...] = qm.astype(BF16)


def _token_rope_tables(seq):
    cosF, sinF = _rope_tables(jnp.arange(seq))
    cos = jnp.tile(cosF.T, (1, LANES // HALF))
    sign = jnp.where((jnp.arange(LANES) % HD) < HALF, -1.0, 1.0).astype(F32)
    sin = jnp.tile(sinF.T, (1, LANES // HALF)) * sign[None, :]
    return cos, sin


def _ret_proj(x, g, w_in, mem_qn, seq):
    t, d = x.shape
    tm = min(PROJ_TM, seq)
    npos = seq // tm
    wq, wk, wv, wg, wqm = jnp.split(w_in, np.cumsum(RET_SIZES)[:-1].tolist(), axis=1)
    wtok = jnp.concatenate([wq, wv, wg], axis=1).astype(BF16)
    wfeat = jnp.concatenate([wk, wqm], axis=1).T.astype(BF16)
    cosT, sinT = _token_rope_tables(seq)
    cosF, sinF = _rope_tables(jnp.arange(seq))
    nq, nv = RET_H * RET_DK, RET_H * RET_DV
    row = lambda n: pl.BlockSpec((tm, n), lambda i: (i, 0))
    col = lambda n: pl.BlockSpec((n, tm), lambda i: (0, i))
    return pl.pallas_call(
        _ret_proj_body,
        out_shape=(
            jax.ShapeDtypeStruct((t, nq), BF16), jax.ShapeDtypeStruct((t, nv), BF16),
            jax.ShapeDtypeStruct((t, nv), F32), jax.ShapeDtypeStruct((nq, t), BF16),
            jax.ShapeDtypeStruct((MEM_W, t), BF16),
        ),
        grid=(t // tm,),
        in_specs=[
            row(d), _const_spec((1, d)), _const_spec(wtok.shape), _const_spec(wfeat.shape),
            pl.BlockSpec((tm, LANES), lambda i: (i % npos, 0)), pl.BlockSpec((tm, LANES), lambda i: (i % npos, 0)),
            pl.BlockSpec((HALF, tm), lambda i: (0, i % npos)), pl.BlockSpec((HALF, tm), lambda i: (0, i % npos)),
            _const_spec((HD, LANES)),
        ],
        out_specs=(row(nq), row(nv), row(nv), col(nq), col(MEM_W)),
        compiler_params=_cparams("parallel"),
        name="ret_proj",
    )(x, g.reshape(1, d), wtok, wfeat, cosT, sinT, cosF, sinF, _lane_gain(mem_qn))


RET_TL = 1024


def _ret_body(q_ref, kT_ref, v_ref, gate_ref, y_ref, state_ref):
    @pl.when(pl.program_id(1) == 0)
    def _():
        state_ref[...] = jnp.zeros_like(state_ref)

    c = RET_CHUNK
    ii = lax.broadcasted_iota(I32, (c, c), 0).astype(F32)
    jj = lax.broadcasted_iota(I32, (c, c), 1).astype(F32)
    diff = ii - jj
    jk = lax.broadcasted_iota(I32, (RET_DK, c), 1).astype(F32)
    log_gs = [math.log(1.0 - 2.0 ** (-5.0 - hh)) for hh in range(RET_H)]
    dmasks = [jnp.where(diff >= 0, jnp.exp(lg * jnp.maximum(diff, 0.0)), 0.0) for lg in log_gs]
    xis = [jnp.exp(lg * (ii + 1.0)) for lg in log_gs]
    zetas = [jnp.exp(lg * (c - 1.0 - jk)) for lg in log_gs]
    for n in range(q_ref.shape[0] // c):
        rows = slice(n * c, (n + 1) * c)
        for hh in range(RET_H):
            dmask, xi, zeta, decay = dmasks[hh], xis[hh], zetas[hh], math.exp(log_gs[hh] * c)
            pair = slice((hh // 2) * LANES, (hh // 2 + 1) * LANES)
            q2 = q_ref[rows, pair]
            kT = kT_ref[hh * RET_DK:(hh + 1) * RET_DK, rows]
            v = v_ref[rows, hh * RET_DV:(hh + 1) * RET_DV]
            state = state_ref[hh]
            inner = _dot(q2, _pad_rows(kT, hh % 2)) * dmask
            o = _dot(inner.astype(BF16), v)
            o = o + _dot(q2, _pad_rows(state.astype(BF16), hh % 2)) * xi
            kv = _dot((kT.astype(F32) * zeta).astype(BF16), v)
            state_ref[hh] = state * decay + kv
            mu = jnp.mean(o, axis=-1, keepdims=True)
            var = jnp.mean(jnp.square(o - mu), axis=-1, keepdims=True)
            o = (o - mu) * lax.rsqrt(var + EPS)
            gte = gate_ref[rows, hh * RET_DV:(hh + 1) * RET_DV]
            y_ref[rows, hh * RET_DV:(hh + 1) * RET_DV] = (gte * jax.nn.sigmoid(gte) * o).astype(BF16)


def _retention(q, kT, v, gate, batch, seq):
    t = q.shape[0]
    tl = min(RET_TL, seq)
    nl = seq // tl
    nq, nv = RET_H * RET_DK, RET_H * RET_DV
    return pl.pallas_call(
        _ret_body,
        out_shape=jax.ShapeDtypeStruct((t, nv), BF16),
        grid=(batch, nl),
        in_specs=[
            pl.BlockSpec((tl, nq), lambda b, i: (b * nl + i, 0)),
            pl.BlockSpec((nq, tl), lambda b, i: (0, b * nl + i)),
            pl.BlockSpec((tl, nv), lambda b, i: (b * nl + i, 0)),
            pl.BlockSpec((tl, nv), lambda b, i: (b * nl + i, 0)),
        ],
        out_specs=pl.BlockSpec((tl, nv), lambda b, i: (b * nl + i, 0)),
        scratch_shapes=[pltpu.VMEM((RET_H, RET_DK, RET_DV), F32)],
        compiler_params=_cparams("parallel", "arbitrary"),
        name="retention",
    )(q, kT, v, gate)


def _dsa_proj_body(x_ref, g_ref, w_ref, cos_ref, sin_ref, qn_ref, kn_ref, mqn_ref,
                   kk_ref, qT_ref, iqT_ref, vT_ref, iwT_ref, qmT_ref):
    h = _rms(x_ref[...], g_ref[...]).astype(BF16)
    nq, niq = DSA_H * HD, IDX_H * IDX_D
    q = _dot_nt(w_ref[:nq, :], h)
    iq = _dot_nt(w_ref[nq:nq + niq, :], h)
    feat = _dot_nt(w_ref[nq + niq:, :], h)
    tm = feat.shape[-1]
    cos, sin = cos_ref[...], sin_ref[...]
    o = 0
    k = feat[o:o + HD]; o += HD
    ik = feat[o:o + IDX_D]; o += IDX_D
    v = feat[o:o + HD]; o += HD
    qm = feat[o:o + MEM_W]; o += MEM_W
    iw = feat[o:o + IDX_H]
    qT_ref[...] = (_heads_norm_rope(q, _tile_lanes(qn_ref[...], tm), cos, sin) * QSCALE).astype(BF16)
    iqT_ref[...] = _heads_norm_rope(iq, None, cos, sin).astype(BF16)
    k = _heads_norm_rope(k, _tile_lanes(kn_ref[...], tm), cos, sin)
    ik = _heads_norm_rope(ik, None, cos, sin)
    kk_ref[...] = jnp.concatenate([k, ik], axis=0).T.astype(BF16)
    vT_ref[...] = _augment_values(v).astype(BF16)
    iwT_ref[...] = iw
    qmT_ref[...] = (_heads_norm_rope(qm, _tile_lanes(mqn_ref[...], tm), None, None) * QSCALE).astype(BF16)


def _dsa_proj(x, g, w_in, qn, kn, mem_qn, seq):
    t, d = x.shape
    tm = min(PROJ_TM, seq)
    npos = seq // tm
    wq, wk, wv, wiq, wik, wiw, wqm = jnp.split(w_in, np.cumsum(DSA_SIZES)[:-1].tolist(), axis=1)
    wfeat = jnp.concatenate([wq, wiq, wk, wik, wv, wqm, wiw], axis=1).T.astype(BF16)
    cosF, sinF = _rope_tables(jnp.arange(seq))
    col = lambda n: pl.BlockSpec((n, tm), lambda i: (0, i))
    tab = pl.BlockSpec((HALF, tm), lambda i: (0, i % npos))
    gain = _const_spec((HD, LANES))
    return pl.pallas_call(
        _dsa_proj_body,
        out_shape=(
            jax.ShapeDtypeStruct((t, LANES), BF16), jax.ShapeDtypeStruct((DSA_H * HD, t), BF16),
            jax.ShapeDtypeStruct((IDX_H * IDX_D, t), BF16), jax.ShapeDtypeStruct((VAUG, t), BF16),
            jax.ShapeDtypeStruct((IDX_H, t), F32), jax.ShapeDtypeStruct((MEM_W, t), BF16),
        ),
        grid=(t // tm,),
        in_specs=[pl.BlockSpec((tm, d), lambda i: (i, 0)), _const_spec((1, d)), _const_spec(wfeat.shape),
                  tab, tab, gain, gain, gain],
        out_specs=(pl.BlockSpec((tm, LANES), lambda i: (i, 0)), col(DSA_H * HD), col(IDX_H * IDX_D), col(VAUG),
                   col(IDX_H), col(MEM_W)),
        compiler_params=_cparams("parallel"),
        name="dsa_proj",
    )(x, g.reshape(1, d), wfeat, cosF, sinF, _lane_gain(qn), _lane_gain(kn), _lane_gain(mem_qn))


DSA_TQ = 256
DSA_TK = 256
HALF_BITS = 16
INT16_MIN = -2 ** (HALF_BITS - 1)


def _dsa_attn_body(kk_ref, qT_ref, iqT_ref, vT_ref, iwT_ref, y_ref, keys_ref, hi_ref, lo_ref, bias_ref, s_ref, mc_ref,
                   m_ref, acc_ref, *, topk, seq):
    tq, tk = DSA_TQ, DSA_TK
    qs = pl.program_id(1) * tq
    nkc = (qs + tq) // tk
    qpos = qs + lax.broadcasted_iota(I32, (1, tq), 1)
    row = lax.broadcasted_iota(I32, (tk, tq), 0)

    def score_chunk(c, carry):
        r0 = pl.multiple_of(c * tk, tk)
        kkc = kk_ref[pl.ds(r0, tk), :]
        acc = jnp.zeros((tk, tq), F32)
        for h in range(IDX_H):
            r = _dot(kkc, _pad_rows(iqT_ref[h * IDX_D:(h + 1) * IDX_D, :], 1))
            acc = acc + jnp.maximum(r, 0.0) * iwT_ref[h:h + 1, :]
        sc = jnp.where(r0 + row <= qpos, acc + 0.0, -jnp.inf)
        bits = pltpu.bitcast(sc, I32)
        key = jnp.where(bits < 0, bits ^ jnp.int32(0x7FFFFFFF), bits)
        keys_ref[pl.ds(r0, tk), :] = key
        hi_ref[pl.ds(r0, tk), :] = lax.shift_right_arithmetic(key, HALF_BITS).astype(I16)
        return carry

    lax.fori_loop(0, nkc, score_chunk, 0)

    def count(pred):
        def body(c, acc):
            r0 = pl.multiple_of(c * tk, tk)
            m = pred(keys_ref[pl.ds(r0, tk), :], r0 + row)
            return acc + jnp.sum(m.astype(I32).reshape(tk // SUBLANES, SUBLANES, tq), axis=0)
        acc = lax.fori_loop(0, nkc, body, jnp.zeros((SUBLANES, tq), I32))
        return jnp.sum(acc, axis=0, keepdims=True)

    def count16(ref, pred):
        def body(c, acc):
            r0 = pl.multiple_of(c * tk, tk)
            ones = jnp.where(pred(ref[pl.ds(r0, tk), :]), jnp.int16(1), jnp.int16(0))
            parts = [ones[i:i + PACKED_SUBLANES] for i in range(0, tk, PACKED_SUBLANES)]
            while len(parts) > 1:
                parts = [a + b for a, b in zip(parts[::2], parts[1::2])]
            return acc + parts[0]
        acc = lax.fori_loop(0, nkc, body, jnp.zeros((PACKED_SUBLANES, tq), I16))
        return jnp.sum(acc.astype(I32), axis=0, keepdims=True)

    def kth_largest16(ref, k):
        thr = jnp.where(count16(ref, lambda v: v >= 0) >= k, jnp.int32(0), jnp.int32(INT16_MIN))

        def bit_step(b, thr):
            cand = thr | lax.shift_left(jnp.int32(1), (HALF_BITS - 2) - b)
            c16 = cand.astype(I16)
            return jnp.where(count16(ref, lambda v: v >= c16) >= k, cand, thr)

        return lax.fori_loop(0, HALF_BITS - 1, bit_step, thr)

    thr_hi = kth_largest16(hi_ref, topk)
    thr_hi16 = thr_hi.astype(I16)
    need_lo = topk - count16(hi_ref, lambda v: v > thr_hi16)

    def low_half_chunk(c, carry):
        r0 = pl.multiple_of(c * tk, tk)
        lo = ((keys_ref[pl.ds(r0, tk), :] & jnp.int32(0xFFFF)) + jnp.int32(INT16_MIN)).astype(I16)
        lo_ref[pl.ds(r0, tk), :] = jnp.where(hi_ref[pl.ds(r0, tk), :] == thr_hi16, lo, jnp.int16(INT16_MIN))
        return carry

    lax.fori_loop(0, nkc, low_half_chunk, 0)
    thr_lo = kth_largest16(lo_ref, need_lo)
    thr = lax.shift_left(thr_hi, HALF_BITS) | (thr_lo - jnp.int32(INT16_MIN))
    def tie_search():
        need = topk - count(lambda k, _: k > thr)

        def idx_step(b, q):
            cand = q | lax.shift_left(jnp.int32(1), (seq.bit_length() - 2) - b)
            return jnp.where(count(lambda k, idx: (k == thr) & (idx < cand)) < need, cand, q)

        return lax.fori_loop(0, seq.bit_length() - 1, idx_step, jnp.zeros((1, tq), I32))

    has_ties = jnp.max(count(lambda k, _: k >= thr)) > topk
    last = lax.cond(has_ties, tie_search, lambda: jnp.full((1, tq), seq, I32))

    def bias_chunk(c, carry):
        r0 = pl.multiple_of(c * tk, tk)
        k = keys_ref[pl.ds(r0, tk), :]
        idx = r0 + row
        sel = (k > thr) | ((k == thr) & (idx <= last))
        bias_ref[pl.ds(r0, tk), :] = jnp.where(sel & (idx <= qpos), 0.0, NEG)
        return carry

    lax.fori_loop(0, nkc, bias_chunk, 0)

    outs = _staged_attention(
        DSA_H, 0, nkc, tk,
        lambda h, r0: _dot(kk_ref[pl.ds(r0, tk), :], _pad_rows(qT_ref[h * HD:(h + 1) * HD, :], 0))
        + bias_ref[pl.ds(r0, tk), :],
        lambda h, r0: vT_ref[:, pl.ds(r0, tk)], s_ref, mc_ref, m_ref, acc_ref)
    y_ref[...] = jnp.concatenate(outs, axis=0).T.astype(BF16)


def _dsa_attn(kk, qT, iqT, vT, iwT, batch, seq):
    t = kk.shape[0]
    tq = DSA_TQ
    nq = seq // tq
    topk = min(DSA_TOPK_MAX, seq // 4)
    colq = lambda n: pl.BlockSpec((n, tq), lambda b, i: (0, b * nq + i))
    return pl.pallas_call(
        functools.partial(_dsa_attn_body, topk=topk, seq=seq),
        out_shape=jax.ShapeDtypeStruct((t, DSA_H * HD), BF16),
        grid=(batch, nq),
        in_specs=[
            pl.BlockSpec((seq, LANES), lambda b, i: (b, 0)),
            colq(DSA_H * HD), colq(IDX_H * IDX_D),
            pl.BlockSpec((VAUG, seq), lambda b, i: (0, b)),
            colq(IDX_H),
        ],
        out_specs=pl.BlockSpec((tq, DSA_H * HD), lambda b, i: (b * nq + i, 0)),
        scratch_shapes=[pltpu.VMEM((seq, tq), I32), pltpu.VMEM((seq, tq), I16), pltpu.VMEM((seq, tq), I16),
                        pltpu.VMEM((seq, tq), F32),
                        pltpu.VMEM((2, DSA_H, DSA_TK, tq), F32), pltpu.VMEM((2, DSA_H * SUBLANES, tq), F32),
                        pltpu.VMEM((2 * SUBLANES, tq), F32), pltpu.VMEM((DSA_H * VAUG, tq), F32)],
        compiler_params=_cparams("parallel", "arbitrary"),
        name="dsa_attn",
    )(kk, qT, iqT, vT, iwT)


NSA_KV = NSA_G * HD
NSA_GATES = NSA_H * 3
NSA_GATES_PAD = -NSA_GATES % SUBLANES


def _nsa_proj_body(x_ref, g_ref, wtok_ref, wfeat_ref, cos_ref, sin_ref, qn_ref, kns_ref, knw_ref, mqn_ref,
                   kc_ref, vc_ref, ks_ref, kw_ref, qT_ref, vsT_ref, vwT_ref, gT_ref, qmT_ref):
    h = _rms(x_ref[...], g_ref[...]).astype(BF16)
    tok = _dot(h, wtok_ref[...])
    kc_ref[...] = tok[:, :NSA_KV]
    vc_ref[...] = tok[:, NSA_KV:]
    nq = NSA_H * HD
    q = _dot_nt(wfeat_ref[:nq, :], h)
    kk = _dot_nt(wfeat_ref[nq:nq + 2 * NSA_KV, :], h)
    feat = _dot_nt(wfeat_ref[nq + 2 * NSA_KV:, :], h)
    tm = feat.shape[-1]
    cos, sin = cos_ref[...], sin_ref[...]
    ks, kw = kk[:NSA_KV], kk[NSA_KV:]
    o = 0
    vs = feat[o:o + NSA_KV]; o += NSA_KV
    vw = feat[o:o + NSA_KV]; o += NSA_KV
    qm = feat[o:o + MEM_W]; o += MEM_W
    gates = feat[o:]
    qT_ref[...] = (_heads_norm_rope(q, _tile_lanes(qn_ref[...], tm), cos, sin) * QSCALE).astype(BF16)
    ks_ref[...] = _heads_norm_rope(ks, _tile_lanes(kns_ref[...], tm), cos, sin).T.astype(BF16)
    kw_ref[...] = _heads_norm_rope(kw, _tile_lanes(knw_ref[...], tm), cos, sin).T.astype(BF16)
    vsT_ref[...] = _augment_values(vs).astype(BF16)
    vwT_ref[...] = _augment_values(vw).astype(BF16)
    gT_ref[...] = jax.nn.sigmoid(gates)
    qmT_ref[...] = (_heads_norm_rope(qm, _tile_lanes(mqn_ref[...], tm), None, None) * QSCALE).astype(BF16)


def _nsa_proj(x, g, w_in, qn, kn, mem_qn, seq):
    t, d = x.shape
    tm = min(PROJ_TM, seq)
    npos = seq // tm
    wq, wkc, wvc, wks, wvs, wkw, wvw, wgt, wqm = jnp.split(w_in, np.cumsum(NSA_SIZES)[:-1].tolist(), axis=1)
    wtok = jnp.concatenate([wkc, wvc], axis=1).astype(BF16)
    wgt = jnp.pad(wgt, ((0, 0), (0, NSA_GATES_PAD)))
    wfeat = jnp.concatenate([wq, wks, wkw, wvs, wvw, wqm, wgt], axis=1).T.astype(BF16)
    cosF, sinF = _rope_tables(jnp.arange(seq))
    row = lambda n: pl.BlockSpec((tm, n), lambda i: (i, 0))
    col = lambda n: pl.BlockSpec((n, tm), lambda i: (0, i))
    tab = pl.BlockSpec((HALF, tm), lambda i: (0, i % npos))
    gain = _const_spec((HD, LANES))
    ngt = NSA_GATES + NSA_GATES_PAD
    return pl.pallas_call(
        _nsa_proj_body,
        out_shape=(
            jax.ShapeDtypeStruct((t, NSA_KV), F32), jax.ShapeDtypeStruct((t, NSA_KV), F32),
            jax.ShapeDtypeStruct((t, NSA_KV), BF16), jax.ShapeDtypeStruct((t, NSA_KV), BF16),
            jax.ShapeDtypeStruct((NSA_H * HD, t), BF16), jax.ShapeDtypeStruct((NSA_G * VAUG, t), BF16),
            jax.ShapeDtypeStruct((NSA_G * VAUG, t), BF16), jax.ShapeDtypeStruct((ngt, t), F32),
            jax.ShapeDtypeStruct((MEM_W, t), BF16),
        ),
        grid=(t // tm,),
        in_specs=[row(d), _const_spec((1, d)), _const_spec(wtok.shape), _const_spec(wfeat.shape),
                  tab, tab, gain, gain, gain, gain],
        out_specs=(row(NSA_KV), row(NSA_KV), row(NSA_KV), row(NSA_KV), col(NSA_H * HD), col(NSA_G * VAUG),
                   col(NSA_G * VAUG), col(ngt), col(MEM_W)),
        compiler_params=_cparams("parallel"),
        name="nsa_proj",
    )(x, g.reshape(1, d), wtok, wfeat, cosF, sinF, _lane_gain(qn), _lane_gain(kn[1]), _lane_gain(kn[2]),
      _lane_gain(mem_qn))


def _nsa_cmp_body(k01_ref, k23_ref, v01_ref, v23_ref, pk_ref, pv_ref, wk_ref, wv_ref, kn_ref, cos_ref, sin_ref,
                  k_ref, vT_ref):
    n = k_ref.shape[1]

    def compress(lo_ref, hi_ref, pos, w_ref):
        parts = []
        for l in range(CMP_S):
            parts += [lo_ref[pl.ds(l, n, stride=CMP_S), :], hi_ref[pl.ds(l, n, stride=CMP_S), :]]
        x = jnp.concatenate(parts, axis=1)
        xa = (x + pos[0:1]).astype(BF16)
        xb = (pltpu.roll(x, n - 1, axis=0) + pos[1:2]).astype(BF16)
        return _dot_nt(w_ref[0], xa) + _dot_nt(w_ref[1], xb)

    kT = compress(k01_ref, k23_ref, pk_ref[...], wk_ref)
    kT = _heads_norm_rope(kT, _tile_lanes(kn_ref[...], kT.shape[-1]), cos_ref[...], sin_ref[...])
    k_ref[0] = kT.T.astype(BF16)
    vT_ref[0] = compress(v01_ref, v23_ref, pv_ref[...], wv_ref).astype(BF16)


def _nsa_cmp_weights(w, pos):
    halves, width = CMP_L // CMP_S, CMP_S * NSA_KV
    w_eld = w.reshape(halves, CMP_S, HD, HD).transpose(0, 3, 1, 2)
    per_group = jnp.broadcast_to(w_eld[:, :, :, None, :], (halves, HD, CMP_S, NSA_G, HD)).reshape(halves, HD, width)
    rows = np.arange(NSA_KV)[:, None] // HD
    cols = (np.arange(width)[None, :] // HD) % NSA_G
    wt = jnp.where(jnp.asarray(rows == cols)[None], jnp.tile(per_group, (1, NSA_G, 1)), 0.0)
    p = jnp.broadcast_to(pos.reshape(halves, CMP_S, 1, HD), (halves, CMP_S, NSA_G, HD))
    return wt.astype(BF16), p.reshape(halves, width).astype(F32)


def _nsa_cmp(kc, vc, pos_k, pos_v, wk, wv, kn0, batch, seq):
    n = seq // CMP_S
    wkt, pk = _nsa_cmp_weights(wk, pos_k)
    wvt, pv = _nsa_cmp_weights(wv, pos_v)
    cosE, sinE = _rope_tables(jnp.arange(n) * CMP_S + (CMP_L - 1))
    lanes_lo = pl.BlockSpec((seq, LANES), lambda b: (b, 0))
    lanes_hi = pl.BlockSpec((seq, LANES), lambda b: (b, 1))
    return pl.pallas_call(
        _nsa_cmp_body,
        out_shape=(jax.ShapeDtypeStruct((batch, n, NSA_KV), BF16), jax.ShapeDtypeStruct((batch, NSA_KV, n), BF16)),
        grid=(batch,),
        in_specs=[lanes_lo, lanes_hi, lanes_lo, lanes_hi, _const_spec(pk.shape), _const_spec(pv.shape),
                  _const_spec(wkt.shape), _const_spec(wvt.shape), _const_spec((HD, LANES)), _const_spec((HALF, n)),
                  _const_spec((HALF, n))],
        out_specs=(pl.BlockSpec((1, n, NSA_KV), lambda b: (b, 0, 0)), pl.BlockSpec((1, NSA_KV, n), lambda b: (b, 0, 0))),
        compiler_params=_cparams("parallel"),
        name="nsa_cmp",
    )(kc, kc, vc, vc, pk, pv, wkt, wvt, _lane_gain(kn0), cosE, sinE)


NSA_TQ = 256
NSA_TK = 256


def _staged_attention(nheads, lo, hi, tk, logits, values, s_ref, mc_ref, m_ref, acc_ref):
    tq = s_ref.shape[-1]
    fold = lambda a: a.reshape(tk // SUBLANES, SUBLANES, tq)
    m_ref[...] = jnp.full(m_ref.shape, -jnp.inf, F32)
    acc_ref[...] = jnp.zeros(acc_ref.shape, F32)

    start = lambda c: c * tk if isinstance(c, int) else pl.multiple_of(c * tk, tk)

    def stage(h, c, slot):
        s = logits(h, start(c))
        s_ref[slot, h] = s
        mc_ref[slot, h * SUBLANES:(h + 1) * SUBLANES, :] = jnp.max(fold(s), axis=0)

    def consume(h, c, slot):
        g8, hrows = slice(h * SUBLANES, (h + 1) * SUBLANES), slice(h * VAUG, (h + 1) * VAUG)
        m_old = m_ref[h:h + 1, :]
        m_new = jnp.maximum(m_old, jnp.max(mc_ref[slot, g8, :], axis=0, keepdims=True))
        alpha = jnp.exp2(m_old - m_new)
        p = jnp.exp2(s_ref[slot, h] - m_new)
        acc_ref[hrows, :] = acc_ref[hrows, :] * alpha + _dot(values(h, start(c)), p.astype(BF16))
        m_ref[h:h + 1, :] = m_new

    def consume_and_stage_next(c, slot):
        for h in range(nheads):
            consume(h, c, slot)
            stage(h, c + 1, 1 - slot)

    def finish(c, slot):
        for h in range(nheads):
            consume(h, c, slot)

    for h in range(nheads):
        stage(h, lo, 0)
    n_fused = hi - 1 - lo

    def two_chunks(j, carry):
        c = lo + 2 * j
        consume_and_stage_next(c, 0)
        consume_and_stage_next(c + 1, 1)
        return carry

    if isinstance(n_fused, int):
        for i in range(n_fused):
            consume_and_stage_next(lo + i, i % 2)
        finish(hi - 1, n_fused % 2)
    else:
        lax.fori_loop(0, n_fused // 2, two_chunks, 0)

        @pl.when(n_fused % 2 == 1)
        def _():
            consume_and_stage_next(hi - 2, 0)
            finish(hi - 1, 1)

        @pl.when(n_fused % 2 == 0)
        def _():
            finish(hi - 1, 0)

    return [acc_ref[h * VAUG:h * VAUG + HD, :] / acc_ref[h * VAUG + HD:h * VAUG + HD + 1, :] for h in range(nheads)]


def _nsa_attn_body(qT_ref, gT_ref, ks_ref, vsT_ref, kw_ref, vwT_ref, kc_ref, vcT_ref, cov_ref, y_ref,
                   bias_ref, out_ref, s_ref, mc_ref, m_ref, acc_ref, *, seq):
    tq, tk = NSA_TQ, NSA_TK
    qs = pl.program_id(1) * tq
    nkc = (qs + tq) // tk
    wlo = jnp.maximum((qs - WIN) // tk, 0)
    qpos = qs + lax.broadcasted_iota(I32, (1, tq), 1)
    ncr = seq // CMP_S
    nblk = seq // SLC_L
    n_sel = min(SLC_N_MAX, nblk)
    valid_c = CMP_S * lax.broadcasted_iota(I32, (ncr, tq), 0) + (CMP_L - 1) <= qpos
    any_c = jnp.where(qpos >= CMP_L - 1, 1.0, 0.0)
    jrow = lax.broadcasted_iota(I32, (nblk, tq), 0)
    qblk = qpos // SLC_L
    forced = (jrow == 0) | (jrow == qblk) | (jrow == qblk - 1)
    causal_blk = jrow * SLC_L <= qpos
    krow = lax.broadcasted_iota(I32, (tk, tq), 0)
    brow = lax.broadcasted_iota(I32, (SLC_L, tq), 0)
    group = lambda h: h // NSA_R
    pair = lambda h: slice((group(h) // 2) * LANES, (group(h) // 2 + 1) * LANES)
    grows = lambda h: slice(group(h) * HD, (group(h) + 1) * HD)
    vrows = lambda h: slice(group(h) * VAUG, (group(h) + 1) * VAUG)
    q_of = lambda h: _pad_rows(qT_ref[h * HD:(h + 1) * HD, :], group(h) % 2)
    gate = lambda h, branch: gT_ref[h * 3 + branch:h * 3 + branch + 1, :]

    for g in range(NSA_G):
        heads = [g * NSA_R + r for r in range(NSA_R)]
        kcm = kc_ref[0][:, pair(heads[0])]
        vcm = vcT_ref[0][grows(heads[0]), :]
        psum = jnp.zeros((ncr, tq), F32)
        for h in heads:
            s = jnp.where(valid_c, _dot(kcm, q_of(h)), NEG)
            p = jnp.exp2(s - jnp.max(s, axis=0, keepdims=True))
            p = p / jnp.sum(p, axis=0, keepdims=True) * any_c
            out_ref[h * HD:(h + 1) * HD, :] = gate(h, 0) * _dot(vcm, p.astype(BF16))
            psum = psum + p
        p_hi = psum.astype(BF16)
        p_lo = (psum - p_hi.astype(F32)).astype(BF16)
        imp = _dot(cov_ref[...], p_hi) + _dot(cov_ref[...], p_lo)
        imp = jnp.where(causal_blk, imp + jnp.where(forced, FORCE, 0.0), NEG)
        rank = jnp.zeros((nblk, tq), I32)
        for j2 in range(nblk):
            rj = imp[j2:j2 + 1]
            beats = (rj > imp) | ((rj == imp) & (j2 < jrow))
            rank = rank + beats.astype(I32)
        selb = jnp.where(rank < n_sel, 0.0, NEG)
        for j in range(nblk):
            bias_ref[g, j * SLC_L:(j + 1) * SLC_L, :] = jnp.where(j * SLC_L + brow <= qpos, selb[j:j + 1], NEG)

    stats = (s_ref, mc_ref, m_ref, acc_ref)
    slc = _staged_attention(
        NSA_H, 0, nkc, tk,
        lambda h, r0: _dot(ks_ref[pl.ds(r0, tk), pair(h)], q_of(h)) + bias_ref[group(h), pl.ds(r0, tk), :],
        lambda h, r0: vsT_ref[vrows(h), pl.ds(r0, tk)], *stats)
    for h in range(NSA_H):
        out_ref[h * HD:(h + 1) * HD, :] = out_ref[h * HD:(h + 1) * HD, :] + gate(h, 1) * slc[h]

    def win_mask_chunk(c, carry):
        r0 = pl.multiple_of(c * tk, tk)
        dist = qpos - (r0 + krow)
        bias_ref[0, pl.ds(r0, tk), :] = jnp.where((dist >= 0) & (dist < WIN), 0.0, NEG)
        return carry

    lax.fori_loop(wlo, nkc, win_mask_chunk, 0)
    win = _staged_attention(
        NSA_H, wlo, nkc, tk,
        lambda h, r0: _dot(kw_ref[pl.ds(r0, tk), pair(h)], q_of(h)) + bias_ref[0, pl.ds(r0, tk), :],
        lambda h, r0: vwT_ref[vrows(h), pl.ds(r0, tk)], *stats)
    outs = [out_ref[h * HD:(h + 1) * HD, :] + gate(h, 2) * win[h] for h in range(NSA_H)]
    y_ref[...] = jnp.concatenate(outs, axis=0).T.astype(BF16)


def _nsa_attn(qT, gT, ks, vsT, kw, vwT, kcmp, vcmpT, batch, seq):
    t = ks.shape[0]
    tq = NSA_TQ
    nq = seq // tq
    ncr, nblk = seq // CMP_S, seq // SLC_L
    starts = np.arange(ncr) * CMP_S
    sstart = np.arange(nblk) * SLC_L
    cover = (starts[None, :] < sstart[:, None] + SLC_L) & (starts[None, :] + CMP_L > sstart[:, None])
    cover[:, ncr - 1] = False
    colq = lambda n: pl.BlockSpec((n, tq), lambda b, i: (0, b * nq + i))
    tok = pl.BlockSpec((seq, NSA_KV), lambda b, i: (b, 0))
    feat = pl.BlockSpec((NSA_G * VAUG, seq), lambda b, i: (0, b))
    return pl.pallas_call(
        functools.partial(_nsa_attn_body, seq=seq),
        out_shape=jax.ShapeDtypeStruct((t, NSA_H * HD), BF16),
        grid=(batch, nq),
        in_specs=[
            colq(NSA_H * HD), colq(gT.shape[0]), tok, feat, tok, feat,
            pl.BlockSpec((1, ncr, NSA_KV), lambda b, i: (b, 0, 0)),
            pl.BlockSpec((1, NSA_KV, ncr), lambda b, i: (b, 0, 0)),
            _const_spec((nblk, ncr)),
        ],
        out_specs=pl.BlockSpec((tq, NSA_H * HD), lambda b, i: (b * nq + i, 0)),
        scratch_shapes=[pltpu.VMEM((NSA_G, seq, tq), F32), pltpu.VMEM((NSA_H * HD, tq), F32),
                        pltpu.VMEM((2, NSA_H, NSA_TK, tq), F32), pltpu.VMEM((2, NSA_H * SUBLANES, tq), F32),
                        pltpu.VMEM((2 * SUBLANES, tq), F32), pltpu.VMEM((NSA_H * VAUG, tq), F32)],
        compiler_params=_cparams("parallel", "arbitrary"),
        name="nsa_attn",
    )(qT, gT, ks, vsT, kw, vwT, kcmp, vcmpT, jnp.asarray(cover, BF16))


def _nsa_layer_mix(x, g, w_in, qn, kn, pos_k, pos_v, wk, wv, mem_qn, batch, seq):
    kc, vc, ks, kw, qT, vsT, vwT, gT, qmT = _nsa_proj(x, g, w_in, qn, kn, mem_qn, seq)
    kcmp, vcmpT = _nsa_cmp(kc, vc, pos_k, pos_v, wk, wv, kn[0], batch, seq)
    return _nsa_attn(qT, gT, ks, vsT, kw, vwT, kcmp, vcmpT, batch, seq), qmT


def kernel(x, mem, ffn_norm, ffn_w_gate, ffn_w_up, ffn_w_down, mix_norm, w_out, mem_norm, mem_w_kv, mem_qn, mem_kn, ret_w_in, dsa_w_in, dsa_qn, dsa_kn, nsa_w_in, nsa_qn, nsa_kn, nsa_cmp_pos_k, nsa_cmp_pos_v, nsa_cmp_wk, nsa_cmp_wv):
    batch, seq, d = x.shape
    x = x.reshape(batch * seq, d)
    ffn_w = (ffn_w_gate.astype(BF16), ffn_w_up.astype(BF16), ffn_w_down.astype(BF16))
    for i in range(ffn_norm.shape[0]):
        x = _ffn(x, ffn_norm[i, 0], *ffn_w, i, 0)
        kind, j = i % N_MIXERS, i // N_MIXERS
        if kind == 0:
            q, v, gate, kT, qmT = _ret_proj(x, mix_norm[i], ret_w_in[j], mem_qn[i], seq)
            y_mix = _retention(q, kT, v, gate, batch, seq)
        elif kind == 1:
            kk, qT, iqT, vT, iwT, qmT = _dsa_proj(x, mix_norm[i], dsa_w_in[j], dsa_qn[j], dsa_kn[j], mem_qn[i], seq)
            y_mix = _dsa_attn(kk, qT, iqT, vT, iwT, batch, seq)
        else:
            y_mix, qmT = _nsa_layer_mix(x, mix_norm[i], nsa_w_in[j], nsa_qn[j], nsa_kn[j], nsa_cmp_pos_k[j],
                                        nsa_cmp_pos_v[j], nsa_cmp_wk[j], nsa_cmp_wv[j], mem_qn[i], batch, seq)
        mem_k, mem_vT = _mem_kv(mem, mem_norm[i], mem_w_kv[i], mem_kn[i])
        y_mem = _mem_attn(qmT, mem_k, mem_vT, seq)
        x = _mix_out_ffn(y_mix, y_mem, x, w_out[i], ffn_norm[i, 1], *ffn_w, i, 1)
    return x.reshape(batch, seq, d)
```

```python
import functools
import math

import jax
import jax.numpy as jnp
import numpy as np
from jax import lax
from jax.experimental import pallas as pl
from jax.experimental.pallas import tpu as pltpu

D_MODEL = 1024
HD = 64
HALF = HD // 2
H_MIX = 12
H_MEM = 4
MIX_W = H_MIX * HD
MEM_W = H_MEM * HD
D_FF = 2816
ROPE_THETA = 10000.0
EPS = 1e-6
NEG = -1e30
FORCE = 1e9
SCALE = HD ** -0.5
QSCALE = SCALE * math.log2(math.e)

RET_H, RET_DK, RET_DV, RET_CHUNK = 6, 64, 128, 128
DSA_H, IDX_H, IDX_D, DSA_TOPK_MAX = 12, 8, 64, 256
NSA_H, NSA_G, CMP_L, CMP_S, SLC_L, SLC_N_MAX, WIN = 12, 4, 32, 16, 64, 16, 512
NSA_R = NSA_H // NSA_G
N_MIXERS = 3

RET_SIZES = [RET_H * RET_DK, RET_H * RET_DK, RET_H * RET_DV, RET_H * RET_DV, MEM_W]
DSA_SIZES = [DSA_H * HD, HD, HD, IDX_H * IDX_D, IDX_D, IDX_H, MEM_W]
NSA_SIZES = [NSA_H * HD] + [NSA_G * HD] * 6 + [NSA_H * 3, MEM_W]

LANES = 128
SUBLANES = 8
PACKED_SUBLANES = 2 * SUBLANES
VAUG = HD + PACKED_SUBLANES
MXU_N = 256
VMEM_LIMIT_BYTES = 56 * 1024 * 1024

BF16 = jnp.bfloat16
F32 = jnp.float32
I32 = jnp.int32
I16 = jnp.int16


def _cparams(*sem):
    return pltpu.CompilerParams(dimension_semantics=sem, vmem_limit_bytes=VMEM_LIMIT_BYTES)


def _const_spec(shape):
    n = len(shape)
    return pl.BlockSpec(shape, lambda *_: (0,) * n, pipeline_mode=pl.Buffered(1))


def _rms(x, g):
    return x * lax.rsqrt(jnp.mean(x * x, axis=-1, keepdims=True) + EPS) * g


def _dot(a, b):
    return jnp.dot(a, b, preferred_element_type=F32)


def _dot_nt(a, b):
    return lax.dot_general(a, b, (((1,), (1,)), ((), ())), preferred_element_type=F32)


def _tile_lanes(a, n):
    reps = n // a.shape[-1]
    return a if reps == 1 else jnp.concatenate([a] * reps, axis=-1)


def _rope_tables(pos):
    inv = ROPE_THETA ** (-jnp.arange(HALF, dtype=F32) / HALF)
    ang = pos.astype(F32)[:, None] * inv[None, :]
    return jnp.cos(ang).T, jnp.sin(ang).T


def _lane_gain(g):
    return jnp.broadcast_to(g.astype(F32)[:, None], (g.shape[0], LANES))


def _augment_values(v):
    t = v.shape[-1]
    v3 = v.reshape(v.shape[0] // HD, HD, t)
    tail = jnp.where(lax.broadcasted_iota(I32, (v3.shape[0], VAUG - HD, t), 1) == 0, 1.0, 0.0).astype(v.dtype)
    return jnp.concatenate([v3, tail], axis=1).reshape(v3.shape[0] * VAUG, t)


def _heads_norm_rope(x, gain, cos, sin):
    t = x.shape[-1]
    x3 = x.reshape(x.shape[0] // HD, HD, t)
    if gain is not None:
        ms = jnp.sum(x3 * x3, axis=1, keepdims=True) * (1.0 / HD)
        x3 = x3 * lax.rsqrt(ms + EPS) * gain[None]
    if cos is not None:
        x1, x2 = x3[:, :HALF], x3[:, HALF:]
        x3 = jnp.concatenate([x1 * cos[None] - x2 * sin[None], x2 * cos[None] + x1 * sin[None]], axis=1)
    return x3.reshape(x.shape)


FFN_TM = 1024
FFN_TF = MXU_N


def _swiglu_half_step(x, g_ref, wg_ref, wu_ref, wd_ref):
    h = _rms(x, g_ref[...]).astype(BF16)
    acc = jnp.zeros(x.shape, F32)
    for c in range(D_FF // FFN_TF):
        sl = slice(c * FFN_TF, (c + 1) * FFN_TF)
        gate = _dot(h, wg_ref[:, sl])
        up = _dot(h, wu_ref[:, sl])
        act = (gate * jax.nn.sigmoid(gate) * up).astype(BF16)
        acc = acc + _dot(act, wd_ref[sl, :])
    return x + 0.5 * acc


def _ffn_body(x_ref, g_ref, wg_ref, wu_ref, wd_ref, o_ref):
    o_ref[...] = _swiglu_half_step(x_ref[...], g_ref, wg_ref, wu_ref, wd_ref)


def _mix_out_ffn_body(ymix_ref, ymem_ref, x_ref, wmix_ref, wmem_ref, g_ref, wg_ref, wu_ref, wd_ref, o_ref):
    x = x_ref[...] + _dot(ymix_ref[...], wmix_ref[...]) + _dot(ymem_ref[...], wmem_ref[...])
    o_ref[...] = _swiglu_half_step(x, g_ref, wg_ref, wu_ref, wd_ref)


def _ffn_specs(d, layer, half):
    pick = lambda r, c: pl.BlockSpec((None, None, r, c), lambda i: (layer, half, 0, 0), pipeline_mode=pl.Buffered(1))
    return [_const_spec((1, d)), pick(d, D_FF), pick(d, D_FF), pick(D_FF, d)]


def _ffn(x, g, wg, wu, wd, layer, half):
    t, d = x.shape
    tm = min(FFN_TM, t)
    return pl.pallas_call(
        _ffn_body,
        out_shape=jax.ShapeDtypeStruct((t, d), F32),
        grid=(t // tm,),
        in_specs=[pl.BlockSpec((tm, d), lambda i: (i, 0))] + _ffn_specs(d, layer, half),
        out_specs=pl.BlockSpec((tm, d), lambda i: (i, 0)),
        compiler_params=_cparams("parallel"),
        name="ffn",
    )(x, g.reshape(1, d), wg, wu, wd)


def _mix_out_ffn(y_mix, y_mem, x, w_out, g, wg, wu, wd, layer, half):
    t, d = x.shape
    tm = min(FFN_TM, t)
    return pl.pallas_call(
        _mix_out_ffn_body,
        out_shape=jax.ShapeDtypeStruct((t, d), F32),
        grid=(t // tm,),
        in_specs=[
            pl.BlockSpec((tm, MIX_W), lambda i: (i, 0)),
            pl.BlockSpec((tm, MEM_W), lambda i: (i, 0)),
            pl.BlockSpec((tm, d), lambda i: (i, 0)),
            _const_spec((MIX_W, d)),
            _const_spec((MEM_W, d)),
        ] + _ffn_specs(d, layer, half),
        out_specs=pl.BlockSpec((tm, d), lambda i: (i, 0)),
        compiler_params=_cparams("parallel"),
        name="mix_out_ffn",
    )(y_mix, y_mem, x, w_out[:MIX_W].astype(BF16), w_out[MIX_W:].astype(BF16), g.reshape(1, d), wg, wu, wd)


PROJ_TM = 1024


def _mem_kv_body(m_ref, g_ref, wk_ref, wvT_ref, kn_ref, k_ref, vT_ref):
    h = _rms(m_ref[0], g_ref[...]).astype(BF16)
    kT = _dot_nt(wk_ref[...], h)
    kT = _heads_norm_rope(kT, _tile_lanes(kn_ref[...], kT.shape[-1]), None, None)
    k_ref[0] = kT.T.astype(BF16)
    vT_ref[0] = _augment_values(_dot_nt(wvT_ref[...], h)).astype(BF16)


def _mem_kv(mem, g, w_kv, kn):
    b, m, d = mem.shape
    return pl.pallas_call(
        _mem_kv_body,
        out_shape=(jax.ShapeDtypeStruct((b, m, MEM_W), BF16), jax.ShapeDtypeStruct((b, H_MEM * VAUG, m), BF16)),
        grid=(b,),
        in_specs=[
            pl.BlockSpec((1, m, d), lambda i: (i, 0, 0)),
            _const_spec((1, d)),
            _const_spec((MEM_W, d)),
            _const_spec((MEM_W, d)),
            _const_spec((HD, LANES)),
        ],
        out_specs=(pl.BlockSpec((1, m, MEM_W), lambda i: (i, 0, 0)),
                   pl.BlockSpec((1, H_MEM * VAUG, m), lambda i: (i, 0, 0))),
        compiler_params=_cparams("parallel"),
        name="mem_kv",
    )(mem, g.reshape(1, d), w_kv[:, :MEM_W].T.astype(BF16), w_kv[:, MEM_W:].T.astype(BF16), _lane_gain(kn))


MEM_TQ = 1024


def _pad_rows(blk, half, total=LANES):
    z = jnp.zeros_like(blk)
    parts = [z] * (total // HD)
    parts[half] = blk
    return jnp.concatenate(parts, axis=0)


def _mem_attn_body(qT_ref, k_ref, vT_ref, o_ref, s_ref, mc_ref, m_ref, acc_ref):
    n_mem = k_ref.shape[1]
    outs = _staged_attention(
        H_MEM, 0, 1, n_mem,
        lambda h, r0: _dot(k_ref[0, :, (h // 2) * LANES:(h // 2 + 1) * LANES],
                           _pad_rows(qT_ref[h * HD:(h + 1) * HD, :], h % 2)),
        lambda h, r0: vT_ref[0, h * VAUG:(h + 1) * VAUG, :], s_ref, mc_ref, m_ref, acc_ref)
    o_ref[...] = jnp.concatenate(outs, axis=0).T.astype(BF16)


def _mem_attn(qmT, k, vT, seq):
    t = qmT.shape[1]
    b, m, _ = k.shape
    tq = min(MEM_TQ, seq)
    nq = seq // tq
    return pl.pallas_call(
        _mem_attn_body,
        out_shape=jax.ShapeDtypeStruct((t, MEM_W), BF16),
        grid=(b, nq),
        in_specs=[
            pl.BlockSpec((MEM_W, tq), lambda bi, i: (0, bi * nq + i)),
            pl.BlockSpec((1, m, MEM_W), lambda bi, i: (bi, 0, 0)),
            pl.BlockSpec((1, H_MEM * VAUG, m), lambda bi, i: (bi, 0, 0)),
        ],
        out_specs=pl.BlockSpec((tq, MEM_W), lambda bi, i: (bi * nq + i, 0)),
        scratch_shapes=[pltpu.VMEM((2, H_MEM, m, tq), F32), pltpu.VMEM((2, H_MEM * SUBLANES, tq), F32),
                        pltpu.VMEM((SUBLANES, tq), F32), pltpu.VMEM((H_MEM * VAUG, tq), F32)],
        compiler_params=_cparams("parallel", "parallel"),
        name="mem_attn",
    )(qmT, k, vT)


def _ret_proj_body(x_ref, g_ref, wtok_ref, wfeat_ref, cosT_ref, sinT_ref, cosF_ref, sinF_ref, mqn_ref,
                   q_ref, v_ref, gate_ref, kT_ref, qmT_ref):
    h = _rms(x_ref[...], g_ref[...]).astype(BF16)
    tok = _dot(h, wtok_ref[...])
    nq, nv = RET_H * RET_DK, RET_H * RET_DV
    q = tok[:, :nq]
    lane = lax.broadcasted_iota(I32, q.shape, 1)
    rot = jnp.where(lane % HD < HALF, pltpu.roll(q, nq - HALF, axis=1), pltpu.roll(q, HALF, axis=1))
    q_ref[...] = (q * _tile_lanes(cosT_ref[...], nq) + rot * _tile_lanes(sinT_ref[...], nq)).astype(BF16)
    v_ref[...] = tok[:, nq:nq + nv].astype(BF16)
    gate_ref[...] = tok[:, nq + nv:]
    feat = _dot_nt(wfeat_ref[...], h)
    tm = feat.shape[-1]
    cos, sin = cosF_ref[...], sinF_ref[...]
    kT = _heads_norm_rope(feat[:nq], None, cos, sin) * (RET_DK ** -0.5)
    kT_ref[...] = kT.astype(BF16)
    qm = _heads_norm_rope(feat[nq:], _tile_lanes(mqn_ref[...], tm), None, None) * QSCALE
    qmT_ref[...] = qm.astype(BF16)


def _token_rope_tables(seq):
    cosF, sinF = _rope_tables(jnp.arange(seq))
    cos = jnp.tile(cosF.T, (1, LANES // HALF))
    sign = jnp.where((jnp.arange(LANES) % HD) < HALF, -1.0, 1.0).astype(F32)
    sin = jnp.tile(sinF.T, (1, LANES // HALF)) * sign[None, :]
    return cos, sin


def _ret_proj(x, g, w_in, mem_qn, seq):
    t, d = x.shape
    tm = min(PROJ_TM, seq)
    npos = seq // tm
    wq, wk, wv, wg, wqm = jnp.split(w_in, np.cumsum(RET_SIZES)[:-1].tolist(), axis=1)
    wtok = jnp.concatenate([wq, wv, wg], axis=1).astype(BF16)
    wfeat = jnp.concatenate([wk, wqm], axis=1).T.astype(BF16)
    cosT, sinT = _token_rope_tables(seq)
    cosF, sinF = _rope_tables(jnp.arange(seq))
    nq, nv = RET_H * RET_DK, RET_H * RET_DV
    row = lambda n: pl.BlockSpec((tm, n), lambda i: (i, 0))
    col = lambda n: pl.BlockSpec((n, tm), lambda i: (0, i))
    return pl.pallas_call(
        _ret_proj_body,
        out_shape=(
            jax.ShapeDtypeStruct((t, nq), BF16), jax.ShapeDtypeStruct((t, nv), BF16),
            jax.ShapeDtypeStruct((t, nv), F32), jax.ShapeDtypeStruct((nq, t), BF16),
            jax.ShapeDtypeStruct((MEM_W, t), BF16),
        ),
        grid=(t // tm,),
        in_specs=[
            row(d), _const_spec((1, d)), _const_spec(wtok.shape), _const_spec(wfeat.shape),
            pl.BlockSpec((tm, LANES), lambda i: (i % npos, 0)), pl.BlockSpec((tm, LANES), lambda i: (i % npos, 0)),
            pl.BlockSpec((HALF, tm), lambda i: (0, i % npos)), pl.BlockSpec((HALF, tm), lambda i: (0, i % npos)),
            _const_spec((HD, LANES)),
        ],
        out_specs=(row(nq), row(nv), row(nv), col(nq), col(MEM_W)),
        compiler_params=_cparams("parallel"),
        name="ret_proj",
    )(x, g.reshape(1, d), wtok, wfeat, cosT, sinT, cosF, sinF, _lane_gain(mem_qn))


RET_TL = 1024


def _ret_body(q_ref, kT_ref, v_ref, gate_ref, y_ref, state_ref):
    @pl.when(pl.program_id(1) == 0)
    def _():
        state_ref[...] = jnp.zeros_like(state_ref)

    c = RET_CHUNK
    ii = lax.broadcasted_iota(I32, (c, c), 0).astype(F32)
    jj = lax.broadcasted_iota(I32, (c, c), 1).astype(F32)
    diff = ii - jj
    jk = lax.broadcasted_iota(I32, (RET_DK, c), 1).astype(F32)
    log_gs = [math.log(1.0 - 2.0 ** (-5.0 - hh)) for hh in range(RET_H)]
    dmasks = [jnp.where(diff >= 0, jnp.exp(lg * jnp.maximum(diff, 0.0)), 0.0) for lg in log_gs]
    xis = [jnp.exp(lg * (ii + 1.0)) for lg in log_gs]
    zetas = [jnp.exp(lg * (c - 1.0 - jk)) for lg in log_gs]
    for n in range(q_ref.shape[0] // c):
        rows = slice(n * c, (n + 1) * c)
        for hh in range(RET_H):
            dmask, xi, zeta, decay = dmasks[hh], xis[hh], zetas[hh], math.exp(log_gs[hh] * c)
            pair = slice((hh // 2) * LANES, (hh // 2 + 1) * LANES)
            q2 = q_ref[rows, pair]
            kT = kT_ref[hh * RET_DK:(hh + 1) * RET_DK, rows]
            v = v_ref[rows, hh * RET_DV:(hh + 1) * RET_DV]
            state = state_ref[hh]
            inner = _dot(q2, _pad_rows(kT, hh % 2)) * dmask
            o = _dot(inner.astype(BF16), v)
            o = o + _dot(q2, _pad_rows(state.astype(BF16), hh % 2)) * xi
            kv = _dot((kT.astype(F32) * zeta).astype(BF16), v)
            state_ref[hh] = state * decay + kv
            mu = jnp.mean(o, axis=-1, keepdims=True)
            var = jnp.mean(jnp.square(o - mu), axis=-1, keepdims=True)
            o = (o - mu) * lax.rsqrt(var + EPS)
            gte = gate_ref[rows, hh * RET_DV:(hh + 1) * RET_DV]
            y_ref[rows, hh * RET_DV:(hh + 1) * RET_DV] = (gte * jax.nn.sigmoid(gte) * o).astype(BF16)


def _retention(q, kT, v, gate, batch, seq):
    t = q.shape[0]
    tl = min(RET_TL, seq)
    nl = seq // tl
    nq, nv = RET_H * RET_DK, RET_H * RET_DV
    return pl.pallas_call(
        _ret_body,
        out_shape=jax.ShapeDtypeStruct((t, nv), BF16),
        grid=(batch, nl),
        in_specs=[
            pl.BlockSpec((tl, nq), lambda b, i: (b * nl + i, 0)),
            pl.BlockSpec((nq, tl), lambda b, i: (0, b * nl + i)),
            pl.BlockSpec((tl, nv), lambda b, i: (b * nl + i, 0)),
            pl.BlockSpec((tl, nv), lambda b, i: (b * nl + i, 0)),
        ],
        out_specs=pl.BlockSpec((tl, nv), lambda b, i: (b * nl + i, 0)),
        scratch_shapes=[pltpu.VMEM((RET_H, RET_DK, RET_DV), F32)],
        compiler_params=_cparams("parallel", "arbitrary"),
        name="retention",
    )(q, kT, v, gate)


def _dsa_proj_body(x_ref, g_ref, w_ref, cos_ref, sin_ref, qn_ref, kn_ref, mqn_ref,
                   kk_ref, qT_ref, iqT_ref, vT_ref, iwT_ref, qmT_ref):
    h = _rms(x_ref[...], g_ref[...]).astype(BF16)
    nq, niq = DSA_H * HD, IDX_H * IDX_D
    q = _dot_nt(w_ref[:nq, :], h)
    iq = _dot_nt(w_ref[nq:nq + niq, :], h)
    feat = _dot_nt(w_ref[nq + niq:, :], h)
    tm = feat.shape[-1]
    cos, sin = cos_ref[...], sin_ref[...]
    o = 0
    k = feat[o:o + HD]; o += HD
    ik = feat[o:o + IDX_D]; o += IDX_D
    v = feat[o:o + HD]; o += HD
    qm = feat[o:o + MEM_W]; o += MEM_W
    iw = feat[o:o + IDX_H]
    qT_ref[...] = (_heads_norm_rope(q, _tile_lanes(qn_ref[...], tm), cos, sin) * QSCALE).astype(BF16)
    iqT_ref[...] = _heads_norm_rope(iq, None, cos, sin).astype(BF16)
    k = _heads_norm_rope(k, _tile_lanes(kn_ref[...], tm), cos, sin)
    ik = _heads_norm_rope(ik, None, cos, sin)
    kk_ref[...] = jnp.concatenate([k, ik], axis=0).T.astype(BF16)
    vT_ref[...] = _augment_values(v).astype(BF16)
    iwT_ref[...] = iw
    qmT_ref[...] = (_heads_norm_rope(qm, _tile_lanes(mqn_ref[...], tm), None, None) * QSCALE).astype(BF16)


def _dsa_proj(x, g, w_in, qn, kn, mem_qn, seq):
    t, d = x.shape
    tm = min(PROJ_TM, seq)
    npos = seq // tm
    wq, wk, wv, wiq, wik, wiw, wqm = jnp.split(w_in, np.cumsum(DSA_SIZES)[:-1].tolist(), axis=1)
    wfeat = jnp.concatenate([wq, wiq, wk, wik, wv, wqm, wiw], axis=1).T.astype(BF16)
    cosF, sinF = _rope_tables(jnp.arange(seq))
    col = lambda n: pl.BlockSpec((n, tm), lambda i: (0, i))
    tab = pl.BlockSpec((HALF, tm), lambda i: (0, i % npos))
    gain = _const_spec((HD, LANES))
    return pl.pallas_call(
        _dsa_proj_body,
        out_shape=(
            jax.ShapeDtypeStruct((t, LANES), BF16), jax.ShapeDtypeStruct((DSA_H * HD, t), BF16),
            jax.ShapeDtypeStruct((IDX_H * IDX_D, t), BF16), jax.ShapeDtypeStruct((VAUG, t), BF16),
            jax.ShapeDtypeStruct((IDX_H, t), F32), jax.ShapeDtypeStruct((MEM_W, t), BF16),
        ),
        grid=(t // tm,),
        in_specs=[pl.BlockSpec((tm, d), lambda i: (i, 0)), _const_spec((1, d)), _const_spec(wfeat.shape),
                  tab, tab, gain, gain, gain],
        out_specs=(pl.BlockSpec((tm, LANES), lambda i: (i, 0)), col(DSA_H * HD), col(IDX_H * IDX_D), col(VAUG),
                   col(IDX_H), col(MEM_W)),
        compiler_params=_cparams("parallel"),
        name="dsa_proj",
    )(x, g.reshape(1, d), wfeat, cosF, sinF, _lane_gain(qn), _lane_gain(kn), _lane_gain(mem_qn))


DSA_TQ = 256
DSA_TK = 256
HALF_BITS = 16
INT16_MIN = -2 ** (HALF_BITS - 1)


def _dsa_attn_body(kk_ref, qT_ref, iqT_ref, vT_ref, iwT_ref, y_ref, keys_ref, hi_ref, lo_ref, bias_ref, s_ref, mc_ref,
                   m_ref, acc_ref, *, topk, seq):
    tq, tk = DSA_TQ, DSA_TK
    qs = pl.program_id(1) * tq
    nkc = (qs + tq) // tk
    qpos = qs + lax.broadcasted_iota(I32, (1, tq), 1)
    row = lax.broadcasted_iota(I32, (tk, tq), 0)

    def score_chunk(c, carry):
        r0 = pl.multiple_of(c * tk, tk)
        kkc = kk_ref[pl.ds(r0, tk), :]
        acc = jnp.zeros((tk, tq), F32)
        for h in range(IDX_H):
            r = _dot(kkc, _pad_rows(iqT_ref[h * IDX_D:(h + 1) * IDX_D, :], 1))
            acc = acc + jnp.maximum(r, 0.0) * iwT_ref[h:h + 1, :]
        sc = jnp.where(r0 + row <= qpos, acc + 0.0, -jnp.inf)
        bits = pltpu.bitcast(sc, I32)
        key = jnp.where(bits < 0, bits ^ jnp.int32(0x7FFFFFFF), bits)
        keys_ref[pl.ds(r0, tk), :] = key
        hi_ref[pl.ds(r0, tk), :] = lax.shift_right_arithmetic(key, HALF_BITS).astype(I16)
        return carry

    lax.fori_loop(0, nkc, score_chunk, 0)

    def count(pred):
        def body(c, acc):
            r0 = pl.multiple_of(c * tk, tk)
            m = pred(keys_ref[pl.ds(r0, tk), :], r0 + row)
            return acc + jnp.sum(m.astype(I32).reshape(tk // SUBLANES, SUBLANES, tq), axis=0)
        acc = lax.fori_loop(0, nkc, body, jnp.zeros((SUBLANES, tq), I32))
        return jnp.sum(acc, axis=0, keepdims=True)

    def fold16(mask):
        ones = jnp.where(mask, jnp.int16(1), jnp.int16(0))
        parts = [ones[i:i + PACKED_SUBLANES] for i in range(0, tk, PACKED_SUBLANES)]
        while len(parts) > 1:
            parts = [a + b for a, b in zip(parts[::2], parts[1::2])]
        return parts[0]

    zeros16 = jnp.zeros((PACKED_SUBLANES, tq), I16)
    total16 = lambda acc: jnp.sum(acc.astype(I32), axis=0, keepdims=True)

    def count16(ref, pred):
        def body(c, acc):
            return acc + fold16(pred(ref[pl.ds(pl.multiple_of(c * tk, tk), tk), :]))
        return total16(lax.fori_loop(0, nkc, body, zeros16))

    def kth_largest16(ref, k):
        thr = jnp.where(count16(ref, lambda v: v >= 0) >= k, jnp.int32(0), jnp.int32(INT16_MIN))

        def bit_step(b, thr):
            cand = thr | lax.shift_left(jnp.int32(1), (HALF_BITS - 2) - b)
            c16 = cand.astype(I16)
            return jnp.where(count16(ref, lambda v: v >= c16) >= k, cand, thr)

        return lax.fori_loop(0, HALF_BITS - 1, bit_step, thr)

    thr_hi = kth_largest16(hi_ref, topk)
    thr_hi16 = thr_hi.astype(I16)

    def low_half_chunk(c, above):
        r0 = pl.multiple_of(c * tk, tk)
        hi = hi_ref[pl.ds(r0, tk), :]
        lo = ((keys_ref[pl.ds(r0, tk), :] & jnp.int32(0xFFFF)) + jnp.int32(INT16_MIN)).astype(I16)
        lo_ref[pl.ds(r0, tk), :] = jnp.where(hi == thr_hi16, lo, jnp.int16(INT16_MIN))
        return above + fold16(hi > thr_hi16)

    n_above = total16(lax.fori_loop(0, nkc, low_half_chunk, zeros16))
    need_lo = topk - n_above
    thr_lo = kth_largest16(lo_ref, need_lo)
    thr = lax.shift_left(thr_hi, HALF_BITS) | (thr_lo - jnp.int32(INT16_MIN))
    def tie_search():
        need = topk - count(lambda k, _: k > thr)

        def idx_step(b, q):
            cand = q | lax.shift_left(jnp.int32(1), (seq.bit_length() - 2) - b)
            return jnp.where(count(lambda k, idx: (k == thr) & (idx < cand)) < need, cand, q)

        return lax.fori_loop(0, seq.bit_length() - 1, idx_step, jnp.zeros((1, tq), I32))

    thr_lo16 = thr_lo.astype(I16)
    has_ties = jnp.max(n_above + count16(lo_ref, lambda v: v >= thr_lo16)) > topk
    last = lax.cond(has_ties, tie_search, lambda: jnp.full((1, tq), seq, I32))

    def bias_chunk(c, carry):
        r0 = pl.multiple_of(c * tk, tk)
        k = keys_ref[pl.ds(r0, tk), :]
        idx = r0 + row
        sel = (k > thr) | ((k == thr) & (idx <= last))
        bias_ref[pl.ds(r0, tk), :] = jnp.where(sel & (idx <= qpos), 0.0, NEG)
        return carry

    lax.fori_loop(0, nkc, bias_chunk, 0)

    outs = _staged_attention(
        DSA_H, 0, nkc, tk,
        lambda h, r0: _dot(kk_ref[pl.ds(r0, tk), :], _pad_rows(qT_ref[h * HD:(h + 1) * HD, :], 0))
        + bias_ref[pl.ds(r0, tk), :],
        lambda h, r0: vT_ref[:, pl.ds(r0, tk)], s_ref, mc_ref, m_ref, acc_ref)
    y_ref[...] = jnp.concatenate(outs, axis=0).T.astype(BF16)


def _dsa_attn(kk, qT, iqT, vT, iwT, batch, seq):
    t = kk.shape[0]
    tq = DSA_TQ
    nq = seq // tq
    topk = min(DSA_TOPK_MAX, seq // 4)
    colq = lambda n: pl.BlockSpec((n, tq), lambda b, i: (0, b * nq + i))
    return pl.pallas_call(
        functools.partial(_dsa_attn_body, topk=topk, seq=seq),
        out_shape=jax.ShapeDtypeStruct((t, DSA_H * HD), BF16),
        grid=(batch, nq),
        in_specs=[
            pl.BlockSpec((seq, LANES), lambda b, i: (b, 0)),
            colq(DSA_H * HD), colq(IDX_H * IDX_D),
            pl.BlockSpec((VAUG, seq), lambda b, i: (0, b)),
            colq(IDX_H),
        ],
        out_specs=pl.BlockSpec((tq, DSA_H * HD), lambda b, i: (b * nq + i, 0)),
        scratch_shapes=[pltpu.VMEM((seq, tq), I32), pltpu.VMEM((seq, tq), I16), pltpu.VMEM((seq, tq), I16),
                        pltpu.VMEM((seq, tq), F32),
                        pltpu.VMEM((2, DSA_H, DSA_TK, tq), F32), pltpu.VMEM((2, DSA_H * SUBLANES, tq), F32),
                        pltpu.VMEM((2 * SUBLANES, tq), F32), pltpu.VMEM((DSA_H * VAUG, tq), F32)],
        compiler_params=_cparams("parallel", "arbitrary"),
        name="dsa_attn",
    )(kk, qT, iqT, vT, iwT)


NSA_KV = NSA_G * HD
NSA_GATES = NSA_H * 3
NSA_GATES_PAD = -NSA_GATES % SUBLANES


def _nsa_proj_body(x_ref, g_ref, wtok_ref, wfeat_ref, cos_ref, sin_ref, qn_ref, kns_ref, knw_ref, mqn_ref,
                   kc_ref, vc_ref, ks_ref, kw_ref, qT_ref, vsT_ref, vwT_ref, gT_ref, qmT_ref):
    h = _rms(x_ref[...], g_ref[...]).astype(BF16)
    tok = _dot(h, wtok_ref[...])
    kc_ref[...] = tok[:, :NSA_KV]
    vc_ref[...] = tok[:, NSA_KV:]
    nq = NSA_H * HD
    q = _dot_nt(wfeat_ref[:nq, :], h)
    kk = _dot_nt(wfeat_ref[nq:nq + 2 * NSA_KV, :], h)
    feat = _dot_nt(wfeat_ref[nq + 2 * NSA_KV:, :], h)
    tm = feat.shape[-1]
    cos, sin = cos_ref[...], sin_ref[...]
    ks, kw = kk[:NSA_KV], kk[NSA_KV:]
    o = 0
    vs = feat[o:o + NSA_KV]; o += NSA_KV
    vw = feat[o:o + NSA_KV]; o += NSA_KV
    qm = feat[o:o + MEM_W]; o += MEM_W
    gates = feat[o:]
    qT_ref[...] = (_heads_norm_rope(q, _tile_lanes(qn_ref[...], tm), cos, sin) * QSCALE).astype(BF16)
    ks_ref[...] = _heads_norm_rope(ks, _tile_lanes(kns_ref[...], tm), cos, sin).T.astype(BF16)
    kw_ref[...] = _heads_norm_rope(kw, _tile_lanes(knw_ref[...], tm), cos, sin).T.astype(BF16)
    vsT_ref[...] = _augment_values(vs).astype(BF16)
    vwT_ref[...] = _augment_values(vw).astype(BF16)
    gT_ref[...] = jax.nn.sigmoid(gates)
    qmT_ref[...] = (_heads_norm_rope(qm, _tile_lanes(mqn_ref[...], tm), None, None) * QSCALE).astype(BF16)


def _nsa_proj(x, g, w_in, qn, kn, mem_qn, seq):
    t, d = x.shape
    tm = min(PROJ_TM, seq)
    npos = seq // tm
    wq, wkc, wvc, wks, wvs, wkw, wvw, wgt, wqm = jnp.split(w_in, np.cumsum(NSA_SIZES)[:-1].tolist(), axis=1)
    wtok = jnp.concatenate([wkc, wvc], axis=1).astype(BF16)
    wgt = jnp.pad(wgt, ((0, 0), (0, NSA_GATES_PAD)))
    wfeat = jnp.concatenate([wq, wks, wkw, wvs, wvw, wqm, wgt], axis=1).T.astype(BF16)
    cosF, sinF = _rope_tables(jnp.arange(seq))
    row = lambda n: pl.BlockSpec((tm, n), lambda i: (i, 0))
    col = lambda n: pl.BlockSpec((n, tm), lambda i: (0, i))
    tab = pl.BlockSpec((HALF, tm), lambda i: (0, i % npos))
    gain = _const_spec((HD, LANES))
    ngt = NSA_GATES + NSA_GATES_PAD
    return pl.pallas_call(
        _nsa_proj_body,
        out_shape=(
            jax.ShapeDtypeStruct((t, NSA_KV), F32), jax.ShapeDtypeStruct((t, NSA_KV), F32),
            jax.ShapeDtypeStruct((t, NSA_KV), BF16), jax.ShapeDtypeStruct((t, NSA_KV), BF16),
            jax.ShapeDtypeStruct((NSA_H * HD, t), BF16), jax.ShapeDtypeStruct((NSA_G * VAUG, t), BF16),
            jax.ShapeDtypeStruct((NSA_G * VAUG, t), BF16), jax.ShapeDtypeStruct((ngt, t), F32),
            jax.ShapeDtypeStruct((MEM_W, t), BF16),
        ),
        grid=(t // tm,),
        in_specs=[row(d), _const_spec((1, d)), _const_spec(wtok.shape), _const_spec(wfeat.shape),
                  tab, tab, gain, gain, gain, gain],
        out_specs=(row(NSA_KV), row(NSA_KV), row(NSA_KV), row(NSA_KV), col(NSA_H * HD), col(NSA_G * VAUG),
                   col(NSA_G * VAUG), col(ngt), col(MEM_W)),
        compiler_params=_cparams("parallel"),
        name="nsa_proj",
    )(x, g.reshape(1, d), wtok, wfeat, cosF, sinF, _lane_gain(qn), _lane_gain(kn[1]), _lane_gain(kn[2]),
      _lane_gain(mem_qn))


def _nsa_cmp_body(k01_ref, k23_ref, v01_ref, v23_ref, pk_ref, pv_ref, wk_ref, wv_ref, kn_ref, cos_ref, sin_ref,
                  k_ref, vT_ref):
    n = k_ref.shape[1]

    def compress(lo_ref, hi_ref, pos, w_ref):
        parts = []
        for l in range(CMP_S):
            parts += [lo_ref[pl.ds(l, n, stride=CMP_S), :], hi_ref[pl.ds(l, n, stride=CMP_S), :]]
        x = jnp.concatenate(parts, axis=1)
        xa = (x + pos[0:1]).astype(BF16)
        xb = (pltpu.roll(x, n - 1, axis=0) + pos[1:2]).astype(BF16)
        return _dot_nt(w_ref[0], xa) + _dot_nt(w_ref[1], xb)

    kT = compress(k01_ref, k23_ref, pk_ref[...], wk_ref)
    kT = _heads_norm_rope(kT, _tile_lanes(kn_ref[...], kT.shape[-1]), cos_ref[...], sin_ref[...])
    k_ref[0] = kT.T.astype(BF16)
    vT_ref[0] = compress(v01_ref, v23_ref, pv_ref[...], wv_ref).astype(BF16)


def _nsa_cmp_weights(w, pos):
    halves, width = CMP_L // CMP_S, CMP_S * NSA_KV
    w_eld = w.reshape(halves, CMP_S, HD, HD).transpose(0, 3, 1, 2)
    per_group = jnp.broadcast_to(w_eld[:, :, :, None, :], (halves, HD, CMP_S, NSA_G, HD)).reshape(halves, HD, width)
    rows = np.arange(NSA_KV)[:, None] // HD
    cols = (np.arange(width)[None, :] // HD) % NSA_G
    wt = jnp.where(jnp.asarray(rows == cols)[None], jnp.tile(per_group, (1, NSA_G, 1)), 0.0)
    p = jnp.broadcast_to(pos.reshape(halves, CMP_S, 1, HD), (halves, CMP_S, NSA_G, HD))
    return wt.astype(BF16), p.reshape(halves, width).astype(F32)


def _nsa_cmp(kc, vc, pos_k, pos_v, wk, wv, kn0, batch, seq):
    n = seq // CMP_S
    wkt, pk = _nsa_cmp_weights(wk, pos_k)
    wvt, pv = _nsa_cmp_weights(wv, pos_v)
    cosE, sinE = _rope_tables(jnp.arange(n) * CMP_S + (CMP_L - 1))
    lanes_lo = pl.BlockSpec((seq, LANES), lambda b: (b, 0))
    lanes_hi = pl.BlockSpec((seq, LANES), lambda b: (b, 1))
    return pl.pallas_call(
        _nsa_cmp_body,
        out_shape=(jax.ShapeDtypeStruct((batch, n, NSA_KV), BF16), jax.ShapeDtypeStruct((batch, NSA_KV, n), BF16)),
        grid=(batch,),
        in_specs=[lanes_lo, lanes_hi, lanes_lo, lanes_hi, _const_spec(pk.shape), _const_spec(pv.shape),
                  _const_spec(wkt.shape), _const_spec(wvt.shape), _const_spec((HD, LANES)), _const_spec((HALF, n)),
                  _const_spec((HALF, n))],
        out_specs=(pl.BlockSpec((1, n, NSA_KV), lambda b: (b, 0, 0)), pl.BlockSpec((1, NSA_KV, n), lambda b: (b, 0, 0))),
        compiler_params=_cparams("parallel"),
        name="nsa_cmp",
    )(kc, kc, vc, vc, pk, pv, wkt, wvt, _lane_gain(kn0), cosE, sinE)


NSA_TQ = 256
NSA_TK = 256


def _staged_attention(nheads, lo, hi, tk, logits, values, s_ref, mc_ref, m_ref, acc_ref):
    tq = s_ref.shape[-1]
    fold = lambda a: a.reshape(tk // SUBLANES, SUBLANES, tq)
    m_ref[...] = jnp.full(m_ref.shape, -jnp.inf, F32)
    acc_ref[...] = jnp.zeros(acc_ref.shape, F32)

    start = lambda c: c * tk if isinstance(c, int) else pl.multiple_of(c * tk, tk)

    def stage(h, c, slot):
        s = logits(h, start(c))
        s_ref[slot, h] = s
        mc_ref[slot, h * SUBLANES:(h + 1) * SUBLANES, :] = jnp.max(fold(s), axis=0)

    def consume(h, c, slot):
        g8, hrows = slice(h * SUBLANES, (h + 1) * SUBLANES), slice(h * VAUG, (h + 1) * VAUG)
        m_old = m_ref[h:h + 1, :]
        m_new = jnp.maximum(m_old, jnp.max(mc_ref[slot, g8, :], axis=0, keepdims=True))
        alpha = jnp.exp2(m_old - m_new)
        p = jnp.exp2(s_ref[slot, h] - m_new)
        acc_ref[hrows, :] = acc_ref[hrows, :] * alpha + _dot(values(h, start(c)), p.astype(BF16))
        m_ref[h:h + 1, :] = m_new

    def consume_and_stage_next(c, slot):
        for h in range(nheads):
            consume(h, c, slot)
            stage(h, c + 1, 1 - slot)

    def finish(c, slot):
        for h in range(nheads):
            consume(h, c, slot)

    for h in range(nheads):
        stage(h, lo, 0)
    n_fused = hi - 1 - lo

    def two_chunks(j, carry):
        c = lo + 2 * j
        consume_and_stage_next(c, 0)
        consume_and_stage_next(c + 1, 1)
        return carry

    if isinstance(n_fused, int):
        for i in range(n_fused):
            consume_and_stage_next(lo + i, i % 2)
        finish(hi - 1, n_fused % 2)
    else:
        lax.fori_loop(0, n_fused // 2, two_chunks, 0)

        @pl.when(n_fused % 2 == 1)
        def _():
            consume_and_stage_next(hi - 2, 0)
            finish(hi - 1, 1)

        @pl.when(n_fused % 2 == 0)
        def _():
            finish(hi - 1, 0)

    return [acc_ref[h * VAUG:h * VAUG + HD, :] / acc_ref[h * VAUG + HD:h * VAUG + HD + 1, :] for h in range(nheads)]


def _nsa_attn_body(qT_ref, gT_ref, ks_ref, vsT_ref, kw_ref, vwT_ref, kc_ref, vcT_ref, cov_ref, y_ref,
                   bias_ref, out_ref, s_ref, mc_ref, m_ref, acc_ref, *, seq):
    tq, tk = NSA_TQ, NSA_TK
    qs = pl.program_id(1) * tq
    nkc = (qs + tq) // tk
    wlo = jnp.maximum((qs - WIN) // tk, 0)
    qpos = qs + lax.broadcasted_iota(I32, (1, tq), 1)
    ncr = seq // CMP_S
    nblk = seq // SLC_L
    n_sel = min(SLC_N_MAX, nblk)
    valid_c = CMP_S * lax.broadcasted_iota(I32, (ncr, tq), 0) + (CMP_L - 1) <= qpos
    any_c = jnp.where(qpos >= CMP_L - 1, 1.0, 0.0)
    jrow = lax.broadcasted_iota(I32, (nblk, tq), 0)
    qblk = qpos // SLC_L
    forced = (jrow == 0) | (jrow == qblk) | (jrow == qblk - 1)
    causal_blk = jrow * SLC_L <= qpos
    krow = lax.broadcasted_iota(I32, (tk, tq), 0)
    brow = lax.broadcasted_iota(I32, (SLC_L, tq), 0)
    group = lambda h: h // NSA_R
    pair = lambda h: slice((group(h) // 2) * LANES, (group(h) // 2 + 1) * LANES)
    grows = lambda h: slice(group(h) * HD, (group(h) + 1) * HD)
    vrows = lambda h: slice(group(h) * VAUG, (group(h) + 1) * VAUG)
    q_of = lambda h: _pad_rows(qT_ref[h * HD:(h + 1) * HD, :], group(h) % 2)
    gate = lambda h, branch: gT_ref[h * 3 + branch:h * 3 + branch + 1, :]

    selbs = []
    for g in range(NSA_G):
        heads = [g * NSA_R + r for r in range(NSA_R)]
        kcm = kc_ref[0][:, pair(heads[0])]
        vcm = vcT_ref[0][grows(heads[0]), :]
        psum = jnp.zeros((ncr, tq), F32)
        for h in heads:
            s = jnp.where(valid_c, _dot(kcm, q_of(h)), NEG)
            p = jnp.exp2(s - jnp.max(s, axis=0, keepdims=True))
            p = p / jnp.sum(p, axis=0, keepdims=True) * any_c
            out_ref[h * HD:(h + 1) * HD, :] = gate(h, 0) * _dot(vcm, p.astype(BF16))
            psum = psum + p
        p_hi = psum.astype(BF16)
        p_lo = (psum - p_hi.astype(F32)).astype(BF16)
        imp = _dot(cov_ref[...], p_hi) + _dot(cov_ref[...], p_lo)
        imp = jnp.where(causal_blk, imp + jnp.where(forced, FORCE, 0.0), NEG)
        rank = jnp.zeros((nblk, tq), I32)
        for j2 in range(nblk):
            rj = imp[j2:j2 + 1]
            beats = (rj > imp) | ((rj == imp) & (j2 < jrow))
            rank = rank + beats.astype(I32)
        selbs.append(jnp.where(rank < n_sel, 0.0, NEG))

    blocks_per_chunk = tk // SLC_L
    for c in range(seq // tk):
        @pl.when(c < nkc)
        def _(c=c):
            for g in range(NSA_G):
                for j in range(c * blocks_per_chunk, (c + 1) * blocks_per_chunk):
                    bias_ref[g, j * SLC_L:(j + 1) * SLC_L, :] = jnp.where(j * SLC_L + brow <= qpos,
                                                                          selbs[g][j:j + 1], NEG)

    stats = (s_ref, mc_ref, m_ref, acc_ref)
    slc = _staged_attention(
        NSA_H, 0, nkc, tk,
        lambda h, r0: _dot(ks_ref[pl.ds(r0, tk), pair(h)], q_of(h)) + bias_ref[group(h), pl.ds(r0, tk), :],
        lambda h, r0: vsT_ref[vrows(h), pl.ds(r0, tk)], *stats)
    for h in range(NSA_H):
        out_ref[h * HD:(h + 1) * HD, :] = out_ref[h * HD:(h + 1) * HD, :] + gate(h, 1) * slc[h]

    def win_mask_chunk(c, carry):
        r0 = pl.multiple_of(c * tk, tk)
        dist = qpos - (r0 + krow)
        bias_ref[0, pl.ds(r0, tk), :] = jnp.where((dist >= 0) & (dist < WIN), 0.0, NEG)
        return carry

    lax.fori_loop(wlo, nkc, win_mask_chunk, 0)
    win = _staged_attention(
        NSA_H, wlo, nkc, tk,
        lambda h, r0: _dot(kw_ref[pl.ds(r0, tk), pair(h)], q_of(h)) + bias_ref[0, pl.ds(r0, tk), :],
        lambda h, r0: vwT_ref[vrows(h), pl.ds(r0, tk)], *stats)
    outs = [out_ref[h * HD:(h + 1) * HD, :] + gate(h, 2) * win[h] for h in range(NSA_H)]
    y_ref[...] = jnp.concatenate(outs, axis=0).T.astype(BF16)


def _nsa_attn(qT, gT, ks, vsT, kw, vwT, kcmp, vcmpT, batch, seq):
    t = ks.shape[0]
    tq = NSA_TQ
    nq = seq // tq
    ncr, nblk = seq // CMP_S, seq // SLC_L
    starts = np.arange(ncr) * CMP_S
    sstart = np.arange(nblk) * SLC_L
    cover = (starts[None, :] < sstart[:, None] + SLC_L) & (starts[None, :] + CMP_L > sstart[:, None])
    cover[:, ncr - 1] = False
    colq = lambda n: pl.BlockSpec((n, tq), lambda b, i: (0, b * nq + i))
    tok = pl.BlockSpec((seq, NSA_KV), lambda b, i: (b, 0))
    feat = pl.BlockSpec((NSA_G * VAUG, seq), lambda b, i: (0, b))
    return pl.pallas_call(
        functools.partial(_nsa_attn_body, seq=seq),
        out_shape=jax.ShapeDtypeStruct((t, NSA_H * HD), BF16),
        grid=(batch, nq),
        in_specs=[
            colq(NSA_H * HD), colq(gT.shape[0]), tok, feat, tok, feat,
            pl.BlockSpec((1, ncr, NSA_KV), lambda b, i: (b, 0, 0)),
            pl.BlockSpec((1, NSA_KV, ncr), lambda b, i: (b, 0, 0)),
            _const_spec((nblk, ncr)),
        ],
        out_specs=pl.BlockSpec((tq, NSA_H * HD), lambda b, i: (b * nq + i, 0)),
        scratch_shapes=[pltpu.VMEM((NSA_G, seq, tq), F32), pltpu.VMEM((NSA_H * HD, tq), F32),
                        pltpu.VMEM((2, NSA_H, NSA_TK, tq), F32), pltpu.VMEM((2, NSA_H * SUBLANES, tq), F32),
                        pltpu.VMEM((2 * SUBLANES, tq), F32), pltpu.VMEM((NSA_H * VAUG, tq), F32)],
        compiler_params=_cparams("parallel", "arbitrary"),
        name="nsa_attn",
    )(qT, gT, ks, vsT, kw, vwT, kcmp, vcmpT, jnp.asarray(cover, BF16))


def _nsa_layer_mix(x, g, w_in, qn, kn, pos_k, pos_v, wk, wv, mem_qn, batch, seq):
    kc, vc, ks, kw, qT, vsT, vwT, gT, qmT = _nsa_proj(x, g, w_in, qn, kn, mem_qn, seq)
    kcmp, vcmpT = _nsa_cmp(kc, vc, pos_k, pos_v, wk, wv, kn[0], batch, seq)
    return _nsa_attn(qT, gT, ks, vsT, kw, vwT, kcmp, vcmpT, batch, seq), qmT


def kernel(x, mem, ffn_norm, ffn_w_gate, ffn_w_up, ffn_w_down, mix_norm, w_out, mem_norm, mem_w_kv, mem_qn, mem_kn, ret_w_in, dsa_w_in, dsa_qn, dsa_kn, nsa_w_in, nsa_qn, nsa_kn, nsa_cmp_pos_k, nsa_cmp_pos_v, nsa_cmp_wk, nsa_cmp_wv):
    batch, seq, d = x.shape
    x = x.reshape(batch * seq, d)
    ffn_w = (ffn_w_gate.astype(BF16), ffn_w_up.astype(BF16), ffn_w_down.astype(BF16))
    for i in range(ffn_norm.shape[0]):
        x = _ffn(x, ffn_norm[i, 0], *ffn_w, i, 0)
        kind, j = i % N_MIXERS, i // N_MIXERS
        if kind == 0:
            q, v, gate, kT, qmT = _ret_proj(x, mix_norm[i], ret_w_in[j], mem_qn[i], seq)
            y_mix = _retention(q, kT, v, gate, batch, seq)
        elif kind == 1:
            kk, qT, iqT, vT, iwT, qmT = _dsa_proj(x, mix_norm[i], dsa_w_in[j], dsa_qn[j], dsa_kn[j], mem_qn[i], seq)
            y_mix = _dsa_attn(kk, qT, iqT, vT, iwT, batch, seq)
        else:
            y_mix, qmT = _nsa_layer_mix(x, mix_norm[i], nsa_w_in[j], nsa_qn[j], nsa_kn[j], nsa_cmp_pos_k[j],
                                        nsa_cmp_pos_v[j], nsa_cmp_wk[j], nsa_cmp_wv[j], mem_qn[i], batch, seq)
        mem_k, mem_vT = _mem_kv(mem, mem_norm[i], mem_w_kv[i], mem_kn[i])
        y_mem = _mem_attn(qmT, mem_k, mem_vT, seq)
        x = _mix_out_ffn(y_mix, y_mem, x, w_out[i], ffn_norm[i, 1], *ffn_w, i, 1)
    return x.reshape(batch, seq, d)
```

```python
import functools
import math

import jax
import jax.numpy as jnp
import numpy as np
from jax import lax
from jax.experimental import pallas as pl
from jax.experimental.pallas import tpu as pltpu

D_MODEL = 1024
HD = 64
HALF = HD // 2
H_MIX = 12
H_MEM = 4
MIX_W = H_MIX * HD
MEM_W = H_MEM * HD
D_FF = 2816
ROPE_THETA = 10000.0
EPS = 1e-6
NEG = -1e30
FORCE = 1e9
SCALE = HD ** -0.5
QSCALE = SCALE * math.log2(math.e)

RET_H, RET_DK, RET_DV, RET_CHUNK = 6, 64, 128, 128
DSA_H, IDX_H, IDX_D, DSA_TOPK_MAX = 12, 8, 64, 256
NSA_H, NSA_G, CMP_L, CMP_S, SLC_L, SLC_N_MAX, WIN = 12, 4, 32, 16, 64, 16, 512
NSA_R = NSA_H // NSA_G
N_MIXERS = 3

RET_SIZES = [RET_H * RET_DK, RET_H * RET_DK, RET_H * RET_DV, RET_H * RET_DV, MEM_W]
DSA_SIZES = [DSA_H * HD, HD, HD, IDX_H * IDX_D, IDX_D, IDX_H, MEM_W]
NSA_SIZES = [NSA_H * HD] + [NSA_G * HD] * 6 + [NSA_H * 3, MEM_W]

LANES = 128
SUBLANES = 8
PACKED_SUBLANES = 2 * SUBLANES
VAUG = HD + PACKED_SUBLANES
MXU_N = 256
VMEM_LIMIT_BYTES = 56 * 1024 * 1024

BF16 = jnp.bfloat16
F32 = jnp.float32
I32 = jnp.int32
I16 = jnp.int16


def _cparams(*sem):
    return pltpu.CompilerParams(dimension_semantics=sem, vmem_limit_bytes=VMEM_LIMIT_BYTES)


def _const_spec(shape):
    n = len(shape)
    return pl.BlockSpec(shape, lambda *_: (0,) * n, pipeline_mode=pl.Buffered(1))


def _rms(x, g):
    return x * lax.rsqrt(jnp.mean(x * x, axis=-1, keepdims=True) + EPS) * g


def _dot(a, b):
    return jnp.dot(a, b, preferred_element_type=F32)


def _dot_nt(a, b):
    return lax.dot_general(a, b, (((1,), (1,)), ((), ())), preferred_element_type=F32)


def _tile_lanes(a, n):
    reps = n // a.shape[-1]
    return a if reps == 1 else jnp.concatenate([a] * reps, axis=-1)


def _rope_tables(pos):
    inv = ROPE_THETA ** (-jnp.arange(HALF, dtype=F32) / HALF)
    ang = pos.astype(F32)[:, None] * inv[None, :]
    return jnp.cos(ang).T, jnp.sin(ang).T


def _lane_gain(g):
    return jnp.broadcast_to(g.astype(F32)[:, None], (g.shape[0], LANES))


def _augment_values(v):
    t = v.shape[-1]
    v3 = v.reshape(v.shape[0] // HD, HD, t)
    tail = jnp.where(lax.broadcasted_iota(I32, (v3.shape[0], VAUG - HD, t), 1) == 0, 1.0, 0.0).astype(v.dtype)
    return jnp.concatenate([v3, tail], axis=1).reshape(v3.shape[0] * VAUG, t)


def _heads_norm_rope(x, gain, cos, sin):
    t = x.shape[-1]
    x3 = x.reshape(x.shape[0] // HD, HD, t)
    if gain is not None:
        ms = jnp.sum(x3 * x3, axis=1, keepdims=True) * (1.0 / HD)
        x3 = x3 * lax.rsqrt(ms + EPS) * gain[None]
    if cos is not None:
        x1, x2 = x3[:, :HALF], x3[:, HALF:]
        x3 = jnp.concatenate([x1 * cos[None] - x2 * sin[None], x2 * cos[None] + x1 * sin[None]], axis=1)
    return x3.reshape(x.shape)


FFN_TM = 1024
FFN_TF = MXU_N


def _swiglu_half_step(x, g_ref, wg_ref, wu_ref, wd_ref):
    h = _rms(x, g_ref[...]).astype(BF16)
    acc = jnp.zeros(x.shape, F32)
    for c in range(D_FF // FFN_TF):
        sl = slice(c * FFN_TF, (c + 1) * FFN_TF)
        gate = _dot(h, wg_ref[:, sl])
        up = _dot(h, wu_ref[:, sl])
        act = (gate * jax.nn.sigmoid(gate) * up).astype(BF16)
        acc = acc + _dot(act, wd_ref[sl, :])
    return x + 0.5 * acc


def _ffn_body(x_ref, g_ref, wg_ref, wu_ref, wd_ref, o_ref):
    o_ref[...] = _swiglu_half_step(x_ref[...], g_ref, wg_ref, wu_ref, wd_ref)


def _mix_out_ffn_body(ymix_ref, ymem_ref, x_ref, wmix_ref, wmem_ref, g_ref, wg_ref, wu_ref, wd_ref, o_ref):
    x = x_ref[...] + _dot(ymix_ref[...], wmix_ref[...]) + _dot(ymem_ref[...], wmem_ref[...])
    o_ref[...] = _swiglu_half_step(x, g_ref, wg_ref, wu_ref, wd_ref)


def _ffn_specs(d, layer, half):
    pick = lambda r, c: pl.BlockSpec((None, None, r, c), lambda i: (layer, half, 0, 0), pipeline_mode=pl.Buffered(1))
    return [_const_spec((1, d)), pick(d, D_FF), pick(d, D_FF), pick(D_FF, d)]


def _ffn(x, g, wg, wu, wd, layer, half):
    t, d = x.shape
    tm = min(FFN_TM, t)
    return pl.pallas_call(
        _ffn_body,
        out_shape=jax.ShapeDtypeStruct((t, d), F32),
        grid=(t // tm,),
        in_specs=[pl.BlockSpec((tm, d), lambda i: (i, 0))] + _ffn_specs(d, layer, half),
        out_specs=pl.BlockSpec((tm, d), lambda i: (i, 0)),
        compiler_params=_cparams("parallel"),
        name="ffn",
    )(x, g.reshape(1, d), wg, wu, wd)


def _mix_out_ffn(y_mix, y_mem, x, w_out, g, wg, wu, wd, layer, half):
    t, d = x.shape
    tm = min(FFN_TM, t)
    return pl.pallas_call(
        _mix_out_ffn_body,
        out_shape=jax.ShapeDtypeStruct((t, d), F32),
        grid=(t // tm,),
        in_specs=[
            pl.BlockSpec((tm, MIX_W), lambda i: (i, 0)),
            pl.BlockSpec((tm, MEM_W), lambda i: (i, 0)),
            pl.BlockSpec((tm, d), lambda i: (i, 0)),
            _const_spec((MIX_W, d)),
            _const_spec((MEM_W, d)),
        ] + _ffn_specs(d, layer, half),
        out_specs=pl.BlockSpec((tm, d), lambda i: (i, 0)),
        compiler_params=_cparams("parallel"),
        name="mix_out_ffn",
    )(y_mix, y_mem, x, w_out[:MIX_W].astype(BF16), w_out[MIX_W:].astype(BF16), g.reshape(1, d), wg, wu, wd)


PROJ_TM = 1024


def _mem_kv_body(m_ref, g_ref, wk_ref, wvT_ref, kn_ref, k_ref, vT_ref):
    h = _rms(m_ref[0], g_ref[...]).astype(BF16)
    kT = _dot_nt(wk_ref[...], h)
    kT = _heads_norm_rope(kT, _tile_lanes(kn_ref[...], kT.shape[-1]), None, None)
    k_ref[0] = kT.T.astype(BF16)
    vT_ref[0] = _augment_values(_dot_nt(wvT_ref[...], h)).astype(BF16)


def _mem_kv(mem, g, w_kv, kn):
    b, m, d = mem.shape
    return pl.pallas_call(
        _mem_kv_body,
        out_shape=(jax.ShapeDtypeStruct((b, m, MEM_W), BF16), jax.ShapeDtypeStruct((b, H_MEM * VAUG, m), BF16)),
        grid=(b,),
        in_specs=[
            pl.BlockSpec((1, m, d), lambda i: (i, 0, 0)),
            _const_spec((1, d)),
            _const_spec((MEM_W, d)),
            _const_spec((MEM_W, d)),
            _const_spec((HD, LANES)),
        ],
        out_specs=(pl.BlockSpec((1, m, MEM_W), lambda i: (i, 0, 0)),
                   pl.BlockSpec((1, H_MEM * VAUG, m), lambda i: (i, 0, 0))),
        compiler_params=_cparams("parallel"),
        name="mem_kv",
    )(mem, g.reshape(1, d), w_kv[:, :MEM_W].T.astype(BF16), w_kv[:, MEM_W:].T.astype(BF16), _lane_gain(kn))


MEM_TQ = 1024


def _pad_rows(blk, half, total=LANES):
    z = jnp.zeros_like(blk)
    parts = [z] * (total // HD)
    parts[half] = blk
    return jnp.concatenate(parts, axis=0)


def _mem_attn_body(qT_ref, k_ref, vT_ref, o_ref, s_ref, mc_ref, m_ref, acc_ref):
    n_mem = k_ref.shape[1]
    outs = _staged_attention(
        H_MEM, 0, 1, n_mem,
        lambda h, r0: _dot(k_ref[0, :, (h // 2) * LANES:(h // 2 + 1) * LANES],
                           _pad_rows(qT_ref[h * HD:(h + 1) * HD, :], h % 2)),
        lambda h, r0: vT_ref[0, h * VAUG:(h + 1) * VAUG, :], s_ref, mc_ref, m_ref, acc_ref)
    o_ref[...] = jnp.concatenate(outs, axis=0).T.astype(BF16)


def _mem_attn(qmT, k, vT, seq):
    t = qmT.shape[1]
    b, m, _ = k.shape
    tq = min(MEM_TQ, seq)
    nq = seq // tq
    return pl.pallas_call(
        _mem_attn_body,
        out_shape=jax.ShapeDtypeStruct((t, MEM_W), BF16),
        grid=(b, nq),
        in_specs=[
            pl.BlockSpec((MEM_W, tq), lambda bi, i: (0, bi * nq + i)),
            pl.BlockSpec((1, m, MEM_W), lambda bi, i: (bi, 0, 0)),
            pl.BlockSpec((1, H_MEM * VAUG, m), lambda bi, i: (bi, 0, 0)),
        ],
        out_specs=pl.BlockSpec((tq, MEM_W), lambda bi, i: (bi * nq + i, 0)),
        scratch_shapes=[pltpu.VMEM((2, H_MEM, m, tq), F32), pltpu.VMEM((2, H_MEM * SUBLANES, tq), F32),
                        pltpu.VMEM((SUBLANES, tq), F32), pltpu.VMEM((H_MEM * VAUG, tq), F32)],
        compiler_params=_cparams("parallel", "parallel"),
        name="mem_attn",
    )(qmT, k, vT)


def _ret_proj_body(x_ref, g_ref, wtok_ref, wfeat_ref, cosT_ref, sinT_ref, cosF_ref, sinF_ref, mqn_ref,
                   q_ref, v_ref, gate_ref, kT_ref, qmT_ref):
    h = _rms(x_ref[...], g_ref[...]).astype(BF16)
    tok = _dot(h, wtok_ref[...])
    nq, nv = RET_H * RET_DK, RET_H * RET_DV
    q = tok[:, :nq]
    lane = lax.broadcasted_iota(I32, q.shape, 1)
    rot = jnp.where(lane % HD < HALF, pltpu.roll(q, nq - HALF, axis=1), pltpu.roll(q, HALF, axis=1))
    q_ref[...] = (q * _tile_lanes(cosT_ref[...], nq) + rot * _tile_lanes(sinT_ref[...], nq)).astype(BF16)
    v_ref[...] = tok[:, nq:nq + nv].astype(BF16)
    gate_ref[...] = tok[:, nq + nv:]
    feat = _dot_nt(wfeat_ref[...], h)
    tm = feat.shape[-1]
    cos, sin = cosF_ref[...], sinF_ref[...]
    kT = _heads_norm_rope(feat[:nq], None, cos, sin) * (RET_DK ** -0.5)
    kT_ref[...] = kT.astype(BF16)
    qm = _heads_norm_rope(feat[nq:], _tile_lanes(mqn_ref[...], tm), None, None) * QSCALE
    qmT_ref[...] = qm.astype(BF16)


def _token_rope_tables(seq):
    cosF, sinF = _rope_tables(jnp.arange(seq))
    cos = jnp.tile(cosF.T, (1, LANES // HALF))
    sign = jnp.where((jnp.arange(LANES) % HD) < HALF, -1.0, 1.0).astype(F32)
    sin = jnp.tile(sinF.T, (1, LANES // HALF)) * sign[None, :]
    return cos, sin


def _ret_proj(x, g, w_in, mem_qn, seq):
    t, d = x.shape
    tm = min(PROJ_TM, seq)
    npos = seq // tm
    wq, wk, wv, wg, wqm = jnp.split(w_in, np.cumsum(RET_SIZES)[:-1].tolist(), axis=1)
    wtok = jnp.concatenate([wq, wv, wg], axis=1).astype(BF16)
    wfeat = jnp.concatenate([wk, wqm], axis=1).T.astype(BF16)
    cosT, sinT = _token_rope_tables(seq)
    cosF, sinF = _rope_tables(jnp.arange(seq))
    nq, nv = RET_H * RET_DK, RET_H * RET_DV
    row = lambda n: pl.BlockSpec((tm, n), lambda i: (i, 0))
    col = lambda n: pl.BlockSpec((n, tm), lambda i: (0, i))
    return pl.pallas_call(
        _ret_proj_body,
        out_shape=(
            jax.ShapeDtypeStruct((t, nq), BF16), jax.ShapeDtypeStruct((t, nv), BF16),
            jax.ShapeDtypeStruct((t, nv), F32), jax.ShapeDtypeStruct((nq, t), BF16),
            jax.ShapeDtypeStruct((MEM_W, t), BF16),
        ),
        grid=(t // tm,),
        in_specs=[
            row(d), _const_spec((1, d)), _const_spec(wtok.shape), _const_spec(wfeat.shape),
            pl.BlockSpec((tm, LANES), lambda i: (i % npos, 0)), pl.BlockSpec((tm, LANES), lambda i: (i % npos, 0)),
            pl.BlockSpec((HALF, tm), lambda i: (0, i % npos)), pl.BlockSpec((HALF, tm), lambda i: (0, i % npos)),
            _const_spec((HD, LANES)),
        ],
        out_specs=(row(nq), row(nv), row(nv), col(nq), col(MEM_W)),
        compiler_params=_cparams("parallel"),
        name="ret_proj",
    )(x, g.reshape(1, d), wtok, wfeat, cosT, sinT, cosF, sinF, _lane_gain(mem_qn))


RET_TL = 1024


def _ret_body(q_ref, kT_ref, v_ref, gate_ref, y_ref, state_ref):
    @pl.when(pl.program_id(1) == 0)
    def _():
        state_ref[...] = jnp.zeros_like(state_ref)

    c = RET_CHUNK
    ii = lax.broadcasted_iota(I32, (c, c), 0).astype(F32)
    jj = lax.broadcasted_iota(I32, (c, c), 1).astype(F32)
    diff = ii - jj
    jk = lax.broadcasted_iota(I32, (RET_DK, c), 1).astype(F32)
    log_gs = [math.log(1.0 - 2.0 ** (-5.0 - hh)) for hh in range(RET_H)]
    dmasks = [jnp.where(diff >= 0, jnp.exp(lg * jnp.maximum(diff, 0.0)), 0.0) for lg in log_gs]
    xis = [jnp.exp(lg * (ii + 1.0)) for lg in log_gs]
    zetas = [jnp.exp(lg * (c - 1.0 - jk)) for lg in log_gs]
    for n in range(q_ref.shape[0] // c):
        rows = slice(n * c, (n + 1) * c)
        for hh in range(RET_H):
            dmask, xi, zeta, decay = dmasks[hh], xis[hh], zetas[hh], math.exp(log_gs[hh] * c)
            pair = slice((hh // 2) * LANES, (hh // 2 + 1) * LANES)
            q2 = q_ref[rows, pair]
            kT = kT_ref[hh * RET_DK:(hh + 1) * RET_DK, rows]
            v = v_ref[rows, hh * RET_DV:(hh + 1) * RET_DV]
            state = state_ref[hh]
            inner = _dot(q2, _pad_rows(kT, hh % 2)) * dmask
            o = _dot(inner.astype(BF16), v)
            o = o + _dot(q2, _pad_rows(state.astype(BF16), hh % 2)) * xi
            kv = _dot((kT.astype(F32) * zeta).astype(BF16), v)
            state_ref[hh] = state * decay + kv
            mu = jnp.mean(o, axis=-1, keepdims=True)
            var = jnp.mean(jnp.square(o - mu), axis=-1, keepdims=True)
            o = (o - mu) * lax.rsqrt(var + EPS)
            gte = gate_ref[rows, hh * RET_DV:(hh + 1) * RET_DV]
            y_ref[rows, hh * RET_DV:(hh + 1) * RET_DV] = (gte * jax.nn.sigmoid(gte) * o).astype(BF16)


def _retention(q, kT, v, gate, batch, seq):
    t = q.shape[0]
    tl = min(RET_TL, seq)
    nl = seq // tl
    nq, nv = RET_H * RET_DK, RET_H * RET_DV
    return pl.pallas_call(
        _ret_body,
        out_shape=jax.ShapeDtypeStruct((t, nv), BF16),
        grid=(batch, nl),
        in_specs=[
            pl.BlockSpec((tl, nq), lambda b, i: (b * nl + i, 0)),
            pl.BlockSpec((nq, tl), lambda b, i: (0, b * nl + i)),
            pl.BlockSpec((tl, nv), lambda b, i: (b * nl + i, 0)),
            pl.BlockSpec((tl, nv), lambda b, i: (b * nl + i, 0)),
        ],
        out_specs=pl.BlockSpec((tl, nv), lambda b, i: (b * nl + i, 0)),
        scratch_shapes=[pltpu.VMEM((RET_H, RET_DK, RET_DV), F32)],
        compiler_params=_cparams("parallel", "arbitrary"),
        name="retention",
    )(q, kT, v, gate)


def _dsa_proj_body(x_ref, g_ref, w_ref, cos_ref, sin_ref, qn_ref, kn_ref, mqn_ref,
                   kk_ref, qT_ref, iqT_ref, vT_ref, iwT_ref, qmT_ref):
    h = _rms(x_ref[...], g_ref[...]).astype(BF16)
    nq, niq = DSA_H * HD, IDX_H * IDX_D
    q = _dot_nt(w_ref[:nq, :], h)
    iq = _dot_nt(w_ref[nq:nq + niq, :], h)
    feat = _dot_nt(w_ref[nq + niq:, :], h)
    tm = feat.shape[-1]
    cos, sin = cos_ref[...], sin_ref[...]
    o = 0
    k = feat[o:o + HD]; o += HD
    ik = feat[o:o + IDX_D]; o += IDX_D
    v = feat[o:o + HD]; o += HD
    qm = feat[o:o + MEM_W]; o += MEM_W
    iw = feat[o:o + IDX_H]
    qT_ref[...] = (_heads_norm_rope(q, _tile_lanes(qn_ref[...], tm), cos, sin) * QSCALE).astype(BF16)
    iqT_ref[...] = _heads_norm_rope(iq, None, cos, sin).astype(BF16)
    k = _heads_norm_rope(k, _tile_lanes(kn_ref[...], tm), cos, sin)
    ik = _heads_norm_rope(ik, None, cos, sin)
    kk_ref[...] = jnp.concatenate([k, ik], axis=0).T.astype(BF16)
    vT_ref[...] = _augment_values(v).astype(BF16)
    iwT_ref[...] = iw
    qmT_ref[...] = (_heads_norm_rope(qm, _tile_lanes(mqn_ref[...], tm), None, None) * QSCALE).astype(BF16)


def _dsa_proj(x, g, w_in, qn, kn, mem_qn, seq):
    t, d = x.shape
    tm = min(PROJ_TM, seq)
    npos = seq // tm
    wq, wk, wv, wiq, wik, wiw, wqm = jnp.split(w_in, np.cumsum(DSA_SIZES)[:-1].tolist(), axis=1)
    wfeat = jnp.concatenate([wq, wiq, wk, wik, wv, wqm, wiw], axis=1).T.astype(BF16)
    cosF, sinF = _rope_tables(jnp.arange(seq))
    col = lambda n: pl.BlockSpec((n, tm), lambda i: (0, i))
    tab = pl.BlockSpec((HALF, tm), lambda i: (0, i % npos))
    gain = _const_spec((HD, LANES))
    return pl.pallas_call(
        _dsa_proj_body,
        out_shape=(
            jax.ShapeDtypeStruct((t, LANES), BF16), jax.ShapeDtypeStruct((DSA_H * HD, t), BF16),
            jax.ShapeDtypeStruct((IDX_H * IDX_D, t), BF16), jax.ShapeDtypeStruct((VAUG, t), BF16),
            jax.ShapeDtypeStruct((IDX_H, t), F32), jax.ShapeDtypeStruct((MEM_W, t), BF16),
        ),
        grid=(t // tm,),
        in_specs=[pl.BlockSpec((tm, d), lambda i: (i, 0)), _const_spec((1, d)), _const_spec(wfeat.shape),
                  tab, tab, gain, gain, gain],
        out_specs=(pl.BlockSpec((tm, LANES), lambda i: (i, 0)), col(DSA_H * HD), col(IDX_H * IDX_D), col(VAUG),
                   col(IDX_H), col(MEM_W)),
        compiler_params=_cparams("parallel"),
        name="dsa_proj",
    )(x, g.reshape(1, d), wfeat, cosF, sinF, _lane_gain(qn), _lane_gain(kn), _lane_gain(mem_qn))


DSA_TQ = 256
DSA_TK = 256
HALF_BITS = 16
INT16_MIN = -2 ** (HALF_BITS - 1)


def _dsa_attn_body(kk_ref, qT_ref, iqT_ref, vT_ref, iwT_ref, y_ref, keys_ref, hi_ref, lo_ref, bias_ref, s_ref, mc_ref,
                   m_ref, acc_ref, *, topk, seq, tile):
    tq, tk = DSA_TQ, DSA_TK
    qs = tile * tq
    nkc = (qs + tq) // tk
    qpos = qs + lax.broadcasted_iota(I32, (1, tq), 1)
    row = lax.broadcasted_iota(I32, (tk, tq), 0)

    def over_chunks(body, carry):
        for c in range(nkc):
            carry = body(c * tk, carry)
        return carry

    def score_chunk(r0, carry):
        kkc = kk_ref[pl.ds(r0, tk), :]
        acc = jnp.zeros((tk, tq), F32)
        for h in range(IDX_H):
            r = _dot(kkc, _pad_rows(iqT_ref[h * IDX_D:(h + 1) * IDX_D, :], 1))
            acc = acc + jnp.maximum(r, 0.0) * iwT_ref[h:h + 1, :]
        sc = jnp.where(r0 + row <= qpos, acc + 0.0, -jnp.inf)
        bits = pltpu.bitcast(sc, I32)
        key = jnp.where(bits < 0, bits ^ jnp.int32(0x7FFFFFFF), bits)
        keys_ref[pl.ds(r0, tk), :] = key
        hi_ref[pl.ds(r0, tk), :] = lax.shift_right_arithmetic(key, HALF_BITS).astype(I16)
        return carry

    over_chunks(score_chunk, 0)

    def count(pred):
        def body(r0, acc):
            m = pred(keys_ref[pl.ds(r0, tk), :], r0 + row)
            return acc + jnp.sum(m.astype(I32).reshape(tk // SUBLANES, SUBLANES, tq), axis=0)
        acc = over_chunks(body, jnp.zeros((SUBLANES, tq), I32))
        return jnp.sum(acc, axis=0, keepdims=True)

    def fold16(mask):
        ones = jnp.where(mask, jnp.int16(1), jnp.int16(0))
        parts = [ones[i:i + PACKED_SUBLANES] for i in range(0, tk, PACKED_SUBLANES)]
        while len(parts) > 1:
            parts = [a + b for a, b in zip(parts[::2], parts[1::2])]
        return parts[0]

    zeros16 = jnp.zeros((PACKED_SUBLANES, tq), I16)
    total16 = lambda acc: jnp.sum(acc.astype(I32), axis=0, keepdims=True)

    def count16(ref, pred):
        def body(r0, acc):
            return acc + fold16(pred(ref[pl.ds(r0, tk), :]))
        return total16(over_chunks(body, zeros16))

    def kth_largest16(ref, k):
        thr = jnp.where(count16(ref, lambda v: v >= 0) >= k, jnp.int32(0), jnp.int32(INT16_MIN))

        def bit_step(b, thr):
            cand = thr | lax.shift_left(jnp.int32(1), (HALF_BITS - 2) - b)
            c16 = cand.astype(I16)
            return jnp.where(count16(ref, lambda v: v >= c16) >= k, cand, thr)

        return lax.fori_loop(0, HALF_BITS - 1, bit_step, thr)

    thr_hi = kth_largest16(hi_ref, topk)
    thr_hi16 = thr_hi.astype(I16)

    def low_half_chunk(r0, above):
        hi = hi_ref[pl.ds(r0, tk), :]
        lo = ((keys_ref[pl.ds(r0, tk), :] & jnp.int32(0xFFFF)) + jnp.int32(INT16_MIN)).astype(I16)
        lo_ref[pl.ds(r0, tk), :] = jnp.where(hi == thr_hi16, lo, jnp.int16(INT16_MIN))
        return above + fold16(hi > thr_hi16)

    n_above = total16(over_chunks(low_half_chunk, zeros16))
    need_lo = topk - n_above
    thr_lo = kth_largest16(lo_ref, need_lo)
    thr = lax.shift_left(thr_hi, HALF_BITS) | (thr_lo - jnp.int32(INT16_MIN))
    def tie_search():
        need = topk - count(lambda k, _: k > thr)

        def idx_step(b, q):
            cand = q | lax.shift_left(jnp.int32(1), (seq.bit_length() - 2) - b)
            return jnp.where(count(lambda k, idx: (k == thr) & (idx < cand)) < need, cand, q)

        return lax.fori_loop(0, seq.bit_length() - 1, idx_step, jnp.zeros((1, tq), I32))

    thr_lo16 = thr_lo.astype(I16)
    has_ties = jnp.max(n_above + count16(lo_ref, lambda v: v >= thr_lo16)) > topk
    last = lax.cond(has_ties, tie_search, lambda: jnp.full((1, tq), seq, I32))

    def bias_chunk(r0, carry):
        k = keys_ref[pl.ds(r0, tk), :]
        idx = r0 + row
        sel = (k > thr) | ((k == thr) & (idx <= last))
        bias_ref[pl.ds(r0, tk), :] = jnp.where(sel & (idx <= qpos), 0.0, NEG)
        return carry

    over_chunks(bias_chunk, 0)

    outs = _staged_attention(
        DSA_H, 0, nkc, tk,
        lambda h, r0: _dot(kk_ref[pl.ds(r0, tk), :], _pad_rows(qT_ref[h * HD:(h + 1) * HD, :], 0))
        + bias_ref[pl.ds(r0, tk), :],
        lambda h, r0: vT_ref[:, pl.ds(r0, tk)], s_ref, mc_ref, m_ref, acc_ref)
    y_ref[...] = jnp.concatenate(outs, axis=0).T.astype(BF16)


def _dsa_attn(kk, qT, iqT, vT, iwT, batch, seq):
    t = kk.shape[0]
    tq = DSA_TQ
    nq = seq // tq
    topk = min(DSA_TOPK_MAX, seq // 4)
    def one_tile(i):
        keys = (i + 1) * tq
        colq = lambda n: pl.BlockSpec((n, tq), lambda b: (0, b * nq + i))
        return pl.pallas_call(
            functools.partial(_dsa_attn_body, topk=topk, seq=seq, tile=i),
            out_shape=jax.ShapeDtypeStruct((batch, tq, DSA_H * HD), BF16),
            grid=(batch,),
            in_specs=[
                pl.BlockSpec((keys, LANES), lambda b: (b * (seq // keys), 0)) if seq % keys == 0 else
                pl.BlockSpec((seq, LANES), lambda b: (b, 0)),
                colq(DSA_H * HD), colq(IDX_H * IDX_D),
                pl.BlockSpec((VAUG, seq), lambda b: (0, b)),
                colq(IDX_H),
            ],
            out_specs=pl.BlockSpec((None, tq, DSA_H * HD), lambda b: (b, 0, 0)),
            scratch_shapes=[pltpu.VMEM((keys, tq), I32), pltpu.VMEM((keys, tq), I16), pltpu.VMEM((keys, tq), I16),
                            pltpu.VMEM((keys, tq), F32),
                            pltpu.VMEM((2, DSA_H, DSA_TK, tq), F32), pltpu.VMEM((2, DSA_H * SUBLANES, tq), F32),
                            pltpu.VMEM((2 * SUBLANES, tq), F32), pltpu.VMEM((DSA_H * VAUG, tq), F32)],
            compiler_params=_cparams("parallel"),
            name=f"dsa_attn_tile{i}",
        )(kk, qT, iqT, vT, iwT)

    tiles = [one_tile(i) for i in range(nq)]
    return jnp.stack(tiles, axis=1).reshape(t, DSA_H * HD)


NSA_KV = NSA_G * HD
NSA_GATES = NSA_H * 3
NSA_GATES_PAD = -NSA_GATES % SUBLANES


def _nsa_proj_body(x_ref, g_ref, wtok_ref, wfeat_ref, cos_ref, sin_ref, qn_ref, kns_ref, knw_ref, mqn_ref,
                   kc_ref, vc_ref, ks_ref, kw_ref, qT_ref, vsT_ref, vwT_ref, gT_ref, qmT_ref):
    h = _rms(x_ref[...], g_ref[...]).astype(BF16)
    tok = _dot(h, wtok_ref[...])
    kc_ref[...] = tok[:, :NSA_KV]
    vc_ref[...] = tok[:, NSA_KV:]
    nq = NSA_H * HD
    q = _dot_nt(wfeat_ref[:nq, :], h)
    kk = _dot_nt(wfeat_ref[nq:nq + 2 * NSA_KV, :], h)
    feat = _dot_nt(wfeat_ref[nq + 2 * NSA_KV:, :], h)
    tm = feat.shape[-1]
    cos, sin = cos_ref[...], sin_ref[...]
    ks, kw = kk[:NSA_KV], kk[NSA_KV:]
    o = 0
    vs = feat[o:o + NSA_KV]; o += NSA_KV
    vw = feat[o:o + NSA_KV]; o += NSA_KV
    qm = feat[o:o + MEM_W]; o += MEM_W
    gates = feat[o:]
    qT_ref[...] = (_heads_norm_rope(q, _tile_lanes(qn_ref[...], tm), cos, sin) * QSCALE).astype(BF16)
    ks_ref[...] = _heads_norm_rope(ks, _tile_lanes(kns_ref[...], tm), cos, sin).T.astype(BF16)
    kw_ref[...] = _heads_norm_rope(kw, _tile_lanes(knw_ref[...], tm), cos, sin).T.astype(BF16)
    vsT_ref[...] = _augment_values(vs).astype(BF16)
    vwT_ref[...] = _augment_values(vw).astype(BF16)
    gT_ref[...] = jax.nn.sigmoid(gates)
    qmT_ref[...] = (_heads_norm_rope(qm, _tile_lanes(mqn_ref[...], tm), None, None) * QSCALE).astype(BF16)


def _nsa_proj(x, g, w_in, qn, kn, mem_qn, seq):
    t, d = x.shape
    tm = min(PROJ_TM, seq)
    npos = seq // tm
    wq, wkc, wvc, wks, wvs, wkw, wvw, wgt, wqm = jnp.split(w_in, np.cumsum(NSA_SIZES)[:-1].tolist(), axis=1)
    wtok = jnp.concatenate([wkc, wvc], axis=1).astype(BF16)
    wgt = jnp.pad(wgt, ((0, 0), (0, NSA_GATES_PAD)))
    wfeat = jnp.concatenate([wq, wks, wkw, wvs, wvw, wqm, wgt], axis=1).T.astype(BF16)
    cosF, sinF = _rope_tables(jnp.arange(seq))
    row = lambda n: pl.BlockSpec((tm, n), lambda i: (i, 0))
    col = lambda n: pl.BlockSpec((n, tm), lambda i: (0, i))
    tab = pl.BlockSpec((HALF, tm), lambda i: (0, i % npos))
    gain = _const_spec((HD, LANES))
    ngt = NSA_GATES + NSA_GATES_PAD
    return pl.pallas_call(
        _nsa_proj_body,
        out_shape=(
            jax.ShapeDtypeStruct((t, NSA_KV), F32), jax.ShapeDtypeStruct((t, NSA_KV), F32),
            jax.ShapeDtypeStruct((t, NSA_KV), BF16), jax.ShapeDtypeStruct((t, NSA_KV), BF16),
            jax.ShapeDtypeStruct((NSA_H * HD, t), BF16), jax.ShapeDtypeStruct((NSA_G * VAUG, t), BF16),
            jax.ShapeDtypeStruct((NSA_G * VAUG, t), BF16), jax.ShapeDtypeStruct((ngt, t), F32),
            jax.ShapeDtypeStruct((MEM_W, t), BF16),
        ),
        grid=(t // tm,),
        in_specs=[row(d), _const_spec((1, d)), _const_spec(wtok.shape), _const_spec(wfeat.shape),
                  tab, tab, gain, gain, gain, gain],
        out_specs=(row(NSA_KV), row(NSA_KV), row(NSA_KV), row(NSA_KV), col(NSA_H * HD), col(NSA_G * VAUG),
                   col(NSA_G * VAUG), col(ngt), col(MEM_W)),
        compiler_params=_cparams("parallel"),
        name="nsa_proj",
    )(x, g.reshape(1, d), wtok, wfeat, cosF, sinF, _lane_gain(qn), _lane_gain(kn[1]), _lane_gain(kn[2]),
      _lane_gain(mem_qn))


def _nsa_cmp_body(k01_ref, k23_ref, v01_ref, v23_ref, pk_ref, pv_ref, wk_ref, wv_ref, kn_ref, cos_ref, sin_ref,
                  k_ref, vT_ref):
    n = k_ref.shape[1]

    def compress(lo_ref, hi_ref, pos, w_ref):
        parts = []
        for l in range(CMP_S):
            parts += [lo_ref[pl.ds(l, n, stride=CMP_S), :], hi_ref[pl.ds(l, n, stride=CMP_S), :]]
        x = jnp.concatenate(parts, axis=1)
        xa = (x + pos[0:1]).astype(BF16)
        xb = (pltpu.roll(x, n - 1, axis=0) + pos[1:2]).astype(BF16)
        return _dot_nt(w_ref[0], xa) + _dot_nt(w_ref[1], xb)

    kT = compress(k01_ref, k23_ref, pk_ref[...], wk_ref)
    kT = _heads_norm_rope(kT, _tile_lanes(kn_ref[...], kT.shape[-1]), cos_ref[...], sin_ref[...])
    k_ref[0] = kT.T.astype(BF16)
    vT_ref[0] = compress(v01_ref, v23_ref, pv_ref[...], wv_ref).astype(BF16)


def _nsa_cmp_weights(w, pos):
    halves, width = CMP_L // CMP_S, CMP_S * NSA_KV
    w_eld = w.reshape(halves, CMP_S, HD, HD).transpose(0, 3, 1, 2)
    per_group = jnp.broadcast_to(w_eld[:, :, :, None, :], (halves, HD, CMP_S, NSA_G, HD)).reshape(halves, HD, width)
    rows = np.arange(NSA_KV)[:, None] // HD
    cols = (np.arange(width)[None, :] // HD) % NSA_G
    wt = jnp.where(jnp.asarray(rows == cols)[None], jnp.tile(per_group, (1, NSA_G, 1)), 0.0)
    p = jnp.broadcast_to(pos.reshape(halves, CMP_S, 1, HD), (halves, CMP_S, NSA_G, HD))
    return wt.astype(BF16), p.reshape(halves, width).astype(F32)


def _nsa_cmp(kc, vc, pos_k, pos_v, wk, wv, kn0, batch, seq):
    n = seq // CMP_S
    wkt, pk = _nsa_cmp_weights(wk, pos_k)
    wvt, pv = _nsa_cmp_weights(wv, pos_v)
    cosE, sinE = _rope_tables(jnp.arange(n) * CMP_S + (CMP_L - 1))
    lanes_lo = pl.BlockSpec((seq, LANES), lambda b: (b, 0))
    lanes_hi = pl.BlockSpec((seq, LANES), lambda b: (b, 1))
    return pl.pallas_call(
        _nsa_cmp_body,
        out_shape=(jax.ShapeDtypeStruct((batch, n, NSA_KV), BF16), jax.ShapeDtypeStruct((batch, NSA_KV, n), BF16)),
        grid=(batch,),
        in_specs=[lanes_lo, lanes_hi, lanes_lo, lanes_hi, _const_spec(pk.shape), _const_spec(pv.shape),
                  _const_spec(wkt.shape), _const_spec(wvt.shape), _const_spec((HD, LANES)), _const_spec((HALF, n)),
                  _const_spec((HALF, n))],
        out_specs=(pl.BlockSpec((1, n, NSA_KV), lambda b: (b, 0, 0)), pl.BlockSpec((1, NSA_KV, n), lambda b: (b, 0, 0))),
        compiler_params=_cparams("parallel"),
        name="nsa_cmp",
    )(kc, kc, vc, vc, pk, pv, wkt, wvt, _lane_gain(kn0), cosE, sinE)


NSA_TQ = 256
NSA_TK = 256


def _staged_attention(nheads, lo, hi, tk, logits, values, s_ref, mc_ref, m_ref, acc_ref):
    tq = s_ref.shape[-1]
    fold = lambda a: a.reshape(tk // SUBLANES, SUBLANES, tq)
    m_ref[...] = jnp.full(m_ref.shape, -jnp.inf, F32)
    acc_ref[...] = jnp.zeros(acc_ref.shape, F32)

    start = lambda c: c * tk if isinstance(c, int) else pl.multiple_of(c * tk, tk)

    def stage(h, c, slot):
        s = logits(h, start(c))
        s_ref[slot, h] = s
        mc_ref[slot, h * SUBLANES:(h + 1) * SUBLANES, :] = jnp.max(fold(s), axis=0)

    def consume(h, c, slot):
        g8, hrows = slice(h * SUBLANES, (h + 1) * SUBLANES), slice(h * VAUG, (h + 1) * VAUG)
        m_old = m_ref[h:h + 1, :]
        m_new = jnp.maximum(m_old, jnp.max(mc_ref[slot, g8, :], axis=0, keepdims=True))
        alpha = jnp.exp2(m_old - m_new)
        p = jnp.exp2(s_ref[slot, h] - m_new)
        acc_ref[hrows, :] = acc_ref[hrows, :] * alpha + _dot(values(h, start(c)), p.astype(BF16))
        m_ref[h:h + 1, :] = m_new

    def consume_and_stage_next(c, slot):
        for h in range(nheads):
            consume(h, c, slot)
            stage(h, c + 1, 1 - slot)

    def finish(c, slot):
        for h in range(nheads):
            consume(h, c, slot)

    for h in range(nheads):
        stage(h, lo, 0)
    n_fused = hi - 1 - lo

    def two_chunks(j, carry):
        c = lo + 2 * j
        consume_and_stage_next(c, 0)
        consume_and_stage_next(c + 1, 1)
        return carry

    if isinstance(n_fused, int):
        for i in range(n_fused):
            consume_and_stage_next(lo + i, i % 2)
        finish(hi - 1, n_fused % 2)
    else:
        lax.fori_loop(0, n_fused // 2, two_chunks, 0)

        @pl.when(n_fused % 2 == 1)
        def _():
            consume_and_stage_next(hi - 2, 0)
            finish(hi - 1, 1)

        @pl.when(n_fused % 2 == 0)
        def _():
            finish(hi - 1, 0)

    return [acc_ref[h * VAUG:h * VAUG + HD, :] / acc_ref[h * VAUG + HD:h * VAUG + HD + 1, :] for h in range(nheads)]


def _nsa_attn_body(qT_ref, gT_ref, ks_ref, vsT_ref, kw_ref, vwT_ref, kc_ref, vcT_ref, cov_ref, y_ref,
                   bias_ref, out_ref, s_ref, mc_ref, m_ref, acc_ref, *, seq):
    tq, tk = NSA_TQ, NSA_TK
    qs = pl.program_id(1) * tq
    nkc = (qs + tq) // tk
    wlo = jnp.maximum((qs - WIN) // tk, 0)
    qpos = qs + lax.broadcasted_iota(I32, (1, tq), 1)
    ncr = seq // CMP_S
    nblk = seq // SLC_L
    n_sel = min(SLC_N_MAX, nblk)
    valid_c = CMP_S * lax.broadcasted_iota(I32, (ncr, tq), 0) + (CMP_L - 1) <= qpos
    any_c = jnp.where(qpos >= CMP_L - 1, 1.0, 0.0)
    jrow = lax.broadcasted_iota(I32, (nblk, tq), 0)
    qblk = qpos // SLC_L
    forced = (jrow == 0) | (jrow == qblk) | (jrow == qblk - 1)
    causal_blk = jrow * SLC_L <= qpos
    krow = lax.broadcasted_iota(I32, (tk, tq), 0)
    brow = lax.broadcasted_iota(I32, (SLC_L, tq), 0)
    group = lambda h: h // NSA_R
    pair = lambda h: slice((group(h) // 2) * LANES, (group(h) // 2 + 1) * LANES)
    grows = lambda h: slice(group(h) * HD, (group(h) + 1) * HD)
    vrows = lambda h: slice(group(h) * VAUG, (group(h) + 1) * VAUG)
    q_of = lambda h: _pad_rows(qT_ref[h * HD:(h + 1) * HD, :], group(h) % 2)
    gate = lambda h, branch: gT_ref[h * 3 + branch:h * 3 + branch + 1, :]

    selbs = []
    for g in range(NSA_G):
        heads = [g * NSA_R + r for r in range(NSA_R)]
        kcm = kc_ref[0][:, pair(heads[0])]
        vcm = vcT_ref[0][grows(heads[0]), :]
        psum = jnp.zeros((ncr, tq), F32)
        for h in heads:
            s = jnp.where(valid_c, _dot(kcm, q_of(h)), NEG)
            p = jnp.exp2(s - jnp.max(s, axis=0, keepdims=True))
            p = p / jnp.sum(p, axis=0, keepdims=True) * any_c
            out_ref[h * HD:(h + 1) * HD, :] = gate(h, 0) * _dot(vcm, p.astype(BF16))
            psum = psum + p
        p_hi = psum.astype(BF16)
        p_lo = (psum - p_hi.astype(F32)).astype(BF16)
        imp = _dot(cov_ref[...], p_hi) + _dot(cov_ref[...], p_lo)
        imp = jnp.where(causal_blk, imp + jnp.where(forced, FORCE, 0.0), NEG)
        rank = jnp.zeros((nblk, tq), I32)
        for j2 in range(nblk):
            rj = imp[j2:j2 + 1]
            beats = (rj > imp) | ((rj == imp) & (j2 < jrow))
            rank = rank + beats.astype(I32)
        selbs.append(jnp.where(rank < n_sel, 0.0, NEG))

    blocks_per_chunk = tk // SLC_L
    for c in range(seq // tk):
        @pl.when(c < nkc)
        def _(c=c):
            for g in range(NSA_G):
                for j in range(c * blocks_per_chunk, (c + 1) * blocks_per_chunk):
                    bias_ref[g, j * SLC_L:(j + 1) * SLC_L, :] = jnp.where(j * SLC_L + brow <= qpos,
                                                                          selbs[g][j:j + 1], NEG)

    stats = (s_ref, mc_ref, m_ref, acc_ref)
    slc = _staged_attention(
        NSA_H, 0, nkc, tk,
        lambda h, r0: _dot(ks_ref[pl.ds(r0, tk), pair(h)], q_of(h)) + bias_ref[group(h), pl.ds(r0, tk), :],
        lambda h, r0: vsT_ref[vrows(h), pl.ds(r0, tk)], *stats)
    for h in range(NSA_H):
        out_ref[h * HD:(h + 1) * HD, :] = out_ref[h * HD:(h + 1) * HD, :] + gate(h, 1) * slc[h]

    def win_mask_chunk(c, carry):
        r0 = pl.multiple_of(c * tk, tk)
        dist = qpos - (r0 + krow)
        bias_ref[0, pl.ds(r0, tk), :] = jnp.where((dist >= 0) & (dist < WIN), 0.0, NEG)
        return carry

    lax.fori_loop(wlo, nkc, win_mask_chunk, 0)
    win = _staged_attention(
        NSA_H, wlo, nkc, tk,
        lambda h, r0: _dot(kw_ref[pl.ds(r0, tk), pair(h)], q_of(h)) + bias_ref[0, pl.ds(r0, tk), :],
        lambda h, r0: vwT_ref[vrows(h), pl.ds(r0, tk)], *stats)
    outs = [out_ref[h * HD:(h + 1) * HD, :] + gate(h, 2) * win[h] for h in range(NSA_H)]
    y_ref[...] = jnp.concatenate(outs, axis=0).T.astype(BF16)


def _nsa_attn(qT, gT, ks, vsT, kw, vwT, kcmp, vcmpT, batch, seq):
    t = ks.shape[0]
    tq = NSA_TQ
    nq = seq // tq
    ncr, nblk = seq // CMP_S, seq // SLC_L
    starts = np.arange(ncr) * CMP_S
    sstart = np.arange(nblk) * SLC_L
    cover = (starts[None, :] < sstart[:, None] + SLC_L) & (starts[None, :] + CMP_L > sstart[:, None])
    cover[:, ncr - 1] = False
    colq = lambda n: pl.BlockSpec((n, tq), lambda b, i: (0, b * nq + i))
    tok = pl.BlockSpec((seq, NSA_KV), lambda b, i: (b, 0))
    feat = pl.BlockSpec((NSA_G * VAUG, seq), lambda b, i: (0, b))
    return pl.pallas_call(
        functools.partial(_nsa_attn_body, seq=seq),
        out_shape=jax.ShapeDtypeStruct((t, NSA_H * HD), BF16),
        grid=(batch, nq),
        in_specs=[
            colq(NSA_H * HD), colq(gT.shape[0]), tok, feat, tok, feat,
            pl.BlockSpec((1, ncr, NSA_KV), lambda b, i: (b, 0, 0)),
            pl.BlockSpec((1, NSA_KV, ncr), lambda b, i: (b, 0, 0)),
            _const_spec((nblk, ncr)),
        ],
        out_specs=pl.BlockSpec((tq, NSA_H * HD), lambda b, i: (b * nq + i, 0)),
        scratch_shapes=[pltpu.VMEM((NSA_G, seq, tq), F32), pltpu.VMEM((NSA_H * HD, tq), F32),
                        pltpu.VMEM((2, NSA_H, NSA_TK, tq), F32), pltpu.VMEM((2, NSA_H * SUBLANES, tq), F32),
                        pltpu.VMEM((2 * SUBLANES, tq), F32), pltpu.VMEM((NSA_H * VAUG, tq), F32)],
        compiler_params=_cparams("parallel", "arbitrary"),
        name="nsa_attn",
    )(qT, gT, ks, vsT, kw, vwT, kcmp, vcmpT, jnp.asarray(cover, BF16))


def _nsa_layer_mix(x, g, w_in, qn, kn, pos_k, pos_v, wk, wv, mem_qn, batch, seq):
    kc, vc, ks, kw, qT, vsT, vwT, gT, qmT = _nsa_proj(x, g, w_in, qn, kn, mem_qn, seq)
    kcmp, vcmpT = _nsa_cmp(kc, vc, pos_k, pos_v, wk, wv, kn[0], batch, seq)
    return _nsa_attn(qT, gT, ks, vsT, kw, vwT, kcmp, vcmpT, batch, seq), qmT


def kernel(x, mem, ffn_norm, ffn_w_gate, ffn_w_up, ffn_w_down, mix_norm, w_out, mem_norm, mem_w_kv, mem_qn, mem_kn, ret_w_in, dsa_w_in, dsa_qn, dsa_kn, nsa_w_in, nsa_qn, nsa_kn, nsa_cmp_pos_k, nsa_cmp_pos_v, nsa_cmp_wk, nsa_cmp_wv):
    batch, seq, d = x.shape
    x = x.reshape(batch * seq, d)
    ffn_w = (ffn_w_gate.astype(BF16), ffn_w_up.astype(BF16), ffn_w_down.astype(BF16))
    for i in range(ffn_norm.shape[0]):
        x = _ffn(x, ffn_norm[i, 0], *ffn_w, i, 0)
        kind, j = i % N_MIXERS, i // N_MIXERS
        if kind == 0:
            q, v, gate, kT, qmT = _ret_proj(x, mix_norm[i], ret_w_in[j], mem_qn[i], seq)
            y_mix = _retention(q, kT, v, gate, batch, seq)
        elif kind == 1:
            kk, qT, iqT, vT, iwT, qmT = _dsa_proj(x, mix_norm[i], dsa_w_in[j], dsa_qn[j], dsa_kn[j], mem_qn[i], seq)
            y_mix = _dsa_attn(kk, qT, iqT, vT, iwT, batch, seq)
        else:
            y_mix, qmT = _nsa_layer_mix(x, mix_norm[i], nsa_w_in[j], nsa_qn[j], nsa_kn[j], nsa_cmp_pos_k[j],
                                        nsa_cmp_pos_v[j], nsa_cmp_wk[j], nsa_cmp_wv[j], mem_qn[i], batch, seq)
        mem_k, mem_vT = _mem_kv(mem, mem_norm[i], mem_w_kv[i], mem_kn[i])
        y_mem = _mem_attn(qmT, mem_k, mem_vT, seq)
        x = _mix_out_ffn(y_mix, y_mem, x, w_out[i], ffn_norm[i, 1], *ffn_w, i, 1)
    return x.reshape(batch, seq, d)
```

```python
import functools
import math

import jax
import jax.numpy as jnp
import numpy as np
from jax import lax
from jax.experimental import pallas as pl
from jax.experimental.pallas import tpu as pltpu

D_MODEL = 1024
HD = 64
HALF = HD // 2
H_MIX = 12
H_MEM = 4
MIX_W = H_MIX * HD
MEM_W = H_MEM * HD
D_FF = 2816
ROPE_THETA = 10000.0
EPS = 1e-6
NEG = -1e30
FORCE = 1e9
SCALE = HD ** -0.5
QSCALE = SCALE * math.log2(math.e)

RET_H, RET_DK, RET_DV, RET_CHUNK = 6, 64, 128, 128
DSA_H, IDX_H, IDX_D, DSA_TOPK_MAX = 12, 8, 64, 256
NSA_H, NSA_G, CMP_L, CMP_S, SLC_L, SLC_N_MAX, WIN = 12, 4, 32, 16, 64, 16, 512
NSA_R = NSA_H // NSA_G
N_MIXERS = 3

RET_SIZES = [RET_H * RET_DK, RET_H * RET_DK, RET_H * RET_DV, RET_H * RET_DV, MEM_W]
DSA_SIZES = [DSA_H * HD, HD, HD, IDX_H * IDX_D, IDX_D, IDX_H, MEM_W]
NSA_SIZES = [NSA_H * HD] + [NSA_G * HD] * 6 + [NSA_H * 3, MEM_W]

LANES = 128
SUBLANES = 8
PACKED_SUBLANES = 2 * SUBLANES
VAUG = HD + PACKED_SUBLANES
MXU_N = 256
VMEM_LIMIT_BYTES = 56 * 1024 * 1024

BF16 = jnp.bfloat16
F32 = jnp.float32
I32 = jnp.int32
I16 = jnp.int16


def _cparams(*sem):
    return pltpu.CompilerParams(dimension_semantics=sem, vmem_limit_bytes=VMEM_LIMIT_BYTES)


def _const_spec(shape):
    n = len(shape)
    return pl.BlockSpec(shape, lambda *_: (0,) * n, pipeline_mode=pl.Buffered(1))


def _rms(x, g):
    return x * lax.rsqrt(jnp.mean(x * x, axis=-1, keepdims=True) + EPS) * g


def _dot(a, b):
    return jnp.dot(a, b, preferred_element_type=F32)


def _dot_nt(a, b):
    return lax.dot_general(a, b, (((1,), (1,)), ((), ())), preferred_element_type=F32)


def _tile_lanes(a, n):
    reps = n // a.shape[-1]
    return a if reps == 1 else jnp.concatenate([a] * reps, axis=-1)


def _rope_tables(pos):
    inv = ROPE_THETA ** (-jnp.arange(HALF, dtype=F32) / HALF)
    ang = pos.astype(F32)[:, None] * inv[None, :]
    return jnp.cos(ang).T, jnp.sin(ang).T


def _lane_gain(g):
    return jnp.broadcast_to(g.astype(F32)[:, None], (g.shape[0], LANES))


def _augment_values(v):
    t = v.shape[-1]
    v3 = v.reshape(v.shape[0] // HD, HD, t)
    tail = jnp.where(lax.broadcasted_iota(I32, (v3.shape[0], VAUG - HD, t), 1) == 0, 1.0, 0.0).astype(v.dtype)
    return jnp.concatenate([v3, tail], axis=1).reshape(v3.shape[0] * VAUG, t)


def _heads_norm_rope(x, gain, cos, sin):
    t = x.shape[-1]
    x3 = x.reshape(x.shape[0] // HD, HD, t)
    if gain is not None:
        ms = jnp.sum(x3 * x3, axis=1, keepdims=True) * (1.0 / HD)
        x3 = x3 * lax.rsqrt(ms + EPS) * gain[None]
    if cos is not None:
        x1, x2 = x3[:, :HALF], x3[:, HALF:]
        x3 = jnp.concatenate([x1 * cos[None] - x2 * sin[None], x2 * cos[None] + x1 * sin[None]], axis=1)
    return x3.reshape(x.shape)


FFN_TM = 1024
FFN_TF = MXU_N


def _swiglu_half_step(x, g_ref, wg_ref, wu_ref, wd_ref):
    h = _rms(x, g_ref[...]).astype(BF16)
    acc = jnp.zeros(x.shape, F32)
    for c in range(D_FF // FFN_TF):
        sl = slice(c * FFN_TF, (c + 1) * FFN_TF)
        gate = _dot(h, wg_ref[:, sl])
        up = _dot(h, wu_ref[:, sl])
        act = (gate * jax.nn.sigmoid(gate) * up).astype(BF16)
        acc = acc + _dot(act, wd_ref[sl, :])
    return x + 0.5 * acc


def _ffn_body(x_ref, g_ref, wg_ref, wu_ref, wd_ref, o_ref):
    o_ref[...] = _swiglu_half_step(x_ref[...], g_ref, wg_ref, wu_ref, wd_ref)


def _mix_out_ffn_body(ymix_ref, ymem_ref, x_ref, wmix_ref, wmem_ref, g_ref, wg_ref, wu_ref, wd_ref, o_ref):
    x = x_ref[...] + _dot(ymix_ref[...], wmix_ref[...]) + _dot(ymem_ref[...], wmem_ref[...])
    o_ref[...] = _swiglu_half_step(x, g_ref, wg_ref, wu_ref, wd_ref)


def _ffn_specs(d, layer, half):
    pick = lambda r, c: pl.BlockSpec((None, None, r, c), lambda i: (layer, half, 0, 0), pipeline_mode=pl.Buffered(1))
    return [_const_spec((1, d)), pick(d, D_FF), pick(d, D_FF), pick(D_FF, d)]


def _ffn(x, g, wg, wu, wd, layer, half):
    t, d = x.shape
    tm = min(FFN_TM, t)
    return pl.pallas_call(
        _ffn_body,
        out_shape=jax.ShapeDtypeStruct((t, d), F32),
        grid=(t // tm,),
        in_specs=[pl.BlockSpec((tm, d), lambda i: (i, 0))] + _ffn_specs(d, layer, half),
        out_specs=pl.BlockSpec((tm, d), lambda i: (i, 0)),
        compiler_params=_cparams("parallel"),
        name="ffn",
    )(x, g.reshape(1, d), wg, wu, wd)


def _mix_out_ffn(y_mix, y_mem, x, w_out, g, wg, wu, wd, layer, half):
    t, d = x.shape
    tm = min(FFN_TM, t)
    return pl.pallas_call(
        _mix_out_ffn_body,
        out_shape=jax.ShapeDtypeStruct((t, d), F32),
        grid=(t // tm,),
        in_specs=[
            pl.BlockSpec((tm, MIX_W), lambda i: (i, 0)),
            pl.BlockSpec((tm, MEM_W), lambda i: (i, 0)),
            pl.BlockSpec((tm, d), lambda i: (i, 0)),
            _const_spec((MIX_W, d)),
            _const_spec((MEM_W, d)),
        ] + _ffn_specs(d, layer, half),
        out_specs=pl.BlockSpec((tm, d), lambda i: (i, 0)),
        compiler_params=_cparams("parallel"),
        name="mix_out_ffn",
    )(y_mix, y_mem, x, w_out[:MIX_W].astype(BF16), w_out[MIX_W:].astype(BF16), g.reshape(1, d), wg, wu, wd)


PROJ_TM = 1024


def _mem_kv_body(m_ref, g_ref, wk_ref, wvT_ref, kn_ref, k_ref, vT_ref):
    h = _rms(m_ref[0], g_ref[...]).astype(BF16)
    kT = _dot_nt(wk_ref[...], h)
    kT = _heads_norm_rope(kT, _tile_lanes(kn_ref[...], kT.shape[-1]), None, None)
    k_ref[0] = kT.T.astype(BF16)
    vT_ref[0] = _augment_values(_dot_nt(wvT_ref[...], h)).astype(BF16)


def _mem_kv(mem, g, w_kv, kn):
    b, m, d = mem.shape
    return pl.pallas_call(
        _mem_kv_body,
        out_shape=(jax.ShapeDtypeStruct((b, m, MEM_W), BF16), jax.ShapeDtypeStruct((b, H_MEM * VAUG, m), BF16)),
        grid=(b,),
        in_specs=[
            pl.BlockSpec((1, m, d), lambda i: (i, 0, 0)),
            _const_spec((1, d)),
            _const_spec((MEM_W, d)),
            _const_spec((MEM_W, d)),
            _const_spec((HD, LANES)),
        ],
        out_specs=(pl.BlockSpec((1, m, MEM_W), lambda i: (i, 0, 0)),
                   pl.BlockSpec((1, H_MEM * VAUG, m), lambda i: (i, 0, 0))),
        compiler_params=_cparams("parallel"),
        name="mem_kv",
    )(mem, g.reshape(1, d), w_kv[:, :MEM_W].T.astype(BF16), w_kv[:, MEM_W:].T.astype(BF16), _lane_gain(kn))


MEM_TQ = 1024


def _pad_rows(blk, half, total=LANES):
    z = jnp.zeros_like(blk)
    parts = [z] * (total // HD)
    parts[half] = blk
    return jnp.concatenate(parts, axis=0)


def _mem_attn_body(qT_ref, k_ref, vT_ref, o_ref, s_ref, mc_ref, m_ref, acc_ref):
    n_mem = k_ref.shape[1]
    outs = _staged_attention(
        H_MEM, 0, 1, n_mem,
        lambda h, r0: _dot(k_ref[0, :, (h // 2) * LANES:(h // 2 + 1) * LANES],
                           _pad_rows(qT_ref[h * HD:(h + 1) * HD, :], h % 2)),
        lambda h, r0: vT_ref[0, h * VAUG:(h + 1) * VAUG, :], s_ref, mc_ref, m_ref, acc_ref)
    o_ref[...] = jnp.concatenate(outs, axis=0).T.astype(BF16)


def _mem_attn(qmT, k, vT, seq):
    t = qmT.shape[1]
    b, m, _ = k.shape
    tq = min(MEM_TQ, seq)
    nq = seq // tq
    return pl.pallas_call(
        _mem_attn_body,
        out_shape=jax.ShapeDtypeStruct((t, MEM_W), BF16),
        grid=(b, nq),
        in_specs=[
            pl.BlockSpec((MEM_W, tq), lambda bi, i: (0, bi * nq + i)),
            pl.BlockSpec((1, m, MEM_W), lambda bi, i: (bi, 0, 0)),
            pl.BlockSpec((1, H_MEM * VAUG, m), lambda bi, i: (bi, 0, 0)),
        ],
        out_specs=pl.BlockSpec((tq, MEM_W), lambda bi, i: (bi * nq + i, 0)),
        scratch_shapes=[pltpu.VMEM((2, H_MEM, m, tq), F32), pltpu.VMEM((2, H_MEM * SUBLANES, tq), F32),
                        pltpu.VMEM((SUBLANES, tq), F32), pltpu.VMEM((H_MEM * VAUG, tq), F32)],
        compiler_params=_cparams("parallel", "parallel"),
        name="mem_attn",
    )(qmT, k, vT)


def _ret_proj_body(x_ref, g_ref, wtok_ref, wfeat_ref, cosT_ref, sinT_ref, cosF_ref, sinF_ref, mqn_ref,
                   q_ref, v_ref, gate_ref, kT_ref, qmT_ref):
    h = _rms(x_ref[...], g_ref[...]).astype(BF16)
    tok = _dot(h, wtok_ref[...])
    nq, nv = RET_H * RET_DK, RET_H * RET_DV
    q = tok[:, :nq]
    lane = lax.broadcasted_iota(I32, q.shape, 1)
    rot = jnp.where(lane % HD < HALF, pltpu.roll(q, nq - HALF, axis=1), pltpu.roll(q, HALF, axis=1))
    q_ref[...] = (q * _tile_lanes(cosT_ref[...], nq) + rot * _tile_lanes(sinT_ref[...], nq)).astype(BF16)
    v_ref[...] = tok[:, nq:nq + nv].astype(BF16)
    gate_ref[...] = tok[:, nq + nv:]
    feat = _dot_nt(wfeat_ref[...], h)
    tm = feat.shape[-1]
    cos, sin = cosF_ref[...], sinF_ref[...]
    kT = _heads_norm_rope(feat[:nq], None, cos, sin) * (RET_DK ** -0.5)
    kT_ref[...] = kT.astype(BF16)
    qm = _heads_norm_rope(feat[nq:], _tile_lanes(mqn_ref[...], tm), None, None) * QSCALE
    qmT_ref[...] = qm.astype(BF16)


def _token_rope_tables(seq):
    cosF, sinF = _rope_tables(jnp.arange(seq))
    cos = jnp.tile(cosF.T, (1, LANES // HALF))
    sign = jnp.where((jnp.arange(LANES) % HD) < HALF, -1.0, 1.0).astype(F32)
    sin = jnp.tile(sinF.T, (1, LANES // HALF)) * sign[None, :]
    return cos, sin


def _ret_proj(x, g, w_in, mem_qn, seq):
    t, d = x.shape
    tm = min(PROJ_TM, seq)
    npos = seq // tm
    wq, wk, wv, wg, wqm = jnp.split(w_in, np.cumsum(RET_SIZES)[:-1].tolist(), axis=1)
    wtok = jnp.concatenate([wq, wv, wg], axis=1).astype(BF16)
    wfeat = jnp.concatenate([wk, wqm], axis=1).T.astype(BF16)
    cosT, sinT = _token_rope_tables(seq)
    cosF, sinF = _rope_tables(jnp.arange(seq))
    nq, nv = RET_H * RET_DK, RET_H * RET_DV
    row = lambda n: pl.BlockSpec((tm, n), lambda i: (i, 0))
    col = lambda n: pl.BlockSpec((n, tm), lambda i: (0, i))
    return pl.pallas_call(
        _ret_proj_body,
        out_shape=(
            jax.ShapeDtypeStruct((t, nq), BF16), jax.ShapeDtypeStruct((t, nv), BF16),
            jax.ShapeDtypeStruct((t, nv), F32), jax.ShapeDtypeStruct((nq, t), BF16),
            jax.ShapeDtypeStruct((MEM_W, t), BF16),
        ),
        grid=(t // tm,),
        in_specs=[
            row(d), _const_spec((1, d)), _const_spec(wtok.shape), _const_spec(wfeat.shape),
            pl.BlockSpec((tm, LANES), lambda i: (i % npos, 0)), pl.BlockSpec((tm, LANES), lambda i: (i % npos, 0)),
            pl.BlockSpec((HALF, tm), lambda i: (0, i % npos)), pl.BlockSpec((HALF, tm), lambda i: (0, i % npos)),
            _const_spec((HD, LANES)),
        ],
        out_specs=(row(nq), row(nv), row(nv), col(nq), col(MEM_W)),
        compiler_params=_cparams("parallel"),
        name="ret_proj",
    )(x, g.reshape(1, d), wtok, wfeat, cosT, sinT, cosF, sinF, _lane_gain(mem_qn))


RET_TL = 1024


def _ret_body(q_ref, kT_ref, v_ref, gate_ref, y_ref, state_ref):
    @pl.when(pl.program_id(1) == 0)
    def _():
        state_ref[...] = jnp.zeros_like(state_ref)

    c = RET_CHUNK
    ii = lax.broadcasted_iota(I32, (c, c), 0).astype(F32)
    jj = lax.broadcasted_iota(I32, (c, c), 1).astype(F32)
    diff = ii - jj
    jk = lax.broadcasted_iota(I32, (RET_DK, c), 1).astype(F32)
    log_gs = [math.log(1.0 - 2.0 ** (-5.0 - hh)) for hh in range(RET_H)]
    dmasks = [jnp.where(diff >= 0, jnp.exp(lg * jnp.maximum(diff, 0.0)), 0.0) for lg in log_gs]
    xis = [jnp.exp(lg * (ii + 1.0)) for lg in log_gs]
    zetas = [jnp.exp(lg * (c - 1.0 - jk)) for lg in log_gs]
    for n in range(q_ref.shape[0] // c):
        rows = slice(n * c, (n + 1) * c)
        for hh in range(RET_H):
            dmask, xi, zeta, decay = dmasks[hh], xis[hh], zetas[hh], math.exp(log_gs[hh] * c)
            pair = slice((hh // 2) * LANES, (hh // 2 + 1) * LANES)
            q2 = q_ref[rows, pair]
            kT = kT_ref[hh * RET_DK:(hh + 1) * RET_DK, rows]
            v = v_ref[rows, hh * RET_DV:(hh + 1) * RET_DV]
            state = state_ref[hh]
            inner = _dot(q2, _pad_rows(kT, hh % 2)) * dmask
            o = _dot(inner.astype(BF16), v)
            o = o + _dot(q2, _pad_rows(state.astype(BF16), hh % 2)) * xi
            kv = _dot((kT.astype(F32) * zeta).astype(BF16), v)
            state_ref[hh] = state * decay + kv
            mu = jnp.mean(o, axis=-1, keepdims=True)
            var = jnp.mean(jnp.square(o - mu), axis=-1, keepdims=True)
            o = (o - mu) * lax.rsqrt(var + EPS)
            gte = gate_ref[rows, hh * RET_DV:(hh + 1) * RET_DV]
            y_ref[rows, hh * RET_DV:(hh + 1) * RET_DV] = (gte * jax.nn.sigmoid(gte) * o).astype(BF16)


def _retention(q, kT, v, gate, batch, seq):
    t = q.shape[0]
    tl = min(RET_TL, seq)
    nl = seq // tl
    nq, nv = RET_H * RET_DK, RET_H * RET_DV
    return pl.pallas_call(
        _ret_body,
        out_shape=jax.ShapeDtypeStruct((t, nv), BF16),
        grid=(batch, nl),
        in_specs=[
            pl.BlockSpec((tl, nq), lambda b, i: (b * nl + i, 0)),
            pl.BlockSpec((nq, tl), lambda b, i: (0, b * nl + i)),
            pl.BlockSpec((tl, nv), lambda b, i: (b * nl + i, 0)),
            pl.BlockSpec((tl, nv), lambda b, i: (b * nl + i, 0)),
        ],
        out_specs=pl.BlockSpec((tl, nv), lambda b, i: (b * nl + i, 0)),
        scratch_shapes=[pltpu.VMEM((RET_H, RET_DK, RET_DV), F32)],
        compiler_params=_cparams("parallel", "arbitrary"),
        name="retention",
    )(q, kT, v, gate)


def _dsa_proj_body(x_ref, g_ref, w_ref, cos_ref, sin_ref, qn_ref, kn_ref, mqn_ref,
                   kk_ref, qT_ref, iqT_ref, vT_ref, iwT_ref, qmT_ref):
    h = _rms(x_ref[...], g_ref[...]).astype(BF16)
    nq, niq = DSA_H * HD, IDX_H * IDX_D
    q = _dot_nt(w_ref[:nq, :], h)
    iq = _dot_nt(w_ref[nq:nq + niq, :], h)
    feat = _dot_nt(w_ref[nq + niq:, :], h)
    tm = feat.shape[-1]
    cos, sin = cos_ref[...], sin_ref[...]
    o = 0
    k = feat[o:o + HD]; o += HD
    ik = feat[o:o + IDX_D]; o += IDX_D
    v = feat[o:o + HD]; o += HD
    qm = feat[o:o + MEM_W]; o += MEM_W
    iw = feat[o:o + IDX_H]
    qT_ref[...] = (_heads_norm_rope(q, _tile_lanes(qn_ref[...], tm), cos, sin) * QSCALE).astype(BF16)
    iqT_ref[...] = _heads_norm_rope(iq, None, cos, sin).astype(BF16)
    k = _heads_norm_rope(k, _tile_lanes(kn_ref[...], tm), cos, sin)
    ik = _heads_norm_rope(ik, None, cos, sin)
    kk_ref[...] = jnp.concatenate([k, ik], axis=0).T.astype(BF16)
    vT_ref[...] = _augment_values(v).astype(BF16)
    iwT_ref[...] = iw
    qmT_ref[...] = (_heads_norm_rope(qm, _tile_lanes(mqn_ref[...], tm), None, None) * QSCALE).astype(BF16)


def _dsa_proj(x, g, w_in, qn, kn, mem_qn, seq):
    t, d = x.shape
    tm = min(PROJ_TM, seq)
    npos = seq // tm
    wq, wk, wv, wiq, wik, wiw, wqm = jnp.split(w_in, np.cumsum(DSA_SIZES)[:-1].tolist(), axis=1)
    wfeat = jnp.concatenate([wq, wiq, wk, wik, wv, wqm, wiw], axis=1).T.astype(BF16)
    cosF, sinF = _rope_tables(jnp.arange(seq))
    col = lambda n: pl.BlockSpec((n, tm), lambda i: (0, i))
    tab = pl.BlockSpec((HALF, tm), lambda i: (0, i % npos))
    gain = _const_spec((HD, LANES))
    return pl.pallas_call(
        _dsa_proj_body,
        out_shape=(
            jax.ShapeDtypeStruct((t, LANES), BF16), jax.ShapeDtypeStruct((DSA_H * HD, t), BF16),
            jax.ShapeDtypeStruct((IDX_H * IDX_D, t), BF16), jax.ShapeDtypeStruct((VAUG, t), BF16),
            jax.ShapeDtypeStruct((IDX_H, t), F32), jax.ShapeDtypeStruct((MEM_W, t), BF16),
        ),
        grid=(t // tm,),
        in_specs=[pl.BlockSpec((tm, d), lambda i: (i, 0)), _const_spec((1, d)), _const_spec(wfeat.shape),
                  tab, tab, gain, gain, gain],
        out_specs=(pl.BlockSpec((tm, LANES), lambda i: (i, 0)), col(DSA_H * HD), col(IDX_H * IDX_D), col(VAUG),
                   col(IDX_H), col(MEM_W)),
        compiler_params=_cparams("parallel"),
        name="dsa_proj",
    )(x, g.reshape(1, d), wfeat, cosF, sinF, _lane_gain(qn), _lane_gain(kn), _lane_gain(mem_qn))


DSA_TQ = 256
DSA_TK = 256
HALF_BITS = 16
INT16_MIN = -2 ** (HALF_BITS - 1)


def _dsa_attn_body(kk_ref, qT_ref, iqT_ref, vT_ref, iwT_ref, y_ref, keys_ref, hi_ref, lo_ref, bias_ref, s_ref, mc_ref,
                   m_ref, acc_ref, *, topk, seq, tile):
    tq, tk = DSA_TQ, DSA_TK
    qs = tile * tq
    nkc = (qs + tq) // tk
    qpos = qs + lax.broadcasted_iota(I32, (1, tq), 1)
    row = lax.broadcasted_iota(I32, (tk, tq), 0)

    def over_chunks(body, carry):
        for c in range(nkc):
            carry = body(c * tk, carry)
        return carry

    def score_chunk(r0, carry):
        kkc = kk_ref[pl.ds(r0, tk), :]
        acc = jnp.zeros((tk, tq), F32)
        for h in range(IDX_H):
            r = _dot(kkc, _pad_rows(iqT_ref[h * IDX_D:(h + 1) * IDX_D, :], 1))
            acc = acc + jnp.maximum(r, 0.0) * iwT_ref[h:h + 1, :]
        sc = jnp.where(r0 + row <= qpos, acc + 0.0, -jnp.inf)
        bits = pltpu.bitcast(sc, I32)
        key = jnp.where(bits < 0, bits ^ jnp.int32(0x7FFFFFFF), bits)
        keys_ref[pl.ds(r0, tk), :] = key
        hi_ref[pl.ds(r0, tk), :] = lax.shift_right_arithmetic(key, HALF_BITS).astype(I16)
        return carry

    over_chunks(score_chunk, 0)

    def count(pred):
        def body(r0, acc):
            m = pred(keys_ref[pl.ds(r0, tk), :], r0 + row)
            return acc + jnp.sum(m.astype(I32).reshape(tk // SUBLANES, SUBLANES, tq), axis=0)
        acc = over_chunks(body, jnp.zeros((SUBLANES, tq), I32))
        return jnp.sum(acc, axis=0, keepdims=True)

    def fold16(mask):
        ones = jnp.where(mask, jnp.int16(1), jnp.int16(0))
        parts = [ones[i:i + PACKED_SUBLANES] for i in range(0, tk, PACKED_SUBLANES)]
        while len(parts) > 1:
            parts = [a + b for a, b in zip(parts[::2], parts[1::2])]
        return parts[0]

    zeros16 = jnp.zeros((PACKED_SUBLANES, tq), I16)
    total16 = lambda acc: jnp.sum(acc.astype(I32), axis=0, keepdims=True)

    def count16(ref, pred):
        def body(r0, acc):
            return acc + fold16(pred(ref[pl.ds(r0, tk), :]))
        return total16(over_chunks(body, zeros16))

    def kth_largest16(ref, k):
        thr = jnp.where(count16(ref, lambda v: v >= 0) >= k, jnp.int32(0), jnp.int32(INT16_MIN))

        def bit_step(b, thr):
            cand = thr | lax.shift_left(jnp.int32(1), (HALF_BITS - 2) - b)
            c16 = cand.astype(I16)
            return jnp.where(count16(ref, lambda v: v >= c16) >= k, cand, thr)

        return lax.fori_loop(0, HALF_BITS - 1, bit_step, thr)

    thr_hi = kth_largest16(hi_ref, topk)
    thr_hi16 = thr_hi.astype(I16)

    def low_half_chunk(r0, above):
        hi = hi_ref[pl.ds(r0, tk), :]
        lo = ((keys_ref[pl.ds(r0, tk), :] & jnp.int32(0xFFFF)) + jnp.int32(INT16_MIN)).astype(I16)
        lo_ref[pl.ds(r0, tk), :] = jnp.where(hi == thr_hi16, lo, jnp.int16(INT16_MIN))
        return above + fold16(hi > thr_hi16)

    n_above = total16(over_chunks(low_half_chunk, zeros16))
    need_lo = topk - n_above
    thr_lo = kth_largest16(lo_ref, need_lo)
    thr = lax.shift_left(thr_hi, HALF_BITS) | (thr_lo - jnp.int32(INT16_MIN))
    def tie_search():
        need = topk - count(lambda k, _: k > thr)

        def idx_step(b, q):
            cand = q | lax.shift_left(jnp.int32(1), (seq.bit_length() - 2) - b)
            return jnp.where(count(lambda k, idx: (k == thr) & (idx < cand)) < need, cand, q)

        return lax.fori_loop(0, seq.bit_length() - 1, idx_step, jnp.zeros((1, tq), I32))

    thr_lo16 = thr_lo.astype(I16)
    has_ties = jnp.max(n_above + count16(lo_ref, lambda v: v >= thr_lo16)) > topk
    last = lax.cond(has_ties, tie_search, lambda: jnp.full((1, tq), seq, I32))

    def bias_chunk(r0, carry):
        k = keys_ref[pl.ds(r0, tk), :]
        idx = r0 + row
        sel = (k > thr) | ((k == thr) & (idx <= last))
        bias_ref[pl.ds(r0, tk), :] = jnp.where(sel & (idx <= qpos), 0.0, NEG)
        return carry

    over_chunks(bias_chunk, 0)

    outs = _staged_attention(
        DSA_H, 0, nkc, tk,
        lambda h, r0: _dot(kk_ref[pl.ds(r0, tk), :], _pad_rows(qT_ref[h * HD:(h + 1) * HD, :], 0))
        + bias_ref[pl.ds(r0, tk), :],
        lambda h, r0: vT_ref[:, pl.ds(r0, tk)], s_ref, mc_ref, m_ref, acc_ref)
    y_ref[...] = jnp.concatenate(outs, axis=0).T.astype(BF16)


def _dsa_attn(kk, qT, iqT, vT, iwT, batch, seq):
    t = kk.shape[0]
    tq = DSA_TQ
    nq = seq // tq
    topk = min(DSA_TOPK_MAX, seq // 4)
    def one_tile(i):
        keys = (i + 1) * tq
        colq = lambda n: pl.BlockSpec((n, tq), lambda b: (0, b * nq + i))
        return pl.pallas_call(
            functools.partial(_dsa_attn_body, topk=topk, seq=seq, tile=i),
            out_shape=jax.ShapeDtypeStruct((batch, tq, DSA_H * HD), BF16),
            grid=(batch,),
            in_specs=[
                pl.BlockSpec((keys, LANES), lambda b: (b * (seq // keys), 0)) if seq % keys == 0 else
                pl.BlockSpec((seq, LANES), lambda b: (b, 0)),
                colq(DSA_H * HD), colq(IDX_H * IDX_D),
                pl.BlockSpec((VAUG, seq), lambda b: (0, b)),
                colq(IDX_H),
            ],
            out_specs=pl.BlockSpec((None, tq, DSA_H * HD), lambda b: (b, 0, 0)),
            scratch_shapes=[pltpu.VMEM((keys, tq), I32), pltpu.VMEM((keys, tq), I16), pltpu.VMEM((keys, tq), I16),
                            pltpu.VMEM((keys, tq), F32),
                            pltpu.VMEM((2, DSA_H, DSA_TK, tq), F32), pltpu.VMEM((2, DSA_H * SUBLANES, tq), F32),
                            pltpu.VMEM((2 * SUBLANES, tq), F32), pltpu.VMEM((DSA_H * VAUG, tq), F32)],
            compiler_params=_cparams("parallel"),
            name=f"dsa_attn_tile{i}",
        )(kk, qT, iqT, vT, iwT)

    tiles = [one_tile(i) for i in range(nq)]
    return jnp.stack(tiles, axis=1).reshape(t, DSA_H * HD)


NSA_KV = NSA_G * HD
NSA_GATES = NSA_H * 3
NSA_GATES_PAD = -NSA_GATES % SUBLANES


def _nsa_proj_body(x_ref, g_ref, wtok_ref, wfeat_ref, cos_ref, sin_ref, qn_ref, kns_ref, knw_ref, mqn_ref,
                   kc_ref, vc_ref, ks_ref, kw_ref, qT_ref, vsT_ref, vwT_ref, gT_ref, qmT_ref):
    h = _rms(x_ref[...], g_ref[...]).astype(BF16)
    tok = _dot(h, wtok_ref[...])
    kc_ref[...] = tok[:, :NSA_KV]
    vc_ref[...] = tok[:, NSA_KV:]
    nq = NSA_H * HD
    q = _dot_nt(wfeat_ref[:nq, :], h)
    kk = _dot_nt(wfeat_ref[nq:nq + 2 * NSA_KV, :], h)
    feat = _dot_nt(wfeat_ref[nq + 2 * NSA_KV:, :], h)
    tm = feat.shape[-1]
    cos, sin = cos_ref[...], sin_ref[...]
    ks, kw = kk[:NSA_KV], kk[NSA_KV:]
    o = 0
    vs = feat[o:o + NSA_KV]; o += NSA_KV
    vw = feat[o:o + NSA_KV]; o += NSA_KV
    qm = feat[o:o + MEM_W]; o += MEM_W
    gates = feat[o:]
    qT_ref[...] = (_heads_norm_rope(q, _tile_lanes(qn_ref[...], tm), cos, sin) * QSCALE).astype(BF16)
    ks_ref[...] = _heads_norm_rope(ks, _tile_lanes(kns_ref[...], tm), cos, sin).T.astype(BF16)
    kw_ref[...] = _heads_norm_rope(kw, _tile_lanes(knw_ref[...], tm), cos, sin).T.astype(BF16)
    vsT_ref[...] = _augment_values(vs).astype(BF16)
    vwT_ref[...] = _augment_values(vw).astype(BF16)
    gT_ref[...] = jax.nn.sigmoid(gates)
    qmT_ref[...] = (_heads_norm_rope(qm, _tile_lanes(mqn_ref[...], tm), None, None) * QSCALE).astype(BF16)


def _nsa_proj(x, g, w_in, qn, kn, mem_qn, seq):
    t, d = x.shape
    tm = min(PROJ_TM, seq)
    npos = seq // tm
    wq, wkc, wvc, wks, wvs, wkw, wvw, wgt, wqm = jnp.split(w_in, np.cumsum(NSA_SIZES)[:-1].tolist(), axis=1)
    wtok = jnp.concatenate([wkc, wvc], axis=1).astype(BF16)
    wgt = jnp.pad(wgt, ((0, 0), (0, NSA_GATES_PAD)))
    wfeat = jnp.concatenate([wq, wks, wkw, wvs, wvw, wqm, wgt], axis=1).T.astype(BF16)
    cosF, sinF = _rope_tables(jnp.arange(seq))
    row = lambda n: pl.BlockSpec((tm, n), lambda i: (i, 0))
    col = lambda n: pl.BlockSpec((n, tm), lambda i: (0, i))
    tab = pl.BlockSpec((HALF, tm), lambda i: (0, i % npos))
    gain = _const_spec((HD, LANES))
    ngt = NSA_GATES + NSA_GATES_PAD
    return pl.pallas_call(
        _nsa_proj_body,
        out_shape=(
            jax.ShapeDtypeStruct((t, NSA_KV), F32), jax.ShapeDtypeStruct((t, NSA_KV), F32),
            jax.ShapeDtypeStruct((t, NSA_KV), BF16), jax.ShapeDtypeStruct((t, NSA_KV), BF16),
            jax.ShapeDtypeStruct((NSA_H * HD, t), BF16), jax.ShapeDtypeStruct((NSA_G * VAUG, t), BF16),
            jax.ShapeDtypeStruct((NSA_G * VAUG, t), BF16), jax.ShapeDtypeStruct((ngt, t), F32),
            jax.ShapeDtypeStruct((MEM_W, t), BF16),
        ),
        grid=(t // tm,),
        in_specs=[row(d), _const_spec((1, d)), _const_spec(wtok.shape), _const_spec(wfeat.shape),
                  tab, tab, gain, gain, gain, gain],
        out_specs=(row(NSA_KV), row(NSA_KV), row(NSA_KV), row(NSA_KV), col(NSA_H * HD), col(NSA_G * VAUG),
                   col(NSA_G * VAUG), col(ngt), col(MEM_W)),
        compiler_params=_cparams("parallel"),
        name="nsa_proj",
    )(x, g.reshape(1, d), wtok, wfeat, cosF, sinF, _lane_gain(qn), _lane_gain(kn[1]), _lane_gain(kn[2]),
      _lane_gain(mem_qn))


def _nsa_cmp_body(k01_ref, k23_ref, v01_ref, v23_ref, pk_ref, pv_ref, wk_ref, wv_ref, kn_ref, cos_ref, sin_ref,
                  k_ref, vT_ref):
    n = k_ref.shape[1]

    def compress(lo_ref, hi_ref, pos, w_ref):
        parts = []
        for l in range(CMP_S):
            parts += [lo_ref[pl.ds(l, n, stride=CMP_S), :], hi_ref[pl.ds(l, n, stride=CMP_S), :]]
        x = jnp.concatenate(parts, axis=1)
        xa = (x + pos[0:1]).astype(BF16)
        xb = (pltpu.roll(x, n - 1, axis=0) + pos[1:2]).astype(BF16)
        return _dot_nt(w_ref[0], xa) + _dot_nt(w_ref[1], xb)

    kT = compress(k01_ref, k23_ref, pk_ref[...], wk_ref)
    kT = _heads_norm_rope(kT, _tile_lanes(kn_ref[...], kT.shape[-1]), cos_ref[...], sin_ref[...])
    k_ref[0] = kT.T.astype(BF16)
    vT_ref[0] = compress(v01_ref, v23_ref, pv_ref[...], wv_ref).astype(BF16)


def _nsa_cmp_weights(w, pos):
    halves, width = CMP_L // CMP_S, CMP_S * NSA_KV
    w_eld = w.reshape(halves, CMP_S, HD, HD).transpose(0, 3, 1, 2)
    per_group = jnp.broadcast_to(w_eld[:, :, :, None, :], (halves, HD, CMP_S, NSA_G, HD)).reshape(halves, HD, width)
    rows = np.arange(NSA_KV)[:, None] // HD
    cols = (np.arange(width)[None, :] // HD) % NSA_G
    wt = jnp.where(jnp.asarray(rows == cols)[None], jnp.tile(per_group, (1, NSA_G, 1)), 0.0)
    p = jnp.broadcast_to(pos.reshape(halves, CMP_S, 1, HD), (halves, CMP_S, NSA_G, HD))
    return wt.astype(BF16), p.reshape(halves, width).astype(F32)


def _nsa_cmp(kc, vc, pos_k, pos_v, wk, wv, kn0, batch, seq):
    n = seq // CMP_S
    wkt, pk = _nsa_cmp_weights(wk, pos_k)
    wvt, pv = _nsa_cmp_weights(wv, pos_v)
    cosE, sinE = _rope_tables(jnp.arange(n) * CMP_S + (CMP_L - 1))
    lanes_lo = pl.BlockSpec((seq, LANES), lambda b: (b, 0))
    lanes_hi = pl.BlockSpec((seq, LANES), lambda b: (b, 1))
    return pl.pallas_call(
        _nsa_cmp_body,
        out_shape=(jax.ShapeDtypeStruct((batch, n, NSA_KV), BF16), jax.ShapeDtypeStruct((batch, NSA_KV, n), BF16)),
        grid=(batch,),
        in_specs=[lanes_lo, lanes_hi, lanes_lo, lanes_hi, _const_spec(pk.shape), _const_spec(pv.shape),
                  _const_spec(wkt.shape), _const_spec(wvt.shape), _const_spec((HD, LANES)), _const_spec((HALF, n)),
                  _const_spec((HALF, n))],
        out_specs=(pl.BlockSpec((1, n, NSA_KV), lambda b: (b, 0, 0)), pl.BlockSpec((1, NSA_KV, n), lambda b: (b, 0, 0))),
        compiler_params=_cparams("parallel"),
        name="nsa_cmp",
    )(kc, kc, vc, vc, pk, pv, wkt, wvt, _lane_gain(kn0), cosE, sinE)


NSA_TQ = 256
NSA_TK = 256


def _staged_attention(nheads, lo, hi, tk, logits, values, s_ref, mc_ref, m_ref, acc_ref):
    tq = s_ref.shape[-1]
    fold = lambda a: a.reshape(tk // SUBLANES, SUBLANES, tq)
    m_ref[...] = jnp.full(m_ref.shape, -jnp.inf, F32)
    acc_ref[...] = jnp.zeros(acc_ref.shape, F32)

    start = lambda c: c * tk if isinstance(c, int) else pl.multiple_of(c * tk, tk)

    def stage(h, c, slot):
        s = logits(h, start(c))
        s_ref[slot, h] = s
        mc_ref[slot, h * SUBLANES:(h + 1) * SUBLANES, :] = jnp.max(fold(s), axis=0)

    def consume(h, c, slot):
        g8, hrows = slice(h * SUBLANES, (h + 1) * SUBLANES), slice(h * VAUG, (h + 1) * VAUG)
        m_old = m_ref[h:h + 1, :]
        m_new = jnp.maximum(m_old, jnp.max(mc_ref[slot, g8, :], axis=0, keepdims=True))
        alpha = jnp.exp2(m_old - m_new)
        p = jnp.exp2(s_ref[slot, h] - m_new)
        acc_ref[hrows, :] = acc_ref[hrows, :] * alpha + _dot(values(h, start(c)), p.astype(BF16))
        m_ref[h:h + 1, :] = m_new

    def consume_and_stage_next(c, slot):
        for h in range(nheads):
            consume(h, c, slot)
            stage(h, c + 1, 1 - slot)

    def finish(c, slot):
        for h in range(nheads):
            consume(h, c, slot)

    for h in range(nheads):
        stage(h, lo, 0)
    n_fused = hi - 1 - lo

    def two_chunks(j, carry):
        c = lo + 2 * j
        consume_and_stage_next(c, 0)
        consume_and_stage_next(c + 1, 1)
        return carry

    if isinstance(n_fused, int):
        for i in range(n_fused):
            consume_and_stage_next(lo + i, i % 2)
        finish(hi - 1, n_fused % 2)
    else:
        lax.fori_loop(0, n_fused // 2, two_chunks, 0)

        @pl.when(n_fused % 2 == 1)
        def _():
            consume_and_stage_next(hi - 2, 0)
            finish(hi - 1, 1)

        @pl.when(n_fused % 2 == 0)
        def _():
            finish(hi - 1, 0)

    return [acc_ref[h * VAUG:h * VAUG + HD, :] / acc_ref[h * VAUG + HD:h * VAUG + HD + 1, :] for h in range(nheads)]


def _nsa_attn_body(qT_ref, gT_ref, ks_ref, vsT_ref, kw_ref, vwT_ref, kc_ref, vcT_ref, cov_ref, y_ref,
                   bias_ref, out_ref, s_ref, mc_ref, m_ref, acc_ref, *, seq, tile):
    tq, tk = NSA_TQ, NSA_TK
    qs = tile * tq
    nkc = (qs + tq) // tk
    wlo = max((qs - WIN) // tk, 0)
    qpos = qs + lax.broadcasted_iota(I32, (1, tq), 1)
    ncr = seq // CMP_S
    nblk = seq // SLC_L
    n_sel = min(SLC_N_MAX, nblk)
    valid_c = CMP_S * lax.broadcasted_iota(I32, (ncr, tq), 0) + (CMP_L - 1) <= qpos
    any_c = jnp.where(qpos >= CMP_L - 1, 1.0, 0.0)
    jrow = lax.broadcasted_iota(I32, (nblk, tq), 0)
    qblk = qpos // SLC_L
    forced = (jrow == 0) | (jrow == qblk) | (jrow == qblk - 1)
    causal_blk = jrow * SLC_L <= qpos
    krow = lax.broadcasted_iota(I32, (tk, tq), 0)
    brow = lax.broadcasted_iota(I32, (SLC_L, tq), 0)
    group = lambda h: h // NSA_R
    pair = lambda h: slice((group(h) // 2) * LANES, (group(h) // 2 + 1) * LANES)
    grows = lambda h: slice(group(h) * HD, (group(h) + 1) * HD)
    vrows = lambda h: slice(group(h) * VAUG, (group(h) + 1) * VAUG)
    q_of = lambda h: _pad_rows(qT_ref[h * HD:(h + 1) * HD, :], group(h) % 2)
    gate = lambda h, branch: gT_ref[h * 3 + branch:h * 3 + branch + 1, :]

    selbs = []
    for g in range(NSA_G):
        heads = [g * NSA_R + r for r in range(NSA_R)]
        kcm = kc_ref[0][:, pair(heads[0])]
        vcm = vcT_ref[0][grows(heads[0]), :]
        psum = jnp.zeros((ncr, tq), F32)
        for h in heads:
            s = jnp.where(valid_c, _dot(kcm, q_of(h)), NEG)
            p = jnp.exp2(s - jnp.max(s, axis=0, keepdims=True))
            p = p / jnp.sum(p, axis=0, keepdims=True) * any_c
            out_ref[h * HD:(h + 1) * HD, :] = gate(h, 0) * _dot(vcm, p.astype(BF16))
            psum = psum + p
        p_hi = psum.astype(BF16)
        p_lo = (psum - p_hi.astype(F32)).astype(BF16)
        imp = _dot(cov_ref[...], p_hi) + _dot(cov_ref[...], p_lo)
        imp = jnp.where(causal_blk, imp + jnp.where(forced, FORCE, 0.0), NEG)
        rank = jnp.zeros((nblk, tq), I32)
        for j2 in range(nblk):
            rj = imp[j2:j2 + 1]
            beats = (rj > imp) | ((rj == imp) & (j2 < jrow))
            rank = rank + beats.astype(I32)
        selbs.append(jnp.where(rank < n_sel, 0.0, NEG))

    blocks_per_chunk = tk // SLC_L
    for g in range(NSA_G):
        for j in range(nkc * blocks_per_chunk):
            bias_ref[g, j * SLC_L:(j + 1) * SLC_L, :] = jnp.where(j * SLC_L + brow <= qpos, selbs[g][j:j + 1], NEG)

    stats = (s_ref, mc_ref, m_ref, acc_ref)
    slc = _staged_attention(
        NSA_H, 0, nkc, tk,
        lambda h, r0: _dot(ks_ref[pl.ds(r0, tk), pair(h)], q_of(h)) + bias_ref[group(h), pl.ds(r0, tk), :],
        lambda h, r0: vsT_ref[vrows(h), pl.ds(r0, tk)], *stats)
    for h in range(NSA_H):
        out_ref[h * HD:(h + 1) * HD, :] = out_ref[h * HD:(h + 1) * HD, :] + gate(h, 1) * slc[h]

    for c in range(wlo, nkc):
        dist = qpos - (c * tk + krow)
        bias_ref[0, c * tk:(c + 1) * tk, :] = jnp.where((dist >= 0) & (dist < WIN), 0.0, NEG)
    win = _staged_attention(
        NSA_H, wlo, nkc, tk,
        lambda h, r0: _dot(kw_ref[pl.ds(r0, tk), pair(h)], q_of(h)) + bias_ref[0, pl.ds(r0, tk), :],
        lambda h, r0: vwT_ref[vrows(h), pl.ds(r0, tk)], *stats)
    outs = [out_ref[h * HD:(h + 1) * HD, :] + gate(h, 2) * win[h] for h in range(NSA_H)]
    y_ref[...] = jnp.concatenate(outs, axis=0).T.astype(BF16)


def _nsa_attn(qT, gT, ks, vsT, kw, vwT, kcmp, vcmpT, batch, seq):
    t = ks.shape[0]
    tq = NSA_TQ
    nq = seq // tq
    ncr, nblk = seq // CMP_S, seq // SLC_L
    starts = np.arange(ncr) * CMP_S
    sstart = np.arange(nblk) * SLC_L
    cover = (starts[None, :] < sstart[:, None] + SLC_L) & (starts[None, :] + CMP_L > sstart[:, None])
    cover[:, ncr - 1] = False
    tok = pl.BlockSpec((seq, NSA_KV), lambda b: (b, 0))
    feat = pl.BlockSpec((NSA_G * VAUG, seq), lambda b: (0, b))
    cover = jnp.asarray(cover, BF16)

    def one_tile(i):
        colq = lambda n: pl.BlockSpec((n, tq), lambda b: (0, b * nq + i))
        return pl.pallas_call(
            functools.partial(_nsa_attn_body, seq=seq, tile=i),
            out_shape=jax.ShapeDtypeStruct((batch, tq, NSA_H * HD), BF16),
            grid=(batch,),
            in_specs=[
                colq(NSA_H * HD), colq(gT.shape[0]), tok, feat, tok, feat,
                pl.BlockSpec((1, ncr, NSA_KV), lambda b: (b, 0, 0)),
                pl.BlockSpec((1, NSA_KV, ncr), lambda b: (b, 0, 0)),
                _const_spec((nblk, ncr)),
            ],
            out_specs=pl.BlockSpec((None, tq, NSA_H * HD), lambda b: (b, 0, 0)),
            scratch_shapes=[pltpu.VMEM((NSA_G, (i + 1) * tq, tq), F32), pltpu.VMEM((NSA_H * HD, tq), F32),
                            pltpu.VMEM((2, NSA_H, NSA_TK, tq), F32), pltpu.VMEM((2, NSA_H * SUBLANES, tq), F32),
                            pltpu.VMEM((2 * SUBLANES, tq), F32), pltpu.VMEM((NSA_H * VAUG, tq), F32)],
            compiler_params=_cparams("parallel"),
            name=f"nsa_attn_tile{i}",
        )(qT, gT, ks, vsT, kw, vwT, kcmp, vcmpT, cover)

    tiles = [one_tile(i) for i in range(nq)]
    return jnp.stack(tiles, axis=1).reshape(t, NSA_H * HD)


def _nsa_layer_mix(x, g, w_in, qn, kn, pos_k, pos_v, wk, wv, mem_qn, batch, seq):
    kc, vc, ks, kw, qT, vsT, vwT, gT, qmT = _nsa_proj(x, g, w_in, qn, kn, mem_qn, seq)
    kcmp, vcmpT = _nsa_cmp(kc, vc, pos_k, pos_v, wk, wv, kn[0], batch, seq)
    return _nsa_attn(qT, gT, ks, vsT, kw, vwT, kcmp, vcmpT, batch, seq), qmT


def kernel(x, mem, ffn_norm, ffn_w_gate, ffn_w_up, ffn_w_down, mix_norm, w_out, mem_norm, mem_w_kv, mem_qn, mem_kn, ret_w_in, dsa_w_in, dsa_qn, dsa_kn, nsa_w_in, nsa_qn, nsa_kn, nsa_cmp_pos_k, nsa_cmp_pos_v, nsa_cmp_wk, nsa_cmp_wv):
    batch, seq, d = x.shape
    x = x.reshape(batch * seq, d)
    ffn_w = (ffn_w_gate.astype(BF16), ffn_w_up.astype(BF16), ffn_w_down.astype(BF16))
    for i in range(ffn_norm.shape[0]):
        x = _ffn(x, ffn_norm[i, 0], *ffn_w, i, 0)
        kind, j = i % N_MIXERS, i // N_MIXERS
        if kind == 0:
            q, v, gate, kT, qmT = _ret_proj(x, mix_norm[i], ret_w_in[j], mem_qn[i], seq)
            y_mix = _retention(q, kT, v, gate, batch, seq)
        elif kind == 1:
            kk, qT, iqT, vT, iwT, qmT = _dsa_proj(x, mix_norm[i], dsa_w_in[j], dsa_qn[j], dsa_kn[j], mem_qn[i], seq)
            y_mix = _dsa_attn(kk, qT, iqT, vT, iwT, batch, seq)
        else:
            y_mix, qmT = _nsa_layer_mix(x, mix_norm[i], nsa_w_in[j], nsa_qn[j], nsa_kn[j], nsa_cmp_pos_k[j],
                                        nsa_cmp_pos_v[j], nsa_cmp_wk[j], nsa_cmp_wv[j], mem_qn[i], batch, seq)
        mem_k, mem_vT = _mem_kv(mem, mem_norm[i], mem_w_kv[i], mem_kn[i])
        y_mem = _mem_attn(qmT, mem_k, mem_vT, seq)
        x = _mix_out_ffn(y_mix, y_mem, x, w_out[i], ffn_norm[i, 1], *ffn_w, i, 1)
    return x.reshape(batch, seq, d)
```

```python
import functools
import math

import jax
import jax.numpy as jnp
import numpy as np
from jax import lax
from jax.experimental import pallas as pl
from jax.experimental.pallas import tpu as pltpu

D_MODEL = 1024
HD = 64
HALF = HD // 2
H_MIX = 12
H_MEM = 4
MIX_W = H_MIX * HD
MEM_W = H_MEM * HD
D_FF = 2816
ROPE_THETA = 10000.0
EPS = 1e-6
NEG = -1e30
FORCE = 1e9
SCALE = HD ** -0.5
QSCALE = SCALE * math.log2(math.e)

RET_H, RET_DK, RET_DV, RET_CHUNK = 6, 64, 128, 128
DSA_H, IDX_H, IDX_D, DSA_TOPK_MAX = 12, 8, 64, 256
NSA_H, NSA_G, CMP_L, CMP_S, SLC_L, SLC_N_MAX, WIN = 12, 4, 32, 16, 64, 16, 512
NSA_R = NSA_H // NSA_G
N_MIXERS = 3

RET_SIZES = [RET_H * RET_DK, RET_H * RET_DK, RET_H * RET_DV, RET_H * RET_DV, MEM_W]
DSA_SIZES = [DSA_H * HD, HD, HD, IDX_H * IDX_D, IDX_D, IDX_H, MEM_W]
NSA_SIZES = [NSA_H * HD] + [NSA_G * HD] * 6 + [NSA_H * 3, MEM_W]

LANES = 128
SUBLANES = 8
PACKED_SUBLANES = 2 * SUBLANES
VAUG = HD + PACKED_SUBLANES
MXU_N = 256
VMEM_LIMIT_BYTES = 56 * 1024 * 1024

BF16 = jnp.bfloat16
F32 = jnp.float32
I32 = jnp.int32
I16 = jnp.int16


def _cparams(*sem):
    return pltpu.CompilerParams(dimension_semantics=sem, vmem_limit_bytes=VMEM_LIMIT_BYTES)


def _const_spec(shape):
    n = len(shape)
    return pl.BlockSpec(shape, lambda *_: (0,) * n, pipeline_mode=pl.Buffered(1))


def _rms(x, g):
    return x * lax.rsqrt(jnp.mean(x * x, axis=-1, keepdims=True) + EPS) * g


def _dot(a, b):
    return jnp.dot(a, b, preferred_element_type=F32)


def _dot_nt(a, b):
    return lax.dot_general(a, b, (((1,), (1,)), ((), ())), preferred_element_type=F32)


def _tile_lanes(a, n):
    reps = n // a.shape[-1]
    return a if reps == 1 else jnp.concatenate([a] * reps, axis=-1)


def _rope_tables(pos):
    inv = ROPE_THETA ** (-jnp.arange(HALF, dtype=F32) / HALF)
    ang = pos.astype(F32)[:, None] * inv[None, :]
    return jnp.cos(ang).T, jnp.sin(ang).T


def _lane_gain(g):
    return jnp.broadcast_to(g.astype(F32)[:, None], (g.shape[0], LANES))


def _augment_values(v):
    t = v.shape[-1]
    v3 = v.reshape(v.shape[0] // HD, HD, t)
    tail = jnp.where(lax.broadcasted_iota(I32, (v3.shape[0], VAUG - HD, t), 1) == 0, 1.0, 0.0).astype(v.dtype)
    return jnp.concatenate([v3, tail], axis=1).reshape(v3.shape[0] * VAUG, t)


def _heads_norm_rope(x, gain, cos, sin):
    t = x.shape[-1]
    x3 = x.reshape(x.shape[0] // HD, HD, t)
    if gain is not None:
        ms = jnp.sum(x3 * x3, axis=1, keepdims=True) * (1.0 / HD)
        x3 = x3 * lax.rsqrt(ms + EPS) * gain[None]
    if cos is not None:
        x1, x2 = x3[:, :HALF], x3[:, HALF:]
        x3 = jnp.concatenate([x1 * cos[None] - x2 * sin[None], x2 * cos[None] + x1 * sin[None]], axis=1)
    return x3.reshape(x.shape)


FFN_TM = 1024
FFN_TF = MXU_N


def _swiglu_half_step(x, g_ref, wg_ref, wu_ref, wd_ref):
    h = _rms(x, g_ref[...]).astype(BF16)
    acc = jnp.zeros(x.shape, F32)
    for c in range(D_FF // FFN_TF):
        sl = slice(c * FFN_TF, (c + 1) * FFN_TF)
        gate = _dot(h, wg_ref[:, sl])
        up = _dot(h, wu_ref[:, sl])
        act = (gate * jax.nn.sigmoid(gate) * up).astype(BF16)
        acc = acc + _dot(act, wd_ref[sl, :])
    return x + 0.5 * acc


def _ffn_body(x_ref, g_ref, wg_ref, wu_ref, wd_ref, o_ref):
    o_ref[...] = _swiglu_half_step(x_ref[...], g_ref, wg_ref, wu_ref, wd_ref)


def _mix_out_ffn_body(ymix_ref, ymem_ref, x_ref, wmix_ref, wmem_ref, g_ref, wg_ref, wu_ref, wd_ref, o_ref):
    x = x_ref[...] + _dot(ymix_ref[...], wmix_ref[...]) + _dot(ymem_ref[...], wmem_ref[...])
    o_ref[...] = _swiglu_half_step(x, g_ref, wg_ref, wu_ref, wd_ref)


def _ffn_specs(d, layer, half):
    pick = lambda r, c: pl.BlockSpec((None, None, r, c), lambda i: (layer, half, 0, 0), pipeline_mode=pl.Buffered(1))
    return [_const_spec((1, d)), pick(d, D_FF), pick(d, D_FF), pick(D_FF, d)]


def _ffn(x, g, wg, wu, wd, layer, half):
    t, d = x.shape
    tm = min(FFN_TM, t)
    return pl.pallas_call(
        _ffn_body,
        out_shape=jax.ShapeDtypeStruct((t, d), F32),
        grid=(t // tm,),
        in_specs=[pl.BlockSpec((tm, d), lambda i: (i, 0))] + _ffn_specs(d, layer, half),
        out_specs=pl.BlockSpec((tm, d), lambda i: (i, 0)),
        compiler_params=_cparams("parallel"),
        name="ffn",
    )(x, g.reshape(1, d), wg, wu, wd)


def _mix_out_ffn(y_mix, y_mem, x, w_out, g, wg, wu, wd, layer, half):
    t, d = x.shape
    tm = min(FFN_TM, t)
    return pl.pallas_call(
        _mix_out_ffn_body,
        out_shape=jax.ShapeDtypeStruct((t, d), F32),
        grid=(t // tm,),
        in_specs=[
            pl.BlockSpec((tm, MIX_W), lambda i: (i, 0)),
            pl.BlockSpec((tm, MEM_W), lambda i: (i, 0)),
            pl.BlockSpec((tm, d), lambda i: (i, 0)),
            _const_spec((MIX_W, d)),
            _const_spec((MEM_W, d)),
        ] + _ffn_specs(d, layer, half),
        out_specs=pl.BlockSpec((tm, d), lambda i: (i, 0)),
        compiler_params=_cparams("parallel"),
        name="mix_out_ffn",
    )(y_mix, y_mem, x, w_out[:MIX_W].astype(BF16), w_out[MIX_W:].astype(BF16), g.reshape(1, d), wg, wu, wd)


PROJ_TM = 1024


def _mem_kv_body(m_ref, g_ref, wk_ref, wvT_ref, kn_ref, k_ref, vT_ref):
    h = _rms(m_ref[0], g_ref[...]).astype(BF16)
    kT = _dot_nt(wk_ref[...], h)
    kT = _heads_norm_rope(kT, _tile_lanes(kn_ref[...], kT.shape[-1]), None, None)
    k_ref[0] = kT.T.astype(BF16)
    vT_ref[0] = _augment_values(_dot_nt(wvT_ref[...], h)).astype(BF16)


def _mem_kv(mem, g, w_kv, kn):
    b, m, d = mem.shape
    return pl.pallas_call(
        _mem_kv_body,
        out_shape=(jax.ShapeDtypeStruct((b, m, MEM_W), BF16), jax.ShapeDtypeStruct((b, H_MEM * VAUG, m), BF16)),
        grid=(b,),
        in_specs=[
            pl.BlockSpec((1, m, d), lambda i: (i, 0, 0)),
            _const_spec((1, d)),
            _const_spec((MEM_W, d)),
            _const_spec((MEM_W, d)),
            _const_spec((HD, LANES)),
        ],
        out_specs=(pl.BlockSpec((1, m, MEM_W), lambda i: (i, 0, 0)),
                   pl.BlockSpec((1, H_MEM * VAUG, m), lambda i: (i, 0, 0))),
        compiler_params=_cparams("parallel"),
        name="mem_kv",
    )(mem, g.reshape(1, d), w_kv[:, :MEM_W].T.astype(BF16), w_kv[:, MEM_W:].T.astype(BF16), _lane_gain(kn))


MEM_TQ = 1024


def _pad_rows(blk, half, total=LANES):
    z = jnp.zeros_like(blk)
    parts = [z] * (total // HD)
    parts[half] = blk
    return jnp.concatenate(parts, axis=0)


def _mem_attn_body(qT_ref, k_ref, vT_ref, o_ref, s_ref, mc_ref, m_ref, acc_ref):
    n_mem = k_ref.shape[1]
    outs = _staged_attention(
        H_MEM, 0, 1, n_mem,
        lambda h, r0: _dot(k_ref[0, :, (h // 2) * LANES:(h // 2 + 1) * LANES],
                           _pad_rows(qT_ref[h * HD:(h + 1) * HD, :], h % 2)),
        lambda h, r0: vT_ref[0, h * VAUG:(h + 1) * VAUG, :], s_ref, mc_ref, m_ref, acc_ref)
    o_ref[...] = jnp.concatenate(outs, axis=0).T.astype(BF16)


def _mem_attn(qmT, k, vT, seq):
    t = qmT.shape[1]
    b, m, _ = k.shape
    tq = min(MEM_TQ, seq)
    nq = seq // tq
    return pl.pallas_call(
        _mem_attn_body,
        out_shape=jax.ShapeDtypeStruct((t, MEM_W), BF16),
        grid=(b, nq),
        in_specs=[
            pl.BlockSpec((MEM_W, tq), lambda bi, i: (0, bi * nq + i)),
            pl.BlockSpec((1, m, MEM_W), lambda bi, i: (bi, 0, 0)),
            pl.BlockSpec((1, H_MEM * VAUG, m), lambda bi, i: (bi, 0, 0)),
        ],
        out_specs=pl.BlockSpec((tq, MEM_W), lambda bi, i: (bi * nq + i, 0)),
        scratch_shapes=[pltpu.VMEM((2, H_MEM, m, tq), F32), pltpu.VMEM((2, H_MEM * SUBLANES, tq), F32),
                        pltpu.VMEM((SUBLANES, tq), F32), pltpu.VMEM((H_MEM * VAUG, tq), F32)],
        compiler_params=_cparams("parallel", "parallel"),
        name="mem_attn",
    )(qmT, k, vT)


def _ret_proj_body(x_ref, g_ref, wtok_ref, wfeat_ref, cosT_ref, sinT_ref, cosF_ref, sinF_ref, mqn_ref,
                   q_ref, v_ref, gate_ref, kT_ref, qmT_ref):
    h = _rms(x_ref[...], g_ref[...]).astype(BF16)
    tok = _dot(h, wtok_ref[...])
    nq, nv = RET_H * RET_DK, RET_H * RET_DV
    q = tok[:, :nq]
    lane = lax.broadcasted_iota(I32, q.shape, 1)
    rot = jnp.where(lane % HD < HALF, pltpu.roll(q, nq - HALF, axis=1), pltpu.roll(q, HALF, axis=1))
    q_ref[...] = (q * _tile_lanes(cosT_ref[...], nq) + rot * _tile_lanes(sinT_ref[...], nq)).astype(BF16)
    v_ref[...] = tok[:, nq:nq + nv].astype(BF16)
    gate_ref[...] = tok[:, nq + nv:]
    feat = _dot_nt(wfeat_ref[...], h)
    tm = feat.shape[-1]
    cos, sin = cosF_ref[...], sinF_ref[...]
    kT = _heads_norm_rope(feat[:nq], None, cos, sin) * (RET_DK ** -0.5)
    kT_ref[...] = kT.astype(BF16)
    qm = _heads_norm_rope(feat[nq:], _tile_lanes(mqn_ref[...], tm), None, None) * QSCALE
    qmT_ref[...] = qm.astype(BF16)


def _token_rope_tables(seq):
    cosF, sinF = _rope_tables(jnp.arange(seq))
    cos = jnp.tile(cosF.T, (1, LANES // HALF))
    sign = jnp.where((jnp.arange(LANES) % HD) < HALF, -1.0, 1.0).astype(F32)
    sin = jnp.tile(sinF.T, (1, LANES // HALF)) * sign[None, :]
    return cos, sin


def _ret_proj(x, g, w_in, mem_qn, seq):
    t, d = x.shape
    tm = min(PROJ_TM, seq)
    npos = seq // tm
    wq, wk, wv, wg, wqm = jnp.split(w_in, np.cumsum(RET_SIZES)[:-1].tolist(), axis=1)
    wtok = jnp.concatenate([wq, wv, wg], axis=1).astype(BF16)
    wfeat = jnp.concatenate([wk, wqm], axis=1).T.astype(BF16)
    cosT, sinT = _token_rope_tables(seq)
    cosF, sinF = _rope_tables(jnp.arange(seq))
    nq, nv = RET_H * RET_DK, RET_H * RET_DV
    row = lambda n: pl.BlockSpec((tm, n), lambda i: (i, 0))
    col = lambda n: pl.BlockSpec((n, tm), lambda i: (0, i))
    return pl.pallas_call(
        _ret_proj_body,
        out_shape=(
            jax.ShapeDtypeStruct((t, nq), BF16), jax.ShapeDtypeStruct((t, nv), BF16),
            jax.ShapeDtypeStruct((t, nv), F32), jax.ShapeDtypeStruct((nq, t), BF16),
            jax.ShapeDtypeStruct((MEM_W, t), BF16),
        ),
        grid=(t // tm,),
        in_specs=[
            row(d), _const_spec((1, d)), _const_spec(wtok.shape), _const_spec(wfeat.shape),
            pl.BlockSpec((tm, LANES), lambda i: (i % npos, 0)), pl.BlockSpec((tm, LANES), lambda i: (i % npos, 0)),
            pl.BlockSpec((HALF, tm), lambda i: (0, i % npos)), pl.BlockSpec((HALF, tm), lambda i: (0, i % npos)),
            _const_spec((HD, LANES)),
        ],
        out_specs=(row(nq), row(nv), row(nv), col(nq), col(MEM_W)),
        compiler_params=_cparams("parallel"),
        name="ret_proj",
    )(x, g.reshape(1, d), wtok, wfeat, cosT, sinT, cosF, sinF, _lane_gain(mem_qn))


RET_TL = 1024


def _ret_body(q_ref, kT_ref, v_ref, gate_ref, y_ref, state_ref):
    @pl.when(pl.program_id(1) == 0)
    def _():
        state_ref[...] = jnp.zeros_like(state_ref)

    c = RET_CHUNK
    ii = lax.broadcasted_iota(I32, (c, c), 0).astype(F32)
    jj = lax.broadcasted_iota(I32, (c, c), 1).astype(F32)
    diff = ii - jj
    jk = lax.broadcasted_iota(I32, (RET_DK, c), 1).astype(F32)
    log_gs = [math.log(1.0 - 2.0 ** (-5.0 - hh)) for hh in range(RET_H)]
    dmasks = [jnp.where(diff >= 0, jnp.exp(lg * jnp.maximum(diff, 0.0)), 0.0) for lg in log_gs]
    xis = [jnp.exp(lg * (ii + 1.0)) for lg in log_gs]
    zetas = [jnp.exp(lg * (c - 1.0 - jk)) for lg in log_gs]
    for n in range(q_ref.shape[0] // c):
        rows = slice(n * c, (n + 1) * c)
        for hh in range(RET_H):
            dmask, xi, zeta, decay = dmasks[hh], xis[hh], zetas[hh], math.exp(log_gs[hh] * c)
            pair = slice((hh // 2) * LANES, (hh // 2 + 1) * LANES)
            q2 = q_ref[rows, pair]
            kT = kT_ref[hh * RET_DK:(hh + 1) * RET_DK, rows]
            v = v_ref[rows, hh * RET_DV:(hh + 1) * RET_DV]
            state = state_ref[hh]
            inner = _dot(q2, _pad_rows(kT, hh % 2)) * dmask
            o = _dot(inner.astype(BF16), v)
            o = o + _dot(q2, _pad_rows(state.astype(BF16), hh % 2)) * xi
            kv = _dot((kT.astype(F32) * zeta).astype(BF16), v)
            state_ref[hh] = state * decay + kv
            mu = jnp.mean(o, axis=-1, keepdims=True)
            var = jnp.mean(jnp.square(o - mu), axis=-1, keepdims=True)
            o = (o - mu) * lax.rsqrt(var + EPS)
            gte = gate_ref[rows, hh * RET_DV:(hh + 1) * RET_DV]
            y_ref[rows, hh * RET_DV:(hh + 1) * RET_DV] = (gte * jax.nn.sigmoid(gte) * o).astype(BF16)


def _retention(q, kT, v, gate, batch, seq):
    t = q.shape[0]
    tl = min(RET_TL, seq)
    nl = seq // tl
    nq, nv = RET_H * RET_DK, RET_H * RET_DV
    return pl.pallas_call(
        _ret_body,
        out_shape=jax.ShapeDtypeStruct((t, nv), BF16),
        grid=(batch, nl),
        in_specs=[
            pl.BlockSpec((tl, nq), lambda b, i: (b * nl + i, 0)),
            pl.BlockSpec((nq, tl), lambda b, i: (0, b * nl + i)),
            pl.BlockSpec((tl, nv), lambda b, i: (b * nl + i, 0)),
            pl.BlockSpec((tl, nv), lambda b, i: (b * nl + i, 0)),
        ],
        out_specs=pl.BlockSpec((tl, nv), lambda b, i: (b * nl + i, 0)),
        scratch_shapes=[pltpu.VMEM((RET_H, RET_DK, RET_DV), F32)],
        compiler_params=_cparams("parallel", "arbitrary"),
        name="retention",
    )(q, kT, v, gate)


def _dsa_proj_body(x_ref, g_ref, w_ref, cos_ref, sin_ref, qn_ref, kn_ref, mqn_ref,
                   kk_ref, qT_ref, iqT_ref, vT_ref, iwT_ref, qmT_ref):
    h = _rms(x_ref[...], g_ref[...]).astype(BF16)
    nq, niq = DSA_H * HD, IDX_H * IDX_D
    q = _dot_nt(w_ref[:nq, :], h)
    iq = _dot_nt(w_ref[nq:nq + niq, :], h)
    feat = _dot_nt(w_ref[nq + niq:, :], h)
    tm = feat.shape[-1]
    cos, sin = cos_ref[...], sin_ref[...]
    o = 0
    k = feat[o:o + HD]; o += HD
    ik = feat[o:o + IDX_D]; o += IDX_D
    v = feat[o:o + HD]; o += HD
    qm = feat[o:o + MEM_W]; o += MEM_W
    iw = feat[o:o + IDX_H]
    qT_ref[...] = (_heads_norm_rope(q, _tile_lanes(qn_ref[...], tm), cos, sin) * QSCALE).astype(BF16)
    iqT_ref[...] = _heads_norm_rope(iq, None, cos, sin).astype(BF16)
    k = _heads_norm_rope(k, _tile_lanes(kn_ref[...], tm), cos, sin)
    ik = _heads_norm_rope(ik, None, cos, sin)
    kk_ref[...] = jnp.concatenate([k, ik], axis=0).T.astype(BF16)
    vT_ref[...] = _augment_values(v).astype(BF16)
    iwT_ref[...] = iw
    qmT_ref[...] = (_heads_norm_rope(qm, _tile_lanes(mqn_ref[...], tm), None, None) * QSCALE).astype(BF16)


def _dsa_proj(x, g, w_in, qn, kn, mem_qn, seq):
    t, d = x.shape
    tm = min(PROJ_TM, seq)
    npos = seq // tm
    wq, wk, wv, wiq, wik, wiw, wqm = jnp.split(w_in, np.cumsum(DSA_SIZES)[:-1].tolist(), axis=1)
    wfeat = jnp.concatenate([wq, wiq, wk, wik, wv, wqm, wiw], axis=1).T.astype(BF16)
    cosF, sinF = _rope_tables(jnp.arange(seq))
    col = lambda n: pl.BlockSpec((n, tm), lambda i: (0, i))
    tab = pl.BlockSpec((HALF, tm), lambda i: (0, i % npos))
    gain = _const_spec((HD, LANES))
    return pl.pallas_call(
        _dsa_proj_body,
        out_shape=(
            jax.ShapeDtypeStruct((t, LANES), BF16), jax.ShapeDtypeStruct((DSA_H * HD, t), BF16),
            jax.ShapeDtypeStruct((IDX_H * IDX_D, t), BF16), jax.ShapeDtypeStruct((VAUG, t), BF16),
            jax.ShapeDtypeStruct((IDX_H, t), F32), jax.ShapeDtypeStruct((MEM_W, t), BF16),
        ),
        grid=(t // tm,),
        in_specs=[pl.BlockSpec((tm, d), lambda i: (i, 0)), _const_spec((1, d)), _const_spec(wfeat.shape),
                  tab, tab, gain, gain, gain],
        out_specs=(pl.BlockSpec((tm, LANES), lambda i: (i, 0)), col(DSA_H * HD), col(IDX_H * IDX_D), col(VAUG),
                   col(IDX_H), col(MEM_W)),
        compiler_params=_cparams("parallel"),
        name="dsa_proj",
    )(x, g.reshape(1, d), wfeat, cosF, sinF, _lane_gain(qn), _lane_gain(kn), _lane_gain(mem_qn))


DSA_TQ = 256
DSA_TK = 256
HALF_BITS = 16
INT16_MIN = -2 ** (HALF_BITS - 1)


def _dsa_attn_body(kk_ref, qT_ref, iqT_ref, vT_ref, iwT_ref, y_ref, keys_ref, hi_ref, lo_ref, bias_ref, s_ref, mc_ref,
                   m_ref, acc_ref, *, topk, seq, tile):
    tq, tk = DSA_TQ, DSA_TK
    qs = tile * tq
    nkc = (qs + tq) // tk
    qpos = qs + lax.broadcasted_iota(I32, (1, tq), 1)
    row = lax.broadcasted_iota(I32, (tk, tq), 0)

    def over_chunks(body, carry):
        for c in range(nkc):
            carry = body(c * tk, carry)
        return carry

    def score_chunk(r0, carry):
        kkc = kk_ref[pl.ds(r0, tk), :]
        acc = jnp.zeros((tk, tq), F32)
        for h in range(IDX_H):
            r = _dot(kkc, _pad_rows(iqT_ref[h * IDX_D:(h + 1) * IDX_D, :], 1))
            acc = acc + jnp.maximum(r, 0.0) * iwT_ref[h:h + 1, :]
        sc = jnp.where(r0 + row <= qpos, acc + 0.0, -jnp.inf)
        bits = pltpu.bitcast(sc, I32)
        key = jnp.where(bits < 0, bits ^ jnp.int32(0x7FFFFFFF), bits)
        keys_ref[pl.ds(r0, tk), :] = key
        hi_ref[pl.ds(r0, tk), :] = lax.shift_right_arithmetic(key, HALF_BITS).astype(I16)
        return carry

    over_chunks(score_chunk, 0)

    def count(pred):
        def body(r0, acc):
            m = pred(keys_ref[pl.ds(r0, tk), :], r0 + row)
            return acc + jnp.sum(m.astype(I32).reshape(tk // SUBLANES, SUBLANES, tq), axis=0)
        acc = over_chunks(body, jnp.zeros((SUBLANES, tq), I32))
        return jnp.sum(acc, axis=0, keepdims=True)

    def fold16(mask):
        ones = jnp.where(mask, jnp.int16(1), jnp.int16(0))
        parts = [ones[i:i + PACKED_SUBLANES] for i in range(0, tk, PACKED_SUBLANES)]
        while len(parts) > 1:
            parts = [a + b for a, b in zip(parts[::2], parts[1::2])]
        return parts[0]

    zeros16 = jnp.zeros((PACKED_SUBLANES, tq), I16)
    total16 = lambda acc: jnp.sum(acc.astype(I32), axis=0, keepdims=True)

    def count16(ref, pred):
        def body(r0, acc):
            return acc + fold16(pred(ref[pl.ds(r0, tk), :]))
        return total16(over_chunks(body, zeros16))

    def kth_largest16(ref, k):
        thr = jnp.where(count16(ref, lambda v: v >= 0) >= k, jnp.int32(0), jnp.int32(INT16_MIN))

        for bit in range(HALF_BITS - 2, -1, -1):
            cand = thr | jnp.int32(1 << bit)
            c16 = cand.astype(I16)
            thr = jnp.where(count16(ref, lambda v, c16=c16: v >= c16) >= k, cand, thr)
        return thr

    thr_hi = kth_largest16(hi_ref, topk)
    thr_hi16 = thr_hi.astype(I16)

    def low_half_chunk(r0, above):
        hi = hi_ref[pl.ds(r0, tk), :]
        lo = ((keys_ref[pl.ds(r0, tk), :] & jnp.int32(0xFFFF)) + jnp.int32(INT16_MIN)).astype(I16)
        lo_ref[pl.ds(r0, tk), :] = jnp.where(hi == thr_hi16, lo, jnp.int16(INT16_MIN))
        return above + fold16(hi > thr_hi16)

    n_above = total16(over_chunks(low_half_chunk, zeros16))
    need_lo = topk - n_above
    thr_lo = kth_largest16(lo_ref, need_lo)
    thr = lax.shift_left(thr_hi, HALF_BITS) | (thr_lo - jnp.int32(INT16_MIN))
    def tie_search():
        need = topk - count(lambda k, _: k > thr)

        def idx_step(b, q):
            cand = q | lax.shift_left(jnp.int32(1), (seq.bit_length() - 2) - b)
            return jnp.where(count(lambda k, idx: (k == thr) & (idx < cand)) < need, cand, q)

        return lax.fori_loop(0, seq.bit_length() - 1, idx_step, jnp.zeros((1, tq), I32))

    thr_lo16 = thr_lo.astype(I16)
    has_ties = jnp.max(n_above + count16(lo_ref, lambda v: v >= thr_lo16)) > topk
    last = lax.cond(has_ties, tie_search, lambda: jnp.full((1, tq), seq, I32))

    def bias_chunk(r0, carry):
        k = keys_ref[pl.ds(r0, tk), :]
        idx = r0 + row
        sel = (k > thr) | ((k == thr) & (idx <= last))
        bias_ref[pl.ds(r0, tk), :] = jnp.where(sel & (idx <= qpos), 0.0, NEG)
        return carry

    over_chunks(bias_chunk, 0)

    outs = _staged_attention(
        DSA_H, 0, nkc, tk,
        lambda h, r0: _dot(kk_ref[pl.ds(r0, tk), :], _pad_rows(qT_ref[h * HD:(h + 1) * HD, :], 0))
        + bias_ref[pl.ds(r0, tk), :],
        lambda h, r0: vT_ref[:, pl.ds(r0, tk)], s_ref, mc_ref, m_ref, acc_ref)
    y_ref[...] = jnp.concatenate(outs, axis=0).T.astype(BF16)


def _dsa_attn(kk, qT, iqT, vT, iwT, batch, seq):
    t = kk.shape[0]
    tq = DSA_TQ
    nq = seq // tq
    topk = min(DSA_TOPK_MAX, seq // 4)
    def one_tile(i):
        keys = (i + 1) * tq
        colq = lambda n: pl.BlockSpec((n, tq), lambda b: (0, b * nq + i))
        return pl.pallas_call(
            functools.partial(_dsa_attn_body, topk=topk, seq=seq, tile=i),
            out_shape=jax.ShapeDtypeStruct((batch, tq, DSA_H * HD), BF16),
            grid=(batch,),
            in_specs=[
                pl.BlockSpec((keys, LANES), lambda b: (b * (seq // keys), 0)) if seq % keys == 0 else
                pl.BlockSpec((seq, LANES), lambda b: (b, 0)),
                colq(DSA_H * HD), colq(IDX_H * IDX_D),
                pl.BlockSpec((VAUG, seq), lambda b: (0, b)),
                colq(IDX_H),
            ],
            out_specs=pl.BlockSpec((None, tq, DSA_H * HD), lambda b: (b, 0, 0)),
            scratch_shapes=[pltpu.VMEM((keys, tq), I32), pltpu.VMEM((keys, tq), I16), pltpu.VMEM((keys, tq), I16),
                            pltpu.VMEM((keys, tq), F32),
                            pltpu.VMEM((2, DSA_H, DSA_TK, tq), F32), pltpu.VMEM((2, DSA_H * SUBLANES, tq), F32),
                            pltpu.VMEM((2 * SUBLANES, tq), F32), pltpu.VMEM((DSA_H * VAUG, tq), F32)],
            compiler_params=_cparams("parallel"),
            name=f"dsa_attn_tile{i}",
        )(kk, qT, iqT, vT, iwT)

    tiles = [one_tile(i) for i in range(nq)]
    return jnp.stack(tiles, axis=1).reshape(t, DSA_H * HD)


NSA_KV = NSA_G * HD
NSA_GATES = NSA_H * 3
NSA_GATES_PAD = -NSA_GATES % SUBLANES


def _nsa_proj_body(x_ref, g_ref, wtok_ref, wfeat_ref, cos_ref, sin_ref, qn_ref, kns_ref, knw_ref, mqn_ref,
                   kc_ref, vc_ref, ks_ref, kw_ref, qT_ref, vsT_ref, vwT_ref, gT_ref, qmT_ref):
    h = _rms(x_ref[...], g_ref[...]).astype(BF16)
    tok = _dot(h, wtok_ref[...])
    kc_ref[...] = tok[:, :NSA_KV]
    vc_ref[...] = tok[:, NSA_KV:]
    nq = NSA_H * HD
    q = _dot_nt(wfeat_ref[:nq, :], h)
    kk = _dot_nt(wfeat_ref[nq:nq + 2 * NSA_KV, :], h)
    feat = _dot_nt(wfeat_ref[nq + 2 * NSA_KV:, :], h)
    tm = feat.shape[-1]
    cos, sin = cos_ref[...], sin_ref[...]
    ks, kw = kk[:NSA_KV], kk[NSA_KV:]
    o = 0
    vs = feat[o:o + NSA_KV]; o += NSA_KV
    vw = feat[o:o + NSA_KV]; o += NSA_KV
    qm = feat[o:o + MEM_W]; o += MEM_W
    gates = feat[o:]
    qT_ref[...] = (_heads_norm_rope(q, _tile_lanes(qn_ref[...], tm), cos, sin) * QSCALE).astype(BF16)
    ks_ref[...] = _heads_norm_rope(ks, _tile_lanes(kns_ref[...], tm), cos, sin).T.astype(BF16)
    kw_ref[...] = _heads_norm_rope(kw, _tile_lanes(knw_ref[...], tm), cos, sin).T.astype(BF16)
    vsT_ref[...] = _augment_values(vs).astype(BF16)
    vwT_ref[...] = _augment_values(vw).astype(BF16)
    gT_ref[...] = jax.nn.sigmoid(gates)
    qmT_ref[...] = (_heads_norm_rope(qm, _tile_lanes(mqn_ref[...], tm), None, None) * QSCALE).astype(BF16)


def _nsa_proj(x, g, w_in, qn, kn, mem_qn, seq):
    t, d = x.shape
    tm = min(PROJ_TM, seq)
    npos = seq // tm
    wq, wkc, wvc, wks, wvs, wkw, wvw, wgt, wqm = jnp.split(w_in, np.cumsum(NSA_SIZES)[:-1].tolist(), axis=1)
    wtok = jnp.concatenate([wkc, wvc], axis=1).astype(BF16)
    wgt = jnp.pad(wgt, ((0, 0), (0, NSA_GATES_PAD)))
    wfeat = jnp.concatenate([wq, wks, wkw, wvs, wvw, wqm, wgt], axis=1).T.astype(BF16)
    cosF, sinF = _rope_tables(jnp.arange(seq))
    row = lambda n: pl.BlockSpec((tm, n), lambda i: (i, 0))
    col = lambda n: pl.BlockSpec((n, tm), lambda i: (0, i))
    tab = pl.BlockSpec((HALF, tm), lambda i: (0, i % npos))
    gain = _const_spec((HD, LANES))
    ngt = NSA_GATES + NSA_GATES_PAD
    return pl.pallas_call(
        _nsa_proj_body,
        out_shape=(
            jax.ShapeDtypeStruct((t, NSA_KV), F32), jax.ShapeDtypeStruct((t, NSA_KV), F32),
            jax.ShapeDtypeStruct((t, NSA_KV), BF16), jax.ShapeDtypeStruct((t, NSA_KV), BF16),
            jax.ShapeDtypeStruct((NSA_H * HD, t), BF16), jax.ShapeDtypeStruct((NSA_G * VAUG, t), BF16),
            jax.ShapeDtypeStruct((NSA_G * VAUG, t), BF16), jax.ShapeDtypeStruct((ngt, t), F32),
            jax.ShapeDtypeStruct((MEM_W, t), BF16),
        ),
        grid=(t // tm,),
        in_specs=[row(d), _const_spec((1, d)), _const_spec(wtok.shape), _const_spec(wfeat.shape),
                  tab, tab, gain, gain, gain, gain],
        out_specs=(row(NSA_KV), row(NSA_KV), row(NSA_KV), row(NSA_KV), col(NSA_H * HD), col(NSA_G * VAUG),
                   col(NSA_G * VAUG), col(ngt), col(MEM_W)),
        compiler_params=_cparams("parallel"),
        name="nsa_proj",
    )(x, g.reshape(1, d), wtok, wfeat, cosF, sinF, _lane_gain(qn), _lane_gain(kn[1]), _lane_gain(kn[2]),
      _lane_gain(mem_qn))


def _nsa_cmp_body(k01_ref, k23_ref, v01_ref, v23_ref, pk_ref, pv_ref, wk_ref, wv_ref, kn_ref, cos_ref, sin_ref,
                  k_ref, vT_ref):
    n = k_ref.shape[1]

    def compress(lo_ref, hi_ref, pos, w_ref):
        parts = []
        for l in range(CMP_S):
            parts += [lo_ref[pl.ds(l, n, stride=CMP_S), :], hi_ref[pl.ds(l, n, stride=CMP_S), :]]
        x = jnp.concatenate(parts, axis=1)
        xa = (x + pos[0:1]).astype(BF16)
        xb = (pltpu.roll(x, n - 1, axis=0) + pos[1:2]).astype(BF16)
        return _dot_nt(w_ref[0], xa) + _dot_nt(w_ref[1], xb)

    kT = compress(k01_ref, k23_ref, pk_ref[...], wk_ref)
    kT = _heads_norm_rope(kT, _tile_lanes(kn_ref[...], kT.shape[-1]), cos_ref[...], sin_ref[...])
    k_ref[0] = kT.T.astype(BF16)
    vT_ref[0] = compress(v01_ref, v23_ref, pv_ref[...], wv_ref).astype(BF16)


def _nsa_cmp_weights(w, pos):
    halves, width = CMP_L // CMP_S, CMP_S * NSA_KV
    w_eld = w.reshape(halves, CMP_S, HD, HD).transpose(0, 3, 1, 2)
    per_group = jnp.broadcast_to(w_eld[:, :, :, None, :], (halves, HD, CMP_S, NSA_G, HD)).reshape(halves, HD, width)
    rows = np.arange(NSA_KV)[:, None] // HD
    cols = (np.arange(width)[None, :] // HD) % NSA_G
    wt = jnp.where(jnp.asarray(rows == cols)[None], jnp.tile(per_group, (1, NSA_G, 1)), 0.0)
    p = jnp.broadcast_to(pos.reshape(halves, CMP_S, 1, HD), (halves, CMP_S, NSA_G, HD))
    return wt.astype(BF16), p.reshape(halves, width).astype(F32)


def _nsa_cmp(kc, vc, pos_k, pos_v, wk, wv, kn0, batch, seq):
    n = seq // CMP_S
    wkt, pk = _nsa_cmp_weights(wk, pos_k)
    wvt, pv = _nsa_cmp_weights(wv, pos_v)
    cosE, sinE = _rope_tables(jnp.arange(n) * CMP_S + (CMP_L - 1))
    lanes_lo = pl.BlockSpec((seq, LANES), lambda b: (b, 0))
    lanes_hi = pl.BlockSpec((seq, LANES), lambda b: (b, 1))
    return pl.pallas_call(
        _nsa_cmp_body,
        out_shape=(jax.ShapeDtypeStruct((batch, n, NSA_KV), BF16), jax.ShapeDtypeStruct((batch, NSA_KV, n), BF16)),
        grid=(batch,),
        in_specs=[lanes_lo, lanes_hi, lanes_lo, lanes_hi, _const_spec(pk.shape), _const_spec(pv.shape),
                  _const_spec(wkt.shape), _const_spec(wvt.shape), _const_spec((HD, LANES)), _const_spec((HALF, n)),
                  _const_spec((HALF, n))],
        out_specs=(pl.BlockSpec((1, n, NSA_KV), lambda b: (b, 0, 0)), pl.BlockSpec((1, NSA_KV, n), lambda b: (b, 0, 0))),
        compiler_params=_cparams("parallel"),
        name="nsa_cmp",
    )(kc, kc, vc, vc, pk, pv, wkt, wvt, _lane_gain(kn0), cosE, sinE)


NSA_TQ = 256
NSA_TK = 256


def _staged_attention(nheads, lo, hi, tk, logits, values, s_ref, mc_ref, m_ref, acc_ref):
    tq = s_ref.shape[-1]
    fold = lambda a: a.reshape(tk // SUBLANES, SUBLANES, tq)
    m_ref[...] = jnp.full(m_ref.shape, -jnp.inf, F32)
    acc_ref[...] = jnp.zeros(acc_ref.shape, F32)

    start = lambda c: c * tk if isinstance(c, int) else pl.multiple_of(c * tk, tk)

    def stage(h, c, slot):
        s = logits(h, start(c))
        s_ref[slot, h] = s
        mc_ref[slot, h * SUBLANES:(h + 1) * SUBLANES, :] = jnp.max(fold(s), axis=0)

    def consume(h, c, slot):
        g8, hrows = slice(h * SUBLANES, (h + 1) * SUBLANES), slice(h * VAUG, (h + 1) * VAUG)
        m_old = m_ref[h:h + 1, :]
        m_new = jnp.maximum(m_old, jnp.max(mc_ref[slot, g8, :], axis=0, keepdims=True))
        alpha = jnp.exp2(m_old - m_new)
        p = jnp.exp2(s_ref[slot, h] - m_new)
        acc_ref[hrows, :] = acc_ref[hrows, :] * alpha + _dot(values(h, start(c)), p.astype(BF16))
        m_ref[h:h + 1, :] = m_new

    def consume_and_stage_next(c, slot):
        for h in range(nheads):
            consume(h, c, slot)
            stage(h, c + 1, 1 - slot)

    def finish(c, slot):
        for h in range(nheads):
            consume(h, c, slot)

    for h in range(nheads):
        stage(h, lo, 0)
    n_fused = hi - 1 - lo

    def two_chunks(j, carry):
        c = lo + 2 * j
        consume_and_stage_next(c, 0)
        consume_and_stage_next(c + 1, 1)
        return carry

    if isinstance(n_fused, int):
        for i in range(n_fused):
            consume_and_stage_next(lo + i, i % 2)
        finish(hi - 1, n_fused % 2)
    else:
        lax.fori_loop(0, n_fused // 2, two_chunks, 0)

        @pl.when(n_fused % 2 == 1)
        def _():
            consume_and_stage_next(hi - 2, 0)
            finish(hi - 1, 1)

        @pl.when(n_fused % 2 == 0)
        def _():
            finish(hi - 1, 0)

    return [acc_ref[h * VAUG:h * VAUG + HD, :] / acc_ref[h * VAUG + HD:h * VAUG + HD + 1, :] for h in range(nheads)]


def _nsa_attn_body(qT_ref, gT_ref, ks_ref, vsT_ref, kw_ref, vwT_ref, kc_ref, vcT_ref, cov_ref, y_ref,
                   bias_ref, out_ref, s_ref, mc_ref, m_ref, acc_ref, *, seq, tile):
    tq, tk = NSA_TQ, NSA_TK
    qs = tile * tq
    nkc = (qs + tq) // tk
    wlo = max((qs - WIN) // tk, 0)
    qpos = qs + lax.broadcasted_iota(I32, (1, tq), 1)
    ncr = seq // CMP_S
    nblk = seq // SLC_L
    n_sel = min(SLC_N_MAX, nblk)
    valid_c = CMP_S * lax.broadcasted_iota(I32, (ncr, tq), 0) + (CMP_L - 1) <= qpos
    any_c = jnp.where(qpos >= CMP_L - 1, 1.0, 0.0)
    jrow = lax.broadcasted_iota(I32, (nblk, tq), 0)
    qblk = qpos // SLC_L
    forced = (jrow == 0) | (jrow == qblk) | (jrow == qblk - 1)
    causal_blk = jrow * SLC_L <= qpos
    krow = lax.broadcasted_iota(I32, (tk, tq), 0)
    brow = lax.broadcasted_iota(I32, (SLC_L, tq), 0)
    group = lambda h: h // NSA_R
    pair = lambda h: slice((group(h) // 2) * LANES, (group(h) // 2 + 1) * LANES)
    grows = lambda h: slice(group(h) * HD, (group(h) + 1) * HD)
    vrows = lambda h: slice(group(h) * VAUG, (group(h) + 1) * VAUG)
    q_of = lambda h: _pad_rows(qT_ref[h * HD:(h + 1) * HD, :], group(h) % 2)
    gate = lambda h, branch: gT_ref[h * 3 + branch:h * 3 + branch + 1, :]

    selbs = []
    for g in range(NSA_G):
        heads = [g * NSA_R + r for r in range(NSA_R)]
        kcm = kc_ref[0][:, pair(heads[0])]
        vcm = vcT_ref[0][grows(heads[0]), :]
        psum = jnp.zeros((ncr, tq), F32)
        for h in heads:
            s = jnp.where(valid_c, _dot(kcm, q_of(h)), NEG)
            p = jnp.exp2(s - jnp.max(s, axis=0, keepdims=True))
            p = p / jnp.sum(p, axis=0, keepdims=True) * any_c
            out_ref[h * HD:(h + 1) * HD, :] = gate(h, 0) * _dot(vcm, p.astype(BF16))
            psum = psum + p
        p_hi = psum.astype(BF16)
        p_lo = (psum - p_hi.astype(F32)).astype(BF16)
        imp = _dot(cov_ref[...], p_hi) + _dot(cov_ref[...], p_lo)
        imp = jnp.where(causal_blk, imp + jnp.where(forced, FORCE, 0.0), NEG)
        rank = jnp.zeros((nblk, tq), I32)
        for j2 in range(nblk):
            rj = imp[j2:j2 + 1]
            beats = (rj > imp) | ((rj == imp) & (j2 < jrow))
            rank = rank + beats.astype(I32)
        selbs.append(jnp.where(rank < n_sel, 0.0, NEG))

    blocks_per_chunk = tk // SLC_L
    for g in range(NSA_G):
        for j in range(nkc * blocks_per_chunk):
            bias_ref[g, j * SLC_L:(j + 1) * SLC_L, :] = jnp.where(j * SLC_L + brow <= qpos, selbs[g][j:j + 1], NEG)

    stats = (s_ref, mc_ref, m_ref, acc_ref)
    slc = _staged_attention(
        NSA_H, 0, nkc, tk,
        lambda h, r0: _dot(ks_ref[pl.ds(r0, tk), pair(h)], q_of(h)) + bias_ref[group(h), pl.ds(r0, tk), :],
        lambda h, r0: vsT_ref[vrows(h), pl.ds(r0, tk)], *stats)
    for h in range(NSA_H):
        out_ref[h * HD:(h + 1) * HD, :] = out_ref[h * HD:(h + 1) * HD, :] + gate(h, 1) * slc[h]

    for c in range(wlo, nkc):
        dist = qpos - (c * tk + krow)
        bias_ref[0, c * tk:(c + 1) * tk, :] = jnp.where((dist >= 0) & (dist < WIN), 0.0, NEG)
    win = _staged_attention(
        NSA_H, wlo, nkc, tk,
        lambda h, r0: _dot(kw_ref[pl.ds(r0, tk), pair(h)], q_of(h)) + bias_ref[0, pl.ds(r0, tk), :],
        lambda h, r0: vwT_ref[vrows(h), pl.ds(r0, tk)], *stats)
    outs = [out_ref[h * HD:(h + 1) * HD, :] + gate(h, 2) * win[h] for h in range(NSA_H)]
    y_ref[...] = jnp.concatenate(outs, axis=0).T.astype(BF16)


def _nsa_attn(qT, gT, ks, vsT, kw, vwT, kcmp, vcmpT, batch, seq):
    t = ks.shape[0]
    tq = NSA_TQ
    nq = seq // tq
    ncr, nblk = seq // CMP_S, seq // SLC_L
    starts = np.arange(ncr) * CMP_S
    sstart = np.arange(nblk) * SLC_L
    cover = (starts[None, :] < sstart[:, None] + SLC_L) & (starts[None, :] + CMP_L > sstart[:, None])
    cover[:, ncr - 1] = False
    tok = pl.BlockSpec((seq, NSA_KV), lambda b: (b, 0))
    feat = pl.BlockSpec((NSA_G * VAUG, seq), lambda b: (0, b))
    cover = jnp.asarray(cover, BF16)

    def one_tile(i):
        colq = lambda n: pl.BlockSpec((n, tq), lambda b: (0, b * nq + i))
        return pl.pallas_call(
            functools.partial(_nsa_attn_body, seq=seq, tile=i),
            out_shape=jax.ShapeDtypeStruct((batch, tq, NSA_H * HD), BF16),
            grid=(batch,),
            in_specs=[
                colq(NSA_H * HD), colq(gT.shape[0]), tok, feat, tok, feat,
                pl.BlockSpec((1, ncr, NSA_KV), lambda b: (b, 0, 0)),
                pl.BlockSpec((1, NSA_KV, ncr), lambda b: (b, 0, 0)),
                _const_spec((nblk, ncr)),
            ],
            out_specs=pl.BlockSpec((None, tq, NSA_H * HD), lambda b: (b, 0, 0)),
            scratch_shapes=[pltpu.VMEM((NSA_G, (i + 1) * tq, tq), F32), pltpu.VMEM((NSA_H * HD, tq), F32),
                            pltpu.VMEM((2, NSA_H, NSA_TK, tq), F32), pltpu.VMEM((2, NSA_H * SUBLANES, tq), F32),
                            pltpu.VMEM((2 * SUBLANES, tq), F32), pltpu.VMEM((NSA_H * VAUG, tq), F32)],
            compiler_params=_cparams("parallel"),
            name=f"nsa_attn_tile{i}",
        )(qT, gT, ks, vsT, kw, vwT, kcmp, vcmpT, cover)

    tiles = [one_tile(i) for i in range(nq)]
    return jnp.stack(tiles, axis=1).reshape(t, NSA_H * HD)


def _nsa_layer_mix(x, g, w_in, qn, kn, pos_k, pos_v, wk, wv, mem_qn, batch, seq):
    kc, vc, ks, kw, qT, vsT, vwT, gT, qmT = _nsa_proj(x, g, w_in, qn, kn, mem_qn, seq)
    kcmp, vcmpT = _nsa_cmp(kc, vc, pos_k, pos_v, wk, wv, kn[0], batch, seq)
    return _nsa_attn(qT, gT, ks, vsT, kw, vwT, kcmp, vcmpT, batch, seq), qmT


def kernel(x, mem, ffn_norm, ffn_w_gate, ffn_w_up, ffn_w_down, mix_norm, w_out, mem_norm, mem_w_kv, mem_qn, mem_kn, ret_w_in, dsa_w_in, dsa_qn, dsa_kn, nsa_w_in, nsa_qn, nsa_kn, nsa_cmp_pos_k, nsa_cmp_pos_v, nsa_cmp_wk, nsa_cmp_wv):
    batch, seq, d = x.shape
    x = x.reshape(batch * seq, d)
    ffn_w = (ffn_w_gate.astype(BF16), ffn_w_up.astype(BF16), ffn_w_down.astype(BF16))
    for i in range(ffn_norm.shape[0]):
        x = _ffn(x, ffn_norm[i, 0], *ffn_w, i, 0)
        kind, j = i % N_MIXERS, i // N_MIXERS
        if kind == 0:
            q, v, gate, kT, qmT = _ret_proj(x, mix_norm[i], ret_w_in[j], mem_qn[i], seq)
            y_mix = _retention(q, kT, v, gate, batch, seq)
        elif kind == 1:
            kk, qT, iqT, vT, iwT, qmT = _dsa_proj(x, mix_norm[i], dsa_w_in[j], dsa_qn[j], dsa_kn[j], mem_qn[i], seq)
            y_mix = _dsa_attn(kk, qT, iqT, vT, iwT, batch, seq)
        else:
            y_mix, qmT = _nsa_layer_mix(x, mix_norm[i], nsa_w_in[j], nsa_qn[j], nsa_kn[j], nsa_cmp_pos_k[j],
                                        nsa_cmp_pos_v[j], nsa_cmp_wk[j], nsa_cmp_wv[j], mem_qn[i], batch, seq)
        mem_k, mem_vT = _mem_kv(mem, mem_norm[i], mem_w_kv[i], mem_kn[i])
        y_mem = _mem_attn(qmT, mem_k, mem_vT, seq)
        x = _mix_out_ffn(y_mix, y_mem, x, w_out[i], ffn_norm[i, 1], *ffn_w, i, 1)
    return x.reshape(batch, seq, d)
```

```python
import functools
import math

import jax
import jax.numpy as jnp
import numpy as np
from jax import lax
from jax.experimental import pallas as pl
from jax.experimental.pallas import tpu as pltpu

D_MODEL = 1024
HD = 64
HALF = HD // 2
H_MIX = 12
H_MEM = 4
MIX_W = H_MIX * HD
MEM_W = H_MEM * HD
D_FF = 2816
ROPE_THETA = 10000.0
EPS = 1e-6
NEG = -1e30
FORCE = 1e9
SCALE = HD ** -0.5
QSCALE = SCALE * math.log2(math.e)

RET_H, RET_DK, RET_DV, RET_CHUNK = 6, 64, 128, 128
DSA_H, IDX_H, IDX_D, DSA_TOPK_MAX = 12, 8, 64, 256
NSA_H, NSA_G, CMP_L, CMP_S, SLC_L, SLC_N_MAX, WIN = 12, 4, 32, 16, 64, 16, 512
NSA_R = NSA_H // NSA_G
N_MIXERS = 3

RET_SIZES = [RET_H * RET_DK, RET_H * RET_DK, RET_H * RET_DV, RET_H * RET_DV, MEM_W]
DSA_SIZES = [DSA_H * HD, HD, HD, IDX_H * IDX_D, IDX_D, IDX_H, MEM_W]
NSA_SIZES = [NSA_H * HD] + [NSA_G * HD] * 6 + [NSA_H * 3, MEM_W]

LANES = 128
SUBLANES = 8
PACKED_SUBLANES = 2 * SUBLANES
VAUG = HD + PACKED_SUBLANES
MXU_N = 256
VMEM_LIMIT_BYTES = 56 * 1024 * 1024

BF16 = jnp.bfloat16
F32 = jnp.float32
I32 = jnp.int32
I16 = jnp.int16


def _cparams(*sem):
    return pltpu.CompilerParams(dimension_semantics=sem, vmem_limit_bytes=VMEM_LIMIT_BYTES)


def _const_spec(shape):
    n = len(shape)
    return pl.BlockSpec(shape, lambda *_: (0,) * n, pipeline_mode=pl.Buffered(1))


def _rms(x, g):
    return x * lax.rsqrt(jnp.mean(x * x, axis=-1, keepdims=True) + EPS) * g


def _dot(a, b):
    return jnp.dot(a, b, preferred_element_type=F32)


def _dot_nt(a, b):
    return lax.dot_general(a, b, (((1,), (1,)), ((), ())), preferred_element_type=F32)


def _tile_lanes(a, n):
    reps = n // a.shape[-1]
    return a if reps == 1 else jnp.concatenate([a] * reps, axis=-1)


def _rope_tables(pos):
    inv = ROPE_THETA ** (-jnp.arange(HALF, dtype=F32) / HALF)
    ang = pos.astype(F32)[:, None] * inv[None, :]
    return jnp.cos(ang).T, jnp.sin(ang).T


def _lane_gain(g):
    return jnp.broadcast_to(g.astype(F32)[:, None], (g.shape[0], LANES))


def _augment_values(v):
    t = v.shape[-1]
    v3 = v.reshape(v.shape[0] // HD, HD, t)
    tail = jnp.where(lax.broadcasted_iota(I32, (v3.shape[0], VAUG - HD, t), 1) == 0, 1.0, 0.0).astype(v.dtype)
    return jnp.concatenate([v3, tail], axis=1).reshape(v3.shape[0] * VAUG, t)


def _heads_norm_rope(x, gain, cos, sin):
    t = x.shape[-1]
    x3 = x.reshape(x.shape[0] // HD, HD, t)
    if gain is not None:
        ms = jnp.sum(x3 * x3, axis=1, keepdims=True) * (1.0 / HD)
        x3 = x3 * lax.rsqrt(ms + EPS) * gain[None]
    if cos is not None:
        x1, x2 = x3[:, :HALF], x3[:, HALF:]
        x3 = jnp.concatenate([x1 * cos[None] - x2 * sin[None], x2 * cos[None] + x1 * sin[None]], axis=1)
    return x3.reshape(x.shape)


FFN_TM = 1024
FFN_TF = MXU_N


def _swiglu_half_step(x, g_ref, wg_ref, wu_ref, wd_ref):
    h = _rms(x, g_ref[...]).astype(BF16)
    acc = jnp.zeros(x.shape, F32)
    for c in range(D_FF // FFN_TF):
        sl = slice(c * FFN_TF, (c + 1) * FFN_TF)
        gate = _dot(h, wg_ref[:, sl])
        up = _dot(h, wu_ref[:, sl])
        act = (gate * jax.nn.sigmoid(gate) * up).astype(BF16)
        acc = acc + _dot(act, wd_ref[sl, :])
    return x + 0.5 * acc


def _ffn_body(x_ref, g_ref, wg_ref, wu_ref, wd_ref, o_ref):
    o_ref[...] = _swiglu_half_step(x_ref[...], g_ref, wg_ref, wu_ref, wd_ref)


def _mix_out_ffn_body(ymix_ref, ymem_ref, x_ref, wmix_ref, wmem_ref, g_ref, wg_ref, wu_ref, wd_ref, o_ref):
    x = x_ref[...] + _dot(ymix_ref[...], wmix_ref[...]) + _dot(ymem_ref[...], wmem_ref[...])
    o_ref[...] = _swiglu_half_step(x, g_ref, wg_ref, wu_ref, wd_ref)


def _ffn_specs(d, layer, half):
    pick = lambda r, c: pl.BlockSpec((None, None, r, c), lambda i: (layer, half, 0, 0), pipeline_mode=pl.Buffered(1))
    return [_const_spec((1, d)), pick(d, D_FF), pick(d, D_FF), pick(D_FF, d)]


def _ffn(x, g, wg, wu, wd, layer, half):
    t, d = x.shape
    tm = min(FFN_TM, t)
    return pl.pallas_call(
        _ffn_body,
        out_shape=jax.ShapeDtypeStruct((t, d), F32),
        grid=(t // tm,),
        in_specs=[pl.BlockSpec((tm, d), lambda i: (i, 0))] + _ffn_specs(d, layer, half),
        out_specs=pl.BlockSpec((tm, d), lambda i: (i, 0)),
        compiler_params=_cparams("parallel"),
        name="ffn",
    )(x, g.reshape(1, d), wg, wu, wd)


def _mix_out_ffn(y_mix, y_mem, x, w_out, g, wg, wu, wd, layer, half):
    t, d = x.shape
    tm = min(FFN_TM, t)
    return pl.pallas_call(
        _mix_out_ffn_body,
        out_shape=jax.ShapeDtypeStruct((t, d), F32),
        grid=(t // tm,),
        in_specs=[
            pl.BlockSpec((tm, MIX_W), lambda i: (i, 0)),
            pl.BlockSpec((tm, MEM_W), lambda i: (i, 0)),
            pl.BlockSpec((tm, d), lambda i: (i, 0)),
            _const_spec((MIX_W, d)),
            _const_spec((MEM_W, d)),
        ] + _ffn_specs(d, layer, half),
        out_specs=pl.BlockSpec((tm, d), lambda i: (i, 0)),
        compiler_params=_cparams("parallel"),
        name="mix_out_ffn",
    )(y_mix, y_mem, x, w_out[:MIX_W].astype(BF16), w_out[MIX_W:].astype(BF16), g.reshape(1, d), wg, wu, wd)


PROJ_TM = 1024


def _mem_kv_body(m_ref, g_ref, wk_ref, wvT_ref, kn_ref, k_ref, vT_ref):
    h = _rms(m_ref[0], g_ref[...]).astype(BF16)
    kT = _dot_nt(wk_ref[...], h)
    kT = _heads_norm_rope(kT, _tile_lanes(kn_ref[...], kT.shape[-1]), None, None)
    k_ref[0] = kT.T.astype(BF16)
    vT_ref[0] = _augment_values(_dot_nt(wvT_ref[...], h)).astype(BF16)


def _mem_kv(mem, g, w_kv, kn):
    b, m, d = mem.shape
    return pl.pallas_call(
        _mem_kv_body,
        out_shape=(jax.ShapeDtypeStruct((b, m, MEM_W), BF16), jax.ShapeDtypeStruct((b, H_MEM * VAUG, m), BF16)),
        grid=(b,),
        in_specs=[
            pl.BlockSpec((1, m, d), lambda i: (i, 0, 0)),
            _const_spec((1, d)),
            _const_spec((MEM_W, d)),
            _const_spec((MEM_W, d)),
            _const_spec((HD, LANES)),
        ],
        out_specs=(pl.BlockSpec((1, m, MEM_W), lambda i: (i, 0, 0)),
                   pl.BlockSpec((1, H_MEM * VAUG, m), lambda i: (i, 0, 0))),
        compiler_params=_cparams("parallel"),
        name="mem_kv",
    )(mem, g.reshape(1, d), w_kv[:, :MEM_W].T.astype(BF16), w_kv[:, MEM_W:].T.astype(BF16), _lane_gain(kn))


MEM_TQ = 1024


def _pad_rows(blk, half, total=LANES):
    z = jnp.zeros_like(blk)
    parts = [z] * (total // HD)
    parts[half] = blk
    return jnp.concatenate(parts, axis=0)


def _mem_attn_body(qT_ref, k_ref, vT_ref, o_ref, s_ref, mc_ref, m_ref, acc_ref):
    n_mem = k_ref.shape[1]
    outs = _staged_attention(
        H_MEM, 0, 1, n_mem,
        lambda h, r0: _dot(k_ref[0, :, (h // 2) * LANES:(h // 2 + 1) * LANES],
                           _pad_rows(qT_ref[h * HD:(h + 1) * HD, :], h % 2)),
        lambda h, r0: vT_ref[0, h * VAUG:(h + 1) * VAUG, :], s_ref, mc_ref, m_ref, acc_ref)
    o_ref[...] = jnp.concatenate(outs, axis=0).T.astype(BF16)


def _mem_attn(qmT, k, vT, seq):
    t = qmT.shape[1]
    b, m, _ = k.shape
    tq = min(MEM_TQ, seq)
    nq = seq // tq
    return pl.pallas_call(
        _mem_attn_body,
        out_shape=jax.ShapeDtypeStruct((t, MEM_W), BF16),
        grid=(b, nq),
        in_specs=[
            pl.BlockSpec((MEM_W, tq), lambda bi, i: (0, bi * nq + i)),
            pl.BlockSpec((1, m, MEM_W), lambda bi, i: (bi, 0, 0)),
            pl.BlockSpec((1, H_MEM * VAUG, m), lambda bi, i: (bi, 0, 0)),
        ],
        out_specs=pl.BlockSpec((tq, MEM_W), lambda bi, i: (bi * nq + i, 0)),
        scratch_shapes=[pltpu.VMEM((2, H_MEM, m, tq), F32), pltpu.VMEM((2, H_MEM * SUBLANES, tq), F32),
                        pltpu.VMEM((SUBLANES, tq), F32), pltpu.VMEM((H_MEM * VAUG, tq), F32)],
        compiler_params=_cparams("parallel", "parallel"),
        name="mem_attn",
    )(qmT, k, vT)


def _ret_proj_body(x_ref, g_ref, wtok_ref, wfeat_ref, cosT_ref, sinT_ref, cosF_ref, sinF_ref, mqn_ref,
                   q_ref, v_ref, gate_ref, kT_ref, qmT_ref):
    h = _rms(x_ref[...], g_ref[...]).astype(BF16)
    tok = _dot(h, wtok_ref[...])
    nq, nv = RET_H * RET_DK, RET_H * RET_DV
    q = tok[:, :nq]
    lane = lax.broadcasted_iota(I32, q.shape, 1)
    rot = jnp.where(lane % HD < HALF, pltpu.roll(q, nq - HALF, axis=1), pltpu.roll(q, HALF, axis=1))
    q_ref[...] = (q * _tile_lanes(cosT_ref[...], nq) + rot * _tile_lanes(sinT_ref[...], nq)).astype(BF16)
    v_ref[...] = tok[:, nq:nq + nv].astype(BF16)
    gate_ref[...] = tok[:, nq + nv:]
    feat = _dot_nt(wfeat_ref[...], h)
    tm = feat.shape[-1]
    cos, sin = cosF_ref[...], sinF_ref[...]
    kT = _heads_norm_rope(feat[:nq], None, cos, sin) * (RET_DK ** -0.5)
    kT_ref[...] = kT.astype(BF16)
    qm = _heads_norm_rope(feat[nq:], _tile_lanes(mqn_ref[...], tm), None, None) * QSCALE
    qmT_ref[...] = qm.astype(BF16)


def _token_rope_tables(seq):
    cosF, sinF = _rope_tables(jnp.arange(seq))
    cos = jnp.tile(cosF.T, (1, LANES // HALF))
    sign = jnp.where((jnp.arange(LANES) % HD) < HALF, -1.0, 1.0).astype(F32)
    sin = jnp.tile(sinF.T, (1, LANES // HALF)) * sign[None, :]
    return cos, sin


def _ret_proj(x, g, w_in, mem_qn, seq):
    t, d = x.shape
    tm = min(PROJ_TM, seq)
    npos = seq // tm
    wq, wk, wv, wg, wqm = jnp.split(w_in, np.cumsum(RET_SIZES)[:-1].tolist(), axis=1)
    wtok = jnp.concatenate([wq, wv, wg], axis=1).astype(BF16)
    wfeat = jnp.concatenate([wk, wqm], axis=1).T.astype(BF16)
    cosT, sinT = _token_rope_tables(seq)
    cosF, sinF = _rope_tables(jnp.arange(seq))
    nq, nv = RET_H * RET_DK, RET_H * RET_DV
    row = lambda n: pl.BlockSpec((tm, n), lambda i: (i, 0))
    col = lambda n: pl.BlockSpec((n, tm), lambda i: (0, i))
    return pl.pallas_call(
        _ret_proj_body,
        out_shape=(
            jax.ShapeDtypeStruct((t, nq), BF16), jax.ShapeDtypeStruct((t, nv), BF16),
            jax.ShapeDtypeStruct((t, nv), F32), jax.ShapeDtypeStruct((nq, t), BF16),
            jax.ShapeDtypeStruct((MEM_W, t), BF16),
        ),
        grid=(t // tm,),
        in_specs=[
            row(d), _const_spec((1, d)), _const_spec(wtok.shape), _const_spec(wfeat.shape),
            pl.BlockSpec((tm, LANES), lambda i: (i % npos, 0)), pl.BlockSpec((tm, LANES), lambda i: (i % npos, 0)),
            pl.BlockSpec((HALF, tm), lambda i: (0, i % npos)), pl.BlockSpec((HALF, tm), lambda i: (0, i % npos)),
            _const_spec((HD, LANES)),
        ],
        out_specs=(row(nq), row(nv), row(nv), col(nq), col(MEM_W)),
        compiler_params=_cparams("parallel"),
        name="ret_proj",
    )(x, g.reshape(1, d), wtok, wfeat, cosT, sinT, cosF, sinF, _lane_gain(mem_qn))


RET_TL = 1024


def _ret_body(q_ref, kT_ref, v_ref, gate_ref, y_ref, state_ref):
    @pl.when(pl.program_id(1) == 0)
    def _():
        state_ref[...] = jnp.zeros_like(state_ref)

    c = RET_CHUNK
    ii = lax.broadcasted_iota(I32, (c, c), 0).astype(F32)
    jj = lax.broadcasted_iota(I32, (c, c), 1).astype(F32)
    diff = ii - jj
    jk = lax.broadcasted_iota(I32, (RET_DK, c), 1).astype(F32)
    log_gs = [math.log(1.0 - 2.0 ** (-5.0 - hh)) for hh in range(RET_H)]
    dmasks = [jnp.where(diff >= 0, jnp.exp(lg * jnp.maximum(diff, 0.0)), 0.0) for lg in log_gs]
    xis = [jnp.exp(lg * (ii + 1.0)) for lg in log_gs]
    zetas = [jnp.exp(lg * (c - 1.0 - jk)) for lg in log_gs]
    for n in range(q_ref.shape[0] // c):
        rows = slice(n * c, (n + 1) * c)
        for hh in range(RET_H):
            dmask, xi, zeta, decay = dmasks[hh], xis[hh], zetas[hh], math.exp(log_gs[hh] * c)
            pair = slice((hh // 2) * LANES, (hh // 2 + 1) * LANES)
            q2 = q_ref[rows, pair]
            kT = kT_ref[hh * RET_DK:(hh + 1) * RET_DK, rows]
            v = v_ref[rows, hh * RET_DV:(hh + 1) * RET_DV]
            state = state_ref[hh]
            inner = _dot(q2, _pad_rows(kT, hh % 2)) * dmask
            o = _dot(inner.astype(BF16), v)
            o = o + _dot(q2, _pad_rows(state.astype(BF16), hh % 2)) * xi
            kv = _dot((kT.astype(F32) * zeta).astype(BF16), v)
            state_ref[hh] = state * decay + kv
            mu = jnp.mean(o, axis=-1, keepdims=True)
            var = jnp.mean(jnp.square(o - mu), axis=-1, keepdims=True)
            o = (o - mu) * lax.rsqrt(var + EPS)
            gte = gate_ref[rows, hh * RET_DV:(hh + 1) * RET_DV]
            y_ref[rows, hh * RET_DV:(hh + 1) * RET_DV] = (gte * jax.nn.sigmoid(gte) * o).astype(BF16)


def _retention(q, kT, v, gate, batch, seq):
    t = q.shape[0]
    tl = min(RET_TL, seq)
    nl = seq // tl
    nq, nv = RET_H * RET_DK, RET_H * RET_DV
    return pl.pallas_call(
        _ret_body,
        out_shape=jax.ShapeDtypeStruct((t, nv), BF16),
        grid=(batch, nl),
        in_specs=[
            pl.BlockSpec((tl, nq), lambda b, i: (b * nl + i, 0)),
            pl.BlockSpec((nq, tl), lambda b, i: (0, b * nl + i)),
            pl.BlockSpec((tl, nv), lambda b, i: (b * nl + i, 0)),
            pl.BlockSpec((tl, nv), lambda b, i: (b * nl + i, 0)),
        ],
        out_specs=pl.BlockSpec((tl, nv), lambda b, i: (b * nl + i, 0)),
        scratch_shapes=[pltpu.VMEM((RET_H, RET_DK, RET_DV), F32)],
        compiler_params=_cparams("parallel", "arbitrary"),
        name="retention",
    )(q, kT, v, gate)


def _dsa_proj_body(x_ref, g_ref, w_ref, cos_ref, sin_ref, qn_ref, kn_ref, mqn_ref,
                   kk_ref, qT_ref, iqT_ref, vT_ref, iwT_ref, qmT_ref):
    h = _rms(x_ref[...], g_ref[...]).astype(BF16)
    nq, niq = DSA_H * HD, IDX_H * IDX_D
    q = _dot_nt(w_ref[:nq, :], h)
    iq = _dot_nt(w_ref[nq:nq + niq, :], h)
    feat = _dot_nt(w_ref[nq + niq:, :], h)
    tm = feat.shape[-1]
    cos, sin = cos_ref[...], sin_ref[...]
    o = 0
    k = feat[o:o + HD]; o += HD
    ik = feat[o:o + IDX_D]; o += IDX_D
    v = feat[o:o + HD]; o += HD
    qm = feat[o:o + MEM_W]; o += MEM_W
    iw = feat[o:o + IDX_H]
    qT_ref[...] = (_heads_norm_rope(q, _tile_lanes(qn_ref[...], tm), cos, sin) * QSCALE).astype(BF16)
    iqT_ref[...] = _heads_norm_rope(iq, None, cos, sin).astype(BF16)
    k = _heads_norm_rope(k, _tile_lanes(kn_ref[...], tm), cos, sin)
    ik = _heads_norm_rope(ik, None, cos, sin)
    kk_ref[...] = jnp.concatenate([k, ik], axis=0).T.astype(BF16)
    vT_ref[...] = _augment_values(v).astype(BF16)
    iwT_ref[...] = iw
    qmT_ref[...] = (_heads_norm_rope(qm, _tile_lanes(mqn_ref[...], tm), None, None) * QSCALE).astype(BF16)


def _dsa_proj(x, g, w_in, qn, kn, mem_qn, seq):
    t, d = x.shape
    tm = min(PROJ_TM, seq)
    npos = seq // tm
    wq, wk, wv, wiq, wik, wiw, wqm = jnp.split(w_in, np.cumsum(DSA_SIZES)[:-1].tolist(), axis=1)
    wfeat = jnp.concatenate([wq, wiq, wk, wik, wv, wqm, wiw], axis=1).T.astype(BF16)
    cosF, sinF = _rope_tables(jnp.arange(seq))
    col = lambda n: pl.BlockSpec((n, tm), lambda i: (0, i))
    tab = pl.BlockSpec((HALF, tm), lambda i: (0, i % npos))
    gain = _const_spec((HD, LANES))
    return pl.pallas_call(
        _dsa_proj_body,
        out_shape=(
            jax.ShapeDtypeStruct((t, LANES), BF16), jax.ShapeDtypeStruct((DSA_H * HD, t), BF16),
            jax.ShapeDtypeStruct((IDX_H * IDX_D, t), BF16), jax.ShapeDtypeStruct((VAUG, t), BF16),
            jax.ShapeDtypeStruct((IDX_H, t), F32), jax.ShapeDtypeStruct((MEM_W, t), BF16),
        ),
        grid=(t // tm,),
        in_specs=[pl.BlockSpec((tm, d), lambda i: (i, 0)), _const_spec((1, d)), _const_spec(wfeat.shape),
                  tab, tab, gain, gain, gain],
        out_specs=(pl.BlockSpec((tm, LANES), lambda i: (i, 0)), col(DSA_H * HD), col(IDX_H * IDX_D), col(VAUG),
                   col(IDX_H), col(MEM_W)),
        compiler_params=_cparams("parallel"),
        name="dsa_proj",
    )(x, g.reshape(1, d), wfeat, cosF, sinF, _lane_gain(qn), _lane_gain(kn), _lane_gain(mem_qn))


DSA_TQ = 256
DSA_TK = 256
HALF_BITS = 16
INT16_MIN = -2 ** (HALF_BITS - 1)


def _dsa_attn_body(kk_ref, qT_ref, iqT_ref, vT_ref, iwT_ref, y_ref, keys_ref, hi_ref, lo_ref, bias_ref, s_ref, mc_ref,
                   m_ref, acc_ref, *, topk, seq, tile):
    tq, tk = DSA_TQ, DSA_TK
    qs = tile * tq
    nkc = (qs + tq) // tk
    qpos = qs + lax.broadcasted_iota(I32, (1, tq), 1)
    row = lax.broadcasted_iota(I32, (tk, tq), 0)

    def over_chunks(body, carry):
        for c in range(nkc):
            carry = body(c * tk, carry)
        return carry

    def attend():
        outs = _staged_attention(
            DSA_H, 0, nkc, tk,
            lambda h, r0: _dot(kk_ref[pl.ds(r0, tk), :], _pad_rows(qT_ref[h * HD:(h + 1) * HD, :], 0))
            + bias_ref[pl.ds(r0, tk), :],
            lambda h, r0: vT_ref[:, pl.ds(r0, tk)], s_ref, mc_ref, m_ref, acc_ref)
        y_ref[...] = jnp.concatenate(outs, axis=0).T.astype(BF16)

    if qs + tq <= topk:
        def causal_chunk(r0, carry):
            bias_ref[pl.ds(r0, tk), :] = jnp.where(r0 + row <= qpos, 0.0, NEG)
            return carry

        over_chunks(causal_chunk, 0)
        attend()
        return

    def score_chunk(r0, carry):
        kkc = kk_ref[pl.ds(r0, tk), :]
        acc = jnp.zeros((tk, tq), F32)
        for h in range(IDX_H):
            r = _dot(kkc, _pad_rows(iqT_ref[h * IDX_D:(h + 1) * IDX_D, :], 1))
            acc = acc + jnp.maximum(r, 0.0) * iwT_ref[h:h + 1, :]
        sc = jnp.where(r0 + row <= qpos, acc + 0.0, -jnp.inf)
        bits = pltpu.bitcast(sc, I32)
        key = jnp.where(bits < 0, bits ^ jnp.int32(0x7FFFFFFF), bits)
        keys_ref[pl.ds(r0, tk), :] = key
        hi_ref[pl.ds(r0, tk), :] = lax.shift_right_arithmetic(key, HALF_BITS).astype(I16)
        return carry

    over_chunks(score_chunk, 0)

    def count(pred):
        def body(r0, acc):
            m = pred(keys_ref[pl.ds(r0, tk), :], r0 + row)
            return acc + jnp.sum(m.astype(I32).reshape(tk // SUBLANES, SUBLANES, tq), axis=0)
        acc = over_chunks(body, jnp.zeros((SUBLANES, tq), I32))
        return jnp.sum(acc, axis=0, keepdims=True)

    def fold16(mask):
        ones = jnp.where(mask, jnp.int16(1), jnp.int16(0))
        parts = [ones[i:i + PACKED_SUBLANES] for i in range(0, tk, PACKED_SUBLANES)]
        while len(parts) > 1:
            parts = [a + b for a, b in zip(parts[::2], parts[1::2])]
        return parts[0]

    zeros16 = jnp.zeros((PACKED_SUBLANES, tq), I16)
    total16 = lambda acc: jnp.sum(acc.astype(I32), axis=0, keepdims=True)

    def count16(ref, pred):
        def body(r0, acc):
            return acc + fold16(pred(ref[pl.ds(r0, tk), :]))
        return total16(over_chunks(body, zeros16))

    def kth_largest16(ref, k):
        thr = jnp.where(count16(ref, lambda v: v >= 0) >= k, jnp.int32(0), jnp.int32(INT16_MIN))

        for bit in range(HALF_BITS - 2, -1, -1):
            cand = thr | jnp.int32(1 << bit)
            c16 = cand.astype(I16)
            thr = jnp.where(count16(ref, lambda v, c16=c16: v >= c16) >= k, cand, thr)
        return thr

    thr_hi = kth_largest16(hi_ref, topk)
    thr_hi16 = thr_hi.astype(I16)

    def low_half_chunk(r0, above):
        hi = hi_ref[pl.ds(r0, tk), :]
        lo = ((keys_ref[pl.ds(r0, tk), :] & jnp.int32(0xFFFF)) + jnp.int32(INT16_MIN)).astype(I16)
        lo_ref[pl.ds(r0, tk), :] = jnp.where(hi == thr_hi16, lo, jnp.int16(INT16_MIN))
        return above + fold16(hi > thr_hi16)

    n_above = total16(over_chunks(low_half_chunk, zeros16))
    need_lo = topk - n_above
    thr_lo = kth_largest16(lo_ref, need_lo)
    thr = lax.shift_left(thr_hi, HALF_BITS) | (thr_lo - jnp.int32(INT16_MIN))
    def tie_search():
        need = topk - count(lambda k, _: k > thr)

        def idx_step(b, q):
            cand = q | lax.shift_left(jnp.int32(1), (seq.bit_length() - 2) - b)
            return jnp.where(count(lambda k, idx: (k == thr) & (idx < cand)) < need, cand, q)

        return lax.fori_loop(0, seq.bit_length() - 1, idx_step, jnp.zeros((1, tq), I32))

    thr_lo16 = thr_lo.astype(I16)
    has_ties = jnp.max(n_above + count16(lo_ref, lambda v: v >= thr_lo16)) > topk
    last = lax.cond(has_ties, tie_search, lambda: jnp.full((1, tq), seq, I32))

    def bias_chunk(r0, carry):
        k = keys_ref[pl.ds(r0, tk), :]
        idx = r0 + row
        sel = (k > thr) | ((k == thr) & (idx <= last))
        bias_ref[pl.ds(r0, tk), :] = jnp.where(sel & (idx <= qpos), 0.0, NEG)
        return carry

    over_chunks(bias_chunk, 0)
    attend()


def _dsa_attn(kk, qT, iqT, vT, iwT, batch, seq):
    t = kk.shape[0]
    tq = DSA_TQ
    nq = seq // tq
    topk = min(DSA_TOPK_MAX, seq // 4)
    def one_tile(i):
        keys = (i + 1) * tq
        colq = lambda n: pl.BlockSpec((n, tq), lambda b: (0, b * nq + i))
        return pl.pallas_call(
            functools.partial(_dsa_attn_body, topk=topk, seq=seq, tile=i),
            out_shape=jax.ShapeDtypeStruct((batch, tq, DSA_H * HD), BF16),
            grid=(batch,),
            in_specs=[
                pl.BlockSpec((keys, LANES), lambda b: (b * (seq // keys), 0)) if seq % keys == 0 else
                pl.BlockSpec((seq, LANES), lambda b: (b, 0)),
                colq(DSA_H * HD), colq(IDX_H * IDX_D),
                pl.BlockSpec((VAUG, seq), lambda b: (0, b)),
                colq(IDX_H),
            ],
            out_specs=pl.BlockSpec((None, tq, DSA_H * HD), lambda b: (b, 0, 0)),
            scratch_shapes=[pltpu.VMEM((keys, tq), I32), pltpu.VMEM((keys, tq), I16), pltpu.VMEM((keys, tq), I16),
                            pltpu.VMEM((keys, tq), F32),
                            pltpu.VMEM((2, DSA_H, DSA_TK, tq), F32), pltpu.VMEM((2, DSA_H * SUBLANES, tq), F32),
                            pltpu.VMEM((2 * SUBLANES, tq), F32), pltpu.VMEM((DSA_H * VAUG, tq), F32)],
            compiler_params=_cparams("parallel"),
            name=f"dsa_attn_tile{i}",
        )(kk, qT, iqT, vT, iwT)

    tiles = [one_tile(i) for i in range(nq)]
    return jnp.stack(tiles, axis=1).reshape(t, DSA_H * HD)


NSA_KV = NSA_G * HD
NSA_GATES = NSA_H * 3
NSA_GATES_PAD = -NSA_GATES % SUBLANES


def _nsa_proj_body(x_ref, g_ref, wtok_ref, wfeat_ref, cos_ref, sin_ref, qn_ref, kns_ref, knw_ref, mqn_ref,
                   kc_ref, vc_ref, ks_ref, kw_ref, qT_ref, vsT_ref, vwT_ref, gT_ref, qmT_ref):
    h = _rms(x_ref[...], g_ref[...]).astype(BF16)
    tok = _dot(h, wtok_ref[...])
    kc_ref[...] = tok[:, :NSA_KV]
    vc_ref[...] = tok[:, NSA_KV:]
    nq = NSA_H * HD
    q = _dot_nt(wfeat_ref[:nq, :], h)
    kk = _dot_nt(wfeat_ref[nq:nq + 2 * NSA_KV, :], h)
    feat = _dot_nt(wfeat_ref[nq + 2 * NSA_KV:, :], h)
    tm = feat.shape[-1]
    cos, sin = cos_ref[...], sin_ref[...]
    ks, kw = kk[:NSA_KV], kk[NSA_KV:]
    o = 0
    vs = feat[o:o + NSA_KV]; o += NSA_KV
    vw = feat[o:o + NSA_KV]; o += NSA_KV
    qm = feat[o:o + MEM_W]; o += MEM_W
    gates = feat[o:]
    qT_ref[...] = (_heads_norm_rope(q, _tile_lanes(qn_ref[...], tm), cos, sin) * QSCALE).astype(BF16)
    ks_ref[...] = _heads_norm_rope(ks, _tile_lanes(kns_ref[...], tm), cos, sin).T.astype(BF16)
    kw_ref[...] = _heads_norm_rope(kw, _tile_lanes(knw_ref[...], tm), cos, sin).T.astype(BF16)
    vsT_ref[...] = _augment_values(vs).astype(BF16)
    vwT_ref[...] = _augment_values(vw).astype(BF16)
    gT_ref[...] = jax.nn.sigmoid(gates)
    qmT_ref[...] = (_heads_norm_rope(qm, _tile_lanes(mqn_ref[...], tm), None, None) * QSCALE).astype(BF16)


def _nsa_proj(x, g, w_in, qn, kn, mem_qn, seq):
    t, d = x.shape
    tm = min(PROJ_TM, seq)
    npos = seq // tm
    wq, wkc, wvc, wks, wvs, wkw, wvw, wgt, wqm = jnp.split(w_in, np.cumsum(NSA_SIZES)[:-1].tolist(), axis=1)
    wtok = jnp.concatenate([wkc, wvc], axis=1).astype(BF16)
    wgt = jnp.pad(wgt, ((0, 0), (0, NSA_GATES_PAD)))
    wfeat = jnp.concatenate([wq, wks, wkw, wvs, wvw, wqm, wgt], axis=1).T.astype(BF16)
    cosF, sinF = _rope_tables(jnp.arange(seq))
    row = lambda n: pl.BlockSpec((tm, n), lambda i: (i, 0))
    col = lambda n: pl.BlockSpec((n, tm), lambda i: (0, i))
    tab = pl.BlockSpec((HALF, tm), lambda i: (0, i % npos))
    gain = _const_spec((HD, LANES))
    ngt = NSA_GATES + NSA_GATES_PAD
    return pl.pallas_call(
        _nsa_proj_body,
        out_shape=(
            jax.ShapeDtypeStruct((t, NSA_KV), F32), jax.ShapeDtypeStruct((t, NSA_KV), F32),
            jax.ShapeDtypeStruct((t, NSA_KV), BF16), jax.ShapeDtypeStruct((t, NSA_KV), BF16),
            jax.ShapeDtypeStruct((NSA_H * HD, t), BF16), jax.ShapeDtypeStruct((NSA_G * VAUG, t), BF16),
            jax.ShapeDtypeStruct((NSA_G * VAUG, t), BF16), jax.ShapeDtypeStruct((ngt, t), F32),
            jax.ShapeDtypeStruct((MEM_W, t), BF16),
        ),
        grid=(t // tm,),
        in_specs=[row(d), _const_spec((1, d)), _const_spec(wtok.shape), _const_spec(wfeat.shape),
                  tab, tab, gain, gain, gain, gain],
        out_specs=(row(NSA_KV), row(NSA_KV), row(NSA_KV), row(NSA_KV), col(NSA_H * HD), col(NSA_G * VAUG),
                   col(NSA_G * VAUG), col(ngt), col(MEM_W)),
        compiler_params=_cparams("parallel"),
        name="nsa_proj",
    )(x, g.reshape(1, d), wtok, wfeat, cosF, sinF, _lane_gain(qn), _lane_gain(kn[1]), _lane_gain(kn[2]),
      _lane_gain(mem_qn))


def _nsa_cmp_body(k01_ref, k23_ref, v01_ref, v23_ref, pk_ref, pv_ref, wk_ref, wv_ref, kn_ref, cos_ref, sin_ref,
                  k_ref, vT_ref):
    n = k_ref.shape[1]

    def compress(lo_ref, hi_ref, pos, w_ref):
        parts = []
        for l in range(CMP_S):
            parts += [lo_ref[pl.ds(l, n, stride=CMP_S), :], hi_ref[pl.ds(l, n, stride=CMP_S), :]]
        x = jnp.concatenate(parts, axis=1)
        xa = (x + pos[0:1]).astype(BF16)
        xb = (pltpu.roll(x, n - 1, axis=0) + pos[1:2]).astype(BF16)
        return _dot_nt(w_ref[0], xa) + _dot_nt(w_ref[1], xb)

    kT = compress(k01_ref, k23_ref, pk_ref[...], wk_ref)
    kT = _heads_norm_rope(kT, _tile_lanes(kn_ref[...], kT.shape[-1]), cos_ref[...], sin_ref[...])
    k_ref[0] = kT.T.astype(BF16)
    vT_ref[0] = compress(v01_ref, v23_ref, pv_ref[...], wv_ref).astype(BF16)


def _nsa_cmp_weights(w, pos):
    halves, width = CMP_L // CMP_S, CMP_S * NSA_KV
    w_eld = w.reshape(halves, CMP_S, HD, HD).transpose(0, 3, 1, 2)
    per_group = jnp.broadcast_to(w_eld[:, :, :, None, :], (halves, HD, CMP_S, NSA_G, HD)).reshape(halves, HD, width)
    rows = np.arange(NSA_KV)[:, None] // HD
    cols = (np.arange(width)[None, :] // HD) % NSA_G
    wt = jnp.where(jnp.asarray(rows == cols)[None], jnp.tile(per_group, (1, NSA_G, 1)), 0.0)
    p = jnp.broadcast_to(pos.reshape(halves, CMP_S, 1, HD), (halves, CMP_S, NSA_G, HD))
    return wt.astype(BF16), p.reshape(halves, width).astype(F32)


def _nsa_cmp(kc, vc, pos_k, pos_v, wk, wv, kn0, batch, seq):
    n = seq // CMP_S
    wkt, pk = _nsa_cmp_weights(wk, pos_k)
    wvt, pv = _nsa_cmp_weights(wv, pos_v)
    cosE, sinE = _rope_tables(jnp.arange(n) * CMP_S + (CMP_L - 1))
    lanes_lo = pl.BlockSpec((seq, LANES), lambda b: (b, 0))
    lanes_hi = pl.BlockSpec((seq, LANES), lambda b: (b, 1))
    return pl.pallas_call(
        _nsa_cmp_body,
        out_shape=(jax.ShapeDtypeStruct((batch, n, NSA_KV), BF16), jax.ShapeDtypeStruct((batch, NSA_KV, n), BF16)),
        grid=(batch,),
        in_specs=[lanes_lo, lanes_hi, lanes_lo, lanes_hi, _const_spec(pk.shape), _const_spec(pv.shape),
                  _const_spec(wkt.shape), _const_spec(wvt.shape), _const_spec((HD, LANES)), _const_spec((HALF, n)),
                  _const_spec((HALF, n))],
        out_specs=(pl.BlockSpec((1, n, NSA_KV), lambda b: (b, 0, 0)), pl.BlockSpec((1, NSA_KV, n), lambda b: (b, 0, 0))),
        compiler_params=_cparams("parallel"),
        name="nsa_cmp",
    )(kc, kc, vc, vc, pk, pv, wkt, wvt, _lane_gain(kn0), cosE, sinE)


NSA_TQ = 256
NSA_TK = 256


def _staged_attention(nheads, lo, hi, tk, logits, values, s_ref, mc_ref, m_ref, acc_ref):
    tq = s_ref.shape[-1]
    fold = lambda a: a.reshape(tk // SUBLANES, SUBLANES, tq)
    m_ref[...] = jnp.full(m_ref.shape, -jnp.inf, F32)
    acc_ref[...] = jnp.zeros(acc_ref.shape, F32)

    start = lambda c: c * tk if isinstance(c, int) else pl.multiple_of(c * tk, tk)

    def stage(h, c, slot):
        s = logits(h, start(c))
        s_ref[slot, h] = s
        mc_ref[slot, h * SUBLANES:(h + 1) * SUBLANES, :] = jnp.max(fold(s), axis=0)

    def consume(h, c, slot):
        g8, hrows = slice(h * SUBLANES, (h + 1) * SUBLANES), slice(h * VAUG, (h + 1) * VAUG)
        m_old = m_ref[h:h + 1, :]
        m_new = jnp.maximum(m_old, jnp.max(mc_ref[slot, g8, :], axis=0, keepdims=True))
        alpha = jnp.exp2(m_old - m_new)
        p = jnp.exp2(s_ref[slot, h] - m_new)
        acc_ref[hrows, :] = acc_ref[hrows, :] * alpha + _dot(values(h, start(c)), p.astype(BF16))
        m_ref[h:h + 1, :] = m_new

    def consume_and_stage_next(c, slot):
        for h in range(nheads):
            consume(h, c, slot)
            stage(h, c + 1, 1 - slot)

    def finish(c, slot):
        for h in range(nheads):
            consume(h, c, slot)

    for h in range(nheads):
        stage(h, lo, 0)
    n_fused = hi - 1 - lo

    def two_chunks(j, carry):
        c = lo + 2 * j
        consume_and_stage_next(c, 0)
        consume_and_stage_next(c + 1, 1)
        return carry

    if isinstance(n_fused, int):
        for i in range(n_fused):
            consume_and_stage_next(lo + i, i % 2)
        finish(hi - 1, n_fused % 2)
    else:
        lax.fori_loop(0, n_fused // 2, two_chunks, 0)

        @pl.when(n_fused % 2 == 1)
        def _():
            consume_and_stage_next(hi - 2, 0)
            finish(hi - 1, 1)

        @pl.when(n_fused % 2 == 0)
        def _():
            finish(hi - 1, 0)

    return [acc_ref[h * VAUG:h * VAUG + HD, :] / acc_ref[h * VAUG + HD:h * VAUG + HD + 1, :] for h in range(nheads)]


def _nsa_attn_body(qT_ref, gT_ref, ks_ref, vsT_ref, kw_ref, vwT_ref, kc_ref, vcT_ref, cov_ref, y_ref,
                   bias_ref, out_ref, s_ref, mc_ref, m_ref, acc_ref, *, seq, tile):
    tq, tk = NSA_TQ, NSA_TK
    qs = tile * tq
    nkc = (qs + tq) // tk
    wlo = max((qs - WIN) // tk, 0)
    qpos = qs + lax.broadcasted_iota(I32, (1, tq), 1)
    ncr = seq // CMP_S
    nblk = seq // SLC_L
    n_sel = min(SLC_N_MAX, nblk)
    valid_c = CMP_S * lax.broadcasted_iota(I32, (ncr, tq), 0) + (CMP_L - 1) <= qpos
    any_c = jnp.where(qpos >= CMP_L - 1, 1.0, 0.0)
    jrow = lax.broadcasted_iota(I32, (nblk, tq), 0)
    qblk = qpos // SLC_L
    forced = (jrow == 0) | (jrow == qblk) | (jrow == qblk - 1)
    causal_blk = jrow * SLC_L <= qpos
    krow = lax.broadcasted_iota(I32, (tk, tq), 0)
    brow = lax.broadcasted_iota(I32, (SLC_L, tq), 0)
    group = lambda h: h // NSA_R
    pair = lambda h: slice((group(h) // 2) * LANES, (group(h) // 2 + 1) * LANES)
    grows = lambda h: slice(group(h) * HD, (group(h) + 1) * HD)
    vrows = lambda h: slice(group(h) * VAUG, (group(h) + 1) * VAUG)
    q_of = lambda h: _pad_rows(qT_ref[h * HD:(h + 1) * HD, :], group(h) % 2)
    gate = lambda h, branch: gT_ref[h * 3 + branch:h * 3 + branch + 1, :]

    selbs = []
    for g in range(NSA_G):
        heads = [g * NSA_R + r for r in range(NSA_R)]
        kcm = kc_ref[0][:, pair(heads[0])]
        vcm = vcT_ref[0][grows(heads[0]), :]
        psum = jnp.zeros((ncr, tq), F32)
        for h in heads:
            s = jnp.where(valid_c, _dot(kcm, q_of(h)), NEG)
            p = jnp.exp2(s - jnp.max(s, axis=0, keepdims=True))
            p = p / jnp.sum(p, axis=0, keepdims=True) * any_c
            out_ref[h * HD:(h + 1) * HD, :] = gate(h, 0) * _dot(vcm, p.astype(BF16))
            psum = psum + p
        p_hi = psum.astype(BF16)
        p_lo = (psum - p_hi.astype(F32)).astype(BF16)
        imp = _dot(cov_ref[...], p_hi) + _dot(cov_ref[...], p_lo)
        imp = jnp.where(causal_blk, imp + jnp.where(forced, FORCE, 0.0), NEG)
        rank = jnp.zeros((nblk, tq), I32)
        for j2 in range(nblk):
            rj = imp[j2:j2 + 1]
            beats = (rj > imp) | ((rj == imp) & (j2 < jrow))
            rank = rank + beats.astype(I32)
        selbs.append(jnp.where(rank < n_sel, 0.0, NEG))

    blocks_per_chunk = tk // SLC_L
    for g in range(NSA_G):
        for j in range(nkc * blocks_per_chunk):
            bias_ref[g, j * SLC_L:(j + 1) * SLC_L, :] = jnp.where(j * SLC_L + brow <= qpos, selbs[g][j:j + 1], NEG)

    stats = (s_ref, mc_ref, m_ref, acc_ref)
    slc = _staged_attention(
        NSA_H, 0, nkc, tk,
        lambda h, r0: _dot(ks_ref[pl.ds(r0, tk), pair(h)], q_of(h)) + bias_ref[group(h), pl.ds(r0, tk), :],
        lambda h, r0: vsT_ref[vrows(h), pl.ds(r0, tk)], *stats)
    for h in range(NSA_H):
        out_ref[h * HD:(h + 1) * HD, :] = out_ref[h * HD:(h + 1) * HD, :] + gate(h, 1) * slc[h]

    for c in range(wlo, nkc):
        dist = qpos - (c * tk + krow)
        bias_ref[0, c * tk:(c + 1) * tk, :] = jnp.where((dist >= 0) & (dist < WIN), 0.0, NEG)
    win = _staged_attention(
        NSA_H, wlo, nkc, tk,
        lambda h, r0: _dot(kw_ref[pl.ds(r0, tk), pair(h)], q_of(h)) + bias_ref[0, pl.ds(r0, tk), :],
        lambda h, r0: vwT_ref[vrows(h), pl.ds(r0, tk)], *stats)
    outs = [out_ref[h * HD:(h + 1) * HD, :] + gate(h, 2) * win[h] for h in range(NSA_H)]
    y_ref[...] = jnp.concatenate(outs, axis=0).T.astype(BF16)


def _nsa_attn(qT, gT, ks, vsT, kw, vwT, kcmp, vcmpT, batch, seq):
    t = ks.shape[0]
    tq = NSA_TQ
    nq = seq // tq
    ncr, nblk = seq // CMP_S, seq // SLC_L
    starts = np.arange(ncr) * CMP_S
    sstart = np.arange(nblk) * SLC_L
    cover = (starts[None, :] < sstart[:, None] + SLC_L) & (starts[None, :] + CMP_L > sstart[:, None])
    cover[:, ncr - 1] = False
    tok = pl.BlockSpec((seq, NSA_KV), lambda b: (b, 0))
    feat = pl.BlockSpec((NSA_G * VAUG, seq), lambda b: (0, b))
    cover = jnp.asarray(cover, BF16)

    def one_tile(i):
        colq = lambda n: pl.BlockSpec((n, tq), lambda b: (0, b * nq + i))
        return pl.pallas_call(
            functools.partial(_nsa_attn_body, seq=seq, tile=i),
            out_shape=jax.ShapeDtypeStruct((batch, tq, NSA_H * HD), BF16),
            grid=(batch,),
            in_specs=[
                colq(NSA_H * HD), colq(gT.shape[0]), tok, feat, tok, feat,
                pl.BlockSpec((1, ncr, NSA_KV), lambda b: (b, 0, 0)),
                pl.BlockSpec((1, NSA_KV, ncr), lambda b: (b, 0, 0)),
                _const_spec((nblk, ncr)),
            ],
            out_specs=pl.BlockSpec((None, tq, NSA_H * HD), lambda b: (b, 0, 0)),
            scratch_shapes=[pltpu.VMEM((NSA_G, (i + 1) * tq, tq), F32), pltpu.VMEM((NSA_H * HD, tq), F32),
                            pltpu.VMEM((2, NSA_H, NSA_TK, tq), F32), pltpu.VMEM((2, NSA_H * SUBLANES, tq), F32),
                            pltpu.VMEM((2 * SUBLANES, tq), F32), pltpu.VMEM((NSA_H * VAUG, tq), F32)],
            compiler_params=_cparams("parallel"),
            name=f"nsa_attn_tile{i}",
        )(qT, gT, ks, vsT, kw, vwT, kcmp, vcmpT, cover)

    tiles = [one_tile(i) for i in range(nq)]
    return jnp.stack(tiles, axis=1).reshape(t, NSA_H * HD)


def _nsa_layer_mix(x, g, w_in, qn, kn, pos_k, pos_v, wk, wv, mem_qn, batch, seq):
    kc, vc, ks, kw, qT, vsT, vwT, gT, qmT = _nsa_proj(x, g, w_in, qn, kn, mem_qn, seq)
    kcmp, vcmpT = _nsa_cmp(kc, vc, pos_k, pos_v, wk, wv, kn[0], batch, seq)
    return _nsa_attn(qT, gT, ks, vsT, kw, vwT, kcmp, vcmpT, batch, seq), qmT


def kernel(x, mem, ffn_norm, ffn_w_gate, ffn_w_up, ffn_w_down, mix_norm, w_out, mem_norm, mem_w_kv, mem_qn, mem_kn, ret_w_in, dsa_w_in, dsa_qn, dsa_kn, nsa_w_in, nsa_qn, nsa_kn, nsa_cmp_pos_k, nsa_cmp_pos_v, nsa_cmp_wk, nsa_cmp_wv):
    batch, seq, d = x.shape
    x = x.reshape(batch * seq, d)
    ffn_w = (ffn_w_gate.astype(BF16), ffn_w_up.astype(BF16), ffn_w_down.astype(BF16))
    for i in range(ffn_norm.shape[0]):
        x = _ffn(x, ffn_norm[i, 0], *ffn_w, i, 0)
        kind, j = i % N_MIXERS, i // N_MIXERS
        if kind == 0:
            q, v, gate, kT, qmT = _ret_proj(x, mix_norm[i], ret_w_in[j], mem_qn[i], seq)
            y_mix = _retention(q, kT, v, gate, batch, seq)
        elif kind == 1:
            kk, qT, iqT, vT, iwT, qmT = _dsa_proj(x, mix_norm[i], dsa_w_in[j], dsa_qn[j], dsa_kn[j], mem_qn[i], seq)
            y_mix = _dsa_attn(kk, qT, iqT, vT, iwT, batch, seq)
        else:
            y_mix, qmT = _nsa_layer_mix(x, mix_norm[i], nsa_w_in[j], nsa_qn[j], nsa_kn[j], nsa_cmp_pos_k[j],
                                        nsa_cmp_pos_v[j], nsa_cmp_wk[j], nsa_cmp_wv[j], mem_qn[i], batch, seq)
        mem_k, mem_vT = _mem_kv(mem, mem_norm[i], mem_w_kv[i], mem_kn[i])
        y_mem = _mem_attn(qmT, mem_k, mem_vT, seq)
        x = _mix_out_ffn(y_mix, y_mem, x, w_out[i], ffn_norm[i, 1], *ffn_w, i, 1)
    return x.reshape(batch, seq, d)
```
